```python
import jax, jax.numpy as jnp
from jax import lax
import numpy as np

D_MODEL = 1024
BATCH = 2
SEQ = 8192
DEPTH = 2

PLE_DIM = 256
N_A_LAYERS = DEPTH // 2
N_B_LAYERS = DEPTH - N_A_LAYERS
N_HEADS = 16
HEAD_DIM = 64
ROPE_DIM = HEAD_DIM // 4
ROPE_THETA = 500000.0
Q_BLOCK = 128
NSA_GROUPS = 4
HEADS_PER_GROUP = N_HEADS // NSA_GROUPS
CMP_BLOCK = 32
CMP_STRIDE = 16
CMP_HIDDEN = 256
SEL_BLOCK = 64
SEL_TOPK = 16
WINDOW = 512
N_EXPERTS = 32
TOP_K = 4
D_EXPERT = D_MODEL
SWIGLU_LIMIT = 7.0
SWIGLU_ALPHA = 1.702
MOE_BLOCK = 256
EPS = 1e-6
NEG = -1e30
FORCE = 1e6

kernel_name = "yoco_fox_nsa_moe_ple"


def rmsnorm(x, g):
    xf = x.astype(jnp.float32)
    y = xf * lax.rsqrt(jnp.mean(xf * xf, axis=-1, keepdims=True) + EPS)
    return (y * g.astype(jnp.float32)).astype(x.dtype)


def rope_partial(x, pos):
    inv = jnp.power(jnp.float32(ROPE_THETA), -jnp.arange(0, ROPE_DIM, 2, dtype=jnp.float32) / ROPE_DIM)
    ang = pos.astype(jnp.float32)[:, None] * inv[None, :]
    cos, sin = jnp.cos(ang), jnp.sin(ang)
    xr = x[..., :ROPE_DIM].astype(jnp.float32)
    half = ROPE_DIM // 2
    x1, x2 = xr[..., :half], xr[..., half:]
    rot = jnp.concatenate([x1 * cos - x2 * sin, x2 * cos + x1 * sin], axis=-1)
    return jnp.concatenate([rot.astype(x.dtype), x[..., ROPE_DIM:]], axis=-1)


def fox_attention(h, w_in, b_f, qk_gain, w_out):
    B, T, _ = h.shape
    hd = N_HEADS * HEAD_DIM
    proj = h @ w_in
    to_heads = lambda z: z.reshape(B, T, N_HEADS, HEAD_DIM).transpose(0, 2, 1, 3)
    q = rmsnorm(to_heads(proj[..., :hd]), qk_gain[0])
    k = rmsnorm(to_heads(proj[..., hd:2 * hd]), qk_gain[1])
    v = to_heads(proj[..., 2 * hd:3 * hd])
    logf = jax.nn.log_sigmoid(proj[..., 3 * hd:].astype(jnp.float32) + b_f.astype(jnp.float32))
    c = jnp.cumsum(logf, axis=1).transpose(0, 2, 1)
    n_blk = T // Q_BLOCK
    qb = q.reshape(B, N_HEADS, n_blk, Q_BLOCK, HEAD_DIM).transpose(2, 0, 1, 3, 4)
    cb = c.reshape(B, N_HEADS, n_blk, Q_BLOCK).transpose(2, 0, 1, 3)
    pb = jnp.arange(T).reshape(n_blk, Q_BLOCK)
    k_pos = jnp.arange(T)
    scale = HEAD_DIM ** -0.5

    def block(args):
        qi, ci, ti = args
        s = jnp.einsum('bhqd,bhkd->bhqk', qi, k).astype(jnp.float32) * scale
        s = s + ci[..., None] - c[:, :, None, :]
        s = jnp.where(k_pos[None, :] <= ti[:, None], s, NEG)
        pr = jax.nn.softmax(s, axis=-1)
        return jnp.einsum('bhqk,bhkd->bhqd', pr.astype(v.dtype), v)

    o = lax.map(block, (qb, cb, pb))
    o = o.transpose(1, 0, 3, 2, 4).reshape(B, T, hd)
    return o @ w_out


def nsa_shared_kv(x, kv_norm, kv_w, kv_k_gain, cmp_pos, cmp_w1, cmp_w2):
    B, T, _ = x.shape
    h = rmsnorm(x, kv_norm)
    kv = (h @ kv_w).reshape(B, T, 6, NSA_GROUPS, HEAD_DIM).transpose(2, 0, 3, 1, 4)
    pos = jnp.arange(T)
    n_cmp = (T - CMP_BLOCK) // CMP_STRIDE + 1
    idx = jnp.arange(n_cmp)[:, None] * CMP_STRIDE + jnp.arange(CMP_BLOCK)[None, :]

    def compress(z, j):
        blocks = z[:, :, idx] + cmp_pos[j]
        flat = blocks.reshape(B, NSA_GROUPS, n_cmp, CMP_BLOCK * HEAD_DIM)
        return jax.nn.silu(flat @ cmp_w1[j]) @ cmp_w2[j]

    cmp_end = jnp.arange(n_cmp) * CMP_STRIDE + CMP_BLOCK - 1
    kc = rope_partial(rmsnorm(compress(kv[0], 0), kv_k_gain[0]), cmp_end)
    vc = compress(kv[1], 1)
    ks = rope_partial(rmsnorm(kv[2], kv_k_gain[1]), pos)
    kw = rope_partial(rmsnorm(kv[4], kv_k_gain[2]), pos)
    n_sel = T // SEL_BLOCK
    ks_b = ks.reshape(B, NSA_GROUPS, n_sel, SEL_BLOCK, HEAD_DIM)
    vs_b = kv[3].reshape(B, NSA_GROUPS, n_sel, SEL_BLOCK, HEAD_DIM)
    pad = ((0, 0), (0, 0), (WINDOW, 0), (0, 0))
    return kc, vc, ks_b, vs_b, jnp.pad(kw, pad), jnp.pad(kv[5], pad)


def nsa_attention(h, w_in, q_gain, w_out, kc, vc, ks_b, vs_b, kw_p, vw_p):
    B, T, _ = h.shape
    G, M = NSA_GROUPS, HEADS_PER_GROUP
    hd = N_HEADS * HEAD_DIM
    proj = h @ w_in
    pos = jnp.arange(T)
    q = proj[..., :hd].reshape(B, T, N_HEADS, HEAD_DIM).transpose(0, 2, 1, 3)
    q = rope_partial(rmsnorm(q, q_gain), pos).reshape(B, G, M, T, HEAD_DIM)
    gates = jax.nn.sigmoid(proj[..., hd:].astype(jnp.float32)).reshape(B, T, 3, G, M).transpose(2, 0, 3, 4, 1)
    n_blk = T // Q_BLOCK
    q_blocks = q.reshape(B, G, M, n_blk, Q_BLOCK, HEAD_DIM).transpose(3, 0, 1, 2, 4, 5)
    g_blocks = gates.reshape(3, B, G, M, n_blk, Q_BLOCK).transpose(4, 0, 1, 2, 3, 5)
    n_cmp = kc.shape[2]
    n_sel = ks_b.shape[2]
    n_top = min(SEL_TOPK, n_sel)
    cmp_start = jnp.arange(n_cmp) * CMP_STRIDE
    cmp_end = cmp_start + CMP_BLOCK - 1
    sel_start = jnp.arange(n_sel) * SEL_BLOCK
    overlap = jnp.clip(jnp.minimum(cmp_start[:, None] + CMP_BLOCK, sel_start[None, :] + SEL_BLOCK)
                       - jnp.maximum(cmp_start[:, None], sel_start[None, :]), 0).astype(jnp.float32)
    b_ix = jnp.arange(B)[:, None, None, None]
    g_ix = jnp.arange(G)[None, :, None, None]
    j_ix = jnp.arange(n_sel)
    scale = HEAD_DIM ** -0.5

    def block(args):
        qi, gi, bi = args
        q0 = bi * Q_BLOCK
        tq = q0 + jnp.arange(Q_BLOCK)
        s = jnp.einsum('bgmqd,bgnd->bgmqn', qi, kc).astype(jnp.float32) * scale
        valid_c = cmp_end[None, :] <= tq[:, None]
        p_c = jax.nn.softmax(jnp.where(valid_c, s, NEG), axis=-1) * valid_c
        o_c = jnp.einsum('bgmqn,bgnd->bgmqd', p_c.astype(vc.dtype), vc)
        imp = jnp.einsum('bgmqn,nj->bgqj', p_c, overlap)
        cur = tq // SEL_BLOCK
        forced = (j_ix[None, :] == 0) | (j_ix[None, :] == cur[:, None]) | (j_ix[None, :] == cur[:, None] - 1)
        causal_j = sel_start[None, :] <= tq[:, None]
        score = jnp.where(causal_j, jnp.where(forced, FORCE, imp), -jnp.inf)
        top_s, top_j = lax.top_k(score, n_top)
        ks = ks_b[b_ix, g_ix, top_j].reshape(B, G, Q_BLOCK, n_top * SEL_BLOCK, HEAD_DIM)
        vs = vs_b[b_ix, g_ix, top_j].reshape(B, G, Q_BLOCK, n_top * SEL_BLOCK, HEAD_DIM)
        kpos = top_j[..., None] * SEL_BLOCK + jnp.arange(SEL_BLOCK)
        mask_s = (jnp.isfinite(top_s)[..., None] & (kpos <= tq[:, None, None])).reshape(B, G, Q_BLOCK, n_top * SEL_BLOCK)
        s = jnp.einsum('bgmqd,bgqkd->bgmqk', qi, ks).astype(jnp.float32) * scale
        p_s = jax.nn.softmax(jnp.where(mask_s[:, :, None], s, NEG), axis=-1)
        o_s = jnp.einsum('bgmqk,bgqkd->bgmqd', p_s.astype(vs.dtype), vs)
        kw = lax.dynamic_slice_in_dim(kw_p, q0, Q_BLOCK + WINDOW, axis=2)
        vw = lax.dynamic_slice_in_dim(vw_p, q0, Q_BLOCK + WINDOW, axis=2)
        wpos = q0 - WINDOW + jnp.arange(Q_BLOCK + WINDOW)
        mask_w = (wpos[None, :] <= tq[:, None]) & (wpos[None, :] > tq[:, None] - WINDOW) & (wpos[None, :] >= 0)
        s = jnp.einsum('bgmqd,bgkd->bgmqk', qi, kw).astype(jnp.float32) * scale
        p_w = jax.nn.softmax(jnp.where(mask_w, s, NEG), axis=-1)
        o_w = jnp.einsum('bgmqk,bgkd->bgmqd', p_w.astype(vw.dtype), vw)
        o = gi[0][..., None] * o_c + gi[1][..., None] * o_s + gi[2][..., None] * o_w
        return o.astype(qi.dtype)

    o = lax.map(block, (q_blocks, g_blocks, jnp.arange(n_blk)))
    o = o.transpose(1, 0, 4, 2, 3, 5).reshape(B, T, hd)
    return o @ w_out


def moe_ffn(h, router_w, router_b, w_gu, b_gu, w_down, b_down):
    B, T, D = h.shape
    xt = h.reshape(B * T, D)
    n_assign = B * T * TOP_K
    logits = (xt @ router_w).astype(jnp.float32) + router_b.astype(jnp.float32)
    top_val, top_idx = lax.top_k(logits, TOP_K)
    gate = jax.nn.softmax(top_val, axis=-1)
    flat_e = top_idx.reshape(-1)
    flat_tok = jnp.arange(n_assign) // TOP_K
    order = jnp.argsort(flat_e)
    se, stok, sg = flat_e[order], flat_tok[order], gate.reshape(-1)[order]
    counts = jnp.bincount(flat_e, length=N_EXPERTS)
    padded = (counts + MOE_BLOCK - 1) // MOE_BLOCK * MOE_BLOCK
    start = jnp.cumsum(counts) - counts
    pend = jnp.cumsum(padded)
    pstart = pend - padded
    dest = pstart[se] + jnp.arange(n_assign) - start[se]
    n_blocks = -(-n_assign // MOE_BLOCK) + N_EXPERTS
    cap = n_blocks * MOE_BLOCK
    buf_tok = jnp.zeros((cap,), jnp.int32).at[dest].set(stok)
    buf_g = jnp.zeros((cap,), jnp.float32).at[dest].set(sg)
    blk_exp = jnp.minimum(jnp.searchsorted(pend, jnp.arange(n_blocks) * MOE_BLOCK, side='right'), N_EXPERTS - 1)

    def run_block(args):
        tok, g, e = args
        xb = xt[tok]
        gu = xb @ w_gu[e] + b_gu[e]
        a, u = gu[:, :D_EXPERT], gu[:, D_EXPERT:]
        a = jnp.minimum(a, SWIGLU_LIMIT)
        u = jnp.clip(u, -SWIGLU_LIMIT, SWIGLU_LIMIT)
        y = (u + 1) * (a * jax.nn.sigmoid(SWIGLU_ALPHA * a))
        out = y @ w_down[e] + b_down[e]
        return (out * g[:, None]).astype(xt.dtype)

    outs = lax.map(run_block, (buf_tok.reshape(n_blocks, MOE_BLOCK), buf_g.reshape(n_blocks, MOE_BLOCK), blk_exp))
    y = jnp.zeros_like(xt).at[buf_tok].add(outs.reshape(cap, D))
    return y.reshape(B, T, D)


def setup_inputs(seed: int = 0) -> dict:
    key = jax.random.key(seed)
    keys = iter(jax.random.split(key, 40))

    def nrm(shape, scale):
        return jax.random.normal(next(keys), shape, jnp.float32) * scale

    def gain(shape):
        return 1.0 + 0.05 * jax.random.normal(next(keys), shape, jnp.float32)

    D, H, hd = D_MODEL, N_HEADS, HEAD_DIM
    F, E = D_EXPERT, N_EXPERTS
    return {
        "x": nrm((BATCH, SEQ, D), 1.0),
        "p": nrm((DEPTH, BATCH, SEQ, PLE_DIM), 1.0),
        "attn_norm": gain((DEPTH, D)),
        "ffn_norm": gain((DEPTH, D)),
        "ple_norm": gain((DEPTH, D)),
        "ple_gate_w": nrm((DEPTH, D, D), D ** -0.5),
        "ple_proj_w": nrm((DEPTH, PLE_DIM, D), PLE_DIM ** -0.5),
        "router_w": nrm((DEPTH, D, E), D ** -0.5),
        "router_b": nrm((DEPTH, E), 0.01),
        "w_gu": nrm((DEPTH, E, D, 2 * F), D ** -0.5),
        "b_gu": nrm((DEPTH, E, 2 * F), 0.02),
        "w_down": nrm((DEPTH, E, F, D), F ** -0.5),
        "b_down": nrm((DEPTH, E, D), 0.02),
        "fox_w_in": nrm((N_A_LAYERS, D, 3 * H * hd + H), D ** -0.5),
        "fox_b_f": 2.0 + nrm((N_A_LAYERS, H), 0.5),
        "fox_qk_gain": gain((N_A_LAYERS, 2, hd)),
        "fox_w_out": nrm((N_A_LAYERS, H * hd, D), (H * hd) ** -0.5),
        "kv_norm": gain((D,)),
        "kv_w": nrm((D, 6 * NSA_GROUPS * hd), D ** -0.5),
        "kv_k_gain": gain((3, hd)),
        "cmp_pos": nrm((2, CMP_BLOCK, hd), 0.5),
        "cmp_w1": nrm((2, CMP_BLOCK * hd, CMP_HIDDEN), (CMP_BLOCK * hd) ** -0.5),
        "cmp_w2": nrm((2, CMP_HIDDEN, hd), CMP_HIDDEN ** -0.5),
        "nsa_w_in": nrm((N_B_LAYERS, D, H * hd + 3 * H), D ** -0.5),
        "nsa_q_gain": gain((N_B_LAYERS, hd)),
        "nsa_w_out": nrm((N_B_LAYERS, H * hd, D), (H * hd) ** -0.5),
    }


def reference(x, p, attn_norm, ffn_norm, ple_norm, ple_gate_w, ple_proj_w, router_w, router_b, w_gu, b_gu, w_down, b_down, fox_w_in, fox_b_f, fox_qk_gain, fox_w_out, kv_norm, kv_w, kv_k_gain, cmp_pos, cmp_w1, cmp_w2, nsa_w_in, nsa_q_gain, nsa_w_out):
    shared = None
    for layer in range(DEPTH):
        if layer == N_A_LAYERS:
            shared = nsa_shared_kv(x, kv_norm, kv_w, kv_k_gain, cmp_pos, cmp_w1, cmp_w2)
        h = rmsnorm(x, attn_norm[layer])
        if layer < N_A_LAYERS:
            x = x + fox_attention(h, fox_w_in[layer], fox_b_f[layer], fox_qk_gain[layer], fox_w_out[layer])
        else:
            i = layer - N_A_LAYERS
            x = x + nsa_attention(h, nsa_w_in[i], nsa_q_gain[i], nsa_w_out[i], *shared)
        x = x + moe_ffn(rmsnorm(x, ffn_norm[layer]), router_w[layer], router_b[layer], w_gu[layer], b_gu[layer], w_down[layer], b_down[layer])
        gate = jax.nn.sigmoid(rmsnorm(x, ple_norm[layer]) @ ple_gate_w[layer])
        x = x + gate * (p[layer] @ ple_proj_w[layer])
    return x
```

```python
import functools

import jax
import jax.numpy as jnp
from jax import lax
from jax.experimental import pallas as pl
from jax.experimental.pallas import tpu as pltpu

F32 = jnp.float32
BF16 = jnp.bfloat16

N_HEADS = 16
HEAD_DIM = 64
ROPE_DIM = HEAD_DIM // 4
ROPE_THETA = 500000.0
NSA_GROUPS = 4
HEADS_PER_GROUP = N_HEADS // NSA_GROUPS
CMP_BLOCK = 32
CMP_STRIDE = 16
SEL_BLOCK = 64
SEL_TOPK = 16
WINDOW = 512
N_EXPERTS = 32
TOP_K = 4
SWIGLU_LIMIT = 7.0
SWIGLU_ALPHA = 1.702
EPS = 1e-6
NEG = -1e30
FORCE = 1e6

LANES = 128
NSA_QB = 128
MOE_BM = 256
VMEM_LIMIT = 56 * 1024 * 1024


def _cparams(sem):
    return pltpu.CompilerParams(dimension_semantics=sem, vmem_limit_bytes=VMEM_LIMIT)


def _sigmoid(x):
    return 1.0 / (1.0 + jnp.exp(-x))


def _dot_nt(a, b):
    return lax.dot_general(a, b, (((1,), (1,)), ((), ())), preferred_element_type=F32)


def _split2(x):
    hi = x.astype(BF16)
    lo = (x - hi.astype(F32)).astype(BF16)
    return hi, lo


def _proj_kernel(*refs, norm_blocks, rope, head_div, tn, out_dtype):
    if norm_blocks and rope:
        x_ref, g_ref, w_ref, gain_ref, c_ref, s1_ref, s2_ref, o_ref, h_ref = refs
    elif norm_blocks:
        x_ref, g_ref, w_ref, gain_ref, o_ref, h_ref = refs
    else:
        x_ref, g_ref, w_ref, o_ref, h_ref = refs
    j = pl.program_id(1)

    @pl.when(j == 0)
    def _():
        xf = x_ref[...]
        ms = jnp.mean(xf * xf, axis=-1, keepdims=True)
        h_ref[...] = (xf * lax.rsqrt(ms + EPS) * g_ref[...]).astype(BF16)

    acc = jnp.dot(h_ref[...], w_ref[...], preferred_element_type=F32)
    if not norm_blocks:
        o_ref[...] = acc.astype(out_dtype)
        return
    is_norm = functools.reduce(jnp.logical_or, [j == c for c in norm_blocks])

    @pl.when(is_norm)
    def _():
        for c in range(tn // LANES):
            sl = slice(c * LANES, (c + 1) * LANES)
            a = acc[:, sl]
            ss = jnp.sum(a * a, axis=-1, keepdims=True) * (1.0 / head_div)
            y = a * lax.rsqrt(ss + EPS) * gain_ref[:, sl]
            if rope:
                y = (y * c_ref[...] + pltpu.roll(y, LANES - ROPE_DIM // 2, 1) * s1_ref[...]
                     + pltpu.roll(y, ROPE_DIM // 2, 1) * s2_ref[...])
            o_ref[:, sl] = y.astype(out_dtype)

    @pl.when(jnp.logical_not(is_norm))
    def _():
        o_ref[...] = acc.astype(out_dtype)


def _rms_proj(x, g, w, *, out_dtype, seq, tm=512, tn=512, gain=None, norm_blocks=(),
              rope_tabs=None, head_div=HEAD_DIM):
    n, d = x.shape
    f = w.shape[1]
    tn = min(tn, f)
    tm = min(tm, seq)
    assert n % tm == 0 and f % tn == 0 and seq % tm == 0
    nt = seq // tm
    in_specs = [
        pl.BlockSpec((tm, d), lambda i, j: (i, 0)),
        pl.BlockSpec((1, d), lambda i, j: (0, 0)),
        pl.BlockSpec((d, tn), lambda i, j: (0, j)),
    ]
    args = [x, g.reshape(1, d).astype(F32), w]
    if norm_blocks:
        in_specs.append(pl.BlockSpec((1, tn), lambda i, j: (0, j)))
        args.append(gain.reshape(1, f).astype(F32))
        if rope_tabs is not None:
            for tab in rope_tabs:
                in_specs.append(pl.BlockSpec((tm, LANES), lambda i, j: (i % nt, 0)))
                args.append(tab)
    kern = functools.partial(_proj_kernel, norm_blocks=tuple(norm_blocks), rope=rope_tabs is not None,
                             head_div=float(head_div), tn=tn, out_dtype=out_dtype)
    return pl.pallas_call(
        kern,
        out_shape=jax.ShapeDtypeStruct((n, f), out_dtype),
        grid=(n // tm, f // tn),
        in_specs=in_specs,
        out_specs=pl.BlockSpec((tm, tn), lambda i, j: (i, j)),
        scratch_shapes=[pltpu.VMEM((tm, d), BF16)],
        compiler_params=_cparams(("parallel", "arbitrary")),
        name="rms_proj",
    )(*args)


def _rope_tables(pos, width=LANES):
    half = ROPE_DIM // 2
    inv = jnp.power(jnp.float32(ROPE_THETA), -jnp.arange(0, ROPE_DIM, 2, dtype=F32) / ROPE_DIM)
    ang = pos.astype(F32)[:, None] * inv[None, :]
    cos, sin = jnp.cos(ang), jnp.sin(ang)
    t = pos.shape[0]
    ones = jnp.ones((t, HEAD_DIM - ROPE_DIM), F32)
    zeros_h = jnp.zeros((t, half), F32)
    zeros_r = jnp.zeros((t, HEAD_DIM - ROPE_DIM), F32)
    c = jnp.concatenate([cos, cos, ones], axis=1)
    s1 = jnp.concatenate([-sin, zeros_h, zeros_r], axis=1)
    s2 = jnp.concatenate([zeros_h, sin, zeros_r], axis=1)
    rep = width // HEAD_DIM
    return tuple(jnp.tile(a, (1, rep)) for a in (c, s1, s2))


def _matmul_res_kernel(a_ref, w_ref, x_ref, o_ref):
    o_ref[...] = x_ref[...] + jnp.dot(a_ref[...], w_ref[...], preferred_element_type=F32)


def _matmul_res(a, w, x, *, tm=512):
    n, k = a.shape
    f = w.shape[1]
    tm = min(tm, n)
    return pl.pallas_call(
        _matmul_res_kernel,
        out_shape=jax.ShapeDtypeStruct((n, f), F32),
        grid=(n // tm,),
        in_specs=[
            pl.BlockSpec((tm, k), lambda i: (i, 0)),
            pl.BlockSpec((k, f), lambda i: (0, 0)),
            pl.BlockSpec((tm, f), lambda i: (i, 0)),
        ],
        out_specs=pl.BlockSpec((tm, f), lambda i: (i, 0)),
        compiler_params=_cparams(("parallel",)),
        name="matmul_res",
    )(a, w, x)


def _fox_c_kernel(z_ref, b_ref, c_ref, carry_ref, *, tm):
    t = pl.program_id(1)

    @pl.when(t == 0)
    def _():
        carry_ref[...] = jnp.zeros_like(carry_ref)

    u = z_ref[...] + b_ref[...]
    logf = jnp.minimum(u, 0.0) - jnp.log(1.0 + jnp.exp(-jnp.abs(u)))
    row = lax.broadcasted_iota(jnp.int32, (tm, tm), 0)
    col = lax.broadcasted_iota(jnp.int32, (tm, tm), 1)
    tri = (row >= col).astype(BF16)
    hi = logf.astype(BF16)
    r1 = logf - hi.astype(F32)
    mid = r1.astype(BF16)
    lo = (r1 - mid.astype(F32)).astype(BF16)
    cs = (jnp.dot(tri, hi, preferred_element_type=F32) + jnp.dot(tri, mid, preferred_element_type=F32)
          + jnp.dot(tri, lo, preferred_element_type=F32))
    c = cs + carry_ref[...]
    c_ref[...] = c
    carry_ref[...] = c[tm - 1:tm, :]


def _fox_c(z, b_pad, *, batch, seq, tm=512):
    tm = min(tm, seq)
    nt = seq // tm
    return pl.pallas_call(
        functools.partial(_fox_c_kernel, tm=tm),
        out_shape=jax.ShapeDtypeStruct(z.shape, F32),
        grid=(batch, nt),
        in_specs=[
            pl.BlockSpec((tm, LANES), lambda b, t: (b * nt + t, 0)),
            pl.BlockSpec((1, LANES), lambda b, t: (0, 0)),
        ],
        out_specs=pl.BlockSpec((tm, LANES), lambda b, t: (b * nt + t, 0)),
        scratch_shapes=[pltpu.VMEM((1, LANES), F32)],
        compiler_params=_cparams(("parallel", "arbitrary")),
        name="fox_cumsum",
    )(z, b_pad)


def _fox_attn_kernel(q0_ref, q1_ref, k0_ref, k1_ref, v_ref, cc0_ref, cc1_ref, cr0_ref, cr1_ref,
                     o_ref, m_ref, l_ref, acc_ref, *, tq, tk):
    qi = pl.program_id(2)
    nk = ((qi + 1) * tq + tk - 1) // tk
    tpos = qi * tq + lax.broadcasted_iota(jnp.int32, (tq, tk), 0)
    kcol = lax.broadcasted_iota(jnp.int32, (tq, tk), 1)
    outs = []
    for q_ref, k_ref, cc_ref, cr_ref in ((q0_ref, k0_ref, cc0_ref, cr0_ref),
                                          (q1_ref, k1_ref, cc1_ref, cr1_ref)):
        q = q_ref[...]
        ct = cc_ref[0]
        m_ref[...] = jnp.full(m_ref.shape, NEG, F32)
        l_ref[...] = jnp.zeros(l_ref.shape, F32)
        acc_ref[...] = jnp.zeros(acc_ref.shape, F32)

        def body(j, carry, q=q, ct=ct, k_ref=k_ref, cr_ref=cr_ref):
            k0 = pl.multiple_of(j * tk, tk)
            k = k_ref[pl.ds(k0, tk), :]
            v = v_ref[pl.ds(k0, tk), :]
            cs = cr_ref[0, :, pl.ds(k0, tk)]
            s = _dot_nt(q, k) + (ct - cs)
            s = jnp.where(k0 + kcol <= tpos, s, NEG)
            m_prev = m_ref[...]
            m_new = jnp.maximum(m_prev, jnp.max(s, axis=-1, keepdims=True))
            p = jnp.exp(s - m_new)
            alpha = jnp.exp(m_prev - m_new)
            l_ref[...] = alpha * l_ref[...] + jnp.sum(p, axis=-1, keepdims=True)
            acc_ref[...] = alpha * acc_ref[...] + jnp.dot(p.astype(BF16), v, preferred_element_type=F32)
            m_ref[...] = m_new
            return carry

        lax.fori_loop(0, nk, body, 0)
        outs.append(acc_ref[...] / l_ref[...])
    lane = lax.broadcasted_iota(jnp.int32, (tq, LANES), 1)
    o_ref[...] = jnp.where(lane < HEAD_DIM, outs[0], outs[1]).astype(o_ref.dtype)


def _fox_attn(qkv, c_cols, c_rows, *, batch, seq, tq=256, tk=512):
    n = qkv.shape[0]
    tq = min(tq, seq)
    tk = min(tk, seq)
    nq = seq // tq
    hp = N_HEADS // 2
    kb = N_HEADS
    vb = 2 * N_HEADS
    in_specs = [
        pl.BlockSpec((tq, LANES), lambda b, h, i: (b * nq + i, 2 * h)),
        pl.BlockSpec((tq, LANES), lambda b, h, i: (b * nq + i, 2 * h + 1)),
        pl.BlockSpec((seq, LANES), lambda b, h, i: (b, kb + 2 * h)),
        pl.BlockSpec((seq, LANES), lambda b, h, i: (b, kb + 2 * h + 1)),
        pl.BlockSpec((seq, LANES), lambda b, h, i: (b, vb + h)),
        pl.BlockSpec((1, tq, 1), lambda b, h, i: (b * N_HEADS + 2 * h, i, 0)),
        pl.BlockSpec((1, tq, 1), lambda b, h, i: (b * N_HEADS + 2 * h + 1, i, 0)),
        pl.BlockSpec((1, 1, seq), lambda b, h, i: (b * N_HEADS + 2 * h, 0, 0)),
        pl.BlockSpec((1, 1, seq), lambda b, h, i: (b * N_HEADS + 2 * h + 1, 0, 0)),
    ]
    return pl.pallas_call(
        functools.partial(_fox_attn_kernel, tq=tq, tk=tk),
        out_shape=jax.ShapeDtypeStruct((n, N_HEADS * HEAD_DIM), BF16),
        grid=(batch, hp, nq),
        in_specs=in_specs,
        out_specs=pl.BlockSpec((tq, LANES), lambda b, h, i: (b * nq + i, h)),
        scratch_shapes=[pltpu.VMEM((tq, 1), F32), pltpu.VMEM((tq, 1), F32), pltpu.VMEM((tq, LANES), F32)],
        compiler_params=_cparams(("parallel", "parallel", "arbitrary")),
        name="fox_attn",
    )(qkv, qkv, qkv, qkv, qkv, c_cols, c_cols, c_rows, c_rows)


def _router_kernel(x_ref, g_ref, whi_ref, wlo_ref, b_ref, h_ref, idx_ref, gate_ref):
    xf = x_ref[...]
    ms = jnp.mean(xf * xf, axis=-1, keepdims=True)
    h = xf * lax.rsqrt(ms + EPS) * g_ref[...]
    h_ref[...] = h
    hhi, hlo = _split2(h)
    whi = whi_ref[...]
    logits = (jnp.dot(hhi, whi, preferred_element_type=F32) + jnp.dot(hlo, whi, preferred_element_type=F32)
              + jnp.dot(hhi, wlo_ref[...], preferred_element_type=F32)) + b_ref[...]
    lane = lax.broadcasted_iota(jnp.int32, logits.shape, 1).astype(F32)
    vals, idxs = [], []
    cur = logits
    for _ in range(TOP_K):
        mx = jnp.max(cur, axis=-1, keepdims=True)
        ix = jnp.min(jnp.where(cur == mx, lane, float(LANES)), axis=-1, keepdims=True)
        vals.append(mx)
        idxs.append(ix)
        cur = jnp.where(lane == ix, -jnp.inf, cur)
    es = [jnp.exp(v - vals[0]) for v in vals]
    tot = es[0] + es[1] + es[2] + es[3]
    idx_out = jnp.zeros(logits.shape, F32)
    gate_out = jnp.zeros(logits.shape, F32)
    for r in range(TOP_K):
        idx_out = jnp.where(lane == r, idxs[r], idx_out)
        gate_out = jnp.where(lane == r, es[r] / tot, gate_out)
    idx_ref[...] = idx_out.astype(jnp.int32)
    gate_ref[...] = gate_out


def _router(x, g, w, b, *, tm=512):
    n, d = x.shape
    e = w.shape[1]
    tm = min(tm, n)
    w_pad = jnp.zeros((d, LANES), F32).at[:, :e].set(w)
    whi = w_pad.astype(BF16)
    wlo = (w_pad - whi.astype(F32)).astype(BF16)
    b_pad = jnp.full((1, LANES), -jnp.inf, F32).at[0, :e].set(b.astype(F32))
    return pl.pallas_call(
        _router_kernel,
        out_shape=(jax.ShapeDtypeStruct((n, d), F32), jax.ShapeDtypeStruct((n, LANES), jnp.int32),
                   jax.ShapeDtypeStruct((n, LANES), F32)),
        grid=(n // tm,),
        in_specs=[
            pl.BlockSpec((tm, d), lambda i: (i, 0)),
            pl.BlockSpec((1, d), lambda i: (0, 0)),
            pl.BlockSpec((d, LANES), lambda i: (0, 0)),
            pl.BlockSpec((d, LANES), lambda i: (0, 0)),
            pl.BlockSpec((1, LANES), lambda i: (0, 0)),
        ],
        out_specs=(pl.BlockSpec((tm, d), lambda i: (i, 0)), pl.BlockSpec((tm, LANES), lambda i: (i, 0)),
                   pl.BlockSpec((tm, LANES), lambda i: (i, 0))),
        compiler_params=_cparams(("parallel",)),
        name="moe_router",
    )(x, g.reshape(1, d).astype(F32), whi, wlo, b_pad)


def _moe_expert_kernel(blk_exp_ref, tok_ref, nused_ref, h_hbm, g_ref, wgu_ref, bgu_ref, wd_ref, bd_ref,
                       o_ref, xbuf, sem, *, bm, dff):
    i = pl.program_id(0)
    used = i < nused_ref[0]

    @pl.when(used)
    def _():
        base = i * bm

        def issue(r, carry):
            tok = tok_ref[base + r]
            pltpu.make_async_copy(h_hbm.at[pl.ds(tok, 1)], xbuf.at[pl.ds(r, 1)], sem).start()
            return carry

        lax.fori_loop(0, bm, issue, 0)
        pltpu.make_async_copy(h_hbm.at[pl.ds(0, bm)], xbuf, sem).wait()
        x = xbuf[...].astype(BF16)
        gu = jnp.dot(x, wgu_ref[0], preferred_element_type=F32) + bgu_ref[0]
        a = jnp.minimum(gu[:, :dff], SWIGLU_LIMIT)
        u = jnp.clip(gu[:, dff:], -SWIGLU_LIMIT, SWIGLU_LIMIT)
        y = (u + 1.0) * (a * _sigmoid(SWIGLU_ALPHA * a))
        out = jnp.dot(y.astype(BF16), wd_ref[0], preferred_element_type=F32) + bd_ref[0]
        o_ref[...] = out * g_ref[...]

    @pl.when(jnp.logical_not(used))
    def _():
        o_ref[...] = jnp.zeros(o_ref.shape, o_ref.dtype)


def _moe_experts(h, blk_exp, buf_tok, n_used, buf_g, w_gu, b_gu, w_down, b_down, *, bm):
    n, d = h.shape
    e, _, f2 = w_gu.shape
    dff = f2 // 2
    cap = buf_tok.shape[0]
    n_blocks = cap // bm
    grid_spec = pltpu.PrefetchScalarGridSpec(
        num_scalar_prefetch=3,
        grid=(n_blocks,),
        in_specs=[
            pl.BlockSpec(memory_space=pl.ANY),
            pl.BlockSpec((bm, 1), lambda i, be, tk, nu: (i, 0)),
            pl.BlockSpec((1, d, f2), lambda i, be, tk, nu: (be[i], 0, 0)),
            pl.BlockSpec((1, 1, f2), lambda i, be, tk, nu: (be[i], 0, 0)),
            pl.BlockSpec((1, dff, d), lambda i, be, tk, nu: (be[i], 0, 0)),
            pl.BlockSpec((1, 1, d), lambda i, be, tk, nu: (be[i], 0, 0)),
        ],
        out_specs=pl.BlockSpec((bm, d), lambda i, be, tk, nu: (i, 0)),
        scratch_shapes=[pltpu.VMEM((bm, d), F32), pltpu.SemaphoreType.DMA],
    )
    return pl.pallas_call(
        functools.partial(_moe_expert_kernel, bm=bm, dff=dff),
        out_shape=jax.ShapeDtypeStruct((cap, d), F32),
        grid_spec=grid_spec,
        compiler_params=_cparams(("arbitrary",)),
        name="moe_experts",
    )(blk_exp, buf_tok, n_used, h, buf_g.reshape(cap, 1), w_gu, b_gu.reshape(e, 1, f2).astype(F32),
      w_down, b_down.reshape(e, 1, d).astype(F32))


def _moe_combine_kernel(pos_ref, x_ref, os_hbm, o_ref, buf, sem, *, tt):
    i = pl.program_id(0)
    base = i * tt * TOP_K

    def issue(r, carry):
        for k in range(TOP_K):
            p = pos_ref[base + r * TOP_K + k]
            pltpu.make_async_copy(os_hbm.at[pl.ds(p, 1)], buf.at[k, pl.ds(r, 1)], sem).start()
        return carry

    lax.fori_loop(0, tt, issue, 0)
    for k in range(TOP_K):
        pltpu.make_async_copy(os_hbm.at[pl.ds(0, tt)], buf.at[k], sem).wait()
    o_ref[...] = x_ref[...] + (((buf[0] + buf[1]) + buf[2]) + buf[3])


def _moe_combine(x, out_sorted, pos, *, tt=128):
    n, d = x.shape
    tt = min(tt, n)
    grid_spec = pltpu.PrefetchScalarGridSpec(
        num_scalar_prefetch=1,
        grid=(n // tt,),
        in_specs=[
            pl.BlockSpec((tt, d), lambda i, ps: (i, 0)),
            pl.BlockSpec(memory_space=pl.ANY),
        ],
        out_specs=pl.BlockSpec((tt, d), lambda i, ps: (i, 0)),
        scratch_shapes=[pltpu.VMEM((TOP_K, tt, d), F32), pltpu.SemaphoreType.DMA],
    )
    return pl.pallas_call(
        functools.partial(_moe_combine_kernel, tt=tt),
        out_shape=jax.ShapeDtypeStruct((n, d), F32),
        grid_spec=grid_spec,
        compiler_params=_cparams(("arbitrary",)),
        name="moe_combine",
    )(pos, x, out_sorted)


def _moe_layer(x, norm_g, router_w, router_b, w_gu, b_gu, w_down, b_down):
    n, d = x.shape
    bm = MOE_BM
    h, idx128, gate128 = _router(x, norm_g, router_w, router_b)
    top_idx = idx128[:, :TOP_K]
    gate = gate128[:, :TOP_K]
    n_assign = n * TOP_K
    flat_e = top_idx.reshape(-1)
    flat_tok = jnp.arange(n_assign, dtype=jnp.int32) // TOP_K
    order = jnp.argsort(flat_e)
    se = flat_e[order]
    stok = flat_tok[order]
    sg = gate.reshape(-1)[order]
    counts = jnp.bincount(flat_e, length=N_EXPERTS)
    padded = (counts + bm - 1) // bm * bm
    start = jnp.cumsum(counts) - counts
    pend = jnp.cumsum(padded)
    pstart = pend - padded
    dest = (pstart[se] + jnp.arange(n_assign) - start[se]).astype(jnp.int32)
    n_blocks = -(-n_assign // bm) + N_EXPERTS
    cap = n_blocks * bm
    buf_tok = jnp.zeros((cap,), jnp.int32).at[dest].set(stok)
    buf_g = jnp.zeros((cap,), F32).at[dest].set(sg)
    blk_exp = jnp.minimum(jnp.searchsorted(pend, jnp.arange(n_blocks) * bm, side='right'),
                          N_EXPERTS - 1).astype(jnp.int32)
    n_used = (pend[-1] // bm).astype(jnp.int32).reshape(1)
    pos = jnp.zeros((n_assign,), jnp.int32).at[order].set(dest)
    out_sorted = _moe_experts(h, blk_exp, buf_tok, n_used, buf_g, w_gu.astype(BF16), b_gu,
                              w_down.astype(BF16), b_down, bm=bm)
    return _moe_combine(x, out_sorted, pos)


def _ple_kernel(x_ref, g_ref, wg_ref, p_ref, wp_ref, o_ref):
    xf = x_ref[...]
    ms = jnp.mean(xf * xf, axis=-1, keepdims=True)
    h = (xf * lax.rsqrt(ms + EPS) * g_ref[...]).astype(BF16)
    gate = _sigmoid(jnp.dot(h, wg_ref[...], preferred_element_type=F32))
    pp = jnp.dot(p_ref[...].astype(BF16), wp_ref[...], preferred_element_type=F32)
    o_ref[...] = xf + gate * pp


def _ple(x, g, wg, p, wp, *, tm=512):
    n, d = x.shape
    pd = p.shape[1]
    tm = min(tm, n)
    return pl.pallas_call(
        _ple_kernel,
        out_shape=jax.ShapeDtypeStruct((n, d), F32),
        grid=(n // tm,),
        in_specs=[
            pl.BlockSpec((tm, d), lambda i: (i, 0)),
            pl.BlockSpec((1, d), lambda i: (0, 0)),
            pl.BlockSpec((d, d), lambda i: (0, 0)),
            pl.BlockSpec((tm, pd), lambda i: (i, 0)),
            pl.BlockSpec((pd, d), lambda i: (0, 0)),
        ],
        out_specs=pl.BlockSpec((tm, d), lambda i: (i, 0)),
        compiler_params=_cparams(("parallel",)),
        name="ple",
    )(x, g.reshape(1, d).astype(F32), wg.astype(BF16), p, wp.astype(BF16))


def _cmp_kernel(z_ref, w1_ref, pos_ref, w2_ref, gain_ref, c_ref, s1_ref, s2_ref, o_ref, *, rows, half):
    j = pl.program_id(1)
    z = z_ref[0, 0, 0]
    pos = pos_ref[0]
    a1 = (z + pos[:, :half]).astype(BF16)
    a2 = (z + pos[:, half:]).astype(BF16)
    u1 = jnp.dot(a1, w1_ref[0, :half, :], preferred_element_type=F32)
    u2 = jnp.dot(a2, w1_ref[0, half:, :], preferred_element_type=F32)
    h = u1 + pltpu.roll(u2, rows - 1, 0)
    hs = h * _sigmoid(h)
    y = jnp.dot(hs.astype(BF16), w2_ref[0], preferred_element_type=F32)

    @pl.when(j == 0)
    def _():
        ss = jnp.sum(y * y, axis=-1, keepdims=True) * (1.0 / LANES)
        yn = y * lax.rsqrt(ss + EPS) * gain_ref[...]
        yn = (yn * c_ref[...] + pltpu.roll(yn, LANES - ROPE_DIM // 2, 1) * s1_ref[...]
              + pltpu.roll(yn, ROPE_DIM // 2, 1) * s2_ref[...])
        o_ref[0, 0, 0] = yn.astype(o_ref.dtype)

    @pl.when(j != 0)
    def _():
        o_ref[0, 0, 0] = y.astype(o_ref.dtype)


def _compress(zr, w1, pos, w2dup, gain_dup, tabs, *, batch):
    rows, width = zr.shape[3], zr.shape[4]
    hid = w1.shape[2]
    g = NSA_GROUPS
    return pl.pallas_call(
        functools.partial(_cmp_kernel, rows=rows, half=width),
        out_shape=jax.ShapeDtypeStruct((batch, 2, g, rows, LANES), BF16),
        grid=(batch, 2, g),
        in_specs=[
            pl.BlockSpec((1, 1, 1, rows, width), lambda b, j, gg: (b, j, gg, 0, 0)),
            pl.BlockSpec((1, 2 * width, hid), lambda b, j, gg: (j, 0, 0)),
            pl.BlockSpec((1, 1, 2 * width), lambda b, j, gg: (j, 0, 0)),
            pl.BlockSpec((1, hid, LANES), lambda b, j, gg: (j, 0, 0)),
            pl.BlockSpec((1, LANES), lambda b, j, gg: (0, 0)),
            pl.BlockSpec((rows, LANES), lambda b, j, gg: (0, 0)),
            pl.BlockSpec((rows, LANES), lambda b, j, gg: (0, 0)),
            pl.BlockSpec((rows, LANES), lambda b, j, gg: (0, 0)),
        ],
        out_specs=pl.BlockSpec((1, 1, 1, rows, LANES), lambda b, j, gg: (b, j, gg, 0, 0)),
        compiler_params=_cparams(("parallel", "parallel", "parallel")),
        name="nsa_compress",
    )(zr, w1, pos, w2dup, gain_dup, *tabs)


def _nsa_attn_kernel(q_ref, gz_ref, kc_ref, vc_ref, ov_ref, ks_ref, vs_ref, kw_ref, vw_ref, o_ref,
                     m_ref, l_ref, acc_ref, oc_ref, os_ref, *, n_cmp_rows, tk_sel):
    g = pl.program_id(1)
    i = pl.program_id(2)
    qb = NSA_QB
    mh = HEADS_PER_GROUP
    rows = mh * qb
    q0 = i * qb
    q4 = jnp.concatenate([q_ref[:, m * LANES:(m + 1) * LANES] for m in range(mh)], axis=0)
    trow = q0 + (lax.broadcasted_iota(jnp.int32, (rows, 1), 0) & (qb - 1))

    s = _dot_nt(q4, kc_ref[0, 0, 0])
    cend = lax.broadcasted_iota(jnp.int32, (rows, n_cmp_rows), 1) * CMP_STRIDE + (CMP_BLOCK - 1)
    valid = cend <= trow
    s = jnp.where(valid, s, NEG)
    mx = jnp.max(s, axis=-1, keepdims=True)
    e = jnp.where(valid, jnp.exp(s - mx), 0.0)
    den = jnp.sum(e, axis=-1, keepdims=True)
    p = e / jnp.maximum(den, 1e-30)
    oc_ref[...] = jnp.dot(p.astype(BF16), vc_ref[0, 0, 0], preferred_element_type=F32)
    psum = (p[0:qb] + p[qb:2 * qb]) + (p[2 * qb:3 * qb] + p[3 * qb:4 * qb])
    phi, plo = _split2(psum)
    ov = ov_ref[...]
    imp = jnp.dot(phi, ov, preferred_element_type=F32) + jnp.dot(plo, ov, preferred_element_type=F32)

    jl = lax.broadcasted_iota(jnp.int32, (qb, LANES), 1)
    tq = q0 + lax.broadcasted_iota(jnp.int32, (qb, LANES), 0)
    cur = tq >> 6
    forced = (jl == 0) | (jl == cur) | (jl == cur - 1)
    causal = (jl << 6) <= tq
    score = jnp.where(causal, jnp.where(forced, FORCE, imp), -jnp.inf)
    notsel = jnp.ones((qb, LANES), F32)
    jlf = jl.astype(F32)
    for _ in range(SEL_TOPK):
        smx = jnp.max(score, axis=-1, keepdims=True)
        ix = jnp.min(jnp.where(score == smx, jlf, float(LANES)), axis=-1, keepdims=True)
        hit = jlf == ix
        notsel = jnp.where(hit, 0.0, notsel)
        score = jnp.where(hit, -jnp.inf, score)
    notsel_b = notsel.astype(BF16)
    q_aug = jnp.concatenate([q4, jnp.concatenate([notsel_b] * mh, axis=0)], axis=1)

    m_ref[...] = jnp.full(m_ref.shape, NEG, F32)
    l_ref[...] = jnp.zeros(l_ref.shape, F32)
    acc_ref[...] = jnp.zeros(acc_ref.shape, F32)
    n_sel_tiles = (q0 + qb + tk_sel - 1) // tk_sel
    kblk_l = lax.broadcasted_iota(jnp.int32, (tk_sel, LANES), 1)
    krow_l = lax.broadcasted_iota(jnp.int32, (tk_sel, LANES), 0)
    kcol = lax.broadcasted_iota(jnp.int32, (rows, tk_sel), 1)

    def sel_body(j, carry):
        k0 = pl.multiple_of(j * tk_sel, tk_sel)
        ks = ks_ref[pl.ds(k0, tk_sel), :]
        onehot = jnp.where(((k0 + krow_l) >> 6) == kblk_l, NEG, 0.0).astype(BF16)
        k_aug = jnp.concatenate([ks, onehot], axis=1)
        sc = _dot_nt(q_aug, k_aug)
        sc = jnp.where(k0 + kcol <= trow, sc, NEG)
        m_prev = m_ref[...]
        m_new = jnp.maximum(m_prev, jnp.max(sc, axis=-1, keepdims=True))
        pe = jnp.exp(sc - m_new)
        alpha = jnp.exp(m_prev - m_new)
        l_ref[...] = alpha * l_ref[...] + jnp.sum(pe, axis=-1, keepdims=True)
        acc_ref[...] = alpha * acc_ref[...] + jnp.dot(pe.astype(BF16), vs_ref[pl.ds(k0, tk_sel), :],
                                                      preferred_element_type=F32)
        m_ref[...] = m_new
        return carry

    lax.fori_loop(0, n_sel_tiles, sel_body, 0)
    os_ref[...] = acc_ref[...] / l_ref[...]

    m_ref[...] = jnp.full(m_ref.shape, NEG, F32)
    l_ref[...] = jnp.zeros(l_ref.shape, F32)
    acc_ref[...] = jnp.zeros(acc_ref.shape, F32)
    n_win_tiles = jnp.minimum(i, WINDOW // qb) + 1
    wcol = lax.broadcasted_iota(jnp.int32, (rows, qb), 1)

    def win_body(t, carry):
        k0 = pl.multiple_of((i - t) * qb, qb)
        kw = kw_ref[pl.ds(k0, qb), :]
        sc = _dot_nt(q4, kw)
        kpos = k0 + wcol
        sc = jnp.where((kpos <= trow) & (kpos > trow - WINDOW), sc, NEG)
        m_prev = m_ref[...]
        m_new = jnp.maximum(m_prev, jnp.max(sc, axis=-1, keepdims=True))
        pe = jnp.exp(sc - m_new)
        alpha = jnp.exp(m_prev - m_new)
        l_ref[...] = alpha * l_ref[...] + jnp.sum(pe, axis=-1, keepdims=True)
        acc_ref[...] = alpha * acc_ref[...] + jnp.dot(pe.astype(BF16), vw_ref[pl.ds(k0, qb), :],
                                                      preferred_element_type=F32)
        m_ref[...] = m_new
        return carry

    lax.fori_loop(0, n_win_tiles, win_body, 0)
    ow = acc_ref[...] / l_ref[...]

    sig = _sigmoid(gz_ref[...])
    heads = []
    for m in range(mh):
        r = slice(m * qb, (m + 1) * qb)
        col = g * mh + m
        gates = [jnp.sum(jnp.where(jl == br * N_HEADS + col, sig, 0.0), axis=-1, keepdims=True)
                 for br in range(3)]
        heads.append(gates[0] * oc_ref[r, :] + gates[1] * os_ref[r, :] + gates[2] * ow[r, :])
    lane_lo = jl < HEAD_DIM
    o_ref[:, 0:LANES] = jnp.where(lane_lo, heads[0], heads[1]).astype(o_ref.dtype)
    o_ref[:, LANES:2 * LANES] = jnp.where(lane_lo, heads[2], heads[3]).astype(o_ref.dtype)


def _nsa_attn(qn, gz, kvc, overlap, kvd, *, batch, seq):
    n = qn.shape[0]
    qb = NSA_QB
    nq = seq // qb
    g = NSA_GROUPS
    rows = HEADS_PER_GROUP * qb
    n_cmp_rows = kvc.shape[3]
    tk_sel = min(512, seq)
    in_specs = [
        pl.BlockSpec((qb, HEADS_PER_GROUP * LANES), lambda b, gg, i: (b * nq + i, gg)),
        pl.BlockSpec((qb, LANES), lambda b, gg, i: (b * nq + i, 0)),
        pl.BlockSpec((1, 1, 1, n_cmp_rows, LANES), lambda b, gg, i: (b, 0, gg, 0, 0)),
        pl.BlockSpec((1, 1, 1, n_cmp_rows, LANES), lambda b, gg, i: (b, 1, gg, 0, 0)),
        pl.BlockSpec((n_cmp_rows, LANES), lambda b, gg, i: (0, 0)),
        pl.BlockSpec((seq, LANES), lambda b, gg, i: (b, 0 * g + gg)),
        pl.BlockSpec((seq, LANES), lambda b, gg, i: (b, 1 * g + gg)),
        pl.BlockSpec((seq, LANES), lambda b, gg, i: (b, 2 * g + gg)),
        pl.BlockSpec((seq, LANES), lambda b, gg, i: (b, 3 * g + gg)),
    ]
    return pl.pallas_call(
        functools.partial(_nsa_attn_kernel, n_cmp_rows=n_cmp_rows, tk_sel=tk_sel),
        out_shape=jax.ShapeDtypeStruct((n, N_HEADS * HEAD_DIM), BF16),
        grid=(batch, g, nq),
        in_specs=in_specs,
        out_specs=pl.BlockSpec((qb, HEADS_PER_GROUP * HEAD_DIM), lambda b, gg, i: (b * nq + i, gg)),
        scratch_shapes=[pltpu.VMEM((rows, 1), F32), pltpu.VMEM((rows, 1), F32), pltpu.VMEM((rows, LANES), F32),
                        pltpu.VMEM((rows, LANES), F32), pltpu.VMEM((rows, LANES), F32)],
        compiler_params=_cparams(("parallel", "parallel", "arbitrary")),
        name="nsa_attn",
    )(qn, gz, kvc, kvc, overlap, kvd, kvd, kvd, kvd)


def _pad_heads(w):
    d, f = w.shape
    h = f // HEAD_DIM
    w3 = w.reshape(d, h, HEAD_DIM)
    return jnp.concatenate([w3, jnp.zeros_like(w3)], axis=-1).reshape(d, h * LANES)


def _dup_heads(w):
    d, f = w.shape
    h = f // HEAD_DIM
    w3 = w.reshape(d, h, HEAD_DIM)
    return jnp.concatenate([w3, w3], axis=-1).reshape(d, h * LANES)


def _pad_cols(w, width=LANES):
    d, f = w.shape
    return jnp.zeros((d, width), w.dtype).at[:, :f].set(w)


def _tile_gain(gain, n_heads, scale=1.0, dup=False):
    g = gain.astype(F32) * scale
    second = g if dup else jnp.zeros_like(g)
    return jnp.tile(jnp.concatenate([g, second]), n_heads)


def _fox_layer(x, batch, seq, attn_g, w_in, b_f, qk_gain, w_out):
    hd = N_HEADS * HEAD_DIM
    scale = HEAD_DIM ** -0.5
    w_all = jnp.concatenate([_pad_heads(w_in[:, :hd]), _pad_heads(w_in[:, hd:2 * hd]), w_in[:, 2 * hd:3 * hd]],
                            axis=1).astype(BF16)
    gain = jnp.concatenate([_tile_gain(qk_gain[0], N_HEADS, scale), _tile_gain(qk_gain[1], N_HEADS),
                            jnp.zeros((hd,), F32)])
    tn = 512
    n_norm = 2 * N_HEADS * LANES // tn
    qkv = _rms_proj(x, attn_g, w_all, out_dtype=BF16, seq=seq, tn=tn, gain=gain,
                    norm_blocks=tuple(range(n_norm)))
    z = _rms_proj(x, attn_g, _pad_cols(w_in[:, 3 * hd:]).astype(BF16), out_dtype=F32, seq=seq)
    b_pad = jnp.zeros((1, LANES), F32).at[0, :N_HEADS].set(b_f.astype(F32))
    c = _fox_c(z, b_pad, batch=batch, seq=seq)
    c_rows = c[:, :N_HEADS].reshape(batch, seq, N_HEADS).transpose(0, 2, 1).reshape(batch * N_HEADS, 1, seq)
    c_cols = c_rows.reshape(batch * N_HEADS, seq, 1)
    o = _fox_attn(qkv, c_cols, c_rows, batch=batch, seq=seq)
    return _matmul_res(o, w_out.astype(BF16), x)


def _nsa_shared_kv(x, batch, seq, kv_norm, kv_w, kv_k_gain, cmp_pos, cmp_w1, cmp_w2):
    g = NSA_GROUPS
    gw = g * HEAD_DIM
    tabs = _rope_tables(jnp.arange(seq))
    w_kvd = jnp.concatenate([_dup_heads(kv_w[:, j * gw:(j + 1) * gw]) for j in (2, 3, 4, 5)], axis=1).astype(BF16)
    zero = jnp.zeros((g * LANES,), F32)
    gain_kvd = jnp.concatenate([_tile_gain(kv_k_gain[1], g, dup=True), zero,
                                _tile_gain(kv_k_gain[2], g, dup=True), zero])
    kvd = _rms_proj(x, kv_norm, w_kvd, out_dtype=BF16, seq=seq, tn=g * LANES, gain=gain_kvd,
                    norm_blocks=(0, 2), rope_tabs=tabs, head_div=LANES)
    zc = _rms_proj(x, kv_norm, kv_w[:, :2 * gw].astype(BF16), out_dtype=F32, seq=seq)
    rows = seq // CMP_STRIDE
    zr = zc.reshape(batch, rows, CMP_STRIDE, 2, g, HEAD_DIM).transpose(0, 3, 4, 1, 2, 5)
    zr = zr.reshape(batch, 2, g, rows, CMP_STRIDE * HEAD_DIM)
    cmp_end = jnp.arange(rows) * CMP_STRIDE + CMP_BLOCK - 1
    ctabs = _rope_tables(cmp_end)
    w2dup = jnp.concatenate([cmp_w2, cmp_w2], axis=-1).astype(BF16)
    kvc = _compress(zr, cmp_w1.astype(BF16), cmp_pos.reshape(2, 1, CMP_BLOCK * HEAD_DIM).astype(F32), w2dup,
                    _tile_gain(kv_k_gain[0], 1, dup=True).reshape(1, LANES), ctabs, batch=batch)
    return kvc, kvd


def _nsa_layer(x, batch, seq, attn_g, w_in, q_gain, w_out, kvc, kvd):
    hd = N_HEADS * HEAD_DIM
    scale = HEAD_DIM ** -0.5
    rows = seq // CMP_STRIDE
    tabs = _rope_tables(jnp.arange(seq))
    qn =_rms_proj(x, attn_g, _pad_heads(w_in[:, :hd]).astype(BF16), out_dtype=BF16, seq=seq,
                   gain=_tile_gain(q_gain, N_HEADS, scale), norm_blocks=tuple(range(N_HEADS * LANES // 512)),
                   rope_tabs=tabs)
    gz = _rms_proj(x, attn_g, _pad_cols(w_in[:, hd:]).astype(BF16), out_dtype=F32, seq=seq)

    n_sel = seq // SEL_BLOCK
    assert n_sel <= LANES
    cmp_start = jnp.arange(rows) * CMP_STRIDE
    sel_start = jnp.arange(LANES) * SEL_BLOCK
    overlap = jnp.clip(jnp.minimum(cmp_start[:, None] + CMP_BLOCK, sel_start[None, :] + SEL_BLOCK)
                       - jnp.maximum(cmp_start[:, None], sel_start[None, :]), 0)
    overlap = jnp.where((jnp.arange(rows) < rows - 1)[:, None] & (jnp.arange(LANES) < n_sel)[None, :], overlap, 0)
    o = _nsa_attn(qn, gz, kvc, overlap.astype(BF16), kvd, batch=batch, seq=seq)
    return _matmul_res(o, w_out.astype(BF16), x)


def kernel(x, p, attn_norm, ffn_norm, ple_norm, ple_gate_w, ple_proj_w, router_w, router_b, w_gu, b_gu, w_down, b_down, fox_w_in, fox_b_f, fox_qk_gain, fox_w_out, kv_norm, kv_w, kv_k_gain, cmp_pos, cmp_w1, cmp_w2, nsa_w_in, nsa_q_gain, nsa_w_out):
    batch, seq, d = x.shape
    depth = p.shape[0]
    n_a = fox_w_in.shape[0]
    xt = x.reshape(batch * seq, d)
    shared = None
    for layer in range(depth):
        if layer == n_a:
            shared = _nsa_shared_kv(xt, batch, seq, kv_norm, kv_w, kv_k_gain, cmp_pos, cmp_w1, cmp_w2)
        if layer < n_a:
            xt = _fox_layer(xt, batch, seq, attn_norm[layer], fox_w_in[layer], fox_b_f[layer],
                            fox_qk_gain[layer], fox_w_out[layer])
        else:
            i = layer - n_a
            xt = _nsa_layer(xt, batch, seq, attn_norm[layer], nsa_w_in[i], nsa_q_gain[i], nsa_w_out[i], *shared)
        xt = _moe_layer(xt, ffn_norm[layer], router_w[layer], router_b[layer], w_gu[layer], b_gu[layer],
                        w_down[layer], b_down[layer])
        xt = _ple(xt, ple_norm[layer], ple_gate_w[layer], p[layer].reshape(batch * seq, -1), ple_proj_w[layer])
    return xt.reshape(batch, seq, d)
```

```python
import functools

import jax
import jax.numpy as jnp
from jax import lax
from jax.experimental import pallas as pl
from jax.experimental.pallas import tpu as pltpu

F32 = jnp.float32
BF16 = jnp.bfloat16

N_HEADS = 16
HEAD_DIM = 64
ROPE_DIM = HEAD_DIM // 4
ROPE_THETA = 500000.0
NSA_GROUPS = 4
HEADS_PER_GROUP = N_HEADS // NSA_GROUPS
CMP_BLOCK = 32
CMP_STRIDE = 16
SEL_BLOCK = 64
SEL_TOPK = 16
WINDOW = 512
N_EXPERTS = 32
TOP_K = 4
SWIGLU_LIMIT = 7.0
SWIGLU_ALPHA = 1.702
EPS = 1e-6
NEG = -1e30
FORCE = 1e6
LOG2E = 1.4426950408889634

LANES = 128
NSA_QB = 128
MOE_BM = 256
VMEM_LIMIT = 56 * 1024 * 1024


def _cparams(sem):
    return pltpu.CompilerParams(dimension_semantics=sem, vmem_limit_bytes=VMEM_LIMIT)


def _sigmoid(x):
    return 1.0 / (1.0 + jnp.exp(-x))


def _dot_nt(a, b):
    return lax.dot_general(a, b, (((1,), (1,)), ((), ())), preferred_element_type=F32)


def _split2(x):
    hi = x.astype(BF16)
    lo = (x - hi.astype(F32)).astype(BF16)
    return hi, lo


def _proj_kernel(*refs, norm_blocks, rope, placed, biased, head_div, tn, out_dtype):
    refs = list(refs)
    x_ref, g_ref, w_ref = refs[:3]
    del refs[:3]
    if norm_blocks:
        gain_ref = refs.pop(0)
    if rope:
        c_ref, s1_ref, s2_ref = refs[:3]
        del refs[:3]
    if placed:
        ex_ref, place_ref = refs[:2]
        del refs[:2]
    if biased:
        bias_ref = refs.pop(0)
    o_ref, h_ref = refs
    j = pl.program_id(1)
    aug = placed or biased

    @pl.when(j == 0)
    def _():
        xf = x_ref[...]
        ms = jnp.mean(xf * xf, axis=-1, keepdims=True)
        h_ref[...] = (xf * lax.rsqrt(ms + EPS) * g_ref[...]).astype(BF16)

    acc = jnp.dot(h_ref[...], w_ref[...], preferred_element_type=F32)
    if placed and biased:
        extra = jnp.dot(ex_ref[...], place_ref[...], preferred_element_type=F32) + bias_ref[...]
    elif biased:
        extra = jnp.broadcast_to(bias_ref[...], acc.shape)
    if not norm_blocks:
        o_ref[...] = ((acc + extra) if aug else acc).astype(out_dtype)
        return
    is_norm = functools.reduce(jnp.logical_or, [j == c for c in norm_blocks])

    @pl.when(is_norm)
    def _():
        for c in range(tn // LANES):
            sl = slice(c * LANES, (c + 1) * LANES)
            a = acc[:, sl]
            ss = jnp.sum(a * a, axis=-1, keepdims=True) * (1.0 / head_div)
            y = a * lax.rsqrt(ss + EPS) * gain_ref[:, sl]
            if rope:
                y = (y * c_ref[...] + pltpu.roll(y, LANES - ROPE_DIM // 2, 1) * s1_ref[...]
                     + pltpu.roll(y, ROPE_DIM // 2, 1) * s2_ref[...])
            if aug:
                y = y + extra[:, sl]
            o_ref[:, sl] = y.astype(out_dtype)

    @pl.when(jnp.logical_not(is_norm))
    def _():
        o_ref[...] = ((acc + extra) if aug else acc).astype(out_dtype)


def _rms_proj(x, g, w, *, out_dtype, seq, tm=512, tn=512, gain=None, norm_blocks=(),
              rope_tabs=None, placed=None, col_bias=None, head_div=HEAD_DIM):
    n, d = x.shape
    f = w.shape[1]
    tn = min(tn, f)
    tm = min(tm, seq)
    assert n % tm == 0 and f % tn == 0 and seq % tm == 0
    nt = seq // tm
    in_specs = [
        pl.BlockSpec((tm, d), lambda i, j: (i, 0)),
        pl.BlockSpec((1, d), lambda i, j: (0, 0)),
        pl.BlockSpec((d, tn), lambda i, j: (0, j)),
    ]
    args = [x, g.reshape(1, d).astype(F32), w]
    if norm_blocks:
        in_specs.append(pl.BlockSpec((1, tn), lambda i, j: (0, j)))
        args.append(gain.reshape(1, f).astype(F32))
    if rope_tabs is not None:
        assert norm_blocks
        for tab in rope_tabs:
            in_specs.append(pl.BlockSpec((tm, LANES), lambda i, j: (i % nt, 0)))
            args.append(tab)
    if placed is not None:
        assert col_bias is not None
        values, place = placed
        in_specs += [pl.BlockSpec((tm, LANES), lambda i, j: (i, 0)),
                     pl.BlockSpec((LANES, tn), lambda i, j: (0, j))]
        args += [values, place]
    if col_bias is not None:
        in_specs.append(pl.BlockSpec((1, tn), lambda i, j: (0, j)))
        args.append(col_bias.reshape(1, f).astype(F32))
    kern = functools.partial(_proj_kernel, norm_blocks=tuple(norm_blocks), rope=rope_tabs is not None,
                             placed=placed is not None, biased=col_bias is not None,
                             head_div=float(head_div), tn=tn, out_dtype=out_dtype)
    return pl.pallas_call(
        kern,
        out_shape=jax.ShapeDtypeStruct((n, f), out_dtype),
        grid=(n // tm, f // tn),
        in_specs=in_specs,
        out_specs=pl.BlockSpec((tm, tn), lambda i, j: (i, j)),
        scratch_shapes=[pltpu.VMEM((tm, d), BF16)],
        compiler_params=_cparams(("parallel", "arbitrary")),
        name="rms_proj",
    )(*args)


def _rope_tables(pos, width=LANES):
    half = ROPE_DIM // 2
    inv = jnp.power(jnp.float32(ROPE_THETA), -jnp.arange(0, ROPE_DIM, 2, dtype=F32) / ROPE_DIM)
    ang = pos.astype(F32)[:, None] * inv[None, :]
    cos, sin = jnp.cos(ang), jnp.sin(ang)
    t = pos.shape[0]
    ones = jnp.ones((t, HEAD_DIM - ROPE_DIM), F32)
    zeros_h = jnp.zeros((t, half), F32)
    zeros_r = jnp.zeros((t, HEAD_DIM - ROPE_DIM), F32)
    c = jnp.concatenate([cos, cos, ones], axis=1)
    s1 = jnp.concatenate([-sin, zeros_h, zeros_r], axis=1)
    s2 = jnp.concatenate([zeros_h, sin, zeros_r], axis=1)
    rep = width // HEAD_DIM
    return tuple(jnp.tile(a, (1, rep)) for a in (c, s1, s2))


def _matmul_res_kernel(a_ref, w_ref, x_ref, o_ref):
    o_ref[...] = x_ref[...] + jnp.dot(a_ref[...], w_ref[...], preferred_element_type=F32)


def _matmul_res(a, w, x, *, tm=512):
    n, k = a.shape
    f = w.shape[1]
    tm = min(tm, n)
    return pl.pallas_call(
        _matmul_res_kernel,
        out_shape=jax.ShapeDtypeStruct((n, f), F32),
        grid=(n // tm,),
        in_specs=[
            pl.BlockSpec((tm, k), lambda i: (i, 0)),
            pl.BlockSpec((k, f), lambda i: (0, 0)),
            pl.BlockSpec((tm, f), lambda i: (i, 0)),
        ],
        out_specs=pl.BlockSpec((tm, f), lambda i: (i, 0)),
        compiler_params=_cparams(("parallel",)),
        name="matmul_res",
    )(a, w, x)


def _fox_c_kernel(z_ref, b_ref, c_ref, carry_ref, *, tm):
    t = pl.program_id(1)

    @pl.when(t == 0)
    def _():
        carry_ref[...] = jnp.zeros_like(carry_ref)

    u = z_ref[...] + b_ref[...]
    logf = jnp.minimum(u, 0.0) - jnp.log(1.0 + jnp.exp(-jnp.abs(u)))
    row = lax.broadcasted_iota(jnp.int32, (tm, tm), 0)
    col = lax.broadcasted_iota(jnp.int32, (tm, tm), 1)
    tri = (row >= col).astype(BF16)
    hi = logf.astype(BF16)
    r1 = logf - hi.astype(F32)
    mid = r1.astype(BF16)
    lo = (r1 - mid.astype(F32)).astype(BF16)
    cs = (jnp.dot(tri, hi, preferred_element_type=F32) + jnp.dot(tri, mid, preferred_element_type=F32)
          + jnp.dot(tri, lo, preferred_element_type=F32))
    c = cs + carry_ref[...]
    carry_ref[...] = c[tm - 1:tm, :]
    c2 = c * LOG2E
    p0 = c2.astype(BF16).astype(F32)
    p1 = (c2 - p0).astype(BF16).astype(F32)
    p2 = ((c2 - p0) - p1).astype(BF16).astype(F32)
    lane = lax.broadcasted_iota(jnp.int32, c.shape, 1)
    out = jnp.where(lane < N_HEADS, p0,
                    jnp.where(lane < 2 * N_HEADS, pltpu.roll(p1, N_HEADS, 1),
                              jnp.where(lane < 3 * N_HEADS, pltpu.roll(p2, 2 * N_HEADS, 1), 0.0)))
    c_ref[...] = out.astype(BF16)


def _fox_c(z, b_pad, *, batch, seq, tm=512):
    tm = min(tm, seq)
    nt = seq // tm
    return pl.pallas_call(
        functools.partial(_fox_c_kernel, tm=tm),
        out_shape=jax.ShapeDtypeStruct(z.shape, BF16),
        grid=(batch, nt),
        in_specs=[
            pl.BlockSpec((tm, LANES), lambda b, t: (b * nt + t, 0)),
            pl.BlockSpec((1, LANES), lambda b, t: (0, 0)),
        ],
        out_specs=pl.BlockSpec((tm, LANES), lambda b, t: (b * nt + t, 0)),
        scratch_shapes=[pltpu.VMEM((1, LANES), F32)],
        compiler_params=_cparams(("parallel", "arbitrary")),
        name="fox_cumsum",
    )(z, b_pad)


def _fox_attn_kernel(q0_ref, q1_ref, k0_ref, k1_ref, v0_ref, v1_ref, o_ref, *, tq, tk, rc, skew):
    qi = pl.program_id(2)
    heads = ((q0_ref, k0_ref, v0_ref), (q1_ref, k1_ref, v1_ref))
    chunks = tuple(range(0, tq, rc))
    state = tuple(tuple((jnp.full((rc, 1), NEG, F32), jnp.zeros((rc, LANES), F32)) for _ in chunks)
                  for _ in heads)

    def step(k0, masked, state):
        kts = [k_ref[pl.ds(k0, tk), :] for (_, k_ref, _) in heads]
        vts = [v_ref[pl.ds(k0, tk), :] for (_, _, v_ref) in heads]
        items = [(h, c) for c in range(len(chunks)) for h in range(len(heads))]
        new_state = [[None] * len(chunks) for _ in heads]
        scores = {}
        for idx in range(len(items) + skew):
            if idx < len(items):
                h, c = items[idx]
                r0 = chunks[c]
                s = _dot_nt(heads[h][0][r0:r0 + rc, :], kts[h])
                if masked:
                    tpos = qi * tq + r0 + lax.broadcasted_iota(jnp.int32, (rc, tk), 0)
                    kpos = k0 + lax.broadcasted_iota(jnp.int32, (rc, tk), 1)
                    s = jnp.where(kpos <= tpos, s, NEG)
                scores[idx] = s
            if idx >= skew:
                h, c = items[idx - skew]
                s = scores.pop(idx - skew)
                m_prev, acc = state[h][c]
                m_new = jnp.maximum(m_prev, jnp.max(s, axis=-1, keepdims=True))
                p = jnp.exp2(s - m_new)
                acc = jnp.exp2(m_prev - m_new) * acc + jnp.dot(p.astype(BF16), vts[h],
                                                               preferred_element_type=F32)
                new_state[h][c] = (m_new, acc)
        return tuple(tuple(x) for x in new_state)

    def full_tile(j, state):
        return step(pl.multiple_of(j * tk, tk), False, state)

    state = lax.fori_loop(0, (qi * tq) // tk, full_tile, state)
    for d in range(tq // tk):
        state = step(pl.multiple_of(qi * tq + d * tk, tk), True, state)

    lane = lax.broadcasted_iota(jnp.int32, (rc, LANES), 1)
    for c, r0 in enumerate(chunks):
        outs = []
        for h in range(2):
            acc = state[h][c][1]
            den = jnp.sum(jnp.where(lane == HEAD_DIM, acc, 0.0), axis=-1, keepdims=True)
            outs.append(acc / den)
        o_ref[r0:r0 + rc, :] = jnp.where(lane < HEAD_DIM, outs[0],
                                         pltpu.roll(outs[1], HEAD_DIM, 1)).astype(o_ref.dtype)


def _fox_attn(qkv, *, batch, seq, tq=512, tk=512, rc=128, skew=2):
    n = qkv.shape[0]
    tq = min(tq, seq)
    tk = min(tk, tq)
    assert seq % tq == 0 and tq % tk == 0
    nq = seq // tq
    hp = N_HEADS // 2
    kb = N_HEADS
    vb = 2 * N_HEADS
    in_specs = [
        pl.BlockSpec((tq, LANES), lambda b, h, i: (b * nq + i, 2 * h)),
        pl.BlockSpec((tq, LANES), lambda b, h, i: (b * nq + i, 2 * h + 1)),
        pl.BlockSpec((seq, LANES), lambda b, h, i: (b, kb + 2 * h)),
        pl.BlockSpec((seq, LANES), lambda b, h, i: (b, kb + 2 * h + 1)),
        pl.BlockSpec((seq, LANES), lambda b, h, i: (b, vb + 2 * h)),
        pl.BlockSpec((seq, LANES), lambda b, h, i: (b, vb + 2 * h + 1)),
    ]
    return pl.pallas_call(
        functools.partial(_fox_attn_kernel, tq=tq, tk=tk, rc=min(rc, tq), skew=skew),
        out_shape=jax.ShapeDtypeStruct((n, N_HEADS * HEAD_DIM), BF16),
        grid=(batch, hp, nq),
        in_specs=in_specs,
        out_specs=pl.BlockSpec((tq, LANES), lambda b, h, i: (b * nq + i, h)),
        compiler_params=_cparams(("parallel", "parallel", "arbitrary")),
        name="fox_attn",
    )(qkv, qkv, qkv, qkv, qkv, qkv)


def _router_kernel(x_ref, g_ref, whi_ref, wlo_ref, b_ref, h_ref, idx_ref, gate_ref):
    xf = x_ref[...]
    ms = jnp.mean(xf * xf, axis=-1, keepdims=True)
    h = xf * lax.rsqrt(ms + EPS) * g_ref[...]
    h_ref[...] = h
    hhi, hlo = _split2(h)
    whi = whi_ref[...]
    logits = (jnp.dot(hhi, whi, preferred_element_type=F32) + jnp.dot(hlo, whi, preferred_element_type=F32)
              + jnp.dot(hhi, wlo_ref[...], preferred_element_type=F32)) + b_ref[...]
    lane = lax.broadcasted_iota(jnp.int32, logits.shape, 1).astype(F32)
    vals, idxs = [], []
    cur = logits
    for _ in range(TOP_K):
        mx = jnp.max(cur, axis=-1, keepdims=True)
        ix = jnp.min(jnp.where(cur == mx, lane, float(LANES)), axis=-1, keepdims=True)
        vals.append(mx)
        idxs.append(ix)
        cur = jnp.where(lane == ix, -jnp.inf, cur)
    es = [jnp.exp(v - vals[0]) for v in vals]
    tot = es[0] + es[1] + es[2] + es[3]
    idx_out = jnp.zeros(logits.shape, F32)
    gate_out = jnp.zeros(logits.shape, F32)
    for r in range(TOP_K):
        idx_out = jnp.where(lane == r, idxs[r], idx_out)
        gate_out = jnp.where(lane == r, es[r] / tot, gate_out)
    idx_ref[...] = idx_out.astype(jnp.int32)
    gate_ref[...] = gate_out


def _router(x, g, w, b, *, tm=512):
    n, d = x.shape
    e = w.shape[1]
    tm = min(tm, n)
    w_pad = jnp.zeros((d, LANES), F32).at[:, :e].set(w)
    whi = w_pad.astype(BF16)
    wlo = (w_pad - whi.astype(F32)).astype(BF16)
    b_pad = jnp.full((1, LANES), -jnp.inf, F32).at[0, :e].set(b.astype(F32))
    return pl.pallas_call(
        _router_kernel,
        out_shape=(jax.ShapeDtypeStruct((n, d), F32), jax.ShapeDtypeStruct((n, LANES), jnp.int32),
                   jax.ShapeDtypeStruct((n, LANES), F32)),
        grid=(n // tm,),
        in_specs=[
            pl.BlockSpec((tm, d), lambda i: (i, 0)),
            pl.BlockSpec((1, d), lambda i: (0, 0)),
            pl.BlockSpec((d, LANES), lambda i: (0, 0)),
            pl.BlockSpec((d, LANES), lambda i: (0, 0)),
            pl.BlockSpec((1, LANES), lambda i: (0, 0)),
        ],
        out_specs=(pl.BlockSpec((tm, d), lambda i: (i, 0)), pl.BlockSpec((tm, LANES), lambda i: (i, 0)),
                   pl.BlockSpec((tm, LANES), lambda i: (i, 0))),
        compiler_params=_cparams(("parallel",)),
        name="moe_router",
    )(x, g.reshape(1, d).astype(F32), whi, wlo, b_pad)


def _moe_expert_kernel(blk_exp_ref, tok_ref, nused_ref, h_hbm, g_ref, wgu_ref, bgu_ref, wd_ref, bd_ref,
                       o_ref, xbuf, sem, *, bm, dff):
    i = pl.program_id(0)
    used = i < nused_ref[0]

    @pl.when(used)
    def _():
        base = i * bm

        def issue(r, carry):
            tok = tok_ref[base + r]
            pltpu.make_async_copy(h_hbm.at[pl.ds(tok, 1)], xbuf.at[pl.ds(r, 1)], sem).start()
            return carry

        lax.fori_loop(0, bm, issue, 0)
        pltpu.make_async_copy(h_hbm.at[pl.ds(0, bm)], xbuf, sem).wait()
        x = xbuf[...].astype(BF16)
        gu = jnp.dot(x, wgu_ref[0], preferred_element_type=F32) + bgu_ref[0]
        a = jnp.minimum(gu[:, :dff], SWIGLU_LIMIT)
        u = jnp.clip(gu[:, dff:], -SWIGLU_LIMIT, SWIGLU_LIMIT)
        y = (u + 1.0) * (a * _sigmoid(SWIGLU_ALPHA * a))
        out = jnp.dot(y.astype(BF16), wd_ref[0], preferred_element_type=F32) + bd_ref[0]
        o_ref[...] = out * g_ref[...]

    @pl.when(jnp.logical_not(used))
    def _():
        o_ref[...] = jnp.zeros(o_ref.shape, o_ref.dtype)


def _moe_experts(h, blk_exp, buf_tok, n_used, buf_g, w_gu, b_gu, w_down, b_down, *, bm):
    n, d = h.shape
    e, _, f2 = w_gu.shape
    dff = f2 // 2
    cap = buf_tok.shape[0]
    n_blocks = cap // bm
    grid_spec = pltpu.PrefetchScalarGridSpec(
        num_scalar_prefetch=3,
        grid=(n_blocks,),
        in_specs=[
            pl.BlockSpec(memory_space=pl.ANY),
            pl.BlockSpec((bm, 1), lambda i, be, tk, nu: (i, 0)),
            pl.BlockSpec((1, d, f2), lambda i, be, tk, nu: (be[i], 0, 0)),
            pl.BlockSpec((1, 1, f2), lambda i, be, tk, nu: (be[i], 0, 0)),
            pl.BlockSpec((1, dff, d), lambda i, be, tk, nu: (be[i], 0, 0)),
            pl.BlockSpec((1, 1, d), lambda i, be, tk, nu: (be[i], 0, 0)),
        ],
        out_specs=pl.BlockSpec((bm, d), lambda i, be, tk, nu: (i, 0)),
        scratch_shapes=[pltpu.VMEM((bm, d), F32), pltpu.SemaphoreType.DMA],
    )
    return pl.pallas_call(
        functools.partial(_moe_expert_kernel, bm=bm, dff=dff),
        out_shape=jax.ShapeDtypeStruct((cap, d), F32),
        grid_spec=grid_spec,
        compiler_params=_cparams(("arbitrary",)),
        name="moe_experts",
    )(blk_exp, buf_tok, n_used, h, buf_g.reshape(cap, 1), w_gu, b_gu.reshape(e, 1, f2).astype(F32),
      w_down, b_down.reshape(e, 1, d).astype(F32))


def _moe_combine_kernel(pos_ref, x_ref, os_hbm, o_ref, buf, sem, *, tt):
    i = pl.program_id(0)
    base = i * tt * TOP_K

    def issue(r, carry):
        for k in range(TOP_K):
            p = pos_ref[base + r * TOP_K + k]
            pltpu.make_async_copy(os_hbm.at[pl.ds(p, 1)], buf.at[k, pl.ds(r, 1)], sem).start()
        return carry

    lax.fori_loop(0, tt, issue, 0)
    for k in range(TOP_K):
        pltpu.make_async_copy(os_hbm.at[pl.ds(0, tt)], buf.at[k], sem).wait()
    o_ref[...] = x_ref[...] + (((buf[0] + buf[1]) + buf[2]) + buf[3])


def _moe_combine(x, out_sorted, pos, *, tt=128):
    n, d = x.shape
    tt = min(tt, n)
    grid_spec = pltpu.PrefetchScalarGridSpec(
        num_scalar_prefetch=1,
        grid=(n // tt,),
        in_specs=[
            pl.BlockSpec((tt, d), lambda i, ps: (i, 0)),
            pl.BlockSpec(memory_space=pl.ANY),
        ],
        out_specs=pl.BlockSpec((tt, d), lambda i, ps: (i, 0)),
        scratch_shapes=[pltpu.VMEM((TOP_K, tt, d), F32), pltpu.SemaphoreType.DMA],
    )
    return pl.pallas_call(
        functools.partial(_moe_combine_kernel, tt=tt),
        out_shape=jax.ShapeDtypeStruct((n, d), F32),
        grid_spec=grid_spec,
        compiler_params=_cparams(("arbitrary",)),
        name="moe_combine",
    )(pos, x, out_sorted)


def _moe_layer(x, norm_g, router_w, router_b, w_gu, b_gu, w_down, b_down):
    n, d = x.shape
    bm = MOE_BM
    h, idx128, gate128 = _router(x, norm_g, router_w, router_b)
    top_idx = idx128[:, :TOP_K]
    gate = gate128[:, :TOP_K]
    n_assign = n * TOP_K
    flat_e = top_idx.reshape(-1)
    flat_tok = jnp.arange(n_assign, dtype=jnp.int32) // TOP_K
    order = jnp.argsort(flat_e)
    se = flat_e[order]
    stok = flat_tok[order]
    sg = gate.reshape(-1)[order]
    counts = jnp.bincount(flat_e, length=N_EXPERTS)
    padded = (counts + bm - 1) // bm * bm
    start = jnp.cumsum(counts) - counts
    pend = jnp.cumsum(padded)
    pstart = pend - padded
    dest = (pstart[se] + jnp.arange(n_assign) - start[se]).astype(jnp.int32)
    n_blocks = -(-n_assign // bm) + N_EXPERTS
    cap = n_blocks * bm
    buf_tok = jnp.zeros((cap,), jnp.int32).at[dest].set(stok)
    buf_g = jnp.zeros((cap,), F32).at[dest].set(sg)
    blk_exp = jnp.minimum(jnp.searchsorted(pend, jnp.arange(n_blocks) * bm, side='right'),
                          N_EXPERTS - 1).astype(jnp.int32)
    n_used = (pend[-1] // bm).astype(jnp.int32).reshape(1)
    pos = jnp.zeros((n_assign,), jnp.int32).at[order].set(dest)
    out_sorted = _moe_experts(h, blk_exp, buf_tok, n_used, buf_g, w_gu.astype(BF16), b_gu,
                              w_down.astype(BF16), b_down, bm=bm)
    return _moe_combine(x, out_sorted, pos)


def _ple_kernel(x_ref, g_ref, wg_ref, p_ref, wp_ref, o_ref):
    xf = x_ref[...]
    ms = jnp.mean(xf * xf, axis=-1, keepdims=True)
    h = (xf * lax.rsqrt(ms + EPS) * g_ref[...]).astype(BF16)
    gate = _sigmoid(jnp.dot(h, wg_ref[...], preferred_element_type=F32))
    pp = jnp.dot(p_ref[...].astype(BF16), wp_ref[...], preferred_element_type=F32)
    o_ref[...] = xf + gate * pp


def _ple(x, g, wg, p, wp, *, tm=512):
    n, d = x.shape
    pd = p.shape[1]
    tm = min(tm, n)
    return pl.pallas_call(
        _ple_kernel,
        out_shape=jax.ShapeDtypeStruct((n, d), F32),
        grid=(n // tm,),
        in_specs=[
            pl.BlockSpec((tm, d), lambda i: (i, 0)),
            pl.BlockSpec((1, d), lambda i: (0, 0)),
            pl.BlockSpec((d, d), lambda i: (0, 0)),
            pl.BlockSpec((tm, pd), lambda i: (i, 0)),
            pl.BlockSpec((pd, d), lambda i: (0, 0)),
        ],
        out_specs=pl.BlockSpec((tm, d), lambda i: (i, 0)),
        compiler_params=_cparams(("parallel",)),
        name="ple",
    )(x, g.reshape(1, d).astype(F32), wg.astype(BF16), p, wp.astype(BF16))


def _cmp_kernel(z_ref, w1_ref, pos_ref, w2_ref, gain_ref, c_ref, s1_ref, s2_ref, o_ref, *, rows, half):
    j = pl.program_id(1)
    z = z_ref[0, 0, 0]
    pos = pos_ref[0]
    a1 = (z + pos[:, :half]).astype(BF16)
    a2 = (z + pos[:, half:]).astype(BF16)
    u1 = jnp.dot(a1, w1_ref[0, :half, :], preferred_element_type=F32)
    u2 = jnp.dot(a2, w1_ref[0, half:, :], preferred_element_type=F32)
    h = u1 + pltpu.roll(u2, rows - 1, 0)
    hs = h * _sigmoid(h)
    y = jnp.dot(hs.astype(BF16), w2_ref[0], preferred_element_type=F32)

    @pl.when(j == 0)
    def _():
        ss = jnp.sum(y * y, axis=-1, keepdims=True) * (1.0 / LANES)
        yn = y * lax.rsqrt(ss + EPS) * gain_ref[...]
        yn = (yn * c_ref[...] + pltpu.roll(yn, LANES - ROPE_DIM // 2, 1) * s1_ref[...]
              + pltpu.roll(yn, ROPE_DIM // 2, 1) * s2_ref[...])
        o_ref[0, 0, 0] = yn.astype(o_ref.dtype)

    @pl.when(j != 0)
    def _():
        o_ref[0, 0, 0] = y.astype(o_ref.dtype)


def _compress(zr, w1, pos, w2dup, gain_dup, tabs, *, batch):
    rows, width = zr.shape[3], zr.shape[4]
    hid = w1.shape[2]
    g = NSA_GROUPS
    return pl.pallas_call(
        functools.partial(_cmp_kernel, rows=rows, half=width),
        out_shape=jax.ShapeDtypeStruct((batch, 2, g, rows, LANES), BF16),
        grid=(batch, 2, g),
        in_specs=[
            pl.BlockSpec((1, 1, 1, rows, width), lambda b, j, gg: (b, j, gg, 0, 0)),
            pl.BlockSpec((1, 2 * width, hid), lambda b, j, gg: (j, 0, 0)),
            pl.BlockSpec((1, 1, 2 * width), lambda b, j, gg: (j, 0, 0)),
            pl.BlockSpec((1, hid, LANES), lambda b, j, gg: (j, 0, 0)),
            pl.BlockSpec((1, LANES), lambda b, j, gg: (0, 0)),
            pl.BlockSpec((rows, LANES), lambda b, j, gg: (0, 0)),
            pl.BlockSpec((rows, LANES), lambda b, j, gg: (0, 0)),
            pl.BlockSpec((rows, LANES), lambda b, j, gg: (0, 0)),
        ],
        out_specs=pl.BlockSpec((1, 1, 1, rows, LANES), lambda b, j, gg: (b, j, gg, 0, 0)),
        compiler_params=_cparams(("parallel", "parallel", "parallel")),
        name="nsa_compress",
    )(zr, w1, pos, w2dup, gain_dup, *tabs)


def _pipelined(n_items, first, second, skew):
    pending, out = {}, [None] * n_items
    for idx in range(n_items + skew):
        if idx < n_items:
            pending[idx] = first(idx)
        if idx >= skew:
            out[idx - skew] = second(idx - skew, pending.pop(idx - skew))
    return out


def _nsa_attn_kernel(q_ref, gz_ref, kc_ref, vct_ref, ovt_ref, ks_ref, vs_ref, kw_ref, vw_ref, o_ref,
                     e_ref, oc_ref, *, n_cmp_rows, tk_sel, skew):
    g = pl.program_id(1)
    i = pl.program_id(2)
    qb = NSA_QB
    mh = HEADS_PER_GROUP
    rows = mh * qb
    q0 = i * qb
    seq = ks_ref.shape[0]

    @pl.when(i == 0)
    def _():
        r = lax.broadcasted_iota(jnp.int32, (seq, LANES), 0)
        lb = lax.broadcasted_iota(jnp.int32, (seq, LANES), 1)
        e_ref[...] = jnp.where((r >> 6) == lb, NEG, 0.0).astype(BF16)

    qh = [q_ref[:, m * LANES:(m + 1) * LANES] for m in range(mh)]
    q4 = jnp.concatenate(qh, axis=0)
    jl = lax.broadcasted_iota(jnp.int32, (qb, LANES), 1)

    st = _dot_nt(kc_ref[0, 0, 0], q4)
    cend = lax.broadcasted_iota(jnp.int32, (n_cmp_rows, rows), 0) * CMP_STRIDE + (CMP_BLOCK - 1)
    tlane = q0 + (lax.broadcasted_iota(jnp.int32, (1, rows), 1) & (qb - 1))
    valid = cend <= tlane
    st = jnp.where(valid, st, NEG)
    e = jnp.where(valid, jnp.exp2(st - jnp.max(st, axis=0, keepdims=True)), 0.0)
    pt = e / jnp.maximum(jnp.sum(e, axis=0, keepdims=True), 1e-30)
    oct = jnp.dot(vct_ref[0, 0], pt.astype(BF16), preferred_element_type=F32)
    for m in range(mh):
        oc_ref[m * qb:(m + 1) * qb, :] = oct[:, m * qb:(m + 1) * qb].T
    psum = (pt[:, 0:qb] + pt[:, qb:2 * qb]) + (pt[:, 2 * qb:3 * qb] + pt[:, 3 * qb:4 * qb])
    phi, plo = _split2(psum)
    ovt = ovt_ref[...]
    imp = jnp.dot(ovt, phi, preferred_element_type=F32) + jnp.dot(ovt, plo, preferred_element_type=F32)

    jb = lax.broadcasted_iota(jnp.int32, (LANES, qb), 0)
    tq = q0 + lax.broadcasted_iota(jnp.int32, (LANES, qb), 1)
    cur = tq >> 6
    forced = (jb == 0) | (jb == cur) | (jb == cur - 1)
    causal = (jb << 6) <= tq
    score = jnp.where(causal, jnp.where(forced, FORCE, imp), -jnp.inf)
    notsel_t = jnp.ones((LANES, qb), F32)
    jbf = jb.astype(F32)
    for _ in range(SEL_TOPK):
        smx = jnp.max(score, axis=0, keepdims=True)
        ix = jnp.min(jnp.where(score == smx, jbf, float(LANES)), axis=0, keepdims=True)
        hit = jbf == ix
        notsel_t = jnp.where(hit, 0.0, notsel_t)
        score = jnp.where(hit, -jnp.inf, score)
    notsel_b = notsel_t.T.astype(BF16)
    q_aug = [jnp.concatenate([qh[m], notsel_b], axis=1) for m in range(mh)]
    tpos = q0 + lax.broadcasted_iota(jnp.int32, (qb, 1), 0)

    def sel_step(k0, masked, state):
        k_aug = jnp.concatenate([ks_ref[pl.ds(k0, tk_sel), :], e_ref[pl.ds(k0, tk_sel), :]], axis=1)
        vt = vs_ref[pl.ds(k0, tk_sel), :]

        def scores(m):
            sc = _dot_nt(q_aug[m], k_aug)
            if masked:
                sc = jnp.where(k0 + lax.broadcasted_iota(jnp.int32, (qb, tk_sel), 1) <= tpos, sc, NEG)
            return sc

        def update(m, sc):
            m_prev, acc = state[m]
            m_new = jnp.maximum(m_prev, jnp.max(sc, axis=-1, keepdims=True))
            pe = jnp.exp2(sc - m_new)
            acc = jnp.exp2(m_prev - m_new) * acc + jnp.dot(pe.astype(BF16), vt, preferred_element_type=F32)
            return (m_new, acc)

        return tuple(_pipelined(mh, scores, update, skew))

    state = tuple((jnp.full((qb, 1), NEG, F32), jnp.zeros((qb, LANES), F32)) for _ in range(mh))
    n_full = q0 // tk_sel
    state = lax.fori_loop(0, n_full, lambda j, st: sel_step(pl.multiple_of(j * tk_sel, tk_sel), False, st), state)
    state = sel_step(pl.multiple_of(n_full * tk_sel, tk_sel), True, state)

    wk = WINDOW + qb
    k0w = pl.multiple_of(jnp.maximum(i - WINDOW // qb, 0) * qb, qb)
    kwt = kw_ref[pl.ds(k0w, wk), :]
    vwt = vw_ref[pl.ds(k0w, wk), :]
    kposw = k0w + lax.broadcasted_iota(jnp.int32, (qb, wk), 1)
    wmask = (kposw <= tpos) & (kposw > tpos - WINDOW)

    def win_scores(m):
        return jnp.where(wmask, _dot_nt(qh[m], kwt), NEG)

    def win_out(m, sc):
        pe = jnp.exp2(sc - jnp.max(sc, axis=-1, keepdims=True))
        return jnp.dot(pe.astype(BF16), vwt, preferred_element_type=F32)

    win_acc = _pipelined(mh, win_scores, win_out, skew)

    sig = _sigmoid(gz_ref[...])
    den_lane = jl == HEAD_DIM
    heads = []
    for m in range(mh):
        col = g * mh + m
        gates = [jnp.sum(jnp.where(jl == br * N_HEADS + col, sig, 0.0), axis=-1, keepdims=True)
                 for br in range(3)]
        acc_s = state[m][1]
        acc_w = win_acc[m]
        den_s = jnp.sum(jnp.where(den_lane, acc_s, 0.0), axis=-1, keepdims=True)
        den_w = jnp.sum(jnp.where(den_lane, acc_w, 0.0), axis=-1, keepdims=True)
        heads.append(gates[0] * oc_ref[m * qb:(m + 1) * qb, :] + (gates[1] / den_s) * acc_s
                     + (gates[2] / den_w) * acc_w)
    lane_lo = jl < HEAD_DIM
    o_ref[:, 0:LANES] = jnp.where(lane_lo, heads[0], pltpu.roll(heads[1], HEAD_DIM, 1)).astype(o_ref.dtype)
    o_ref[:, LANES:2 * LANES] = jnp.where(lane_lo, heads[2], pltpu.roll(heads[3], HEAD_DIM, 1)).astype(o_ref.dtype)


def _nsa_attn(qn, gz, kvc, overlap, kvd, *, batch, seq, skew=2):
    n = qn.shape[0]
    qb = NSA_QB
    nq = seq // qb
    g = NSA_GROUPS
    rows = HEADS_PER_GROUP * qb
    n_cmp_rows = kvc.shape[3]
    tk_sel = min(512, seq)
    assert seq >= WINDOW + qb and seq % tk_sel == 0 and tk_sel % qb == 0
    in_specs = [
        pl.BlockSpec((qb, HEADS_PER_GROUP * LANES), lambda b, gg, i: (b * nq + i, gg)),
        pl.BlockSpec((qb, LANES), lambda b, gg, i: (b * nq + i, 0)),
        pl.BlockSpec((1, 1, 1, n_cmp_rows, LANES), lambda b, gg, i: (b, 0, gg, 0, 0)),
        pl.BlockSpec((1, 1, LANES, n_cmp_rows), lambda b, gg, i: (b, gg, 0, 0)),
        pl.BlockSpec((LANES, n_cmp_rows), lambda b, gg, i: (0, 0)),
        pl.BlockSpec((seq, LANES), lambda b, gg, i: (b, 0 * g + gg)),
        pl.BlockSpec((seq, LANES), lambda b, gg, i: (b, 1 * g + gg)),
        pl.BlockSpec((seq, LANES), lambda b, gg, i: (b, 2 * g + gg)),
        pl.BlockSpec((seq, LANES), lambda b, gg, i: (b, 3 * g + gg)),
    ]
    return pl.pallas_call(
        functools.partial(_nsa_attn_kernel, n_cmp_rows=n_cmp_rows, tk_sel=tk_sel, skew=skew),
        out_shape=jax.ShapeDtypeStruct((n, N_HEADS * HEAD_DIM), BF16),
        grid=(batch, g, nq),
        in_specs=in_specs,
        out_specs=pl.BlockSpec((qb, HEADS_PER_GROUP * HEAD_DIM), lambda b, gg, i: (b * nq + i, gg)),
        scratch_shapes=[pltpu.VMEM((seq, LANES), BF16), pltpu.VMEM((rows, LANES), F32)],
        compiler_params=_cparams(("parallel", "parallel", "arbitrary")),
        name="nsa_attn",
    )(qn, gz, kvc, jnp.swapaxes(kvc[:, 1], -1, -2), overlap.T, kvd, kvd, kvd, kvd)


def _pad_heads(w):
    d, f = w.shape
    h = f // HEAD_DIM
    w3 = w.reshape(d, h, HEAD_DIM)
    return jnp.concatenate([w3, jnp.zeros_like(w3)], axis=-1).reshape(d, h * LANES)


def _dup_heads(w):
    d, f = w.shape
    h = f // HEAD_DIM
    w3 = w.reshape(d, h, HEAD_DIM)
    return jnp.concatenate([w3, w3], axis=-1).reshape(d, h * LANES)


def _pad_cols(w, width=LANES):
    d, f = w.shape
    return jnp.zeros((d, width), w.dtype).at[:, :f].set(w)


def _tile_gain(gain, n_heads, scale=1.0, dup=False):
    g = gain.astype(F32) * scale
    second = g if dup else jnp.zeros_like(g)
    return jnp.tile(jnp.concatenate([g, second]), n_heads)


def _fox_aug_tables():
    hw = N_HEADS * LANES
    piece = jnp.arange(3)[:, None]
    head = jnp.arange(N_HEADS)[None, :]
    src = (piece * N_HEADS + head).reshape(-1)
    q_dst = (head * LANES + HEAD_DIM + piece).reshape(-1)
    k_dst = (hw + head * LANES + HEAD_DIM + 3 + piece).reshape(-1)
    place = jnp.zeros((LANES, 3 * hw), F32).at[src, q_dst].set(1.0).at[src, k_dst].set(-1.0)
    lane = jnp.arange(LANES)
    q_bias = ((lane >= HEAD_DIM + 3) & (lane < HEAD_DIM + 6)).astype(F32)
    k_bias = ((lane >= HEAD_DIM) & (lane < HEAD_DIM + 3)).astype(F32)
    v_bias = (lane == HEAD_DIM).astype(F32)
    bias = jnp.concatenate([jnp.tile(q_bias, N_HEADS), jnp.tile(k_bias, N_HEADS), jnp.tile(v_bias, N_HEADS)])
    return place.astype(BF16), bias


def _fox_layer(x, batch, seq, attn_g, w_in, b_f, qk_gain, w_out):
    hd = N_HEADS * HEAD_DIM
    hw = N_HEADS * LANES
    scale = HEAD_DIM ** -0.5
    z = _rms_proj(x, attn_g, _pad_cols(w_in[:, 3 * hd:]).astype(BF16), out_dtype=F32, seq=seq)
    b_pad = jnp.zeros((1, LANES), F32).at[0, :N_HEADS].set(b_f.astype(F32))
    c3 = _fox_c(z, b_pad, batch=batch, seq=seq)
    w_all = jnp.concatenate([_pad_heads(w_in[:, :hd]), _pad_heads(w_in[:, hd:2 * hd]),
                             _pad_heads(w_in[:, 2 * hd:3 * hd])], axis=1).astype(BF16)
    gain = jnp.concatenate([_tile_gain(qk_gain[0], N_HEADS, scale * LOG2E), _tile_gain(qk_gain[1], N_HEADS),
                            jnp.zeros((hw,), F32)])
    tn = 512
    place, bias = _fox_aug_tables()
    qkv = _rms_proj(x, attn_g, w_all, out_dtype=BF16, seq=seq, tn=tn, gain=gain,
                    norm_blocks=tuple(range(2 * hw // tn)), placed=(c3, place), col_bias=bias)
    o = _fox_attn(qkv, batch=batch, seq=seq)
    return _matmul_res(o, w_out.astype(BF16), x)


def _nsa_shared_kv(x, batch, seq, kv_norm, kv_w, kv_k_gain, cmp_pos, cmp_w1, cmp_w2):
    g = NSA_GROUPS
    gw = g * HEAD_DIM
    tabs = _rope_tables(jnp.arange(seq))
    w_kvd = jnp.concatenate([_pad_heads(kv_w[:, j * gw:(j + 1) * gw]) for j in (2, 3, 4, 5)], axis=1).astype(BF16)
    zero = jnp.zeros((g * LANES,), F32)
    one_lane = jnp.tile((jnp.arange(LANES) == HEAD_DIM).astype(F32), g)
    gain_kvd = jnp.concatenate([_tile_gain(kv_k_gain[1], g), zero, _tile_gain(kv_k_gain[2], g), zero])
    bias_kvd = jnp.concatenate([zero, one_lane, zero, one_lane])
    kvd = _rms_proj(x, kv_norm, w_kvd, out_dtype=BF16, seq=seq, tn=g * LANES, gain=gain_kvd,
                    norm_blocks=(0, 2), rope_tabs=tabs, col_bias=bias_kvd)
    zc = _rms_proj(x, kv_norm, kv_w[:, :2 * gw].astype(BF16), out_dtype=F32, seq=seq)
    rows = seq // CMP_STRIDE
    zr = zc.reshape(batch, rows, CMP_STRIDE, 2, g, HEAD_DIM).transpose(0, 3, 4, 1, 2, 5)
    zr = zr.reshape(batch, 2, g, rows, CMP_STRIDE * HEAD_DIM)
    cmp_end = jnp.arange(rows) * CMP_STRIDE + CMP_BLOCK - 1
    ctabs = _rope_tables(cmp_end)
    w2dup = jnp.concatenate([cmp_w2, cmp_w2], axis=-1).astype(BF16)
    kvc = _compress(zr, cmp_w1.astype(BF16), cmp_pos.reshape(2, 1, CMP_BLOCK * HEAD_DIM).astype(F32), w2dup,
                    _tile_gain(kv_k_gain[0], 1, dup=True).reshape(1, LANES), ctabs, batch=batch)
    return kvc, kvd


def _nsa_layer(x, batch, seq, attn_g, w_in, q_gain, w_out, kvc, kvd):
    hd = N_HEADS * HEAD_DIM
    scale = HEAD_DIM ** -0.5
    rows = seq // CMP_STRIDE
    tabs = _rope_tables(jnp.arange(seq))
    qn =_rms_proj(x, attn_g, _pad_heads(w_in[:, :hd]).astype(BF16), out_dtype=BF16, seq=seq,
                   gain=_tile_gain(q_gain, N_HEADS, scale * LOG2E), norm_blocks=tuple(range(N_HEADS * LANES // 512)),
                   rope_tabs=tabs)
    gz = _rms_proj(x, attn_g, _pad_cols(w_in[:, hd:]).astype(BF16), out_dtype=F32, seq=seq)

    n_sel = seq // SEL_BLOCK
    assert n_sel <= LANES
    cmp_start = jnp.arange(rows) * CMP_STRIDE
    sel_start = jnp.arange(LANES) * SEL_BLOCK
    overlap = jnp.clip(jnp.minimum(cmp_start[:, None] + CMP_BLOCK, sel_start[None, :] + SEL_BLOCK)
                       - jnp.maximum(cmp_start[:, None], sel_start[None, :]), 0)
    overlap = jnp.where((jnp.arange(rows) < rows - 1)[:, None] & (jnp.arange(LANES) < n_sel)[None, :], overlap, 0)
    o = _nsa_attn(qn, gz, kvc, overlap.astype(BF16), kvd, batch=batch, seq=seq)
    return _matmul_res(o, w_out.astype(BF16), x)


def kernel(x, p, attn_norm, ffn_norm, ple_norm, ple_gate_w, ple_proj_w, router_w, router_b, w_gu, b_gu, w_down, b_down, fox_w_in, fox_b_f, fox_qk_gain, fox_w_out, kv_norm, kv_w, kv_k_gain, cmp_pos, cmp_w1, cmp_w2, nsa_w_in, nsa_q_gain, nsa_w_out):
    batch, seq, d = x.shape
    depth = p.shape[0]
    n_a = fox_w_in.shape[0]
    xt = x.reshape(batch * seq, d)
    shared = None
    for layer in range(depth):
        if layer == n_a:
            shared = _nsa_shared_kv(xt, batch, seq, kv_norm, kv_w, kv_k_gain, cmp_pos, cmp_w1, cmp_w2)
        if layer < n_a:
            xt = _fox_layer(xt, batch, seq, attn_norm[layer], fox_w_in[layer], fox_b_f[layer],
                            fox_qk_gain[layer], fox_w_out[layer])
        else:
            i = layer - n_a
            xt = _nsa_layer(xt, batch, seq, attn_norm[layer], nsa_w_in[i], nsa_q_gain[i], nsa_w_out[i], *shared)
        xt = _moe_layer(xt, ffn_norm[layer], router_w[layer], router_b[layer], w_gu[layer], b_gu[layer],
                        w_down[layer], b_down[layer])
        xt = _ple(xt, ple_norm[layer], ple_gate_w[layer], p[layer].reshape(batch * seq, -1), ple_proj_w[layer])
    return xt.reshape(batch, seq, d)
```

```python
import functools

import jax
import jax.numpy as jnp
from jax import lax
from jax.experimental import pallas as pl
from jax.experimental.pallas import tpu as pltpu

F32 = jnp.float32
BF16 = jnp.bfloat16

N_HEADS = 16
HEAD_DIM = 64
ROPE_DIM = HEAD_DIM // 4
ROPE_THETA = 500000.0
NSA_GROUPS = 4
HEADS_PER_GROUP = N_HEADS // NSA_GROUPS
CMP_BLOCK = 32
CMP_STRIDE = 16
SEL_BLOCK = 64
SEL_TOPK = 16
WINDOW = 512
N_EXPERTS = 32
TOP_K = 4
SWIGLU_LIMIT = 7.0
SWIGLU_ALPHA = 1.702
EPS = 1e-6
NEG = -1e30
FORCE = 1e6
LOG2E = 1.4426950408889634

LANES = 128
NSA_QB = 128
MOE_BM = 256
VMEM_LIMIT = 56 * 1024 * 1024


def _cparams(sem):
    return pltpu.CompilerParams(dimension_semantics=sem, vmem_limit_bytes=VMEM_LIMIT)


def _sigmoid(x):
    return 1.0 / (1.0 + jnp.exp(-x))


def _dot_nt(a, b):
    return lax.dot_general(a, b, (((1,), (1,)), ((), ())), preferred_element_type=F32)


def _split2(x):
    hi = x.astype(BF16)
    lo = (x - hi.astype(F32)).astype(BF16)
    return hi, lo


def _proj_kernel(*refs, norm_blocks, rope, placed, biased, head_div, tn, out_dtype):
    refs = list(refs)
    x_ref, g_ref, w_ref = refs[:3]
    del refs[:3]
    if norm_blocks:
        gain_ref = refs.pop(0)
    if rope:
        c_ref, s1_ref, s2_ref = refs[:3]
        del refs[:3]
    if placed:
        ex_ref, place_ref = refs[:2]
        del refs[:2]
    if biased:
        bias_ref = refs.pop(0)
    o_ref, h_ref = refs
    j = pl.program_id(1)
    aug = placed or biased

    @pl.when(j == 0)
    def _():
        xf = x_ref[...]
        ms = jnp.mean(xf * xf, axis=-1, keepdims=True)
        h_ref[...] = (xf * lax.rsqrt(ms + EPS) * g_ref[...]).astype(BF16)

    acc = jnp.dot(h_ref[...], w_ref[...], preferred_element_type=F32)
    if placed and biased:
        extra = jnp.dot(ex_ref[...], place_ref[...], preferred_element_type=F32) + bias_ref[...]
    elif biased:
        extra = jnp.broadcast_to(bias_ref[...], acc.shape)
    if not norm_blocks:
        o_ref[...] = ((acc + extra) if aug else acc).astype(out_dtype)
        return
    is_norm = functools.reduce(jnp.logical_or, [j == c for c in norm_blocks])

    @pl.when(is_norm)
    def _():
        for c in range(tn // LANES):
            sl = slice(c * LANES, (c + 1) * LANES)
            a = acc[:, sl]
            ss = jnp.sum(a * a, axis=-1, keepdims=True) * (1.0 / head_div)
            y = a * lax.rsqrt(ss + EPS) * gain_ref[:, sl]
            if rope:
                y = (y * c_ref[...] + pltpu.roll(y, LANES - ROPE_DIM // 2, 1) * s1_ref[...]
                     + pltpu.roll(y, ROPE_DIM // 2, 1) * s2_ref[...])
            if aug:
                y = y + extra[:, sl]
            o_ref[:, sl] = y.astype(out_dtype)

    @pl.when(jnp.logical_not(is_norm))
    def _():
        o_ref[...] = ((acc + extra) if aug else acc).astype(out_dtype)


def _rms_proj(x, g, w, *, out_dtype, seq, tm=512, tn=512, gain=None, norm_blocks=(),
              rope_tabs=None, placed=None, col_bias=None, head_div=HEAD_DIM):
    n, d = x.shape
    f = w.shape[1]
    tn = min(tn, f)
    tm = min(tm, seq)
    assert n % tm == 0 and f % tn == 0 and seq % tm == 0
    nt = seq // tm
    in_specs = [
        pl.BlockSpec((tm, d), lambda i, j: (i, 0)),
        pl.BlockSpec((1, d), lambda i, j: (0, 0)),
        pl.BlockSpec((d, tn), lambda i, j: (0, j)),
    ]
    args = [x, g.reshape(1, d).astype(F32), w]
    if norm_blocks:
        in_specs.append(pl.BlockSpec((1, tn), lambda i, j: (0, j)))
        args.append(gain.reshape(1, f).astype(F32))
    if rope_tabs is not None:
        assert norm_blocks
        for tab in rope_tabs:
            in_specs.append(pl.BlockSpec((tm, LANES), lambda i, j: (i % nt, 0)))
            args.append(tab)
    if placed is not None:
        assert col_bias is not None
        values, place = placed
        in_specs += [pl.BlockSpec((tm, LANES), lambda i, j: (i, 0)),
                     pl.BlockSpec((LANES, tn), lambda i, j: (0, j))]
        args += [values, place]
    if col_bias is not None:
        in_specs.append(pl.BlockSpec((1, tn), lambda i, j: (0, j)))
        args.append(col_bias.reshape(1, f).astype(F32))
    kern = functools.partial(_proj_kernel, norm_blocks=tuple(norm_blocks), rope=rope_tabs is not None,
                             placed=placed is not None, biased=col_bias is not None,
                             head_div=float(head_div), tn=tn, out_dtype=out_dtype)
    return pl.pallas_call(
        kern,
        out_shape=jax.ShapeDtypeStruct((n, f), out_dtype),
        grid=(n // tm, f // tn),
        in_specs=in_specs,
        out_specs=pl.BlockSpec((tm, tn), lambda i, j: (i, j)),
        scratch_shapes=[pltpu.VMEM((tm, d), BF16)],
        compiler_params=_cparams(("parallel", "arbitrary")),
        name="rms_proj",
    )(*args)


def _rope_tables(pos, width=LANES):
    half = ROPE_DIM // 2
    inv = jnp.power(jnp.float32(ROPE_THETA), -jnp.arange(0, ROPE_DIM, 2, dtype=F32) / ROPE_DIM)
    ang = pos.astype(F32)[:, None] * inv[None, :]
    cos, sin = jnp.cos(ang), jnp.sin(ang)
    t = pos.shape[0]
    ones = jnp.ones((t, HEAD_DIM - ROPE_DIM), F32)
    zeros_h = jnp.zeros((t, half), F32)
    zeros_r = jnp.zeros((t, HEAD_DIM - ROPE_DIM), F32)
    c = jnp.concatenate([cos, cos, ones], axis=1)
    s1 = jnp.concatenate([-sin, zeros_h, zeros_r], axis=1)
    s2 = jnp.concatenate([zeros_h, sin, zeros_r], axis=1)
    rep = width // HEAD_DIM
    return tuple(jnp.tile(a, (1, rep)) for a in (c, s1, s2))


def _matmul_res_kernel(a_ref, w_ref, x_ref, o_ref):
    o_ref[...] = x_ref[...] + jnp.dot(a_ref[...], w_ref[...], preferred_element_type=F32)


def _matmul_res(a, w, x, *, tm=512):
    n, k = a.shape
    f = w.shape[1]
    tm = min(tm, n)
    return pl.pallas_call(
        _matmul_res_kernel,
        out_shape=jax.ShapeDtypeStruct((n, f), F32),
        grid=(n // tm,),
        in_specs=[
            pl.BlockSpec((tm, k), lambda i: (i, 0)),
            pl.BlockSpec((k, f), lambda i: (0, 0)),
            pl.BlockSpec((tm, f), lambda i: (i, 0)),
        ],
        out_specs=pl.BlockSpec((tm, f), lambda i: (i, 0)),
        compiler_params=_cparams(("parallel",)),
        name="matmul_res",
    )(a, w, x)


def _fox_c_kernel(z_ref, b_ref, c_ref, carry_ref, *, tm):
    t = pl.program_id(1)

    @pl.when(t == 0)
    def _():
        carry_ref[...] = jnp.zeros_like(carry_ref)

    u = z_ref[...] + b_ref[...]
    logf = jnp.minimum(u, 0.0) - jnp.log(1.0 + jnp.exp(-jnp.abs(u)))
    row = lax.broadcasted_iota(jnp.int32, (tm, tm), 0)
    col = lax.broadcasted_iota(jnp.int32, (tm, tm), 1)
    tri = (row >= col).astype(BF16)
    hi = logf.astype(BF16)
    r1 = logf - hi.astype(F32)
    mid = r1.astype(BF16)
    lo = (r1 - mid.astype(F32)).astype(BF16)
    cs = (jnp.dot(tri, hi, preferred_element_type=F32) + jnp.dot(tri, mid, preferred_element_type=F32)
          + jnp.dot(tri, lo, preferred_element_type=F32))
    c = cs + carry_ref[...]
    carry_ref[...] = c[tm - 1:tm, :]
    c2 = c * LOG2E
    p0 = c2.astype(BF16).astype(F32)
    p1 = (c2 - p0).astype(BF16).astype(F32)
    p2 = ((c2 - p0) - p1).astype(BF16).astype(F32)
    lane = lax.broadcasted_iota(jnp.int32, c.shape, 1)
    out = jnp.where(lane < N_HEADS, p0,
                    jnp.where(lane < 2 * N_HEADS, pltpu.roll(p1, N_HEADS, 1),
                              jnp.where(lane < 3 * N_HEADS, pltpu.roll(p2, 2 * N_HEADS, 1), 0.0)))
    c_ref[...] = out.astype(BF16)


def _fox_c(z, b_pad, *, batch, seq, tm=512):
    tm = min(tm, seq)
    nt = seq // tm
    return pl.pallas_call(
        functools.partial(_fox_c_kernel, tm=tm),
        out_shape=jax.ShapeDtypeStruct(z.shape, BF16),
        grid=(batch, nt),
        in_specs=[
            pl.BlockSpec((tm, LANES), lambda b, t: (b * nt + t, 0)),
            pl.BlockSpec((1, LANES), lambda b, t: (0, 0)),
        ],
        out_specs=pl.BlockSpec((tm, LANES), lambda b, t: (b * nt + t, 0)),
        scratch_shapes=[pltpu.VMEM((1, LANES), F32)],
        compiler_params=_cparams(("parallel", "arbitrary")),
        name="fox_cumsum",
    )(z, b_pad)


def _fox_attn_kernel(q0_ref, q1_ref, k0_ref, k1_ref, v0_ref, v1_ref, o_ref, *, tq, tk, rc, skew):
    qi = pl.program_id(2)
    heads = ((q0_ref, k0_ref, v0_ref), (q1_ref, k1_ref, v1_ref))
    chunks = tuple(range(0, tq, rc))
    state = tuple(tuple((jnp.full((rc, 1), NEG, F32), jnp.zeros((rc, LANES), F32)) for _ in chunks)
                  for _ in heads)

    def step(k0, masked, state):
        kts = [k_ref[pl.ds(k0, tk), :] for (_, k_ref, _) in heads]
        vts = [v_ref[pl.ds(k0, tk), :] for (_, _, v_ref) in heads]
        items = [(h, c) for c in range(len(chunks)) for h in range(len(heads))]
        new_state = [[None] * len(chunks) for _ in heads]
        scores = {}
        for idx in range(len(items) + skew):
            if idx < len(items):
                h, c = items[idx]
                r0 = chunks[c]
                s = _dot_nt(heads[h][0][r0:r0 + rc, :], kts[h])
                if masked:
                    tpos = qi * tq + r0 + lax.broadcasted_iota(jnp.int32, (rc, tk), 0)
                    kpos = k0 + lax.broadcasted_iota(jnp.int32, (rc, tk), 1)
                    s = jnp.where(kpos <= tpos, s, NEG)
                scores[idx] = s
            if idx >= skew:
                h, c = items[idx - skew]
                s = scores.pop(idx - skew)
                m_prev, acc = state[h][c]
                m_new = jnp.maximum(m_prev, jnp.max(s, axis=-1, keepdims=True))
                p = jnp.exp2(s - m_new)
                acc = jnp.exp2(m_prev - m_new) * acc + jnp.dot(p.astype(BF16), vts[h],
                                                               preferred_element_type=F32)
                new_state[h][c] = (m_new, acc)
        return tuple(tuple(x) for x in new_state)

    def full_tile(j, state):
        return step(pl.multiple_of(j * tk, tk), False, state)

    state = lax.fori_loop(0, (qi * tq) // tk, full_tile, state)
    for d in range(tq // tk):
        state = step(pl.multiple_of(qi * tq + d * tk, tk), True, state)

    lane = lax.broadcasted_iota(jnp.int32, (rc, LANES), 1)
    for c, r0 in enumerate(chunks):
        outs = []
        for h in range(2):
            acc = state[h][c][1]
            den = jnp.sum(jnp.where(lane == HEAD_DIM, acc, 0.0), axis=-1, keepdims=True)
            outs.append(acc / den)
        o_ref[r0:r0 + rc, :] = jnp.where(lane < HEAD_DIM, outs[0],
                                         pltpu.roll(outs[1], HEAD_DIM, 1)).astype(o_ref.dtype)


def _fox_attn(qkv, *, batch, seq, tq=512, tk=512, rc=512, skew=1):
    n = qkv.shape[0]
    tq = min(tq, seq)
    tk = min(tk, tq)
    assert seq % tq == 0 and tq % tk == 0
    nq = seq // tq
    hp = N_HEADS // 2
    kb = N_HEADS
    vb = 2 * N_HEADS
    in_specs = [
        pl.BlockSpec((tq, LANES), lambda b, h, i: (b * nq + i, 2 * h)),
        pl.BlockSpec((tq, LANES), lambda b, h, i: (b * nq + i, 2 * h + 1)),
        pl.BlockSpec((seq, LANES), lambda b, h, i: (b, kb + 2 * h)),
        pl.BlockSpec((seq, LANES), lambda b, h, i: (b, kb + 2 * h + 1)),
        pl.BlockSpec((seq, LANES), lambda b, h, i: (b, vb + 2 * h)),
        pl.BlockSpec((seq, LANES), lambda b, h, i: (b, vb + 2 * h + 1)),
    ]
    return pl.pallas_call(
        functools.partial(_fox_attn_kernel, tq=tq, tk=tk, rc=min(rc, tq), skew=skew),
        out_shape=jax.ShapeDtypeStruct((n, N_HEADS * HEAD_DIM), BF16),
        grid=(batch, hp, nq),
        in_specs=in_specs,
        out_specs=pl.BlockSpec((tq, LANES), lambda b, h, i: (b * nq + i, h)),
        compiler_params=_cparams(("parallel", "parallel", "arbitrary")),
        name="fox_attn",
    )(qkv, qkv, qkv, qkv, qkv, qkv)


def _router_kernel(x_ref, g_ref, whi_ref, wlo_ref, b_ref, h_ref, idx_ref, gate_ref):
    xf = x_ref[...]
    ms = jnp.mean(xf * xf, axis=-1, keepdims=True)
    h = xf * lax.rsqrt(ms + EPS) * g_ref[...]
    h_ref[...] = h
    hhi, hlo = _split2(h)
    whi = whi_ref[...]
    logits = (jnp.dot(hhi, whi, preferred_element_type=F32) + jnp.dot(hlo, whi, preferred_element_type=F32)
              + jnp.dot(hhi, wlo_ref[...], preferred_element_type=F32)) + b_ref[...]
    lane = lax.broadcasted_iota(jnp.int32, logits.shape, 1).astype(F32)
    vals, idxs = [], []
    cur = logits
    for _ in range(TOP_K):
        mx = jnp.max(cur, axis=-1, keepdims=True)
        ix = jnp.min(jnp.where(cur == mx, lane, float(LANES)), axis=-1, keepdims=True)
        vals.append(mx)
        idxs.append(ix)
        cur = jnp.where(lane == ix, -jnp.inf, cur)
    es = [jnp.exp(v - vals[0]) for v in vals]
    tot = es[0] + es[1] + es[2] + es[3]
    idx_out = jnp.zeros(logits.shape, F32)
    gate_out = jnp.zeros(logits.shape, F32)
    for r in range(TOP_K):
        idx_out = jnp.where(lane == r, idxs[r], idx_out)
        gate_out = jnp.where(lane == r, es[r] / tot, gate_out)
    idx_ref[...] = idx_out.astype(jnp.int32)
    gate_ref[...] = gate_out


def _router(x, g, w, b, *, tm=512):
    n, d = x.shape
    e = w.shape[1]
    tm = min(tm, n)
    w_pad = jnp.zeros((d, LANES), F32).at[:, :e].set(w)
    whi = w_pad.astype(BF16)
    wlo = (w_pad - whi.astype(F32)).astype(BF16)
    b_pad = jnp.full((1, LANES), -jnp.inf, F32).at[0, :e].set(b.astype(F32))
    return pl.pallas_call(
        _router_kernel,
        out_shape=(jax.ShapeDtypeStruct((n, d), F32), jax.ShapeDtypeStruct((n, LANES), jnp.int32),
                   jax.ShapeDtypeStruct((n, LANES), F32)),
        grid=(n // tm,),
        in_specs=[
            pl.BlockSpec((tm, d), lambda i: (i, 0)),
            pl.BlockSpec((1, d), lambda i: (0, 0)),
            pl.BlockSpec((d, LANES), lambda i: (0, 0)),
            pl.BlockSpec((d, LANES), lambda i: (0, 0)),
            pl.BlockSpec((1, LANES), lambda i: (0, 0)),
        ],
        out_specs=(pl.BlockSpec((tm, d), lambda i: (i, 0)), pl.BlockSpec((tm, LANES), lambda i: (i, 0)),
                   pl.BlockSpec((tm, LANES), lambda i: (i, 0))),
        compiler_params=_cparams(("parallel",)),
        name="moe_router",
    )(x, g.reshape(1, d).astype(F32), whi, wlo, b_pad)


def _moe_expert_kernel(blk_exp_ref, tok_ref, dst_ref, h_hbm, ga_ref, gb_ref, wgu_a, bgu_a, wd_a, bd_a,
                       wgu_b, bgu_b, wd_b, bd_b, y_hbm, xa, xb, oa, ob, gsem, ssem, *, bm, dff, n_steps):
    i = pl.program_id(0)
    dump = y_hbm.shape[0] - 2 * bm

    def gather(block, buf, sem):
        base = block * bm
        for r in range(bm):
            pltpu.make_async_copy(h_hbm.at[pl.ds(tok_ref[base + r], 1)], buf.at[pl.ds(r, 1)], sem).start()

    def scatter(block, buf, sem):
        base = block * bm
        for r in range(bm):
            pltpu.make_async_copy(buf.at[pl.ds(r, 1)], y_hbm.at[pl.ds(dst_ref[base + r], 1)], sem).start()

    def wait_gather(buf, sem):
        pltpu.make_async_copy(h_hbm.at[pl.ds(0, bm)], buf, sem).wait()

    def wait_scatter(buf, sem):
        pltpu.make_async_copy(buf, y_hbm.at[pl.ds(0, bm)], sem).wait()

    def experts(x, g_ref, wgu, bgu, wd, bd):
        gu = jnp.dot(x.astype(BF16), wgu[0], preferred_element_type=F32) + bgu[0]
        a = jnp.minimum(gu[:, :dff], SWIGLU_LIMIT)
        u = jnp.clip(gu[:, dff:], -SWIGLU_LIMIT, SWIGLU_LIMIT)
        y = (u + 1.0) * (a * _sigmoid(SWIGLU_ALPHA * a))
        return (jnp.dot(y.astype(BF16), wd[0], preferred_element_type=F32) + bd[0]) * g_ref[...]

    @pl.when(i == 0)
    def _():
        gather(0, xa, gsem.at[0])
        for buf, sem, base in ((oa, ssem.at[0], dump), (ob, ssem.at[1], dump + bm)):
            buf[...] = jnp.zeros(buf.shape, buf.dtype)
            for r in range(bm):
                pltpu.make_async_copy(buf.at[pl.ds(r, 1)], y_hbm.at[pl.ds(base + r, 1)], sem).start()

    blk_a = 2 * i
    blk_b = 2 * i + 1
    nxt_a = jnp.minimum(2 * i + 2, 2 * n_steps - 1)

    gather(blk_b, xb, gsem.at[1])
    wait_gather(xa, gsem.at[0])
    wait_scatter(oa, ssem.at[0])
    oa[...] = experts(xa[...], ga_ref, wgu_a, bgu_a, wd_a, bd_a)
    scatter(blk_a, oa, ssem.at[0])

    gather(nxt_a, xa, gsem.at[0])
    wait_gather(xb, gsem.at[1])
    wait_scatter(ob, ssem.at[1])
    ob[...] = experts(xb[...], gb_ref, wgu_b, bgu_b, wd_b, bd_b)
    scatter(blk_b, ob, ssem.at[1])

    @pl.when(i == n_steps - 1)
    def _():
        wait_gather(xa, gsem.at[0])
        wait_scatter(oa, ssem.at[0])
        wait_scatter(ob, ssem.at[1])


def _moe_experts(h, blk_exp, tok, dst, gates, w_gu, b_gu, w_down, b_down, *, bm):
    n, d = h.shape
    e, _, f2 = w_gu.shape
    dff = f2 // 2
    cap = tok.shape[0]
    n_steps = cap // (2 * bm)
    assert cap == n_steps * 2 * bm

    def wspecs(off):
        return [
            pl.BlockSpec((1, d, f2), lambda i, be, tk, ds: (be[2 * i + off], 0, 0)),
            pl.BlockSpec((1, 1, f2), lambda i, be, tk, ds: (be[2 * i + off], 0, 0)),
            pl.BlockSpec((1, dff, d), lambda i, be, tk, ds: (be[2 * i + off], 0, 0)),
            pl.BlockSpec((1, 1, d), lambda i, be, tk, ds: (be[2 * i + off], 0, 0)),
        ]

    grid_spec = pltpu.PrefetchScalarGridSpec(
        num_scalar_prefetch=3,
        grid=(n_steps,),
        in_specs=[
            pl.BlockSpec(memory_space=pl.ANY),
            pl.BlockSpec((bm, 1), lambda i, be, tk, ds: (2 * i, 0)),
            pl.BlockSpec((bm, 1), lambda i, be, tk, ds: (2 * i + 1, 0)),
        ] + wspecs(0) + wspecs(1),
        out_specs=pl.BlockSpec(memory_space=pl.ANY),
        scratch_shapes=[pltpu.VMEM((bm, d), F32), pltpu.VMEM((bm, d), F32), pltpu.VMEM((bm, d), F32),
                        pltpu.VMEM((bm, d), F32), pltpu.SemaphoreType.DMA((2,)), pltpu.SemaphoreType.DMA((2,))],
    )
    bgu3 = b_gu.reshape(e, 1, f2).astype(F32)
    bd3 = b_down.reshape(e, 1, d).astype(F32)
    g2 = gates.reshape(cap, 1)
    return pl.pallas_call(
        functools.partial(_moe_expert_kernel, bm=bm, dff=dff, n_steps=n_steps),
        out_shape=jax.ShapeDtypeStruct((TOP_K * n + 2 * bm, d), F32),
        grid_spec=grid_spec,
        compiler_params=_cparams(("arbitrary",)),
        name="moe_experts",
    )(blk_exp, tok, dst, h, g2, g2, w_gu, bgu3, w_down, bd3, w_gu, bgu3, w_down, bd3)


def _moe_layer(x, norm_g, router_w, router_b, w_gu, b_gu, w_down, b_down):
    n, d = x.shape
    bm = MOE_BM
    h, idx128, gate128 = _router(x, norm_g, router_w, router_b)
    n_assign = n * TOP_K
    flat_e = idx128[:, :TOP_K].reshape(-1)
    flat_g = gate128[:, :TOP_K].reshape(-1)
    experts = jnp.arange(N_EXPERTS, dtype=jnp.int32)
    order = jnp.argsort(flat_e).astype(jnp.int32)
    counts = jnp.sum((flat_e[:, None] == experts[None, :]).astype(jnp.int32), axis=0)
    padded = (counts + bm - 1) // bm * bm
    start = jnp.cumsum(counts) - counts
    pend = jnp.cumsum(padded)
    pstart = pend - padded
    n_blocks = -(-n_assign // bm) + N_EXPERTS
    n_blocks += n_blocks % 2
    cap = n_blocks * bm
    blk_start = jnp.arange(n_blocks, dtype=jnp.int32) * bm
    blk_exp = jnp.minimum(jnp.sum((pend[None, :] <= blk_start[:, None]).astype(jnp.int32), axis=1),
                          N_EXPERTS - 1).astype(jnp.int32)
    slot = jnp.arange(cap, dtype=jnp.int32)
    slot_e = jnp.repeat(blk_exp, bm)
    off = slot - pstart[slot_e]
    valid = off < counts[slot_e]
    a = order[jnp.clip(start[slot_e] + off, 0, n_assign - 1)]
    tok = jnp.where(valid, a // TOP_K, 0).astype(jnp.int32)
    dst = jnp.where(valid, (a % TOP_K) * n + a // TOP_K, TOP_K * n + slot % (2 * bm)).astype(jnp.int32)
    gates = jnp.where(valid, flat_g[a], 0.0)
    return _moe_experts(h, blk_exp, tok, dst, gates, w_gu.astype(BF16), b_gu, w_down.astype(BF16), b_down, bm=bm)


def _ple_kernel(x_ref, y0_ref, y1_ref, y2_ref, y3_ref, g_ref, wg_ref, p_ref, wp_ref, o_ref):
    xf = x_ref[...] + (((y0_ref[...] + y1_ref[...]) + y2_ref[...]) + y3_ref[...])
    ms = jnp.mean(xf * xf, axis=-1, keepdims=True)
    h = (xf * lax.rsqrt(ms + EPS) * g_ref[...]).astype(BF16)
    gate = _sigmoid(jnp.dot(h, wg_ref[...], preferred_element_type=F32))
    pp = jnp.dot(p_ref[...].astype(BF16), wp_ref[...], preferred_element_type=F32)
    o_ref[...] = xf + gate * pp


def _ple(x, y, g, wg, p, wp, *, tm=512):
    n, d = x.shape
    pd = p.shape[1]
    tm = min(tm, n)
    nb = n // tm
    yspecs = [pl.BlockSpec((tm, d), functools.partial(lambda i, k: (k * nb + i, 0), k=k)) for k in range(TOP_K)]
    return pl.pallas_call(
        _ple_kernel,
        out_shape=jax.ShapeDtypeStruct((n, d), F32),
        grid=(nb,),
        in_specs=[pl.BlockSpec((tm, d), lambda i: (i, 0))] + yspecs + [
            pl.BlockSpec((1, d), lambda i: (0, 0)),
            pl.BlockSpec((d, d), lambda i: (0, 0)),
            pl.BlockSpec((tm, pd), lambda i: (i, 0)),
            pl.BlockSpec((pd, d), lambda i: (0, 0)),
        ],
        out_specs=pl.BlockSpec((tm, d), lambda i: (i, 0)),
        compiler_params=_cparams(("parallel",)),
        name="ple",
    )(x, y, y, y, y, g.reshape(1, d).astype(F32), wg.astype(BF16), p, wp.astype(BF16))


def _cmp_kernel(z_ref, w1_ref, pos_ref, w2_ref, gain_ref, c_ref, s1_ref, s2_ref, o_ref, *, rows, half):
    j = pl.program_id(1)
    z = z_ref[0, 0, 0]
    pos = pos_ref[0]
    a1 = (z + pos[:, :half]).astype(BF16)
    a2 = (z + pos[:, half:]).astype(BF16)
    u1 = jnp.dot(a1, w1_ref[0, :half, :], preferred_element_type=F32)
    u2 = jnp.dot(a2, w1_ref[0, half:, :], preferred_element_type=F32)
    h = u1 + pltpu.roll(u2, rows - 1, 0)
    hs = h * _sigmoid(h)
    y = jnp.dot(hs.astype(BF16), w2_ref[0], preferred_element_type=F32)

    @pl.when(j == 0)
    def _():
        ss = jnp.sum(y * y, axis=-1, keepdims=True) * (1.0 / LANES)
        yn = y * lax.rsqrt(ss + EPS) * gain_ref[...]
        yn = (yn * c_ref[...] + pltpu.roll(yn, LANES - ROPE_DIM // 2, 1) * s1_ref[...]
              + pltpu.roll(yn, ROPE_DIM // 2, 1) * s2_ref[...])
        o_ref[0, 0, 0] = yn.astype(o_ref.dtype)

    @pl.when(j != 0)
    def _():
        o_ref[0, 0, 0] = y.astype(o_ref.dtype)


def _compress(zr, w1, pos, w2dup, gain_dup, tabs, *, batch):
    rows, width = zr.shape[3], zr.shape[4]
    hid = w1.shape[2]
    g = NSA_GROUPS
    return pl.pallas_call(
        functools.partial(_cmp_kernel, rows=rows, half=width),
        out_shape=jax.ShapeDtypeStruct((batch, 2, g, rows, LANES), BF16),
        grid=(batch, 2, g),
        in_specs=[
            pl.BlockSpec((1, 1, 1, rows, width), lambda b, j, gg: (b, j, gg, 0, 0)),
            pl.BlockSpec((1, 2 * width, hid), lambda b, j, gg: (j, 0, 0)),
            pl.BlockSpec((1, 1, 2 * width), lambda b, j, gg: (j, 0, 0)),
            pl.BlockSpec((1, hid, LANES), lambda b, j, gg: (j, 0, 0)),
            pl.BlockSpec((1, LANES), lambda b, j, gg: (0, 0)),
            pl.BlockSpec((rows, LANES), lambda b, j, gg: (0, 0)),
            pl.BlockSpec((rows, LANES), lambda b, j, gg: (0, 0)),
            pl.BlockSpec((rows, LANES), lambda b, j, gg: (0, 0)),
        ],
        out_specs=pl.BlockSpec((1, 1, 1, rows, LANES), lambda b, j, gg: (b, j, gg, 0, 0)),
        compiler_params=_cparams(("parallel", "parallel", "parallel")),
        name="nsa_compress",
    )(zr, w1, pos, w2dup, gain_dup, *tabs)


def _pipelined(n_items, first, second, skew):
    pending, out = {}, [None] * n_items
    for idx in range(n_items + skew):
        if idx < n_items:
            pending[idx] = first(idx)
        if idx >= skew:
            out[idx - skew] = second(idx - skew, pending.pop(idx - skew))
    return out


def _nsa_attn_kernel(q_ref, gz_ref, kc_ref, vct_ref, ovt_ref, ks_ref, vs_ref, kw_ref, vw_ref, o_ref,
                     e_ref, oc_ref, *, n_cmp_rows, tk_sel, skew, grp):
    g = pl.program_id(1)
    i = pl.program_id(2)
    qb = NSA_QB
    mh = HEADS_PER_GROUP
    rows = mh * qb
    q0 = i * qb
    seq = ks_ref.shape[0]

    @pl.when(i == 0)
    def _():
        r = lax.broadcasted_iota(jnp.int32, (seq, LANES), 0)
        lb = lax.broadcasted_iota(jnp.int32, (seq, LANES), 1)
        e_ref[...] = jnp.where((r >> 6) == lb, NEG, 0.0).astype(BF16)

    qh = [q_ref[:, m * LANES:(m + 1) * LANES] for m in range(mh)]
    q4 = jnp.concatenate(qh, axis=0)
    jl = lax.broadcasted_iota(jnp.int32, (qb, LANES), 1)

    st = _dot_nt(kc_ref[0, 0, 0], q4)
    cend = lax.broadcasted_iota(jnp.int32, (n_cmp_rows, rows), 0) * CMP_STRIDE + (CMP_BLOCK - 1)
    tlane = q0 + (lax.broadcasted_iota(jnp.int32, (1, rows), 1) & (qb - 1))
    valid = cend <= tlane
    st = jnp.where(valid, st, NEG)
    e = jnp.where(valid, jnp.exp2(st - jnp.max(st, axis=0, keepdims=True)), 0.0)
    pt = e / jnp.maximum(jnp.sum(e, axis=0, keepdims=True), 1e-30)
    oct = jnp.dot(vct_ref[0, 0], pt.astype(BF16), preferred_element_type=F32)
    for m in range(mh):
        oc_ref[m * qb:(m + 1) * qb, :] = oct[:, m * qb:(m + 1) * qb].T
    psum = (pt[:, 0:qb] + pt[:, qb:2 * qb]) + (pt[:, 2 * qb:3 * qb] + pt[:, 3 * qb:4 * qb])
    phi, plo = _split2(psum)
    ovt = ovt_ref[...]
    imp = jnp.dot(ovt, phi, preferred_element_type=F32) + jnp.dot(ovt, plo, preferred_element_type=F32)

    jb = lax.broadcasted_iota(jnp.int32, (LANES, qb), 0)
    tq = q0 + lax.broadcasted_iota(jnp.int32, (LANES, qb), 1)
    cur = tq >> 6
    forced = (jb == 0) | (jb == cur) | (jb == cur - 1)
    causal = (jb << 6) <= tq
    score = jnp.where(causal, jnp.where(forced, FORCE, imp), -jnp.inf)
    notsel_t = jnp.ones((LANES, qb), F32)
    jbf = jb.astype(F32)
    for _ in range(SEL_TOPK):
        smx = jnp.max(score, axis=0, keepdims=True)
        ix = jnp.min(jnp.where(score == smx, jbf, float(LANES)), axis=0, keepdims=True)
        hit = jbf == ix
        notsel_t = jnp.where(hit, 0.0, notsel_t)
        score = jnp.where(hit, -jnp.inf, score)
    notsel_b = notsel_t.T.astype(BF16)
    n_grp = mh // grp
    q_aug = [jnp.concatenate([jnp.concatenate([qh[m], notsel_b], axis=1) for m in range(c * grp, (c + 1) * grp)], axis=0)
             for c in range(n_grp)]
    tpos = q0 + lax.broadcasted_iota(jnp.int32, (qb, 1), 0)
    tpos_g = jnp.concatenate([tpos] * grp, axis=0)

    def sel_step(k0, masked, state):
        k_aug = jnp.concatenate([ks_ref[pl.ds(k0, tk_sel), :], e_ref[pl.ds(k0, tk_sel), :]], axis=1)
        vt = vs_ref[pl.ds(k0, tk_sel), :]

        def scores(c):
            sc = _dot_nt(q_aug[c], k_aug)
            if masked:
                sc = jnp.where(k0 + lax.broadcasted_iota(jnp.int32, (grp * qb, tk_sel), 1) <= tpos_g, sc, NEG)
            return sc

        def update(c, sc):
            m_prev, acc = state[c]
            m_new = jnp.maximum(m_prev, jnp.max(sc, axis=-1, keepdims=True))
            pe = jnp.exp2(sc - m_new)
            acc = jnp.exp2(m_prev - m_new) * acc + jnp.dot(pe.astype(BF16), vt, preferred_element_type=F32)
            return (m_new, acc)

        return tuple(_pipelined(n_grp, scores, update, min(skew, n_grp - 1)))

    state = tuple((jnp.full((grp * qb, 1), NEG, F32), jnp.zeros((grp * qb, LANES), F32)) for _ in range(n_grp))
    n_full = q0 // tk_sel
    state = lax.fori_loop(0, n_full, lambda j, st: sel_step(pl.multiple_of(j * tk_sel, tk_sel), False, st), state)
    state = sel_step(pl.multiple_of(n_full * tk_sel, tk_sel), True, state)
    sel_acc = [state[m // grp][1][(m % grp) * qb:(m % grp + 1) * qb] for m in range(mh)]

    wk = WINDOW + qb
    k0w = pl.multiple_of(jnp.maximum(i - WINDOW // qb, 0) * qb, qb)
    kwt = kw_ref[pl.ds(k0w, wk), :]
    vwt = vw_ref[pl.ds(k0w, wk), :]
    kposw = k0w + lax.broadcasted_iota(jnp.int32, (grp * qb, wk), 1)
    wmask = (kposw <= tpos_g) & (kposw > tpos_g - WINDOW)
    q_grp = [jnp.concatenate(qh[c * grp:(c + 1) * grp], axis=0) for c in range(n_grp)]

    def win_scores(c):
        return jnp.where(wmask, _dot_nt(q_grp[c], kwt), NEG)

    def win_out(c, sc):
        pe = jnp.exp2(sc - jnp.max(sc, axis=-1, keepdims=True))
        return jnp.dot(pe.astype(BF16), vwt, preferred_element_type=F32)

    win_grp = _pipelined(n_grp, win_scores, win_out, min(skew, n_grp - 1))
    win_acc = [win_grp[m // grp][(m % grp) * qb:(m % grp + 1) * qb] for m in range(mh)]

    sig = _sigmoid(gz_ref[...])
    den_lane = jl == HEAD_DIM
    heads = []
    for m in range(mh):
        col = g * mh + m
        gates = [jnp.sum(jnp.where(jl == br * N_HEADS + col, sig, 0.0), axis=-1, keepdims=True)
                 for br in range(3)]
        acc_s = sel_acc[m]
        acc_w = win_acc[m]
        den_s = jnp.sum(jnp.where(den_lane, acc_s, 0.0), axis=-1, keepdims=True)
        den_w = jnp.sum(jnp.where(den_lane, acc_w, 0.0), axis=-1, keepdims=True)
        heads.append(gates[0] * oc_ref[m * qb:(m + 1) * qb, :] + (gates[1] / den_s) * acc_s
                     + (gates[2] / den_w) * acc_w)
    lane_lo = jl < HEAD_DIM
    o_ref[:, 0:LANES] = jnp.where(lane_lo, heads[0], pltpu.roll(heads[1], HEAD_DIM, 1)).astype(o_ref.dtype)
    o_ref[:, LANES:2 * LANES] = jnp.where(lane_lo, heads[2], pltpu.roll(heads[3], HEAD_DIM, 1)).astype(o_ref.dtype)


def _nsa_attn(qn, gz, kvc, overlap, kvd, *, batch, seq, skew=2, grp=2):
    n = qn.shape[0]
    qb = NSA_QB
    nq = seq // qb
    g = NSA_GROUPS
    rows = HEADS_PER_GROUP * qb
    n_cmp_rows = kvc.shape[3]
    tk_sel = min(512, seq)
    assert seq >= WINDOW + qb and seq % tk_sel == 0 and tk_sel % qb == 0
    in_specs = [
        pl.BlockSpec((qb, HEADS_PER_GROUP * LANES), lambda b, gg, i: (b * nq + i, gg)),
        pl.BlockSpec((qb, LANES), lambda b, gg, i: (b * nq + i, 0)),
        pl.BlockSpec((1, 1, 1, n_cmp_rows, LANES), lambda b, gg, i: (b, 0, gg, 0, 0)),
        pl.BlockSpec((1, 1, LANES, n_cmp_rows), lambda b, gg, i: (b, gg, 0, 0)),
        pl.BlockSpec((LANES, n_cmp_rows), lambda b, gg, i: (0, 0)),
        pl.BlockSpec((seq, LANES), lambda b, gg, i: (b, 0 * g + gg)),
        pl.BlockSpec((seq, LANES), lambda b, gg, i: (b, 1 * g + gg)),
        pl.BlockSpec((seq, LANES), lambda b, gg, i: (b, 2 * g + gg)),
        pl.BlockSpec((seq, LANES), lambda b, gg, i: (b, 3 * g + gg)),
    ]
    return pl.pallas_call(
        functools.partial(_nsa_attn_kernel, n_cmp_rows=n_cmp_rows, tk_sel=tk_sel, skew=skew, grp=grp),
        out_shape=jax.ShapeDtypeStruct((n, N_HEADS * HEAD_DIM), BF16),
        grid=(batch, g, nq),
        in_specs=in_specs,
        out_specs=pl.BlockSpec((qb, HEADS_PER_GROUP * HEAD_DIM), lambda b, gg, i: (b * nq + i, gg)),
        scratch_shapes=[pltpu.VMEM((seq, LANES), BF16), pltpu.VMEM((rows, LANES), F32)],
        compiler_params=_cparams(("parallel", "parallel", "arbitrary")),
        name="nsa_attn",
    )(qn, gz, kvc, jnp.swapaxes(kvc[:, 1], -1, -2), overlap.T, kvd, kvd, kvd, kvd)


def _pad_heads(w):
    d, f = w.shape
    h = f // HEAD_DIM
    w3 = w.reshape(d, h, HEAD_DIM)
    return jnp.concatenate([w3, jnp.zeros_like(w3)], axis=-1).reshape(d, h * LANES)


def _dup_heads(w):
    d, f = w.shape
    h = f // HEAD_DIM
    w3 = w.reshape(d, h, HEAD_DIM)
    return jnp.concatenate([w3, w3], axis=-1).reshape(d, h * LANES)


def _pad_cols(w, width=LANES):
    d, f = w.shape
    return jnp.zeros((d, width), w.dtype).at[:, :f].set(w)


def _tile_gain(gain, n_heads, scale=1.0, dup=False):
    g = gain.astype(F32) * scale
    second = g if dup else jnp.zeros_like(g)
    return jnp.tile(jnp.concatenate([g, second]), n_heads)


def _fox_aug_tables():
    hw = N_HEADS * LANES
    piece = jnp.arange(3)[:, None]
    head = jnp.arange(N_HEADS)[None, :]
    src = (piece * N_HEADS + head).reshape(-1)
    q_dst = (head * LANES + HEAD_DIM + piece).reshape(-1)
    k_dst = (hw + head * LANES + HEAD_DIM + 3 + piece).reshape(-1)
    place = jnp.zeros((LANES, 3 * hw), F32).at[src, q_dst].set(1.0).at[src, k_dst].set(-1.0)
    lane = jnp.arange(LANES)
    q_bias = ((lane >= HEAD_DIM + 3) & (lane < HEAD_DIM + 6)).astype(F32)
    k_bias = ((lane >= HEAD_DIM) & (lane < HEAD_DIM + 3)).astype(F32)
    v_bias = (lane == HEAD_DIM).astype(F32)
    bias = jnp.concatenate([jnp.tile(q_bias, N_HEADS), jnp.tile(k_bias, N_HEADS), jnp.tile(v_bias, N_HEADS)])
    return place.astype(BF16), bias


def _fox_layer(x, batch, seq, attn_g, w_in, b_f, qk_gain, w_out):
    hd = N_HEADS * HEAD_DIM
    hw = N_HEADS * LANES
    scale = HEAD_DIM ** -0.5
    z = _rms_proj(x, attn_g, _pad_cols(w_in[:, 3 * hd:]).astype(BF16), out_dtype=F32, seq=seq)
    b_pad = jnp.zeros((1, LANES), F32).at[0, :N_HEADS].set(b_f.astype(F32))
    c3 = _fox_c(z, b_pad, batch=batch, seq=seq)
    w_all = jnp.concatenate([_pad_heads(w_in[:, :hd]), _pad_heads(w_in[:, hd:2 * hd]),
                             _pad_heads(w_in[:, 2 * hd:3 * hd])], axis=1).astype(BF16)
    gain = jnp.concatenate([_tile_gain(qk_gain[0], N_HEADS, scale * LOG2E), _tile_gain(qk_gain[1], N_HEADS),
                            jnp.zeros((hw,), F32)])
    tn = 512
    place, bias = _fox_aug_tables()
    qkv = _rms_proj(x, attn_g, w_all, out_dtype=BF16, seq=seq, tn=tn, gain=gain,
                    norm_blocks=tuple(range(2 * hw // tn)), placed=(c3, place), col_bias=bias)
    o = _fox_attn(qkv, batch=batch, seq=seq)
    return _matmul_res(o, w_out.astype(BF16), x)


def _nsa_shared_kv(x, batch, seq, kv_norm, kv_w, kv_k_gain, cmp_pos, cmp_w1, cmp_w2):
    g = NSA_GROUPS
    gw = g * HEAD_DIM
    tabs = _rope_tables(jnp.arange(seq))
    w_kvd = jnp.concatenate([_pad_heads(kv_w[:, j * gw:(j + 1) * gw]) for j in (2, 3, 4, 5)], axis=1).astype(BF16)
    zero = jnp.zeros((g * LANES,), F32)
    one_lane = jnp.tile((jnp.arange(LANES) == HEAD_DIM).astype(F32), g)
    gain_kvd = jnp.concatenate([_tile_gain(kv_k_gain[1], g), zero, _tile_gain(kv_k_gain[2], g), zero])
    bias_kvd = jnp.concatenate([zero, one_lane, zero, one_lane])
    kvd = _rms_proj(x, kv_norm, w_kvd, out_dtype=BF16, seq=seq, tn=g * LANES, gain=gain_kvd,
                    norm_blocks=(0, 2), rope_tabs=tabs, col_bias=bias_kvd)
    zc = _rms_proj(x, kv_norm, kv_w[:, :2 * gw].astype(BF16), out_dtype=F32, seq=seq)
    rows = seq // CMP_STRIDE
    zr = zc.reshape(batch, rows, CMP_STRIDE, 2, g, HEAD_DIM).transpose(0, 3, 4, 1, 2, 5)
    zr = zr.reshape(batch, 2, g, rows, CMP_STRIDE * HEAD_DIM)
    cmp_end = jnp.arange(rows) * CMP_STRIDE + CMP_BLOCK - 1
    ctabs = _rope_tables(cmp_end)
    w2dup = jnp.concatenate([cmp_w2, cmp_w2], axis=-1).astype(BF16)
    kvc = _compress(zr, cmp_w1.astype(BF16), cmp_pos.reshape(2, 1, CMP_BLOCK * HEAD_DIM).astype(F32), w2dup,
                    _tile_gain(kv_k_gain[0], 1, dup=True).reshape(1, LANES), ctabs, batch=batch)
    return kvc, kvd


def _nsa_layer(x, batch, seq, attn_g, w_in, q_gain, w_out, kvc, kvd):
    hd = N_HEADS * HEAD_DIM
    scale = HEAD_DIM ** -0.5
    rows = seq // CMP_STRIDE
    tabs = _rope_tables(jnp.arange(seq))
    qn =_rms_proj(x, attn_g, _pad_heads(w_in[:, :hd]).astype(BF16), out_dtype=BF16, seq=seq,
                   gain=_tile_gain(q_gain, N_HEADS, scale * LOG2E), norm_blocks=tuple(range(N_HEADS * LANES // 512)),
                   rope_tabs=tabs)
    gz = _rms_proj(x, attn_g, _pad_cols(w_in[:, hd:]).astype(BF16), out_dtype=F32, seq=seq)

    n_sel = seq // SEL_BLOCK
    assert n_sel <= LANES
    cmp_start = jnp.arange(rows) * CMP_STRIDE
    sel_start = jnp.arange(LANES) * SEL_BLOCK
    overlap = jnp.clip(jnp.minimum(cmp_start[:, None] + CMP_BLOCK, sel_start[None, :] + SEL_BLOCK)
                       - jnp.maximum(cmp_start[:, None], sel_start[None, :]), 0)
    overlap = jnp.where((jnp.arange(rows) < rows - 1)[:, None] & (jnp.arange(LANES) < n_sel)[None, :], overlap, 0)
    o = _nsa_attn(qn, gz, kvc, overlap.astype(BF16), kvd, batch=batch, seq=seq)
    return _matmul_res(o, w_out.astype(BF16), x)


def kernel(x, p, attn_norm, ffn_norm, ple_norm, ple_gate_w, ple_proj_w, router_w, router_b, w_gu, b_gu, w_down, b_down, fox_w_in, fox_b_f, fox_qk_gain, fox_w_out, kv_norm, kv_w, kv_k_gain, cmp_pos, cmp_w1, cmp_w2, nsa_w_in, nsa_q_gain, nsa_w_out):
    batch, seq, d = x.shape
    depth = p.shape[0]
    n_a = fox_w_in.shape[0]
    xt = x.reshape(batch * seq, d)
    shared = None
    for layer in range(depth):
        if layer == n_a:
            shared = _nsa_shared_kv(xt, batch, seq, kv_norm, kv_w, kv_k_gain, cmp_pos, cmp_w1, cmp_w2)
        if layer < n_a:
            xt = _fox_layer(xt, batch, seq, attn_norm[layer], fox_w_in[layer], fox_b_f[layer],
                            fox_qk_gain[layer], fox_w_out[layer])
        else:
            i = layer - n_a
            xt = _nsa_layer(xt, batch, seq, attn_norm[layer], nsa_w_in[i], nsa_q_gain[i], nsa_w_out[i], *shared)
        y = _moe_layer(xt, ffn_norm[layer], router_w[layer], router_b[layer], w_gu[layer], b_gu[layer],
                       w_down[layer], b_down[layer])
        xt = _ple(xt, y, ple_norm[layer], ple_gate_w[layer], p[layer].reshape(batch * seq, -1), ple_proj_w[layer])
    return xt.reshape(batch, seq, d)
```

```python
import functools

import jax
import jax.numpy as jnp
from jax import lax
from jax.experimental import pallas as pl
from jax.experimental.pallas import tpu as pltpu

F32 = jnp.float32
BF16 = jnp.bfloat16

N_HEADS = 16
HEAD_DIM = 64
ROPE_DIM = HEAD_DIM // 4
ROPE_THETA = 500000.0
NSA_GROUPS = 4
HEADS_PER_GROUP = N_HEADS // NSA_GROUPS
CMP_BLOCK = 32
CMP_STRIDE = 16
SEL_BLOCK = 64
SEL_TOPK = 16
WINDOW = 512
N_EXPERTS = 32
TOP_K = 4
SWIGLU_LIMIT = 7.0
SWIGLU_ALPHA = 1.702
EPS = 1e-6
NEG = -1e30
FORCE = 1e6
LOG2E = 1.4426950408889634

LANES = 128
NSA_QB = 128
MOE_BM = 256
VMEM_LIMIT = 56 * 1024 * 1024


def _cparams(sem):
    return pltpu.CompilerParams(dimension_semantics=sem, vmem_limit_bytes=VMEM_LIMIT)


def _sigmoid(x):
    return 1.0 / (1.0 + jnp.exp(-x))


def _dot_nt(a, b):
    return lax.dot_general(a, b, (((1,), (1,)), ((), ())), preferred_element_type=F32)


def _split2(x):
    hi = x.astype(BF16)
    lo = (x - hi.astype(F32)).astype(BF16)
    return hi, lo


def _proj_kernel(*refs, norm_blocks, rope, placed, biased, expand, head_div, tn, out_dtype):
    refs = list(refs)
    x_ref, g_ref, w_ref = refs[:3]
    del refs[:3]
    if norm_blocks:
        gain_ref = refs.pop(0)
    if rope:
        c_ref, s1_ref, s2_ref = refs[:3]
        del refs[:3]
    if placed:
        ex_ref, place_ref = refs[:2]
        del refs[:2]
    if biased:
        bias_ref = refs.pop(0)
    o_ref, h_ref = refs
    j = pl.program_id(1)
    aug = placed or biased

    @pl.when(j == 0)
    def _():
        xf = x_ref[...]
        ms = jnp.mean(xf * xf, axis=-1, keepdims=True)
        h_ref[...] = (xf * lax.rsqrt(ms + EPS) * g_ref[...]).astype(BF16)

    acc = jnp.dot(h_ref[...], w_ref[...], preferred_element_type=F32)
    if placed and biased:
        extra = jnp.dot(ex_ref[...], place_ref[...], preferred_element_type=F32) + bias_ref[...]
    elif biased:
        extra = jnp.broadcast_to(bias_ref[...], (acc.shape[0], tn))
    if expand:
        lo = lax.broadcasted_iota(jnp.int32, (acc.shape[0], LANES), 1) < HEAD_DIM
        blocks = []
        for c in range(acc.shape[1] // LANES):
            a = acc[:, c * LANES:(c + 1) * LANES]
            blocks += [jnp.where(lo, a, 0.0), jnp.where(lo, pltpu.roll(a, HEAD_DIM, 1), 0.0)]
    else:
        blocks = [acc[:, c * LANES:(c + 1) * LANES] for c in range(acc.shape[1] // LANES)]

    def plain():
        for c, a in enumerate(blocks):
            sl = slice(c * LANES, (c + 1) * LANES)
            o_ref[:, sl] = ((a + extra[:, sl]) if aug else a).astype(out_dtype)

    if not norm_blocks:
        plain()
        return
    is_norm = functools.reduce(jnp.logical_or, [j == c for c in norm_blocks])

    @pl.when(is_norm)
    def _():
        for c, a in enumerate(blocks):
            sl = slice(c * LANES, (c + 1) * LANES)
            ss = jnp.sum(a * a, axis=-1, keepdims=True) * (1.0 / head_div)
            y = a * lax.rsqrt(ss + EPS) * gain_ref[:, sl]
            if rope:
                y = (y * c_ref[...] + pltpu.roll(y, LANES - ROPE_DIM // 2, 1) * s1_ref[...]
                     + pltpu.roll(y, ROPE_DIM // 2, 1) * s2_ref[...])
            if aug:
                y = y + extra[:, sl]
            o_ref[:, sl] = y.astype(out_dtype)

    @pl.when(jnp.logical_not(is_norm))
    def _():
        plain()


def _rms_proj(x, g, w, *, out_dtype, seq, tm=512, tn=512, gain=None, norm_blocks=(),
              rope_tabs=None, placed=None, col_bias=None, expand=False, head_div=HEAD_DIM):
    n, d = x.shape
    f = w.shape[1] * (2 if expand else 1)
    tn = min(tn, f)
    tw = tn // 2 if expand else tn
    tm = min(tm, seq)
    assert n % tm == 0 and f % tn == 0 and seq % tm == 0
    nt = seq // tm
    in_specs = [
        pl.BlockSpec((tm, d), lambda i, j: (i, 0)),
        pl.BlockSpec((1, d), lambda i, j: (0, 0)),
        pl.BlockSpec((d, tw), lambda i, j: (0, j)),
    ]
    args = [x, g.reshape(1, d).astype(F32), w]
    if norm_blocks:
        in_specs.append(pl.BlockSpec((1, tn), lambda i, j: (0, j)))
        args.append(gain.reshape(1, f).astype(F32))
    if rope_tabs is not None:
        assert norm_blocks
        for tab in rope_tabs:
            in_specs.append(pl.BlockSpec((tm, LANES), lambda i, j: (i % nt, 0)))
            args.append(tab)
    if placed is not None:
        assert col_bias is not None
        values, place = placed
        in_specs += [pl.BlockSpec((tm, LANES), lambda i, j: (i, 0)),
                     pl.BlockSpec((LANES, tn), lambda i, j: (0, j))]
        args += [values, place]
    if col_bias is not None:
        in_specs.append(pl.BlockSpec((1, tn), lambda i, j: (0, j)))
        args.append(col_bias.reshape(1, f).astype(F32))
    kern = functools.partial(_proj_kernel, norm_blocks=tuple(norm_blocks), rope=rope_tabs is not None,
                             placed=placed is not None, biased=col_bias is not None, expand=expand,
                             head_div=float(head_div), tn=tn, out_dtype=out_dtype)
    return pl.pallas_call(
        kern,
        out_shape=jax.ShapeDtypeStruct((n, f), out_dtype),
        grid=(n // tm, f // tn),
        in_specs=in_specs,
        out_specs=pl.BlockSpec((tm, tn), lambda i, j: (i, j)),
        scratch_shapes=[pltpu.VMEM((tm, d), BF16)],
        compiler_params=_cparams(("parallel", "arbitrary")),
        name="rms_proj",
    )(*args)


def _rope_tables(pos, width=LANES):
    half = ROPE_DIM // 2
    inv = jnp.power(jnp.float32(ROPE_THETA), -jnp.arange(0, ROPE_DIM, 2, dtype=F32) / ROPE_DIM)
    ang = pos.astype(F32)[:, None] * inv[None, :]
    cos, sin = jnp.cos(ang), jnp.sin(ang)
    t = pos.shape[0]
    ones = jnp.ones((t, HEAD_DIM - ROPE_DIM), F32)
    zeros_h = jnp.zeros((t, half), F32)
    zeros_r = jnp.zeros((t, HEAD_DIM - ROPE_DIM), F32)
    c = jnp.concatenate([cos, cos, ones], axis=1)
    s1 = jnp.concatenate([-sin, zeros_h, zeros_r], axis=1)
    s2 = jnp.concatenate([zeros_h, sin, zeros_r], axis=1)
    rep = width // HEAD_DIM
    return tuple(jnp.tile(a, (1, rep)) for a in (c, s1, s2))


def _matmul_res_kernel(a_ref, w_ref, x_ref, o_ref):
    o_ref[...] = x_ref[...] + jnp.dot(a_ref[...], w_ref[...], preferred_element_type=F32)


def _matmul_res(a, w, x, *, tm=512):
    n, k = a.shape
    f = w.shape[1]
    tm = min(tm, n)
    return pl.pallas_call(
        _matmul_res_kernel,
        out_shape=jax.ShapeDtypeStruct((n, f), F32),
        grid=(n // tm,),
        in_specs=[
            pl.BlockSpec((tm, k), lambda i: (i, 0)),
            pl.BlockSpec((k, f), lambda i: (0, 0)),
            pl.BlockSpec((tm, f), lambda i: (i, 0)),
        ],
        out_specs=pl.BlockSpec((tm, f), lambda i: (i, 0)),
        compiler_params=_cparams(("parallel",)),
        name="matmul_res",
    )(a, w, x)


def _fox_c_kernel(z_ref, b_ref, c_ref, carry_ref, *, tm):
    t = pl.program_id(1)

    @pl.when(t == 0)
    def _():
        carry_ref[...] = jnp.zeros_like(carry_ref)

    u = z_ref[...] + b_ref[...]
    logf = jnp.minimum(u, 0.0) - jnp.log(1.0 + jnp.exp(-jnp.abs(u)))
    row = lax.broadcasted_iota(jnp.int32, (tm, tm), 0)
    col = lax.broadcasted_iota(jnp.int32, (tm, tm), 1)
    tri = (row >= col).astype(BF16)
    hi = logf.astype(BF16)
    r1 = logf - hi.astype(F32)
    mid = r1.astype(BF16)
    lo = (r1 - mid.astype(F32)).astype(BF16)
    cs = (jnp.dot(tri, hi, preferred_element_type=F32) + jnp.dot(tri, mid, preferred_element_type=F32)
          + jnp.dot(tri, lo, preferred_element_type=F32))
    c = cs + carry_ref[...]
    carry_ref[...] = c[tm - 1:tm, :]
    c2 = c * LOG2E
    p0 = c2.astype(BF16).astype(F32)
    p1 = (c2 - p0).astype(BF16).astype(F32)
    p2 = ((c2 - p0) - p1).astype(BF16).astype(F32)
    lane = lax.broadcasted_iota(jnp.int32, c.shape, 1)
    out = jnp.where(lane < N_HEADS, p0,
                    jnp.where(lane < 2 * N_HEADS, pltpu.roll(p1, N_HEADS, 1),
                              jnp.where(lane < 3 * N_HEADS, pltpu.roll(p2, 2 * N_HEADS, 1), 0.0)))
    c_ref[...] = out.astype(BF16)


def _fox_c(z, b_pad, *, batch, seq, tm=512):
    tm = min(tm, seq)
    nt = seq // tm
    return pl.pallas_call(
        functools.partial(_fox_c_kernel, tm=tm),
        out_shape=jax.ShapeDtypeStruct(z.shape, BF16),
        grid=(batch, nt),
        in_specs=[
            pl.BlockSpec((tm, LANES), lambda b, t: (b * nt + t, 0)),
            pl.BlockSpec((1, LANES), lambda b, t: (0, 0)),
        ],
        out_specs=pl.BlockSpec((tm, LANES), lambda b, t: (b * nt + t, 0)),
        scratch_shapes=[pltpu.VMEM((1, LANES), F32)],
        compiler_params=_cparams(("parallel", "arbitrary")),
        name="fox_cumsum",
    )(z, b_pad)


def _attend(qs, ks, vs, state, skew, mask=None):
    items = [(t, c) for t in range(len(ks)) for c in range(len(qs))]
    state = list(state)
    pending = {}
    depth = min(skew, len(items) - 1)
    for idx in range(len(items) + depth):
        if idx < len(items):
            t, c = items[idx]
            sc = _dot_nt(qs[c], ks[t][c])
            pending[idx] = sc if mask is None else mask(t, c, sc)
        if idx >= depth:
            t, c = items[idx - depth]
            sc = pending.pop(idx - depth)
            m_prev, acc = state[c]
            m_new = jnp.maximum(m_prev, jnp.max(sc, axis=-1, keepdims=True))
            pe = jnp.exp2(sc - m_new)
            acc = jnp.exp2(m_prev - m_new) * acc + jnp.dot(pe.astype(BF16), vs[t][c], preferred_element_type=F32)
            state[c] = (m_new, acc)
    return tuple(state)


def _tile_loops(n_tiles, subs, step, state):
    done = 0
    for sub in subs:
        n_here = (n_tiles - done) // sub
        state = lax.fori_loop(0, n_here, lambda j, st, done=done, sub=sub: step(done + j * sub, sub, st), state)
        done = done + n_here * sub
    return state


def _fox_attn_kernel(q0_ref, q1_ref, k0_ref, k1_ref, v0_ref, v1_ref, o_ref, *, tq, tk, skew, subs):
    qi = pl.program_id(2)
    k_refs = (k0_ref, k1_ref)
    v_refs = (v0_ref, v1_ref)
    qs = [q0_ref[...], q1_ref[...]]

    def tiles(refs, first, n_sub):
        return [[r[pl.ds(pl.multiple_of((first + t) * tk, tk), tk), :] for r in refs] for t in range(n_sub)]

    def full_step(first, n_sub, state):
        return _attend(qs, tiles(k_refs, first, n_sub), tiles(v_refs, first, n_sub), state, skew)

    def causal(t, c, sc):
        tpos = lax.broadcasted_iota(jnp.int32, (tq, tk), 0)
        kpos = t * tk + lax.broadcasted_iota(jnp.int32, (tq, tk), 1)
        return jnp.where(kpos <= tpos, sc, NEG)

    state = tuple((jnp.full((tq, 1), NEG, F32), jnp.zeros((tq, LANES), F32)) for _ in qs)
    n_below = (qi * tq) // tk
    state = _tile_loops(n_below, subs, full_step, state)
    n_diag = tq // tk
    state = _attend(qs, tiles(k_refs, n_below, n_diag), tiles(v_refs, n_below, n_diag), state, skew, mask=causal)

    lane = lax.broadcasted_iota(jnp.int32, (tq, LANES), 1)
    outs = []
    for _, acc in state:
        den = jnp.sum(jnp.where(lane == HEAD_DIM, acc, 0.0), axis=-1, keepdims=True)
        outs.append(acc / den)
    o_ref[...] = jnp.where(lane < HEAD_DIM, outs[0], pltpu.roll(outs[1], HEAD_DIM, 1)).astype(o_ref.dtype)


def _fox_attn(qkv, *, batch, seq, tq=512, tk=512, skew=2, subs=(4, 2, 1)):
    n = qkv.shape[0]
    tq = min(tq, seq)
    tk = min(tk, tq)
    assert seq % tq == 0 and tq % tk == 0
    nq = seq // tq
    hp = N_HEADS // 2
    kb = N_HEADS
    vb = 2 * N_HEADS
    in_specs = [
        pl.BlockSpec((tq, LANES), lambda b, h, i: (b * nq + i, 2 * h)),
        pl.BlockSpec((tq, LANES), lambda b, h, i: (b * nq + i, 2 * h + 1)),
        pl.BlockSpec((seq, LANES), lambda b, h, i: (b, kb + 2 * h)),
        pl.BlockSpec((seq, LANES), lambda b, h, i: (b, kb + 2 * h + 1)),
        pl.BlockSpec((seq, LANES), lambda b, h, i: (b, vb + 2 * h)),
        pl.BlockSpec((seq, LANES), lambda b, h, i: (b, vb + 2 * h + 1)),
    ]
    return pl.pallas_call(
        functools.partial(_fox_attn_kernel, tq=tq, tk=tk, skew=skew, subs=subs),
        out_shape=jax.ShapeDtypeStruct((n, N_HEADS * HEAD_DIM), BF16),
        grid=(batch, hp, nq),
        in_specs=in_specs,
        out_specs=pl.BlockSpec((tq, LANES), lambda b, h, i: (b * nq + i, h)),
        compiler_params=_cparams(("parallel", "parallel", "arbitrary")),
        name="fox_attn",
    )(qkv, qkv, qkv, qkv, qkv, qkv)


def _router_kernel(x_ref, g_ref, whi_ref, wlo_ref, b_ref, h_ref, idx_ref, gate_ref, cnt_ref):
    xf = x_ref[...]
    ms = jnp.mean(xf * xf, axis=-1, keepdims=True)
    h = xf * lax.rsqrt(ms + EPS) * g_ref[...]
    h_ref[...] = h
    hhi, hlo = _split2(h)
    whi = whi_ref[...]
    logits = (jnp.dot(hhi, whi, preferred_element_type=F32) + jnp.dot(hlo, whi, preferred_element_type=F32)
              + jnp.dot(hhi, wlo_ref[...], preferred_element_type=F32)) + b_ref[...]
    lane = lax.broadcasted_iota(jnp.int32, logits.shape, 1).astype(F32)
    vals, idxs = [], []
    cur = logits
    for _ in range(TOP_K):
        mx = jnp.max(cur, axis=-1, keepdims=True)
        ix = jnp.min(jnp.where(cur == mx, lane, float(LANES)), axis=-1, keepdims=True)
        vals.append(mx)
        idxs.append(ix)
        cur = jnp.where(lane == ix, -jnp.inf, cur)
    es = [jnp.exp(v - vals[0]) for v in vals]
    tot = es[0] + es[1] + es[2] + es[3]
    idx_out = jnp.zeros(logits.shape, F32)
    gate_out = jnp.zeros(logits.shape, F32)
    for r in range(TOP_K):
        idx_out = jnp.where(lane == r, idxs[r], idx_out)
        gate_out = jnp.where(lane == r, es[r] / tot, gate_out)
    idx_ref[...] = idx_out.astype(jnp.int32)
    gate_ref[...] = gate_out
    hits = sum(jnp.sum((lane == ix).astype(F32), axis=0, keepdims=True) for ix in idxs)

    @pl.when(pl.program_id(0) == 0)
    def _():
        cnt_ref[...] = jnp.zeros(cnt_ref.shape, cnt_ref.dtype)

    cnt_ref[...] += jnp.broadcast_to(hits, cnt_ref.shape)


def _router(x, g, w, b, *, tm=512):
    n, d = x.shape
    e = w.shape[1]
    tm = min(tm, n)
    w_pad = jnp.zeros((d, LANES), F32).at[:, :e].set(w)
    whi = w_pad.astype(BF16)
    wlo = (w_pad - whi.astype(F32)).astype(BF16)
    b_pad = jnp.full((1, LANES), -jnp.inf, F32).at[0, :e].set(b.astype(F32))
    return pl.pallas_call(
        _router_kernel,
        out_shape=(jax.ShapeDtypeStruct((n, d), F32), jax.ShapeDtypeStruct((n, LANES), jnp.int32),
                   jax.ShapeDtypeStruct((n, LANES), F32), jax.ShapeDtypeStruct((8, LANES), F32)),
        grid=(n // tm,),
        in_specs=[
            pl.BlockSpec((tm, d), lambda i: (i, 0)),
            pl.BlockSpec((1, d), lambda i: (0, 0)),
            pl.BlockSpec((d, LANES), lambda i: (0, 0)),
            pl.BlockSpec((d, LANES), lambda i: (0, 0)),
            pl.BlockSpec((1, LANES), lambda i: (0, 0)),
        ],
        out_specs=(pl.BlockSpec((tm, d), lambda i: (i, 0)), pl.BlockSpec((tm, LANES), lambda i: (i, 0)),
                   pl.BlockSpec((tm, LANES), lambda i: (i, 0)), pl.BlockSpec((8, LANES), lambda i: (0, 0))),
        compiler_params=_cparams(("arbitrary",)),
        name="moe_router",
    )(x, g.reshape(1, d).astype(F32), whi, wlo, b_pad)


def _moe_expert_kernel(blk_exp_ref, tok_ref, dst_ref, h_hbm, ga_ref, gb_ref, wgu_a, bgu_a, wd_a, bd_a,
                       wgu_b, bgu_b, wd_b, bd_b, y_hbm, xa, xb, oa, ob, gsem, ssem, *, bm, dff, n_steps):
    i = pl.program_id(0)
    dump = y_hbm.shape[0] - 2 * bm

    def gather(block, buf, sem):
        base = block * bm
        for r in range(bm):
            pltpu.make_async_copy(h_hbm.at[pl.ds(tok_ref[base + r], 1)], buf.at[pl.ds(r, 1)], sem).start()

    def scatter(block, buf, sem):
        base = block * bm
        for r in range(bm):
            pltpu.make_async_copy(buf.at[pl.ds(r, 1)], y_hbm.at[pl.ds(dst_ref[base + r], 1)], sem).start()

    def wait_gather(buf, sem):
        pltpu.make_async_copy(h_hbm.at[pl.ds(0, bm)], buf, sem).wait()

    def wait_scatter(buf, sem):
        pltpu.make_async_copy(buf, y_hbm.at[pl.ds(0, bm)], sem).wait()

    def experts(x, g_ref, wgu, bgu, wd, bd):
        gu = jnp.dot(x.astype(BF16), wgu[0], preferred_element_type=F32) + bgu[0]
        a = jnp.minimum(gu[:, :dff], SWIGLU_LIMIT)
        u = jnp.clip(gu[:, dff:], -SWIGLU_LIMIT, SWIGLU_LIMIT)
        y = (u + 1.0) * (a * _sigmoid(SWIGLU_ALPHA * a))
        return (jnp.dot(y.astype(BF16), wd[0], preferred_element_type=F32) + bd[0]) * g_ref[...]

    @pl.when(i == 0)
    def _():
        gather(0, xa, gsem.at[0])
        for buf, sem, base in ((oa, ssem.at[0], dump), (ob, ssem.at[1], dump + bm)):
            buf[...] = jnp.zeros(buf.shape, buf.dtype)
            for r in range(bm):
                pltpu.make_async_copy(buf.at[pl.ds(r, 1)], y_hbm.at[pl.ds(base + r, 1)], sem).start()

    blk_a = 2 * i
    blk_b = 2 * i + 1
    nxt_a = jnp.minimum(2 * i + 2, 2 * n_steps - 1)

    gather(blk_b, xb, gsem.at[1])
    wait_gather(xa, gsem.at[0])
    wait_scatter(oa, ssem.at[0])
    oa[...] = experts(xa[...], ga_ref, wgu_a, bgu_a, wd_a, bd_a)
    scatter(blk_a, oa, ssem.at[0])

    gather(nxt_a, xa, gsem.at[0])
    wait_gather(xb, gsem.at[1])
    wait_scatter(ob, ssem.at[1])
    ob[...] = experts(xb[...], gb_ref, wgu_b, bgu_b, wd_b, bd_b)
    scatter(blk_b, ob, ssem.at[1])

    @pl.when(i == n_steps - 1)
    def _():
        wait_gather(xa, gsem.at[0])
        wait_scatter(oa, ssem.at[0])
        wait_scatter(ob, ssem.at[1])


def _moe_experts(h, blk_exp, tok, dst, gates, w_gu, b_gu, w_down, b_down, *, bm):
    n, d = h.shape
    e, _, f2 = w_gu.shape
    dff = f2 // 2
    cap = tok.shape[0]
    n_steps = cap // (2 * bm)
    assert cap == n_steps * 2 * bm

    def wspecs(off):
        return [
            pl.BlockSpec((1, d, f2), lambda i, be, tk, ds: (be[2 * i + off], 0, 0)),
            pl.BlockSpec((1, 1, f2), lambda i, be, tk, ds: (be[2 * i + off], 0, 0)),
            pl.BlockSpec((1, dff, d), lambda i, be, tk, ds: (be[2 * i + off], 0, 0)),
            pl.BlockSpec((1, 1, d), lambda i, be, tk, ds: (be[2 * i + off], 0, 0)),
        ]

    grid_spec = pltpu.PrefetchScalarGridSpec(
        num_scalar_prefetch=3,
        grid=(n_steps,),
        in_specs=[
            pl.BlockSpec(memory_space=pl.ANY),
            pl.BlockSpec((bm, 1), lambda i, be, tk, ds: (2 * i, 0)),
            pl.BlockSpec((bm, 1), lambda i, be, tk, ds: (2 * i + 1, 0)),
        ] + wspecs(0) + wspecs(1),
        out_specs=pl.BlockSpec(memory_space=pl.ANY),
        scratch_shapes=[pltpu.VMEM((bm, d), F32), pltpu.VMEM((bm, d), F32), pltpu.VMEM((bm, d), F32),
                        pltpu.VMEM((bm, d), F32), pltpu.SemaphoreType.DMA((2,)), pltpu.SemaphoreType.DMA((2,))],
    )
    bgu3 = b_gu.reshape(e, 1, f2).astype(F32)
    bd3 = b_down.reshape(e, 1, d).astype(F32)
    g2 = gates.reshape(cap, 1)
    return pl.pallas_call(
        functools.partial(_moe_expert_kernel, bm=bm, dff=dff, n_steps=n_steps),
        out_shape=jax.ShapeDtypeStruct((TOP_K * n + 2 * bm, d), F32),
        grid_spec=grid_spec,
        compiler_params=_cparams(("arbitrary",)),
        name="moe_experts",
    )(blk_exp, tok, dst, h, g2, g2, w_gu, bgu3, w_down, bd3, w_gu, bgu3, w_down, bd3)


def _moe_layer(x, norm_g, router_w, router_b, w_gu, b_gu, w_down, b_down):
    n, d = x.shape
    bm = MOE_BM
    h, idx128, gate128, cnt = _router(x, norm_g, router_w, router_b)
    n_assign = n * TOP_K
    flat_e = idx128[:, :TOP_K].reshape(-1)
    flat_g = gate128[:, :TOP_K].reshape(-1)
    order = jnp.argsort(flat_e).astype(jnp.int32)
    counts = cnt[0, :N_EXPERTS].astype(jnp.int32)
    padded = (counts + bm - 1) // bm * bm
    start = jnp.cumsum(counts) - counts
    pend = jnp.cumsum(padded)
    pstart = pend - padded
    n_blocks = -(-n_assign // bm) + N_EXPERTS
    n_blocks += n_blocks % 2
    blk_start = jnp.arange(n_blocks, dtype=jnp.int32) * bm
    blk_exp = jnp.minimum(jnp.sum((pend[None, :] <= blk_start[:, None]).astype(jnp.int32), axis=1),
                          N_EXPERTS - 1).astype(jnp.int32)
    row = jnp.arange(bm, dtype=jnp.int32)[None, :]
    off = (blk_start - pstart[blk_exp])[:, None] + row
    valid = off < counts[blk_exp][:, None]
    a = order[jnp.clip(start[blk_exp][:, None] + off, 0, n_assign - 1).reshape(-1)].reshape(n_blocks, bm)
    tok = jnp.where(valid, a // TOP_K, 0).astype(jnp.int32).reshape(-1)
    dump_row = TOP_K * n + (jnp.arange(n_blocks, dtype=jnp.int32) % 2)[:, None] * bm + row
    dst = jnp.where(valid, (a % TOP_K) * n + a // TOP_K, dump_row).astype(jnp.int32).reshape(-1)
    gates = jnp.where(valid, flat_g[a.reshape(-1)].reshape(n_blocks, bm), 0.0).reshape(-1)
    return _moe_experts(h, blk_exp, tok, dst, gates, w_gu.astype(BF16), b_gu, w_down.astype(BF16), b_down, bm=bm)


def _ple_kernel(x_ref, y0_ref, y1_ref, y2_ref, y3_ref, g_ref, wg_ref, p_ref, wp_ref, o_ref):
    xf = x_ref[...] + (((y0_ref[...] + y1_ref[...]) + y2_ref[...]) + y3_ref[...])
    ms = jnp.mean(xf * xf, axis=-1, keepdims=True)
    h = (xf * lax.rsqrt(ms + EPS) * g_ref[...]).astype(BF16)
    gate = _sigmoid(jnp.dot(h, wg_ref[...], preferred_element_type=F32))
    pp = jnp.dot(p_ref[...].astype(BF16), wp_ref[...], preferred_element_type=F32)
    o_ref[...] = xf + gate * pp


def _ple(x, y, g, wg, p, wp, *, tm=512):
    n, d = x.shape
    pd = p.shape[1]
    tm = min(tm, n)
    nb = n // tm
    yspecs = [pl.BlockSpec((tm, d), functools.partial(lambda i, k: (k * nb + i, 0), k=k)) for k in range(TOP_K)]
    return pl.pallas_call(
        _ple_kernel,
        out_shape=jax.ShapeDtypeStruct((n, d), F32),
        grid=(nb,),
        in_specs=[pl.BlockSpec((tm, d), lambda i: (i, 0))] + yspecs + [
            pl.BlockSpec((1, d), lambda i: (0, 0)),
            pl.BlockSpec((d, d), lambda i: (0, 0)),
            pl.BlockSpec((tm, pd), lambda i: (i, 0)),
            pl.BlockSpec((pd, d), lambda i: (0, 0)),
        ],
        out_specs=pl.BlockSpec((tm, d), lambda i: (i, 0)),
        compiler_params=_cparams(("parallel",)),
        name="ple",
    )(x, y, y, y, y, g.reshape(1, d).astype(F32), wg.astype(BF16), p, wp.astype(BF16))


def _cmp_kernel(z_ref, w1_ref, pos_ref, w2_ref, gain_ref, c_ref, s1_ref, s2_ref, o_ref, *, rows, half):
    j = pl.program_id(1)
    z = z_ref[0, 0, 0]
    pos = pos_ref[0]
    a1 = (z + pos[:, :half]).astype(BF16)
    a2 = (z + pos[:, half:]).astype(BF16)
    u1 = jnp.dot(a1, w1_ref[0, :half, :], preferred_element_type=F32)
    u2 = jnp.dot(a2, w1_ref[0, half:, :], preferred_element_type=F32)
    h = u1 + pltpu.roll(u2, rows - 1, 0)
    hs = h * _sigmoid(h)
    y = jnp.dot(hs.astype(BF16), w2_ref[0], preferred_element_type=F32)

    @pl.when(j == 0)
    def _():
        ss = jnp.sum(y * y, axis=-1, keepdims=True) * (1.0 / LANES)
        yn = y * lax.rsqrt(ss + EPS) * gain_ref[...]
        yn = (yn * c_ref[...] + pltpu.roll(yn, LANES - ROPE_DIM // 2, 1) * s1_ref[...]
              + pltpu.roll(yn, ROPE_DIM // 2, 1) * s2_ref[...])
        o_ref[0, 0, 0] = yn.astype(o_ref.dtype)

    @pl.when(j != 0)
    def _():
        o_ref[0, 0, 0] = y.astype(o_ref.dtype)


def _compress(zr, w1, pos, w2dup, gain_dup, tabs, *, batch):
    rows, width = zr.shape[3], zr.shape[4]
    hid = w1.shape[2]
    g = NSA_GROUPS
    return pl.pallas_call(
        functools.partial(_cmp_kernel, rows=rows, half=width),
        out_shape=jax.ShapeDtypeStruct((batch, 2, g, rows, LANES), BF16),
        grid=(batch, 2, g),
        in_specs=[
            pl.BlockSpec((1, 1, 1, rows, width), lambda b, j, gg: (b, j, gg, 0, 0)),
            pl.BlockSpec((1, 2 * width, hid), lambda b, j, gg: (j, 0, 0)),
            pl.BlockSpec((1, 1, 2 * width), lambda b, j, gg: (j, 0, 0)),
            pl.BlockSpec((1, hid, LANES), lambda b, j, gg: (j, 0, 0)),
            pl.BlockSpec((1, LANES), lambda b, j, gg: (0, 0)),
            pl.BlockSpec((rows, LANES), lambda b, j, gg: (0, 0)),
            pl.BlockSpec((rows, LANES), lambda b, j, gg: (0, 0)),
            pl.BlockSpec((rows, LANES), lambda b, j, gg: (0, 0)),
        ],
        out_specs=pl.BlockSpec((1, 1, 1, rows, LANES), lambda b, j, gg: (b, j, gg, 0, 0)),
        compiler_params=_cparams(("parallel", "parallel", "parallel")),
        name="nsa_compress",
    )(zr, w1, pos, w2dup, gain_dup, *tabs)


def _pipelined(n_items, first, second, skew):
    pending, out = {}, [None] * n_items
    for idx in range(n_items + skew):
        if idx < n_items:
            pending[idx] = first(idx)
        if idx >= skew:
            out[idx - skew] = second(idx - skew, pending.pop(idx - skew))
    return out


def _nsa_attn_kernel(q_ref, gz_ref, kc_ref, vct_ref, ovt_ref, ks_ref, vs_ref, kw_ref, vw_ref, o_ref,
                     e_ref, oc_ref, *, n_cmp_rows, tk_sel, skew, grp, subs):
    g = pl.program_id(1)
    i = pl.program_id(2)
    qb = NSA_QB
    mh = HEADS_PER_GROUP
    rows = mh * qb
    q0 = i * qb
    seq = ks_ref.shape[0]

    @pl.when(i == 0)
    def _():
        r = lax.broadcasted_iota(jnp.int32, (seq, LANES), 0)
        lb = lax.broadcasted_iota(jnp.int32, (seq, LANES), 1)
        e_ref[...] = jnp.where((r >> 6) == lb, NEG, 0.0).astype(BF16)

    qh = [q_ref[:, m * LANES:(m + 1) * LANES] for m in range(mh)]
    q4 = jnp.concatenate(qh, axis=0)
    jl = lax.broadcasted_iota(jnp.int32, (qb, LANES), 1)

    st = _dot_nt(kc_ref[0, 0, 0], q4)
    cend = lax.broadcasted_iota(jnp.int32, (n_cmp_rows, rows), 0) * CMP_STRIDE + (CMP_BLOCK - 1)
    tlane = q0 + (lax.broadcasted_iota(jnp.int32, (1, rows), 1) & (qb - 1))
    valid = cend <= tlane
    st = jnp.where(valid, st, NEG)
    e = jnp.where(valid, jnp.exp2(st - jnp.max(st, axis=0, keepdims=True)), 0.0)
    pt = e / jnp.maximum(jnp.sum(e, axis=0, keepdims=True), 1e-30)
    oct = jnp.dot(vct_ref[0, 0], pt.astype(BF16), preferred_element_type=F32)
    for m in range(mh):
        oc_ref[m * qb:(m + 1) * qb, :] = oct[:, m * qb:(m + 1) * qb].T
    psum = (pt[:, 0:qb] + pt[:, qb:2 * qb]) + (pt[:, 2 * qb:3 * qb] + pt[:, 3 * qb:4 * qb])
    phi, plo = _split2(psum)
    ovt = ovt_ref[...]
    imp = jnp.dot(ovt, phi, preferred_element_type=F32) + jnp.dot(ovt, plo, preferred_element_type=F32)

    jb = lax.broadcasted_iota(jnp.int32, (LANES, qb), 0)
    tq = q0 + lax.broadcasted_iota(jnp.int32, (LANES, qb), 1)
    cur = tq >> 6
    forced = (jb == 0) | (jb == cur) | (jb == cur - 1)
    causal = (jb << 6) <= tq
    score = jnp.where(causal, jnp.where(forced, FORCE, imp), -jnp.inf)
    notsel_t = jnp.ones((LANES, qb), F32)
    jbf = jb.astype(F32)
    for _ in range(SEL_TOPK):
        smx = jnp.max(score, axis=0, keepdims=True)
        ix = jnp.min(jnp.where(score == smx, jbf, float(LANES)), axis=0, keepdims=True)
        hit = jbf == ix
        notsel_t = jnp.where(hit, 0.0, notsel_t)
        score = jnp.where(hit, -jnp.inf, score)
    notsel_b = notsel_t.T.astype(BF16)
    n_grp = mh // grp
    q_aug = [jnp.concatenate([jnp.concatenate([qh[m], notsel_b], axis=1) for m in range(c * grp, (c + 1) * grp)], axis=0)
             for c in range(n_grp)]
    tpos = q0 + lax.broadcasted_iota(jnp.int32, (qb, 1), 0)
    tpos_g = jnp.concatenate([tpos] * grp, axis=0)

    def sel_tiles(first, n_sub):
        ks, vs = [], []
        for t in range(n_sub):
            k0 = pl.multiple_of((first + t) * tk_sel, tk_sel)
            k_aug = jnp.concatenate([ks_ref[pl.ds(k0, tk_sel), :], e_ref[pl.ds(k0, tk_sel), :]], axis=1)
            ks.append([k_aug] * n_grp)
            vs.append([vs_ref[pl.ds(k0, tk_sel), :]] * n_grp)
        return ks, vs

    def sel_step(first, n_sub, state):
        ks, vs = sel_tiles(first, n_sub)
        return _attend(q_aug, ks, vs, state, skew)

    state = tuple((jnp.full((grp * qb, 1), NEG, F32), jnp.zeros((grp * qb, LANES), F32)) for _ in range(n_grp))
    n_below = q0 // tk_sel
    state = _tile_loops(n_below, subs, sel_step, state)

    def causal(t, c, sc):
        kpos = n_below * tk_sel + lax.broadcasted_iota(jnp.int32, (grp * qb, tk_sel), 1)
        return jnp.where(kpos <= tpos_g, sc, NEG)

    ks, vs = sel_tiles(n_below, 1)
    state = _attend(q_aug, ks, vs, state, skew, mask=causal)
    sel_acc = [state[m // grp][1][(m % grp) * qb:(m % grp + 1) * qb] for m in range(mh)]

    wk = WINDOW + qb
    k0w = pl.multiple_of(jnp.maximum(i - WINDOW // qb, 0) * qb, qb)
    kwt = kw_ref[pl.ds(k0w, wk), :]
    vwt = vw_ref[pl.ds(k0w, wk), :]
    kposw = k0w + lax.broadcasted_iota(jnp.int32, (grp * qb, wk), 1)
    wmask = (kposw <= tpos_g) & (kposw > tpos_g - WINDOW)
    q_grp = [jnp.concatenate(qh[c * grp:(c + 1) * grp], axis=0) for c in range(n_grp)]

    def win_scores(c):
        return jnp.where(wmask, _dot_nt(q_grp[c], kwt), NEG)

    def win_out(c, sc):
        pe = jnp.exp2(sc - jnp.max(sc, axis=-1, keepdims=True))
        return jnp.dot(pe.astype(BF16), vwt, preferred_element_type=F32)

    win_grp = _pipelined(n_grp, win_scores, win_out, min(skew, n_grp - 1))
    win_acc = [win_grp[m // grp][(m % grp) * qb:(m % grp + 1) * qb] for m in range(mh)]

    sig = _sigmoid(gz_ref[...])
    den_lane = jl == HEAD_DIM
    heads = []
    for m in range(mh):
        col = g * mh + m
        gates = [jnp.sum(jnp.where(jl == br * N_HEADS + col, sig, 0.0), axis=-1, keepdims=True)
                 for br in range(3)]
        acc_s = sel_acc[m]
        acc_w = win_acc[m]
        den_s = jnp.sum(jnp.where(den_lane, acc_s, 0.0), axis=-1, keepdims=True)
        den_w = jnp.sum(jnp.where(den_lane, acc_w, 0.0), axis=-1, keepdims=True)
        heads.append(gates[0] * oc_ref[m * qb:(m + 1) * qb, :] + (gates[1] / den_s) * acc_s
                     + (gates[2] / den_w) * acc_w)
    lane_lo = jl < HEAD_DIM
    o_ref[:, 0:LANES] = jnp.where(lane_lo, heads[0], pltpu.roll(heads[1], HEAD_DIM, 1)).astype(o_ref.dtype)
    o_ref[:, LANES:2 * LANES] = jnp.where(lane_lo, heads[2], pltpu.roll(heads[3], HEAD_DIM, 1)).astype(o_ref.dtype)


def _nsa_attn(qn, gz, kvc, overlap, kvd, *, batch, seq, skew=2, grp=2, subs=(4, 2, 1)):
    n = qn.shape[0]
    qb = NSA_QB
    nq = seq // qb
    g = NSA_GROUPS
    rows = HEADS_PER_GROUP * qb
    n_cmp_rows = kvc.shape[3]
    tk_sel = min(512, seq)
    assert seq >= WINDOW + qb and seq % tk_sel == 0 and tk_sel % qb == 0
    in_specs = [
        pl.BlockSpec((qb, HEADS_PER_GROUP * LANES), lambda b, gg, i: (b * nq + i, gg)),
        pl.BlockSpec((qb, LANES), lambda b, gg, i: (b * nq + i, 0)),
        pl.BlockSpec((1, 1, 1, n_cmp_rows, LANES), lambda b, gg, i: (b, 0, gg, 0, 0)),
        pl.BlockSpec((1, 1, LANES, n_cmp_rows), lambda b, gg, i: (b, gg, 0, 0)),
        pl.BlockSpec((LANES, n_cmp_rows), lambda b, gg, i: (0, 0)),
        pl.BlockSpec((seq, LANES), lambda b, gg, i: (b, 0 * g + gg)),
        pl.BlockSpec((seq, LANES), lambda b, gg, i: (b, 1 * g + gg)),
        pl.BlockSpec((seq, LANES), lambda b, gg, i: (b, 2 * g + gg)),
        pl.BlockSpec((seq, LANES), lambda b, gg, i: (b, 3 * g + gg)),
    ]
    return pl.pallas_call(
        functools.partial(_nsa_attn_kernel, n_cmp_rows=n_cmp_rows, tk_sel=tk_sel, skew=skew, grp=grp, subs=subs),
        out_shape=jax.ShapeDtypeStruct((n, N_HEADS * HEAD_DIM), BF16),
        grid=(batch, g, nq),
        in_specs=in_specs,
        out_specs=pl.BlockSpec((qb, HEADS_PER_GROUP * HEAD_DIM), lambda b, gg, i: (b * nq + i, gg)),
        scratch_shapes=[pltpu.VMEM((seq, LANES), BF16), pltpu.VMEM((rows, LANES), F32)],
        compiler_params=_cparams(("parallel", "parallel", "arbitrary")),
        name="nsa_attn",
    )(qn, gz, kvc, jnp.swapaxes(kvc[:, 1], -1, -2), overlap.T, kvd, kvd, kvd, kvd)


def _pad_cols(w, width=LANES):
    d, f = w.shape
    return jnp.zeros((d, width), w.dtype).at[:, :f].set(w)


def _tile_gain(gain, n_heads, scale=1.0, dup=False):
    g = gain.astype(F32) * scale
    second = g if dup else jnp.zeros_like(g)
    return jnp.tile(jnp.concatenate([g, second]), n_heads)


def _fox_aug_tables():
    hw = N_HEADS * LANES
    piece = jnp.arange(3)[:, None]
    head = jnp.arange(N_HEADS)[None, :]
    src = (piece * N_HEADS + head).reshape(-1)
    q_dst = (head * LANES + HEAD_DIM + piece).reshape(-1)
    k_dst = (hw + head * LANES + HEAD_DIM + 3 + piece).reshape(-1)
    place = jnp.zeros((LANES, 3 * hw), F32).at[src, q_dst].set(1.0).at[src, k_dst].set(-1.0)
    lane = jnp.arange(LANES)
    q_bias = ((lane >= HEAD_DIM + 3) & (lane < HEAD_DIM + 6)).astype(F32)
    k_bias = ((lane >= HEAD_DIM) & (lane < HEAD_DIM + 3)).astype(F32)
    v_bias = (lane == HEAD_DIM).astype(F32)
    bias = jnp.concatenate([jnp.tile(q_bias, N_HEADS), jnp.tile(k_bias, N_HEADS), jnp.tile(v_bias, N_HEADS)])
    return place.astype(BF16), bias


def _fox_layer(x, batch, seq, attn_g, w_in, b_f, qk_gain, w_out):
    hd = N_HEADS * HEAD_DIM
    hw = N_HEADS * LANES
    scale = HEAD_DIM ** -0.5
    z = _rms_proj(x, attn_g, _pad_cols(w_in[:, 3 * hd:]).astype(BF16), out_dtype=F32, seq=seq)
    b_pad = jnp.zeros((1, LANES), F32).at[0, :N_HEADS].set(b_f.astype(F32))
    c3 = _fox_c(z, b_pad, batch=batch, seq=seq)
    w_all = w_in[:, :3 * hd].astype(BF16)
    gain = jnp.concatenate([_tile_gain(qk_gain[0], N_HEADS, scale * LOG2E), _tile_gain(qk_gain[1], N_HEADS),
                            jnp.zeros((hw,), F32)])
    tn = 512
    place, bias = _fox_aug_tables()
    qkv = _rms_proj(x, attn_g, w_all, out_dtype=BF16, seq=seq, tn=tn, gain=gain,
                    norm_blocks=tuple(range(2 * hw // tn)), placed=(c3, place), col_bias=bias, expand=True)
    o = _fox_attn(qkv, batch=batch, seq=seq)
    return _matmul_res(o, w_out.astype(BF16), x)


def _nsa_shared_kv(x, batch, seq, kv_norm, kv_w, kv_k_gain, cmp_pos, cmp_w1, cmp_w2):
    g = NSA_GROUPS
    gw = g * HEAD_DIM
    tabs = _rope_tables(jnp.arange(seq))
    w_kvd = kv_w[:, 2 * gw:].astype(BF16)
    zero = jnp.zeros((g * LANES,), F32)
    one_lane = jnp.tile((jnp.arange(LANES) == HEAD_DIM).astype(F32), g)
    gain_kvd = jnp.concatenate([_tile_gain(kv_k_gain[1], g), zero, _tile_gain(kv_k_gain[2], g), zero])
    bias_kvd = jnp.concatenate([zero, one_lane, zero, one_lane])
    kvd = _rms_proj(x, kv_norm, w_kvd, out_dtype=BF16, seq=seq, tn=g * LANES, gain=gain_kvd,
                    norm_blocks=(0, 2), rope_tabs=tabs, col_bias=bias_kvd, expand=True)
    zc = _rms_proj(x, kv_norm, kv_w[:, :2 * gw].astype(BF16), out_dtype=F32, seq=seq)
    rows = seq // CMP_STRIDE
    zr = zc.reshape(batch, rows, CMP_STRIDE, 2, g, HEAD_DIM).transpose(0, 3, 4, 1, 2, 5)
    zr = zr.reshape(batch, 2, g, rows, CMP_STRIDE * HEAD_DIM)
    cmp_end = jnp.arange(rows) * CMP_STRIDE + CMP_BLOCK - 1
    ctabs = _rope_tables(cmp_end)
    w2dup = jnp.concatenate([cmp_w2, cmp_w2], axis=-1).astype(BF16)
    kvc = _compress(zr, cmp_w1.astype(BF16), cmp_pos.reshape(2, 1, CMP_BLOCK * HEAD_DIM).astype(F32), w2dup,
                    _tile_gain(kv_k_gain[0], 1, dup=True).reshape(1, LANES), ctabs, batch=batch)
    return kvc, kvd


def _nsa_layer(x, batch, seq, attn_g, w_in, q_gain, w_out, kvc, kvd):
    hd = N_HEADS * HEAD_DIM
    scale = HEAD_DIM ** -0.5
    rows = seq // CMP_STRIDE
    tabs = _rope_tables(jnp.arange(seq))
    qn = _rms_proj(x, attn_g, w_in[:, :hd].astype(BF16), out_dtype=BF16, seq=seq, expand=True,
                   gain=_tile_gain(q_gain, N_HEADS, scale * LOG2E), norm_blocks=tuple(range(N_HEADS * LANES // 512)),
                   rope_tabs=tabs)
    gz = _rms_proj(x, attn_g, _pad_cols(w_in[:, hd:]).astype(BF16), out_dtype=F32, seq=seq)

    n_sel = seq // SEL_BLOCK
    assert n_sel <= LANES
    cmp_start = jnp.arange(rows) * CMP_STRIDE
    sel_start = jnp.arange(LANES) * SEL_BLOCK
    overlap = jnp.clip(jnp.minimum(cmp_start[:, None] + CMP_BLOCK, sel_start[None, :] + SEL_BLOCK)
                       - jnp.maximum(cmp_start[:, None], sel_start[None, :]), 0)
    overlap = jnp.where((jnp.arange(rows) < rows - 1)[:, None] & (jnp.arange(LANES) < n_sel)[None, :], overlap, 0)
    o = _nsa_attn(qn, gz, kvc, overlap.astype(BF16), kvd, batch=batch, seq=seq)
    return _matmul_res(o, w_out.astype(BF16), x)


def kernel(x, p, attn_norm, ffn_norm, ple_norm, ple_gate_w, ple_proj_w, router_w, router_b, w_gu, b_gu, w_down, b_down, fox_w_in, fox_b_f, fox_qk_gain, fox_w_out, kv_norm, kv_w, kv_k_gain, cmp_pos, cmp_w1, cmp_w2, nsa_w_in, nsa_q_gain, nsa_w_out):
    batch, seq, d = x.shape
    depth = p.shape[0]
    n_a = fox_w_in.shape[0]
    xt = x.reshape(batch * seq, d)
    shared = None
    for layer in range(depth):
        if layer == n_a:
            shared = _nsa_shared_kv(xt, batch, seq, kv_norm, kv_w, kv_k_gain, cmp_pos, cmp_w1, cmp_w2)
        if layer < n_a:
            xt = _fox_layer(xt, batch, seq, attn_norm[layer], fox_w_in[layer], fox_b_f[layer],
                            fox_qk_gain[layer], fox_w_out[layer])
        else:
            i = layer - n_a
            xt = _nsa_layer(xt, batch, seq, attn_norm[layer], nsa_w_in[i], nsa_q_gain[i], nsa_w_out[i], *shared)
        y = _moe_layer(xt, ffn_norm[layer], router_w[layer], router_b[layer], w_gu[layer], b_gu[layer],
                       w_down[layer], b_down[layer])
        xt = _ple(xt, y, ple_norm[layer], ple_gate_w[layer], p[layer].reshape(batch * seq, -1), ple_proj_w[layer])
    return xt.reshape(batch, seq, d)
```

```python
import functools

import jax
import jax.numpy as jnp
from jax import lax
from jax.experimental import pallas as pl
from jax.experimental.pallas import tpu as pltpu

F32 = jnp.float32
BF16 = jnp.bfloat16

N_HEADS = 16
HEAD_DIM = 64
ROPE_DIM = HEAD_DIM // 4
ROPE_THETA = 500000.0
NSA_GROUPS = 4
HEADS_PER_GROUP = N_HEADS // NSA_GROUPS
CMP_BLOCK = 32
CMP_STRIDE = 16
SEL_BLOCK = 64
SEL_TOPK = 16
WINDOW = 512
N_EXPERTS = 32
TOP_K = 4
SWIGLU_LIMIT = 7.0
SWIGLU_ALPHA = 1.702
EPS = 1e-6
NEG = -1e30
FORCE = 1e6
LOG2E = 1.4426950408889634

LANES = 128
NSA_QB = 128
MOE_BM = 256
VMEM_LIMIT = 56 * 1024 * 1024


def _cparams(sem):
    return pltpu.CompilerParams(dimension_semantics=sem, vmem_limit_bytes=VMEM_LIMIT)


def _sigmoid(x):
    return 1.0 / (1.0 + jnp.exp(-x))


def _dot_nt(a, b):
    return lax.dot_general(a, b, (((1,), (1,)), ((), ())), preferred_element_type=F32)


def _split2(x):
    hi = x.astype(BF16)
    lo = (x - hi.astype(F32)).astype(BF16)
    return hi, lo


def _proj_kernel(*refs, norm_blocks, rope, placed, biased, expand, head_div, tn, out_dtype):
    refs = list(refs)
    x_ref, g_ref, w_ref = refs[:3]
    del refs[:3]
    if norm_blocks:
        gain_ref = refs.pop(0)
    if rope:
        c_ref, s1_ref, s2_ref = refs[:3]
        del refs[:3]
    if placed:
        ex_ref, place_ref = refs[:2]
        del refs[:2]
    if biased:
        bias_ref = refs.pop(0)
    o_ref, h_ref = refs
    j = pl.program_id(1)
    aug = placed or biased

    @pl.when(j == 0)
    def _():
        xf = x_ref[...]
        ms = jnp.mean(xf * xf, axis=-1, keepdims=True)
        h_ref[...] = (xf * lax.rsqrt(ms + EPS) * g_ref[...]).astype(BF16)

    acc = jnp.dot(h_ref[...], w_ref[...], preferred_element_type=F32)
    if placed and biased:
        extra = jnp.dot(ex_ref[...], place_ref[...], preferred_element_type=F32) + bias_ref[...]
    elif biased:
        extra = jnp.broadcast_to(bias_ref[...], (acc.shape[0], tn))
    if expand:
        lo = lax.broadcasted_iota(jnp.int32, (acc.shape[0], LANES), 1) < HEAD_DIM
        blocks = []
        for c in range(acc.shape[1] // LANES):
            a = acc[:, c * LANES:(c + 1) * LANES]
            blocks += [jnp.where(lo, a, 0.0), jnp.where(lo, pltpu.roll(a, HEAD_DIM, 1), 0.0)]
    else:
        blocks = [acc[:, c * LANES:(c + 1) * LANES] for c in range(acc.shape[1] // LANES)]

    def plain():
        for c, a in enumerate(blocks):
            sl = slice(c * LANES, (c + 1) * LANES)
            o_ref[:, sl] = ((a + extra[:, sl]) if aug else a).astype(out_dtype)

    if not norm_blocks:
        plain()
        return
    is_norm = functools.reduce(jnp.logical_or, [j == c for c in norm_blocks])

    @pl.when(is_norm)
    def _():
        for c, a in enumerate(blocks):
            sl = slice(c * LANES, (c + 1) * LANES)
            ss = jnp.sum(a * a, axis=-1, keepdims=True) * (1.0 / head_div)
            y = a * lax.rsqrt(ss + EPS) * gain_ref[:, sl]
            if rope:
                y = (y * c_ref[...] + pltpu.roll(y, LANES - ROPE_DIM // 2, 1) * s1_ref[...]
                     + pltpu.roll(y, ROPE_DIM // 2, 1) * s2_ref[...])
            if aug:
                y = y + extra[:, sl]
            o_ref[:, sl] = y.astype(out_dtype)

    @pl.when(jnp.logical_not(is_norm))
    def _():
        plain()


def _rms_proj(x, g, w, *, out_dtype, seq, tm=512, tn=512, gain=None, norm_blocks=(),
              rope_tabs=None, placed=None, col_bias=None, expand=False, head_div=HEAD_DIM,
              w_col0=0, w_cols=None):
    n, d = x.shape
    w_cols = w.shape[1] - w_col0 if w_cols is None else w_cols
    f = w_cols * (2 if expand else 1)
    tn = min(tn, f)
    tw = tn // 2 if expand else tn
    assert w_col0 % tw == 0
    j0 = w_col0 // tw
    tm = min(tm, seq)
    assert n % tm == 0 and f % tn == 0 and seq % tm == 0
    nt = seq // tm
    in_specs = [
        pl.BlockSpec((tm, d), lambda i, j: (i, 0)),
        pl.BlockSpec((1, d), lambda i, j: (0, 0)),
        pl.BlockSpec((d, tw), lambda i, j: (0, j0 + j)),
    ]
    args = [x, g.reshape(1, d).astype(F32), w]
    if norm_blocks:
        in_specs.append(pl.BlockSpec((1, tn), lambda i, j: (0, j)))
        args.append(gain.reshape(1, f).astype(F32))
    if rope_tabs is not None:
        assert norm_blocks
        for tab in rope_tabs:
            in_specs.append(pl.BlockSpec((tm, LANES), lambda i, j: (i % nt, 0)))
            args.append(tab)
    if placed is not None:
        assert col_bias is not None
        values, place = placed
        in_specs += [pl.BlockSpec((tm, LANES), lambda i, j: (i, 0)),
                     pl.BlockSpec((LANES, tn), lambda i, j: (0, j))]
        args += [values, place]
    if col_bias is not None:
        in_specs.append(pl.BlockSpec((1, tn), lambda i, j: (0, j)))
        args.append(col_bias.reshape(1, f).astype(F32))
    kern = functools.partial(_proj_kernel, norm_blocks=tuple(norm_blocks), rope=rope_tabs is not None,
                             placed=placed is not None, biased=col_bias is not None, expand=expand,
                             head_div=float(head_div), tn=tn, out_dtype=out_dtype)
    return pl.pallas_call(
        kern,
        out_shape=jax.ShapeDtypeStruct((n, f), out_dtype),
        grid=(n // tm, f // tn),
        in_specs=in_specs,
        out_specs=pl.BlockSpec((tm, tn), lambda i, j: (i, j)),
        scratch_shapes=[pltpu.VMEM((tm, d), BF16)],
        compiler_params=_cparams(("parallel", "arbitrary")),
        name="rms_proj",
    )(*args)


def _rope_tables(pos, width=LANES):
    half = ROPE_DIM // 2
    inv = jnp.power(jnp.float32(ROPE_THETA), -jnp.arange(0, ROPE_DIM, 2, dtype=F32) / ROPE_DIM)
    ang = pos.astype(F32)[:, None] * inv[None, :]
    cos, sin = jnp.cos(ang), jnp.sin(ang)
    t = pos.shape[0]
    ones = jnp.ones((t, HEAD_DIM - ROPE_DIM), F32)
    zeros_h = jnp.zeros((t, half), F32)
    zeros_r = jnp.zeros((t, HEAD_DIM - ROPE_DIM), F32)
    c = jnp.concatenate([cos, cos, ones], axis=1)
    s1 = jnp.concatenate([-sin, zeros_h, zeros_r], axis=1)
    s2 = jnp.concatenate([zeros_h, sin, zeros_r], axis=1)
    rep = width // HEAD_DIM
    return tuple(jnp.tile(a, (1, rep)) for a in (c, s1, s2))


def _matmul_res_kernel(a_ref, w_ref, x_ref, o_ref):
    o_ref[...] = x_ref[...] + jnp.dot(a_ref[...], w_ref[...], preferred_element_type=F32)


def _matmul_res(a, w, x, *, tm=512):
    n, k = a.shape
    f = w.shape[1]
    tm = min(tm, n)
    return pl.pallas_call(
        _matmul_res_kernel,
        out_shape=jax.ShapeDtypeStruct((n, f), F32),
        grid=(n // tm,),
        in_specs=[
            pl.BlockSpec((tm, k), lambda i: (i, 0)),
            pl.BlockSpec((k, f), lambda i: (0, 0)),
            pl.BlockSpec((tm, f), lambda i: (i, 0)),
        ],
        out_specs=pl.BlockSpec((tm, f), lambda i: (i, 0)),
        compiler_params=_cparams(("parallel",)),
        name="matmul_res",
    )(a, w, x)


def _fox_c_kernel(z_ref, b_ref, c_ref, carry_ref, *, tm):
    t = pl.program_id(1)

    @pl.when(t == 0)
    def _():
        carry_ref[...] = jnp.zeros_like(carry_ref)

    u = z_ref[...] + b_ref[...]
    logf = jnp.minimum(u, 0.0) - jnp.log(1.0 + jnp.exp(-jnp.abs(u)))
    row = lax.broadcasted_iota(jnp.int32, (tm, tm), 0)
    col = lax.broadcasted_iota(jnp.int32, (tm, tm), 1)
    tri = (row >= col).astype(BF16)
    hi = logf.astype(BF16)
    r1 = logf - hi.astype(F32)
    mid = r1.astype(BF16)
    lo = (r1 - mid.astype(F32)).astype(BF16)
    cs = (jnp.dot(tri, hi, preferred_element_type=F32) + jnp.dot(tri, mid, preferred_element_type=F32)
          + jnp.dot(tri, lo, preferred_element_type=F32))
    c = cs + carry_ref[...]
    carry_ref[...] = c[tm - 1:tm, :]
    c2 = c * LOG2E
    p0 = c2.astype(BF16).astype(F32)
    p1 = (c2 - p0).astype(BF16).astype(F32)
    p2 = ((c2 - p0) - p1).astype(BF16).astype(F32)
    lane = lax.broadcasted_iota(jnp.int32, c.shape, 1)
    out = jnp.where(lane < N_HEADS, p0,
                    jnp.where(lane < 2 * N_HEADS, pltpu.roll(p1, N_HEADS, 1),
                              jnp.where(lane < 3 * N_HEADS, pltpu.roll(p2, 2 * N_HEADS, 1), 0.0)))
    c_ref[...] = out.astype(BF16)


def _fox_c(z, b_pad, *, batch, seq, tm=512):
    tm = min(tm, seq)
    nt = seq // tm
    return pl.pallas_call(
        functools.partial(_fox_c_kernel, tm=tm),
        out_shape=jax.ShapeDtypeStruct(z.shape, BF16),
        grid=(batch, nt),
        in_specs=[
            pl.BlockSpec((tm, LANES), lambda b, t: (b * nt + t, 0)),
            pl.BlockSpec((1, LANES), lambda b, t: (0, 0)),
        ],
        out_specs=pl.BlockSpec((tm, LANES), lambda b, t: (b * nt + t, 0)),
        scratch_shapes=[pltpu.VMEM((1, LANES), F32)],
        compiler_params=_cparams(("parallel", "arbitrary")),
        name="fox_cumsum",
    )(z, b_pad)


def _attend(qs, ks, vs, state, skew, mask=None):
    items = [(t, c) for t in range(len(ks)) for c in range(len(qs))]
    state = list(state)
    pending = {}
    depth = min(skew, len(items) - 1)
    for idx in range(len(items) + depth):
        if idx < len(items):
            t, c = items[idx]
            sc = _dot_nt(qs[c], ks[t][c])
            pending[idx] = sc if mask is None else mask(t, c, sc)
        if idx >= depth:
            t, c = items[idx - depth]
            sc = pending.pop(idx - depth)
            m_prev, acc = state[c]
            m_new = jnp.maximum(m_prev, jnp.max(sc, axis=-1, keepdims=True))
            pe = jnp.exp2(sc - m_new)
            acc = jnp.exp2(m_prev - m_new) * acc + jnp.dot(pe.astype(BF16), vs[t][c], preferred_element_type=F32)
            state[c] = (m_new, acc)
    return tuple(state)


def _tile_loops(n_tiles, subs, step, state):
    done = 0
    for sub in subs:
        n_here = (n_tiles - done) // sub
        state = lax.fori_loop(0, n_here, lambda j, st, done=done, sub=sub: step(done + j * sub, sub, st), state)
        done = done + n_here * sub
    return state


def _fox_attn_kernel(q0_ref, q1_ref, k0_ref, k1_ref, v0_ref, v1_ref, o_ref, *, tq, tk, skew, subs):
    qi = pl.program_id(2)
    k_refs = (k0_ref, k1_ref)
    v_refs = (v0_ref, v1_ref)
    qs = [q0_ref[...], q1_ref[...]]

    def tiles(refs, first, n_sub):
        return [[r[pl.ds(pl.multiple_of((first + t) * tk, tk), tk), :] for r in refs] for t in range(n_sub)]

    def full_step(first, n_sub, state):
        return _attend(qs, tiles(k_refs, first, n_sub), tiles(v_refs, first, n_sub), state, skew)

    def causal(t, c, sc):
        tpos = lax.broadcasted_iota(jnp.int32, (tq, tk), 0)
        kpos = t * tk + lax.broadcasted_iota(jnp.int32, (tq, tk), 1)
        return jnp.where(kpos <= tpos, sc, NEG)

    state = tuple((jnp.full((tq, 1), NEG, F32), jnp.zeros((tq, LANES), F32)) for _ in qs)
    n_below = (qi * tq) // tk
    state = _tile_loops(n_below, subs, full_step, state)
    n_diag = tq // tk
    state = _attend(qs, tiles(k_refs, n_below, n_diag), tiles(v_refs, n_below, n_diag), state, skew, mask=causal)

    lane = lax.broadcasted_iota(jnp.int32, (tq, LANES), 1)
    outs = []
    for _, acc in state:
        den = jnp.sum(jnp.where(lane == HEAD_DIM, acc, 0.0), axis=-1, keepdims=True)
        outs.append(acc / den)
    o_ref[...] = jnp.where(lane < HEAD_DIM, outs[0], pltpu.roll(outs[1], HEAD_DIM, 1)).astype(o_ref.dtype)


def _fox_attn(qkv, *, batch, seq, tq=512, tk=512, skew=2, subs=(4, 2, 1)):
    n = qkv.shape[0]
    tq = min(tq, seq)
    tk = min(tk, tq)
    assert seq % tq == 0 and tq % tk == 0
    nq = seq // tq
    hp = N_HEADS // 2
    kb = N_HEADS
    vb = 2 * N_HEADS
    in_specs = [
        pl.BlockSpec((tq, LANES), lambda b, h, i: (b * nq + i, 2 * h)),
        pl.BlockSpec((tq, LANES), lambda b, h, i: (b * nq + i, 2 * h + 1)),
        pl.BlockSpec((seq, LANES), lambda b, h, i: (b, kb + 2 * h)),
        pl.BlockSpec((seq, LANES), lambda b, h, i: (b, kb + 2 * h + 1)),
        pl.BlockSpec((seq, LANES), lambda b, h, i: (b, vb + 2 * h)),
        pl.BlockSpec((seq, LANES), lambda b, h, i: (b, vb + 2 * h + 1)),
    ]
    return pl.pallas_call(
        functools.partial(_fox_attn_kernel, tq=tq, tk=tk, skew=skew, subs=subs),
        out_shape=jax.ShapeDtypeStruct((n, N_HEADS * HEAD_DIM), BF16),
        grid=(batch, hp, nq),
        in_specs=in_specs,
        out_specs=pl.BlockSpec((tq, LANES), lambda b, h, i: (b * nq + i, h)),
        compiler_params=_cparams(("parallel", "parallel", "arbitrary")),
        name="fox_attn",
    )(qkv, qkv, qkv, qkv, qkv, qkv)


def _router_kernel(x_ref, g_ref, whi_ref, wlo_ref, b_ref, h_ref, idx_ref, gate_ref, cnt_ref):
    xf = x_ref[...]
    ms = jnp.mean(xf * xf, axis=-1, keepdims=True)
    h = xf * lax.rsqrt(ms + EPS) * g_ref[...]
    nc = h.shape[1] // LANES
    for c in range(nc):
        h_ref[pl.ds(c, h.shape[0], stride=nc), :] = h[:, c * LANES:(c + 1) * LANES]
    hhi, hlo = _split2(h)
    whi = whi_ref[...]
    logits = (jnp.dot(hhi, whi, preferred_element_type=F32) + jnp.dot(hlo, whi, preferred_element_type=F32)
              + jnp.dot(hhi, wlo_ref[...], preferred_element_type=F32)) + b_ref[...]
    lane = lax.broadcasted_iota(jnp.int32, logits.shape, 1).astype(F32)
    vals, idxs = [], []
    cur = logits
    for _ in range(TOP_K):
        mx = jnp.max(cur, axis=-1, keepdims=True)
        ix = jnp.min(jnp.where(cur == mx, lane, float(LANES)), axis=-1, keepdims=True)
        vals.append(mx)
        idxs.append(ix)
        cur = jnp.where(lane == ix, -jnp.inf, cur)
    es = [jnp.exp(v - vals[0]) for v in vals]
    tot = es[0] + es[1] + es[2] + es[3]
    idx_out = jnp.zeros(logits.shape, F32)
    gate_out = jnp.zeros(logits.shape, F32)
    for r in range(TOP_K):
        idx_out = jnp.where(lane == r, idxs[r], idx_out)
        gate_out = jnp.where(lane == r, es[r] / tot, gate_out)
    idx_ref[...] = idx_out.astype(jnp.int32)
    gate_ref[...] = gate_out
    hits = sum(jnp.sum((lane == ix).astype(F32), axis=0, keepdims=True) for ix in idxs)

    @pl.when(pl.program_id(0) == 0)
    def _():
        cnt_ref[...] = jnp.zeros(cnt_ref.shape, cnt_ref.dtype)

    cnt_ref[...] += jnp.broadcast_to(hits, cnt_ref.shape)


def _router(x, g, w, b, *, tm=512):
    n, d = x.shape
    e = w.shape[1]
    tm = min(tm, n)
    w_pad = jnp.zeros((d, LANES), F32).at[:, :e].set(w)
    whi = w_pad.astype(BF16)
    wlo = (w_pad - whi.astype(F32)).astype(BF16)
    b_pad = jnp.full((1, LANES), -jnp.inf, F32).at[0, :e].set(b.astype(F32))
    return pl.pallas_call(
        _router_kernel,
        out_shape=(jax.ShapeDtypeStruct((n * (d // LANES), LANES), F32), jax.ShapeDtypeStruct((n, LANES), jnp.int32),
                   jax.ShapeDtypeStruct((n, LANES), F32), jax.ShapeDtypeStruct((8, LANES), F32)),
        grid=(n // tm,),
        in_specs=[
            pl.BlockSpec((tm, d), lambda i: (i, 0)),
            pl.BlockSpec((1, d), lambda i: (0, 0)),
            pl.BlockSpec((d, LANES), lambda i: (0, 0)),
            pl.BlockSpec((d, LANES), lambda i: (0, 0)),
            pl.BlockSpec((1, LANES), lambda i: (0, 0)),
        ],
        out_specs=(pl.BlockSpec((tm * (d // LANES), LANES), lambda i: (i, 0)), pl.BlockSpec((tm, LANES), lambda i: (i, 0)),
                   pl.BlockSpec((tm, LANES), lambda i: (i, 0)), pl.BlockSpec((8, LANES), lambda i: (0, 0))),
        compiler_params=_cparams(("arbitrary",)),
        name="moe_router",
    )(x, g.reshape(1, d).astype(F32), whi, wlo, b_pad)


def _moe_expert_kernel(blk_exp_ref, tok_ref, dst_ref, h_hbm, ga_ref, gb_ref, wgu_a, bgu_a, wd_a, bd_a,
                       wgu_b, bgu_b, wd_b, bd_b, y_hbm, xa, xb, oa, ob, gsem, ssem, *, bm, dff, n_steps):
    i = pl.program_id(0)
    nc = xa.shape[0] // bm
    dump = y_hbm.shape[0] // nc - 2 * bm

    def gather(block, buf, sem):
        base = block * bm
        for r in range(bm):
            pltpu.make_async_copy(h_hbm.at[pl.ds(pl.multiple_of(tok_ref[base + r] * nc, nc), nc)], buf.at[pl.ds(r * nc, nc)], sem).start()

    def scatter(block, buf, sem):
        base = block * bm
        for r in range(bm):
            pltpu.make_async_copy(buf.at[pl.ds(r * nc, nc)], y_hbm.at[pl.ds(pl.multiple_of(dst_ref[base + r] * nc, nc), nc)], sem).start()

    def wait_gather(buf, sem):
        pltpu.make_async_copy(h_hbm.at[pl.ds(0, bm * nc)], buf, sem).wait()

    def wait_scatter(buf, sem):
        pltpu.make_async_copy(buf, y_hbm.at[pl.ds(0, bm * nc)], sem).wait()

    def experts(x_ref, o_ref, g_ref, wgu, bgu, wd, bd):
        x = jnp.concatenate([x_ref[pl.ds(c, bm, stride=nc), :] for c in range(nc)], axis=1)
        gu = jnp.dot(x.astype(BF16), wgu[0], preferred_element_type=F32) + bgu[0]
        a = jnp.minimum(gu[:, :dff], SWIGLU_LIMIT)
        u = jnp.clip(gu[:, dff:], -SWIGLU_LIMIT, SWIGLU_LIMIT)
        y = (u + 1.0) * (a * _sigmoid(SWIGLU_ALPHA * a))
        out = (jnp.dot(y.astype(BF16), wd[0], preferred_element_type=F32) + bd[0]) * g_ref[...]
        for c in range(nc):
            o_ref[pl.ds(c, bm, stride=nc), :] = out[:, c * LANES:(c + 1) * LANES]

    @pl.when(i == 0)
    def _():
        gather(0, xa, gsem.at[0])
        for buf, sem, base in ((oa, ssem.at[0], dump), (ob, ssem.at[1], dump + bm)):
            buf[...] = jnp.zeros(buf.shape, buf.dtype)
            for r in range(bm):
                pltpu.make_async_copy(buf.at[pl.ds(r * nc, nc)], y_hbm.at[pl.ds((base + r) * nc, nc)], sem).start()

    blk_a = 2 * i
    blk_b = 2 * i + 1
    nxt_a = jnp.minimum(2 * i + 2, 2 * n_steps - 1)

    gather(blk_b, xb, gsem.at[1])
    wait_gather(xa, gsem.at[0])
    wait_scatter(oa, ssem.at[0])
    experts(xa, oa, ga_ref, wgu_a, bgu_a, wd_a, bd_a)
    scatter(blk_a, oa, ssem.at[0])

    gather(nxt_a, xa, gsem.at[0])
    wait_gather(xb, gsem.at[1])
    wait_scatter(ob, ssem.at[1])
    experts(xb, ob, gb_ref, wgu_b, bgu_b, wd_b, bd_b)
    scatter(blk_b, ob, ssem.at[1])

    @pl.when(i == n_steps - 1)
    def _():
        wait_gather(xa, gsem.at[0])
        wait_scatter(oa, ssem.at[0])
        wait_scatter(ob, ssem.at[1])


def _moe_experts(h, blk_exp, tok, dst, gates, w_gu, b_gu, w_down, b_down, *, bm):
    e, d, f2 = w_gu.shape
    nc = d // LANES
    n = h.shape[0] // nc
    dff = f2 // 2
    cap = tok.shape[0]
    n_steps = cap // (2 * bm)
    assert cap == n_steps * 2 * bm

    def wspecs(off):
        return [
            pl.BlockSpec((1, d, f2), lambda i, be, tk, ds: (be[2 * i + off], 0, 0)),
            pl.BlockSpec((1, 1, f2), lambda i, be, tk, ds: (be[2 * i + off], 0, 0)),
            pl.BlockSpec((1, dff, d), lambda i, be, tk, ds: (be[2 * i + off], 0, 0)),
            pl.BlockSpec((1, 1, d), lambda i, be, tk, ds: (be[2 * i + off], 0, 0)),
        ]

    grid_spec = pltpu.PrefetchScalarGridSpec(
        num_scalar_prefetch=3,
        grid=(n_steps,),
        in_specs=[
            pl.BlockSpec(memory_space=pl.ANY),
            pl.BlockSpec((bm, 1), lambda i, be, tk, ds: (2 * i, 0)),
            pl.BlockSpec((bm, 1), lambda i, be, tk, ds: (2 * i + 1, 0)),
        ] + wspecs(0) + wspecs(1),
        out_specs=pl.BlockSpec(memory_space=pl.ANY),
        scratch_shapes=[pltpu.VMEM((bm * nc, LANES), F32)] * 4 + [pltpu.SemaphoreType.DMA((2,)),
                                                                  pltpu.SemaphoreType.DMA((2,))],
    )
    bgu3 = b_gu.reshape(e, 1, f2).astype(F32)
    bd3 = b_down.reshape(e, 1, d).astype(F32)
    g2 = gates.reshape(cap, 1)
    return pl.pallas_call(
        functools.partial(_moe_expert_kernel, bm=bm, dff=dff, n_steps=n_steps),
        out_shape=jax.ShapeDtypeStruct(((TOP_K * n + 2 * bm) * nc, LANES), F32),
        grid_spec=grid_spec,
        compiler_params=_cparams(("arbitrary",)),
        name="moe_experts",
    )(blk_exp, tok, dst, h, g2, g2, w_gu, bgu3, w_down, bd3, w_gu, bgu3, w_down, bd3)


def _moe_layer(x, norm_g, router_w, router_b, w_gu, b_gu, w_down, b_down):
    n, d = x.shape
    bm = MOE_BM
    h, idx128, gate128, cnt = _router(x, norm_g, router_w, router_b)
    n_assign = n * TOP_K
    flat_e = idx128[:, :TOP_K].reshape(-1)
    flat_g = gate128[:, :TOP_K].reshape(-1)
    order = jnp.argsort(flat_e).astype(jnp.int32)
    counts = cnt[0, :N_EXPERTS].astype(jnp.int32)
    padded = (counts + bm - 1) // bm * bm
    start = jnp.cumsum(counts) - counts
    pend = jnp.cumsum(padded)
    pstart = pend - padded
    n_blocks = -(-n_assign // bm) + N_EXPERTS
    n_blocks += n_blocks % 2
    blk_start = jnp.arange(n_blocks, dtype=jnp.int32) * bm
    blk_exp = jnp.minimum(jnp.sum((pend[None, :] <= blk_start[:, None]).astype(jnp.int32), axis=1),
                          N_EXPERTS - 1).astype(jnp.int32)
    row = jnp.arange(bm, dtype=jnp.int32)[None, :]
    off = (blk_start - pstart[blk_exp])[:, None] + row
    valid = off < counts[blk_exp][:, None]
    a = order[jnp.clip(start[blk_exp][:, None] + off, 0, n_assign - 1).reshape(-1)].reshape(n_blocks, bm)
    tok = jnp.where(valid, a // TOP_K, 0).astype(jnp.int32).reshape(-1)
    dump_row = TOP_K * n + (jnp.arange(n_blocks, dtype=jnp.int32) % 2)[:, None] * bm + row
    dst = jnp.where(valid, (a % TOP_K) * n + a // TOP_K, dump_row).astype(jnp.int32).reshape(-1)
    gates = jnp.where(valid, flat_g[a.reshape(-1)].reshape(n_blocks, bm), 0.0).reshape(-1)
    return _moe_experts(h, blk_exp, tok, dst, gates, w_gu.astype(BF16), b_gu, w_down.astype(BF16), b_down, bm=bm)


def _ple_kernel(x_ref, y0_ref, y1_ref, y2_ref, y3_ref, g_ref, wg_ref, p_ref, wp_ref, o_ref):
    tm, d = x_ref.shape
    nc = d // LANES

    def plane(y_ref):
        return jnp.concatenate([y_ref[pl.ds(c, tm, stride=nc), :] for c in range(nc)], axis=1)

    xf = x_ref[...] + (((plane(y0_ref) + plane(y1_ref)) + plane(y2_ref)) + plane(y3_ref))
    ms = jnp.mean(xf * xf, axis=-1, keepdims=True)
    h = (xf * lax.rsqrt(ms + EPS) * g_ref[...]).astype(BF16)
    gate = _sigmoid(jnp.dot(h, wg_ref[...], preferred_element_type=F32))
    pp = jnp.dot(p_ref[...].astype(BF16), wp_ref[...], preferred_element_type=F32)
    o_ref[...] = xf + gate * pp


def _ple(x, y, g, wg, p, wp, *, tm=512):
    n, d = x.shape
    pd = p.shape[1]
    tm = min(tm, n)
    nb = n // tm
    nc = d // LANES
    yspecs = [pl.BlockSpec((tm * nc, LANES), functools.partial(lambda i, k: (k * nb + i, 0), k=k)) for k in range(TOP_K)]
    return pl.pallas_call(
        _ple_kernel,
        out_shape=jax.ShapeDtypeStruct((n, d), F32),
        grid=(nb,),
        in_specs=[pl.BlockSpec((tm, d), lambda i: (i, 0))] + yspecs + [
            pl.BlockSpec((1, d), lambda i: (0, 0)),
            pl.BlockSpec((d, d), lambda i: (0, 0)),
            pl.BlockSpec((tm, pd), lambda i: (i, 0)),
            pl.BlockSpec((pd, d), lambda i: (0, 0)),
        ],
        out_specs=pl.BlockSpec((tm, d), lambda i: (i, 0)),
        compiler_params=_cparams(("parallel",)),
        name="ple",
    )(x, y, y, y, y, g.reshape(1, d).astype(F32), wg.astype(BF16), p, wp.astype(BF16))


def _cmp_kernel(z_ref, w1_ref, pos_ref, w2_ref, gain_ref, c_ref, s1_ref, s2_ref, o_ref, *, rows, half):
    j = pl.program_id(1)
    z = z_ref[0, 0, 0]
    pos = pos_ref[0]
    a1 = (z + pos[:, :half]).astype(BF16)
    a2 = (z + pos[:, half:]).astype(BF16)
    u1 = jnp.dot(a1, w1_ref[0, :half, :], preferred_element_type=F32)
    u2 = jnp.dot(a2, w1_ref[0, half:, :], preferred_element_type=F32)
    h = u1 + pltpu.roll(u2, rows - 1, 0)
    hs = h * _sigmoid(h)
    y = jnp.dot(hs.astype(BF16), w2_ref[0], preferred_element_type=F32)

    @pl.when(j == 0)
    def _():
        ss = jnp.sum(y * y, axis=-1, keepdims=True) * (1.0 / LANES)
        yn = y * lax.rsqrt(ss + EPS) * gain_ref[...]
        yn = (yn * c_ref[...] + pltpu.roll(yn, LANES - ROPE_DIM // 2, 1) * s1_ref[...]
              + pltpu.roll(yn, ROPE_DIM // 2, 1) * s2_ref[...])
        o_ref[0, 0, 0] = yn.astype(o_ref.dtype)

    @pl.when(j != 0)
    def _():
        o_ref[0, 0, 0] = y.astype(o_ref.dtype)


def _compress(zr, w1, pos, w2dup, gain_dup, tabs, *, batch):
    rows, width = zr.shape[3], zr.shape[4]
    hid = w1.shape[2]
    g = NSA_GROUPS
    return pl.pallas_call(
        functools.partial(_cmp_kernel, rows=rows, half=width),
        out_shape=jax.ShapeDtypeStruct((batch, 2, g, rows, LANES), BF16),
        grid=(batch, 2, g),
        in_specs=[
            pl.BlockSpec((1, 1, 1, rows, width), lambda b, j, gg: (b, j, gg, 0, 0)),
            pl.BlockSpec((1, 2 * width, hid), lambda b, j, gg: (j, 0, 0)),
            pl.BlockSpec((1, 1, 2 * width), lambda b, j, gg: (j, 0, 0)),
            pl.BlockSpec((1, hid, LANES), lambda b, j, gg: (j, 0, 0)),
            pl.BlockSpec((1, LANES), lambda b, j, gg: (0, 0)),
            pl.BlockSpec((rows, LANES), lambda b, j, gg: (0, 0)),
            pl.BlockSpec((rows, LANES), lambda b, j, gg: (0, 0)),
            pl.BlockSpec((rows, LANES), lambda b, j, gg: (0, 0)),
        ],
        out_specs=pl.BlockSpec((1, 1, 1, rows, LANES), lambda b, j, gg: (b, j, gg, 0, 0)),
        compiler_params=_cparams(("parallel", "parallel", "parallel")),
        name="nsa_compress",
    )(zr, w1, pos, w2dup, gain_dup, *tabs)


def _pipelined(n_items, first, second, skew):
    pending, out = {}, [None] * n_items
    for idx in range(n_items + skew):
        if idx < n_items:
            pending[idx] = first(idx)
        if idx >= skew:
            out[idx - skew] = second(idx - skew, pending.pop(idx - skew))
    return out


def _nsa_attn_kernel(q_ref, gz_ref, kc_ref, vct_ref, ovt_ref, ks_ref, vs_ref, kw_ref, vw_ref, o_ref,
                     e_ref, oc_ref, *, n_cmp_rows, tk_sel, skew, grp, subs):
    g = pl.program_id(1)
    i = pl.program_id(2)
    qb = NSA_QB
    mh = HEADS_PER_GROUP
    rows = mh * qb
    q0 = i * qb
    seq = ks_ref.shape[0]

    @pl.when(i == 0)
    def _():
        r = lax.broadcasted_iota(jnp.int32, (seq, LANES), 0)
        lb = lax.broadcasted_iota(jnp.int32, (seq, LANES), 1)
        e_ref[...] = jnp.where((r >> 6) == lb, NEG, 0.0).astype(BF16)

    qh = [q_ref[:, m * LANES:(m + 1) * LANES] for m in range(mh)]
    q4 = jnp.concatenate(qh, axis=0)
    jl = lax.broadcasted_iota(jnp.int32, (qb, LANES), 1)

    st = _dot_nt(kc_ref[0, 0, 0], q4)
    cend = lax.broadcasted_iota(jnp.int32, (n_cmp_rows, rows), 0) * CMP_STRIDE + (CMP_BLOCK - 1)
    tlane = q0 + (lax.broadcasted_iota(jnp.int32, (1, rows), 1) & (qb - 1))
    valid = cend <= tlane
    st = jnp.where(valid, st, NEG)
    e = jnp.where(valid, jnp.exp2(st - jnp.max(st, axis=0, keepdims=True)), 0.0)
    pt = e / jnp.maximum(jnp.sum(e, axis=0, keepdims=True), 1e-30)
    oct = jnp.dot(vct_ref[0, 0], pt.astype(BF16), preferred_element_type=F32)
    for m in range(mh):
        oc_ref[m * qb:(m + 1) * qb, :] = oct[:, m * qb:(m + 1) * qb].T
    psum = (pt[:, 0:qb] + pt[:, qb:2 * qb]) + (pt[:, 2 * qb:3 * qb] + pt[:, 3 * qb:4 * qb])
    phi, plo = _split2(psum)
    ovt = ovt_ref[...]
    imp = jnp.dot(ovt, phi, preferred_element_type=F32) + jnp.dot(ovt, plo, preferred_element_type=F32)

    jb = lax.broadcasted_iota(jnp.int32, (LANES, qb), 0)
    tq = q0 + lax.broadcasted_iota(jnp.int32, (LANES, qb), 1)
    cur = tq >> 6
    forced = (jb == 0) | (jb == cur) | (jb == cur - 1)
    causal = (jb << 6) <= tq
    score = jnp.where(causal, jnp.where(forced, FORCE, imp), -jnp.inf)
    notsel_t = jnp.ones((LANES, qb), F32)
    jbf = jb.astype(F32)
    for _ in range(SEL_TOPK):
        smx = jnp.max(score, axis=0, keepdims=True)
        ix = jnp.min(jnp.where(score == smx, jbf, float(LANES)), axis=0, keepdims=True)
        hit = jbf == ix
        notsel_t = jnp.where(hit, 0.0, notsel_t)
        score = jnp.where(hit, -jnp.inf, score)
    notsel_b = notsel_t.T.astype(BF16)
    n_grp = mh // grp
    q_aug = [jnp.concatenate([jnp.concatenate([qh[m], notsel_b], axis=1) for m in range(c * grp, (c + 1) * grp)], axis=0)
             for c in range(n_grp)]
    tpos = q0 + lax.broadcasted_iota(jnp.int32, (qb, 1), 0)
    tpos_g = jnp.concatenate([tpos] * grp, axis=0)

    def sel_tiles(first, n_sub):
        ks, vs = [], []
        for t in range(n_sub):
            k0 = pl.multiple_of((first + t) * tk_sel, tk_sel)
            k_aug = jnp.concatenate([ks_ref[pl.ds(k0, tk_sel), :], e_ref[pl.ds(k0, tk_sel), :]], axis=1)
            ks.append([k_aug] * n_grp)
            vs.append([vs_ref[pl.ds(k0, tk_sel), :]] * n_grp)
        return ks, vs

    def sel_step(first, n_sub, state):
        ks, vs = sel_tiles(first, n_sub)
        return _attend(q_aug, ks, vs, state, skew)

    state = tuple((jnp.full((grp * qb, 1), NEG, F32), jnp.zeros((grp * qb, LANES), F32)) for _ in range(n_grp))
    n_below = q0 // tk_sel
    state = _tile_loops(n_below, subs, sel_step, state)

    def causal(t, c, sc):
        kpos = n_below * tk_sel + lax.broadcasted_iota(jnp.int32, (grp * qb, tk_sel), 1)
        return jnp.where(kpos <= tpos_g, sc, NEG)

    ks, vs = sel_tiles(n_below, 1)
    state = _attend(q_aug, ks, vs, state, skew, mask=causal)
    sel_acc = [state[m // grp][1][(m % grp) * qb:(m % grp + 1) * qb] for m in range(mh)]

    wk = WINDOW + qb
    k0w = pl.multiple_of(jnp.maximum(i - WINDOW // qb, 0) * qb, qb)
    kwt = kw_ref[pl.ds(k0w, wk), :]
    vwt = vw_ref[pl.ds(k0w, wk), :]
    kposw = k0w + lax.broadcasted_iota(jnp.int32, (grp * qb, wk), 1)
    wmask = (kposw <= tpos_g) & (kposw > tpos_g - WINDOW)
    q_grp = [jnp.concatenate(qh[c * grp:(c + 1) * grp], axis=0) for c in range(n_grp)]

    def win_scores(c):
        return jnp.where(wmask, _dot_nt(q_grp[c], kwt), NEG)

    def win_out(c, sc):
        pe = jnp.exp2(sc - jnp.max(sc, axis=-1, keepdims=True))
        return jnp.dot(pe.astype(BF16), vwt, preferred_element_type=F32)

    win_grp = _pipelined(n_grp, win_scores, win_out, min(skew, n_grp - 1))
    win_acc = [win_grp[m // grp][(m % grp) * qb:(m % grp + 1) * qb] for m in range(mh)]

    sig = _sigmoid(gz_ref[...])
    den_lane = jl == HEAD_DIM
    heads = []
    for m in range(mh):
        col = g * mh + m
        gates = [jnp.sum(jnp.where(jl == br * N_HEADS + col, sig, 0.0), axis=-1, keepdims=True)
                 for br in range(3)]
        acc_s = sel_acc[m]
        acc_w = win_acc[m]
        den_s = jnp.sum(jnp.where(den_lane, acc_s, 0.0), axis=-1, keepdims=True)
        den_w = jnp.sum(jnp.where(den_lane, acc_w, 0.0), axis=-1, keepdims=True)
        heads.append(gates[0] * oc_ref[m * qb:(m + 1) * qb, :] + (gates[1] / den_s) * acc_s
                     + (gates[2] / den_w) * acc_w)
    lane_lo = jl < HEAD_DIM
    o_ref[:, 0:LANES] = jnp.where(lane_lo, heads[0], pltpu.roll(heads[1], HEAD_DIM, 1)).astype(o_ref.dtype)
    o_ref[:, LANES:2 * LANES] = jnp.where(lane_lo, heads[2], pltpu.roll(heads[3], HEAD_DIM, 1)).astype(o_ref.dtype)


def _nsa_attn(qn, gz, kvc, overlap, kvd, *, batch, seq, skew=2, grp=2, subs=(4, 2, 1)):
    n = qn.shape[0]
    qb = NSA_QB
    nq = seq // qb
    g = NSA_GROUPS
    rows = HEADS_PER_GROUP * qb
    n_cmp_rows = kvc.shape[3]
    tk_sel = min(512, seq)
    assert seq >= WINDOW + qb and seq % tk_sel == 0 and tk_sel % qb == 0
    in_specs = [
        pl.BlockSpec((qb, HEADS_PER_GROUP * LANES), lambda b, gg, i: (b * nq + i, gg)),
        pl.BlockSpec((qb, LANES), lambda b, gg, i: (b * nq + i, 0)),
        pl.BlockSpec((1, 1, 1, n_cmp_rows, LANES), lambda b, gg, i: (b, 0, gg, 0, 0)),
        pl.BlockSpec((1, 1, LANES, n_cmp_rows), lambda b, gg, i: (b, gg, 0, 0)),
        pl.BlockSpec((LANES, n_cmp_rows), lambda b, gg, i: (0, 0)),
        pl.BlockSpec((seq, LANES), lambda b, gg, i: (b, 0 * g + gg)),
        pl.BlockSpec((seq, LANES), lambda b, gg, i: (b, 1 * g + gg)),
        pl.BlockSpec((seq, LANES), lambda b, gg, i: (b, 2 * g + gg)),
        pl.BlockSpec((seq, LANES), lambda b, gg, i: (b, 3 * g + gg)),
    ]
    return pl.pallas_call(
        functools.partial(_nsa_attn_kernel, n_cmp_rows=n_cmp_rows, tk_sel=tk_sel, skew=skew, grp=grp, subs=subs),
        out_shape=jax.ShapeDtypeStruct((n, N_HEADS * HEAD_DIM), BF16),
        grid=(batch, g, nq),
        in_specs=in_specs,
        out_specs=pl.BlockSpec((qb, HEADS_PER_GROUP * HEAD_DIM), lambda b, gg, i: (b * nq + i, gg)),
        scratch_shapes=[pltpu.VMEM((seq, LANES), BF16), pltpu.VMEM((rows, LANES), F32)],
        compiler_params=_cparams(("parallel", "parallel", "arbitrary")),
        name="nsa_attn",
    )(qn, gz, kvc, jnp.swapaxes(kvc[:, 1], -1, -2), overlap.T, kvd, kvd, kvd, kvd)


def _pad_cols(w, width=LANES):
    d, f = w.shape
    return jnp.zeros((d, width), w.dtype).at[:, :f].set(w)


def _tile_gain(gain, n_heads, scale=1.0, dup=False):
    g = gain.astype(F32) * scale
    second = g if dup else jnp.zeros_like(g)
    return jnp.tile(jnp.concatenate([g, second]), n_heads)


def _fox_aug_tables():
    hw = N_HEADS * LANES
    piece = jnp.arange(3)[:, None]
    head = jnp.arange(N_HEADS)[None, :]
    src = (piece * N_HEADS + head).reshape(-1)
    q_dst = (head * LANES + HEAD_DIM + piece).reshape(-1)
    k_dst = (hw + head * LANES + HEAD_DIM + 3 + piece).reshape(-1)
    place = jnp.zeros((LANES, 3 * hw), F32).at[src, q_dst].set(1.0).at[src, k_dst].set(-1.0)
    lane = jnp.arange(LANES)
    q_bias = ((lane >= HEAD_DIM + 3) & (lane < HEAD_DIM + 6)).astype(F32)
    k_bias = ((lane >= HEAD_DIM) & (lane < HEAD_DIM + 3)).astype(F32)
    v_bias = (lane == HEAD_DIM).astype(F32)
    bias = jnp.concatenate([jnp.tile(q_bias, N_HEADS), jnp.tile(k_bias, N_HEADS), jnp.tile(v_bias, N_HEADS)])
    return place.astype(BF16), bias


def _fox_layer(x, batch, seq, attn_g, w_in, b_f, qk_gain, w_out):
    hd = N_HEADS * HEAD_DIM
    hw = N_HEADS * LANES
    scale = HEAD_DIM ** -0.5
    z = _rms_proj(x, attn_g, _pad_cols(w_in[:, 3 * hd:]).astype(BF16), out_dtype=F32, seq=seq)
    b_pad = jnp.zeros((1, LANES), F32).at[0, :N_HEADS].set(b_f.astype(F32))
    c3 = _fox_c(z, b_pad, batch=batch, seq=seq)
    w_bf = w_in.astype(BF16)
    gain = jnp.concatenate([_tile_gain(qk_gain[0], N_HEADS, scale * LOG2E), _tile_gain(qk_gain[1], N_HEADS),
                            jnp.zeros((hw,), F32)])
    tn = 512
    place, bias = _fox_aug_tables()
    qkv = _rms_proj(x, attn_g, w_bf, out_dtype=BF16, seq=seq, tn=tn, gain=gain, w_cols=3 * hd,
                    norm_blocks=tuple(range(2 * hw // tn)), placed=(c3, place), col_bias=bias, expand=True)
    o = _fox_attn(qkv, batch=batch, seq=seq)
    return _matmul_res(o, w_out.astype(BF16), x)


def _nsa_shared_kv(x, batch, seq, kv_norm, kv_w, kv_k_gain, cmp_pos, cmp_w1, cmp_w2):
    g = NSA_GROUPS
    gw = g * HEAD_DIM
    tabs = _rope_tables(jnp.arange(seq))
    kv_bf = kv_w.astype(BF16)
    zero = jnp.zeros((g * LANES,), F32)
    one_lane = jnp.tile((jnp.arange(LANES) == HEAD_DIM).astype(F32), g)
    gain_kvd = jnp.concatenate([_tile_gain(kv_k_gain[1], g), zero, _tile_gain(kv_k_gain[2], g), zero])
    bias_kvd = jnp.concatenate([zero, one_lane, zero, one_lane])
    kvd = _rms_proj(x, kv_norm, kv_bf, out_dtype=BF16, seq=seq, tn=g * LANES, gain=gain_kvd, w_col0=2 * gw,
                    norm_blocks=(0, 2), rope_tabs=tabs, col_bias=bias_kvd, expand=True)
    zc = _rms_proj(x, kv_norm, kv_bf, out_dtype=F32, seq=seq, w_cols=2 * gw)
    rows = seq // CMP_STRIDE
    zr = zc.reshape(batch, rows, CMP_STRIDE, 2, g, HEAD_DIM).transpose(0, 3, 4, 1, 2, 5)
    zr = zr.reshape(batch, 2, g, rows, CMP_STRIDE * HEAD_DIM)
    cmp_end = jnp.arange(rows) * CMP_STRIDE + CMP_BLOCK - 1
    ctabs = _rope_tables(cmp_end)
    w2dup = jnp.concatenate([cmp_w2, cmp_w2], axis=-1).astype(BF16)
    kvc = _compress(zr, cmp_w1.astype(BF16), cmp_pos.reshape(2, 1, CMP_BLOCK * HEAD_DIM).astype(F32), w2dup,
                    _tile_gain(kv_k_gain[0], 1, dup=True).reshape(1, LANES), ctabs, batch=batch)
    return kvc, kvd


def _nsa_layer(x, batch, seq, attn_g, w_in, q_gain, w_out, kvc, kvd):
    hd = N_HEADS * HEAD_DIM
    scale = HEAD_DIM ** -0.5
    rows = seq // CMP_STRIDE
    tabs = _rope_tables(jnp.arange(seq))
    qn = _rms_proj(x, attn_g, w_in.astype(BF16), out_dtype=BF16, seq=seq, expand=True, w_cols=hd,
                   gain=_tile_gain(q_gain, N_HEADS, scale * LOG2E), norm_blocks=tuple(range(N_HEADS * LANES // 512)),
                   rope_tabs=tabs)
    gz = _rms_proj(x, attn_g, _pad_cols(w_in[:, hd:]).astype(BF16), out_dtype=F32, seq=seq)

    n_sel = seq // SEL_BLOCK
    assert n_sel <= LANES
    cmp_start = jnp.arange(rows) * CMP_STRIDE
    sel_start = jnp.arange(LANES) * SEL_BLOCK
    overlap = jnp.clip(jnp.minimum(cmp_start[:, None] + CMP_BLOCK, sel_start[None, :] + SEL_BLOCK)
                       - jnp.maximum(cmp_start[:, None], sel_start[None, :]), 0)
    overlap = jnp.where((jnp.arange(rows) < rows - 1)[:, None] & (jnp.arange(LANES) < n_sel)[None, :], overlap, 0)
    o = _nsa_attn(qn, gz, kvc, overlap.astype(BF16), kvd, batch=batch, seq=seq)
    return _matmul_res(o, w_out.astype(BF16), x)


def kernel(x, p, attn_norm, ffn_norm, ple_norm, ple_gate_w, ple_proj_w, router_w, router_b, w_gu, b_gu, w_down, b_down, fox_w_in, fox_b_f, fox_qk_gain, fox_w_out, kv_norm, kv_w, kv_k_gain, cmp_pos, cmp_w1, cmp_w2, nsa_w_in, nsa_q_gain, nsa_w_out):
    batch, seq, d = x.shape
    depth = p.shape[0]
    n_a = fox_w_in.shape[0]
    xt = x.reshape(batch * seq, d)
    shared = None
    for layer in range(depth):
        if layer == n_a:
            shared = _nsa_shared_kv(xt, batch, seq, kv_norm, kv_w, kv_k_gain, cmp_pos, cmp_w1, cmp_w2)
        if layer < n_a:
            xt = _fox_layer(xt, batch, seq, attn_norm[layer], fox_w_in[layer], fox_b_f[layer],
                            fox_qk_gain[layer], fox_w_out[layer])
        else:
            i = layer - n_a
            xt = _nsa_layer(xt, batch, seq, attn_norm[layer], nsa_w_in[i], nsa_q_gain[i], nsa_w_out[i], *shared)
        y = _moe_layer(xt, ffn_norm[layer], router_w[layer], router_b[layer], w_gu[layer], b_gu[layer],
                       w_down[layer], b_down[layer])
        xt = _ple(xt, y, ple_norm[layer], ple_gate_w[layer], p[layer].reshape(batch * seq, -1), ple_proj_w[layer])
    return xt.reshape(batch, seq, d)
```

```python
import functools

import jax
import jax.numpy as jnp
from jax import lax
from jax.experimental import pallas as pl
from jax.experimental.pallas import tpu as pltpu

F32 = jnp.float32
BF16 = jnp.bfloat16

N_HEADS = 16
HEAD_DIM = 64
ROPE_DIM = HEAD_DIM // 4
ROPE_THETA = 500000.0
NSA_GROUPS = 4
HEADS_PER_GROUP = N_HEADS // NSA_GROUPS
CMP_BLOCK = 32
CMP_STRIDE = 16
SEL_BLOCK = 64
SEL_TOPK = 16
WINDOW = 512
N_EXPERTS = 32
TOP_K = 4
SWIGLU_LIMIT = 7.0
SWIGLU_ALPHA = 1.702
EPS = 1e-6
NEG = -1e30
FORCE = 1e6
LOG2E = 1.4426950408889634

LANES = 128
NSA_QB = 256
MOE_BM = 256
VMEM_LIMIT = 56 * 1024 * 1024


def _cparams(sem):
    return pltpu.CompilerParams(dimension_semantics=sem, vmem_limit_bytes=VMEM_LIMIT)


def _sigmoid(x):
    return 1.0 / (1.0 + jnp.exp(-x))


def _dot_nt(a, b):
    return lax.dot_general(a, b, (((1,), (1,)), ((), ())), preferred_element_type=F32)


def _split2(x):
    hi = x.astype(BF16)
    lo = (x - hi.astype(F32)).astype(BF16)
    return hi, lo


def _proj_kernel(*refs, norm_blocks, rope, placed, biased, expand, head_div, tn, out_dtype):
    refs = list(refs)
    x_ref, g_ref, w_ref = refs[:3]
    del refs[:3]
    if norm_blocks:
        gain_ref = refs.pop(0)
    if rope:
        c_ref, s1_ref, s2_ref = refs[:3]
        del refs[:3]
    if placed:
        ex_ref, place_ref = refs[:2]
        del refs[:2]
    if biased:
        bias_ref = refs.pop(0)
    o_ref, h_ref = refs
    j = pl.program_id(1)
    aug = placed or biased

    @pl.when(j == 0)
    def _():
        xf = x_ref[...]
        ms = jnp.mean(xf * xf, axis=-1, keepdims=True)
        h_ref[...] = (xf * lax.rsqrt(ms + EPS) * g_ref[...]).astype(BF16)

    acc = jnp.dot(h_ref[...], w_ref[...], preferred_element_type=F32)
    if placed and biased:
        extra = jnp.dot(ex_ref[...], place_ref[...], preferred_element_type=F32) + bias_ref[...]
    elif biased:
        extra = jnp.broadcast_to(bias_ref[...], (acc.shape[0], tn))
    if expand:
        lo = lax.broadcasted_iota(jnp.int32, (acc.shape[0], LANES), 1) < HEAD_DIM
        blocks = []
        for c in range(acc.shape[1] // LANES):
            a = acc[:, c * LANES:(c + 1) * LANES]
            blocks += [jnp.where(lo, a, 0.0), jnp.where(lo, pltpu.roll(a, HEAD_DIM, 1), 0.0)]
    else:
        blocks = [acc[:, c * LANES:(c + 1) * LANES] for c in range(acc.shape[1] // LANES)]

    def plain():
        for c, a in enumerate(blocks):
            sl = slice(c * LANES, (c + 1) * LANES)
            o_ref[:, sl] = ((a + extra[:, sl]) if aug else a).astype(out_dtype)

    if not norm_blocks:
        plain()
        return
    is_norm = functools.reduce(jnp.logical_or, [j == c for c in norm_blocks])

    @pl.when(is_norm)
    def _():
        for c, a in enumerate(blocks):
            sl = slice(c * LANES, (c + 1) * LANES)
            ss = jnp.sum(a * a, axis=-1, keepdims=True) * (1.0 / head_div)
            y = a * lax.rsqrt(ss + EPS) * gain_ref[:, sl]
            if rope:
                y = (y * c_ref[...] + pltpu.roll(y, LANES - ROPE_DIM // 2, 1) * s1_ref[...]
                     + pltpu.roll(y, ROPE_DIM // 2, 1) * s2_ref[...])
            if aug:
                y = y + extra[:, sl]
            o_ref[:, sl] = y.astype(out_dtype)

    @pl.when(jnp.logical_not(is_norm))
    def _():
        plain()


def _rms_proj(x, g, w, *, out_dtype, seq, tm=512, tn=512, gain=None, norm_blocks=(),
              rope_tabs=None, placed=None, col_bias=None, expand=False, head_div=HEAD_DIM,
              w_col0=0, w_cols=None):
    n, d = x.shape
    w_cols = w.shape[1] - w_col0 if w_cols is None else w_cols
    f = w_cols * (2 if expand else 1)
    tn = min(tn, f)
    tw = tn // 2 if expand else tn
    assert w_col0 % tw == 0
    j0 = w_col0 // tw
    tm = min(tm, seq)
    assert n % tm == 0 and f % tn == 0 and seq % tm == 0
    nt = seq // tm
    in_specs = [
        pl.BlockSpec((tm, d), lambda i, j: (i, 0)),
        pl.BlockSpec((1, d), lambda i, j: (0, 0)),
        pl.BlockSpec((d, tw), lambda i, j: (0, j0 + j)),
    ]
    args = [x, g.reshape(1, d).astype(F32), w]
    if norm_blocks:
        in_specs.append(pl.BlockSpec((1, tn), lambda i, j: (0, j)))
        args.append(gain.reshape(1, f).astype(F32))
    if rope_tabs is not None:
        assert norm_blocks
        for tab in rope_tabs:
            in_specs.append(pl.BlockSpec((tm, LANES), lambda i, j: (i % nt, 0)))
            args.append(tab)
    if placed is not None:
        assert col_bias is not None
        values, place = placed
        in_specs += [pl.BlockSpec((tm, LANES), lambda i, j: (i, 0)),
                     pl.BlockSpec((LANES, tn), lambda i, j: (0, j))]
        args += [values, place]
    if col_bias is not None:
        in_specs.append(pl.BlockSpec((1, tn), lambda i, j: (0, j)))
        args.append(col_bias.reshape(1, f).astype(F32))
    kern = functools.partial(_proj_kernel, norm_blocks=tuple(norm_blocks), rope=rope_tabs is not None,
                             placed=placed is not None, biased=col_bias is not None, expand=expand,
                             head_div=float(head_div), tn=tn, out_dtype=out_dtype)
    return pl.pallas_call(
        kern,
        out_shape=jax.ShapeDtypeStruct((n, f), out_dtype),
        grid=(n // tm, f // tn),
        in_specs=in_specs,
        out_specs=pl.BlockSpec((tm, tn), lambda i, j: (i, j)),
        scratch_shapes=[pltpu.VMEM((tm, d), BF16)],
        compiler_params=_cparams(("parallel", "arbitrary")),
        name="rms_proj",
    )(*args)


def _rope_tables(pos, width=LANES):
    half = ROPE_DIM // 2
    inv = jnp.power(jnp.float32(ROPE_THETA), -jnp.arange(0, ROPE_DIM, 2, dtype=F32) / ROPE_DIM)
    ang = pos.astype(F32)[:, None] * inv[None, :]
    cos, sin = jnp.cos(ang), jnp.sin(ang)
    t = pos.shape[0]
    ones = jnp.ones((t, HEAD_DIM - ROPE_DIM), F32)
    zeros_h = jnp.zeros((t, half), F32)
    zeros_r = jnp.zeros((t, HEAD_DIM - ROPE_DIM), F32)
    c = jnp.concatenate([cos, cos, ones], axis=1)
    s1 = jnp.concatenate([-sin, zeros_h, zeros_r], axis=1)
    s2 = jnp.concatenate([zeros_h, sin, zeros_r], axis=1)
    rep = width // HEAD_DIM
    return tuple(jnp.tile(a, (1, rep)) for a in (c, s1, s2))


def _matmul_res_kernel(a_ref, w_ref, x_ref, o_ref):
    o_ref[...] = x_ref[...] + jnp.dot(a_ref[...], w_ref[...], preferred_element_type=F32)


def _matmul_res(a, w, x, *, tm=512):
    n, k = a.shape
    f = w.shape[1]
    tm = min(tm, n)
    return pl.pallas_call(
        _matmul_res_kernel,
        out_shape=jax.ShapeDtypeStruct((n, f), F32),
        grid=(n // tm,),
        in_specs=[
            pl.BlockSpec((tm, k), lambda i: (i, 0)),
            pl.BlockSpec((k, f), lambda i: (0, 0)),
            pl.BlockSpec((tm, f), lambda i: (i, 0)),
        ],
        out_specs=pl.BlockSpec((tm, f), lambda i: (i, 0)),
        compiler_params=_cparams(("parallel",)),
        name="matmul_res",
    )(a, w, x)


def _fox_c_kernel(z_ref, b_ref, c_ref, carry_ref, *, tm):
    t = pl.program_id(1)

    @pl.when(t == 0)
    def _():
        carry_ref[...] = jnp.zeros_like(carry_ref)

    u = z_ref[...] + b_ref[...]
    logf = jnp.minimum(u, 0.0) - jnp.log(1.0 + jnp.exp(-jnp.abs(u)))
    row = lax.broadcasted_iota(jnp.int32, (tm, tm), 0)
    col = lax.broadcasted_iota(jnp.int32, (tm, tm), 1)
    tri = (row >= col).astype(BF16)
    hi = logf.astype(BF16)
    r1 = logf - hi.astype(F32)
    mid = r1.astype(BF16)
    lo = (r1 - mid.astype(F32)).astype(BF16)
    cs = (jnp.dot(tri, hi, preferred_element_type=F32) + jnp.dot(tri, mid, preferred_element_type=F32)
          + jnp.dot(tri, lo, preferred_element_type=F32))
    c = cs + carry_ref[...]
    carry_ref[...] = c[tm - 1:tm, :]
    c2 = c * LOG2E
    p0 = c2.astype(BF16).astype(F32)
    p1 = (c2 - p0).astype(BF16).astype(F32)
    p2 = ((c2 - p0) - p1).astype(BF16).astype(F32)
    lane = lax.broadcasted_iota(jnp.int32, c.shape, 1)
    out = jnp.where(lane < N_HEADS, p0,
                    jnp.where(lane < 2 * N_HEADS, pltpu.roll(p1, N_HEADS, 1),
                              jnp.where(lane < 3 * N_HEADS, pltpu.roll(p2, 2 * N_HEADS, 1), 0.0)))
    c_ref[...] = out.astype(BF16)


def _fox_c(z, b_pad, *, batch, seq, tm=512):
    tm = min(tm, seq)
    nt = seq // tm
    return pl.pallas_call(
        functools.partial(_fox_c_kernel, tm=tm),
        out_shape=jax.ShapeDtypeStruct(z.shape, BF16),
        grid=(batch, nt),
        in_specs=[
            pl.BlockSpec((tm, LANES), lambda b, t: (b * nt + t, 0)),
            pl.BlockSpec((1, LANES), lambda b, t: (0, 0)),
        ],
        out_specs=pl.BlockSpec((tm, LANES), lambda b, t: (b * nt + t, 0)),
        scratch_shapes=[pltpu.VMEM((1, LANES), F32)],
        compiler_params=_cparams(("parallel", "arbitrary")),
        name="fox_cumsum",
    )(z, b_pad)


def _attend(qs, ks, vs, state, skew, mask=None):
    items = [(t, c) for t in range(len(ks)) for c in range(len(qs))]
    state = list(state)
    pending = {}
    depth = min(skew, len(items) - 1)
    for idx in range(len(items) + depth):
        if idx < len(items):
            t, c = items[idx]
            sc = _dot_nt(qs[c], ks[t][c])
            pending[idx] = sc if mask is None else mask(t, c, sc)
        if idx >= depth:
            t, c = items[idx - depth]
            sc = pending.pop(idx - depth)
            m_prev, acc = state[c]
            m_new = jnp.maximum(m_prev, jnp.max(sc, axis=-1, keepdims=True))
            pe = jnp.exp2(sc - m_new)
            acc = jnp.exp2(m_prev - m_new) * acc + jnp.dot(pe.astype(BF16), vs[t][c], preferred_element_type=F32)
            state[c] = (m_new, acc)
    return tuple(state)


def _tile_loops(n_tiles, subs, step, state):
    done = 0
    for sub in subs:
        n_here = (n_tiles - done) // sub
        state = lax.fori_loop(0, n_here, lambda j, st, done=done, sub=sub: step(done + j * sub, sub, st), state)
        done = done + n_here * sub
    return state


def _fox_attn_kernel(q0_ref, q1_ref, k0_ref, k1_ref, v0_ref, v1_ref, o_ref, *, tq, tk, skew, subs):
    qi = pl.program_id(2)
    k_refs = (k0_ref, k1_ref)
    v_refs = (v0_ref, v1_ref)
    qs = [q0_ref[...], q1_ref[...]]

    def tiles(refs, first, n_sub):
        return [[r[pl.ds(pl.multiple_of((first + t) * tk, tk), tk), :] for r in refs] for t in range(n_sub)]

    def full_step(first, n_sub, state):
        return _attend(qs, tiles(k_refs, first, n_sub), tiles(v_refs, first, n_sub), state, skew)

    def causal(t, c, sc):
        tpos = lax.broadcasted_iota(jnp.int32, (tq, tk), 0)
        kpos = t * tk + lax.broadcasted_iota(jnp.int32, (tq, tk), 1)
        return jnp.where(kpos <= tpos, sc, NEG)

    state = tuple((jnp.full((tq, 1), NEG, F32), jnp.zeros((tq, LANES), F32)) for _ in qs)
    n_below = (qi * tq) // tk
    state = _tile_loops(n_below, subs, full_step, state)
    n_diag = tq // tk
    state = _attend(qs, tiles(k_refs, n_below, n_diag), tiles(v_refs, n_below, n_diag), state, skew, mask=causal)

    lane = lax.broadcasted_iota(jnp.int32, (tq, LANES), 1)
    outs = []
    for _, acc in state:
        den = jnp.sum(jnp.where(lane == HEAD_DIM, acc, 0.0), axis=-1, keepdims=True)
        outs.append(acc / den)
    o_ref[...] = jnp.where(lane < HEAD_DIM, outs[0], pltpu.roll(outs[1], HEAD_DIM, 1)).astype(o_ref.dtype)


def _fox_attn(qkv, *, batch, seq, tq=512, tk=512, skew=2, subs=(4, 2, 1)):
    n = qkv.shape[0]
    tq = min(tq, seq)
    tk = min(tk, tq)
    assert seq % tq == 0 and tq % tk == 0
    nq = seq // tq
    hp = N_HEADS // 2
    kb = N_HEADS
    vb = 2 * N_HEADS
    in_specs = [
        pl.BlockSpec((tq, LANES), lambda b, h, i: (b * nq + i, 2 * h)),
        pl.BlockSpec((tq, LANES), lambda b, h, i: (b * nq + i, 2 * h + 1)),
        pl.BlockSpec((seq, LANES), lambda b, h, i: (b, kb + 2 * h)),
        pl.BlockSpec((seq, LANES), lambda b, h, i: (b, kb + 2 * h + 1)),
        pl.BlockSpec((seq, LANES), lambda b, h, i: (b, vb + 2 * h)),
        pl.BlockSpec((seq, LANES), lambda b, h, i: (b, vb + 2 * h + 1)),
    ]
    return pl.pallas_call(
        functools.partial(_fox_attn_kernel, tq=tq, tk=tk, skew=skew, subs=subs),
        out_shape=jax.ShapeDtypeStruct((n, N_HEADS * HEAD_DIM), BF16),
        grid=(batch, hp, nq),
        in_specs=in_specs,
        out_specs=pl.BlockSpec((tq, LANES), lambda b, h, i: (b * nq + i, h)),
        compiler_params=_cparams(("parallel", "parallel", "arbitrary")),
        name="fox_attn",
    )(qkv, qkv, qkv, qkv, qkv, qkv)


def _router_kernel(x_ref, g_ref, whi_ref, wlo_ref, b_ref, h_ref, idx_ref, gate_ref, cnt_ref):
    xf = x_ref[...]
    ms = jnp.mean(xf * xf, axis=-1, keepdims=True)
    h = xf * lax.rsqrt(ms + EPS) * g_ref[...]
    nc = h.shape[1] // LANES
    for c in range(nc):
        h_ref[pl.ds(c, h.shape[0], stride=nc), :] = h[:, c * LANES:(c + 1) * LANES]
    hhi, hlo = _split2(h)
    whi = whi_ref[...]
    logits = (jnp.dot(hhi, whi, preferred_element_type=F32) + jnp.dot(hlo, whi, preferred_element_type=F32)
              + jnp.dot(hhi, wlo_ref[...], preferred_element_type=F32)) + b_ref[...]
    lane = lax.broadcasted_iota(jnp.int32, logits.shape, 1).astype(F32)
    vals, idxs = [], []
    cur = logits
    for _ in range(TOP_K):
        mx = jnp.max(cur, axis=-1, keepdims=True)
        ix = jnp.min(jnp.where(cur == mx, lane, float(LANES)), axis=-1, keepdims=True)
        vals.append(mx)
        idxs.append(ix)
        cur = jnp.where(lane == ix, -jnp.inf, cur)
    es = [jnp.exp(v - vals[0]) for v in vals]
    tot = es[0] + es[1] + es[2] + es[3]
    idx_out = jnp.zeros(logits.shape, F32)
    gate_out = jnp.zeros(logits.shape, F32)
    for r in range(TOP_K):
        idx_out = jnp.where(lane == r, idxs[r], idx_out)
        gate_out = jnp.where(lane == r, es[r] / tot, gate_out)
    idx_ref[...] = idx_out.astype(jnp.int32)
    gate_ref[...] = gate_out
    hits = sum(jnp.sum((lane == ix).astype(F32), axis=0, keepdims=True) for ix in idxs)

    @pl.when(pl.program_id(0) == 0)
    def _():
        cnt_ref[...] = jnp.zeros(cnt_ref.shape, cnt_ref.dtype)

    cnt_ref[...] += jnp.broadcast_to(hits, cnt_ref.shape)


def _router(x, g, w, b, *, tm=512):
    n, d = x.shape
    e = w.shape[1]
    tm = min(tm, n)
    w_pad = jnp.zeros((d, LANES), F32).at[:, :e].set(w)
    whi = w_pad.astype(BF16)
    wlo = (w_pad - whi.astype(F32)).astype(BF16)
    b_pad = jnp.full((1, LANES), -jnp.inf, F32).at[0, :e].set(b.astype(F32))
    return pl.pallas_call(
        _router_kernel,
        out_shape=(jax.ShapeDtypeStruct((n * (d // LANES), LANES), F32), jax.ShapeDtypeStruct((n, LANES), jnp.int32),
                   jax.ShapeDtypeStruct((n, LANES), F32), jax.ShapeDtypeStruct((8, LANES), F32)),
        grid=(n // tm,),
        in_specs=[
            pl.BlockSpec((tm, d), lambda i: (i, 0)),
            pl.BlockSpec((1, d), lambda i: (0, 0)),
            pl.BlockSpec((d, LANES), lambda i: (0, 0)),
            pl.BlockSpec((d, LANES), lambda i: (0, 0)),
            pl.BlockSpec((1, LANES), lambda i: (0, 0)),
        ],
        out_specs=(pl.BlockSpec((tm * (d // LANES), LANES), lambda i: (i, 0)), pl.BlockSpec((tm, LANES), lambda i: (i, 0)),
                   pl.BlockSpec((tm, LANES), lambda i: (i, 0)), pl.BlockSpec((8, LANES), lambda i: (0, 0))),
        compiler_params=_cparams(("arbitrary",)),
        name="moe_router",
    )(x, g.reshape(1, d).astype(F32), whi, wlo, b_pad)


def _moe_expert_kernel(blk_exp_ref, tok_ref, dst_ref, live_ref, h_hbm, ga_ref, gb_ref, wgu_a, bgu_a, wd_a, bd_a,
                       wgu_b, bgu_b, wd_b, bd_b, y_hbm, xa, xb, oa, ob, gsem, ssem, *, bm, dff):
    i = pl.program_id(0)
    live = live_ref[0]
    nc = xa.shape[0] // bm
    dump = y_hbm.shape[0] // nc - 2 * bm

    def gather(block, buf, sem):
        base = block * bm
        for r in range(bm):
            pltpu.make_async_copy(h_hbm.at[pl.ds(pl.multiple_of(tok_ref[base + r], nc), nc)], buf.at[pl.ds(r * nc, nc)], sem).start()

    def scatter(block, buf, sem):
        base = block * bm
        for r in range(bm):
            pltpu.make_async_copy(buf.at[pl.ds(r * nc, nc)], y_hbm.at[pl.ds(pl.multiple_of(dst_ref[base + r], nc), nc)], sem).start()

    def wait_gather(buf, sem):
        pltpu.make_async_copy(h_hbm.at[pl.ds(0, bm * nc)], buf, sem).wait()

    def wait_scatter(buf, sem):
        pltpu.make_async_copy(buf, y_hbm.at[pl.ds(0, bm * nc)], sem).wait()

    def experts(x_ref, o_ref, g_ref, wgu, bgu, wd, bd):
        x = jnp.concatenate([x_ref[pl.ds(c, bm, stride=nc), :] for c in range(nc)], axis=1)
        gu = jnp.dot(x.astype(BF16), wgu[0, 0], preferred_element_type=F32) + bgu[0]
        a = jnp.minimum(gu[:, :dff], SWIGLU_LIMIT)
        u = jnp.clip(gu[:, dff:], -SWIGLU_LIMIT, SWIGLU_LIMIT)
        y = (u + 1.0) * (a * _sigmoid(SWIGLU_ALPHA * a))
        out = (jnp.dot(y.astype(BF16), wd[0, 0], preferred_element_type=F32) + bd[0]) * g_ref[...]
        for c in range(nc):
            o_ref[pl.ds(c, bm, stride=nc), :] = out[:, c * LANES:(c + 1) * LANES]

    @pl.when(i == 0)
    def _():
        gather(0, xa, gsem.at[0])
        for buf, sem, base in ((oa, ssem.at[0], dump), (ob, ssem.at[1], dump + bm)):
            buf[...] = jnp.zeros(buf.shape, buf.dtype)
            for r in range(bm):
                pltpu.make_async_copy(buf.at[pl.ds(r * nc, nc)], y_hbm.at[pl.ds((base + r) * nc, nc)], sem).start()

    @pl.when(i < live)
    def _():
        blk_a = 2 * i
        blk_b = 2 * i + 1
        nxt_a = jnp.minimum(2 * i + 2, 2 * live - 1)

        gather(blk_b, xb, gsem.at[1])
        wait_gather(xa, gsem.at[0])
        wait_scatter(oa, ssem.at[0])
        experts(xa, oa, ga_ref, wgu_a, bgu_a, wd_a, bd_a)
        scatter(blk_a, oa, ssem.at[0])

        gather(nxt_a, xa, gsem.at[0])
        wait_gather(xb, gsem.at[1])
        wait_scatter(ob, ssem.at[1])
        experts(xb, ob, gb_ref, wgu_b, bgu_b, wd_b, bd_b)
        scatter(blk_b, ob, ssem.at[1])

    @pl.when(i == live - 1)
    def _():
        wait_gather(xa, gsem.at[0])
        wait_scatter(oa, ssem.at[0])
        wait_scatter(ob, ssem.at[1])


def _moe_experts(h, blk_exp, tok, dst, live, gates, w_gu, b_gu, w_down, b_down, *, bm, layer):
    _, e, d, f2 = w_gu.shape
    nc = d // LANES
    n = h.shape[0] // nc
    dff = f2 // 2
    cap = tok.shape[0]
    n_steps = cap // (2 * bm)
    assert cap == n_steps * 2 * bm

    def wspecs(off):
        return [
            pl.BlockSpec((1, 1, d, f2), lambda i, be, tk, ds, lv: (layer, be[2 * i + off], 0, 0)),
            pl.BlockSpec((1, 1, f2), lambda i, be, tk, ds, lv: (be[2 * i + off], 0, 0)),
            pl.BlockSpec((1, 1, dff, d), lambda i, be, tk, ds, lv: (layer, be[2 * i + off], 0, 0)),
            pl.BlockSpec((1, 1, d), lambda i, be, tk, ds, lv: (be[2 * i + off], 0, 0)),
        ]

    grid_spec = pltpu.PrefetchScalarGridSpec(
        num_scalar_prefetch=4,
        grid=(n_steps,),
        in_specs=[
            pl.BlockSpec(memory_space=pl.ANY),
            pl.BlockSpec((bm, 1), lambda i, be, tk, ds, lv: (2 * i, 0)),
            pl.BlockSpec((bm, 1), lambda i, be, tk, ds, lv: (2 * i + 1, 0)),
        ] + wspecs(0) + wspecs(1),
        out_specs=pl.BlockSpec(memory_space=pl.ANY),
        scratch_shapes=[pltpu.VMEM((bm * nc, LANES), F32)] * 4 + [pltpu.SemaphoreType.DMA((2,)),
                                                                  pltpu.SemaphoreType.DMA((2,))],
    )
    bgu3 = b_gu.reshape(e, 1, f2).astype(F32)
    bd3 = b_down.reshape(e, 1, d).astype(F32)
    g2 = gates.reshape(cap, 1)
    return pl.pallas_call(
        functools.partial(_moe_expert_kernel, bm=bm, dff=dff),
        out_shape=jax.ShapeDtypeStruct(((TOP_K * n + 2 * bm) * nc, LANES), F32),
        grid_spec=grid_spec,
        compiler_params=_cparams(("arbitrary",)),
        name="moe_experts",
    )(blk_exp, tok, dst, live, h, g2, g2, w_gu, bgu3, w_down, bd3, w_gu, bgu3, w_down, bd3)


def _moe_layer(x, norm_g, router_w, router_b, w_gu, b_gu, w_down, b_down, layer):
    n, d = x.shape
    bm = MOE_BM
    h, idx128, gate128, cnt = _router(x, norm_g, router_w, router_b)
    n_assign = n * TOP_K
    flat_e = idx128[:, :TOP_K].reshape(-1)
    flat_g = gate128[:, :TOP_K].reshape(-1)
    order = jnp.argsort(flat_e).astype(jnp.int32)
    counts = cnt[0, :N_EXPERTS].astype(jnp.int32)
    padded = (counts + bm - 1) // bm * bm
    start = jnp.cumsum(counts) - counts
    pend = jnp.cumsum(padded)
    pstart = pend - padded
    n_blocks = -(-n_assign // bm) + N_EXPERTS
    n_blocks += n_blocks % 2
    blk_start = jnp.arange(n_blocks, dtype=jnp.int32) * bm
    blk_exp = jnp.minimum(jnp.sum((pend[None, :] <= blk_start[:, None]).astype(jnp.int32), axis=1),
                          N_EXPERTS - 1).astype(jnp.int32)
    row = jnp.arange(bm, dtype=jnp.int32)[None, :]
    off = (blk_start - pstart[blk_exp])[:, None] + row
    valid = off < counts[blk_exp][:, None]
    a = order[jnp.clip(start[blk_exp][:, None] + off, 0, n_assign - 1).reshape(-1)].reshape(n_blocks, bm)
    nc = d // LANES
    tok = (jnp.where(valid, a // TOP_K, 0) * nc).astype(jnp.int32).reshape(-1)
    dump_row = TOP_K * n + (jnp.arange(n_blocks, dtype=jnp.int32) % 2)[:, None] * bm + row
    dst = (jnp.where(valid, (a % TOP_K) * n + a // TOP_K, dump_row) * nc).astype(jnp.int32).reshape(-1)
    live = jnp.maximum((pend[-1] // bm + 1) // 2, 1).astype(jnp.int32).reshape(1)
    gates = jnp.where(valid, flat_g[a.reshape(-1)].reshape(n_blocks, bm), 0.0).reshape(-1)
    return _moe_experts(h, blk_exp, tok, dst, live, gates, w_gu, b_gu, w_down, b_down, bm=bm, layer=layer)


def _ple_kernel(x_ref, y0_ref, y1_ref, y2_ref, y3_ref, g_ref, wg_ref, p_ref, wp_ref, o_ref):
    tm, d = x_ref.shape
    nc = d // LANES

    def plane(y_ref):
        return jnp.concatenate([y_ref[pl.ds(c, tm, stride=nc), :] for c in range(nc)], axis=1)

    xf = x_ref[...] + (((plane(y0_ref) + plane(y1_ref)) + plane(y2_ref)) + plane(y3_ref))
    ms = jnp.mean(xf * xf, axis=-1, keepdims=True)
    h = (xf * lax.rsqrt(ms + EPS) * g_ref[...]).astype(BF16)
    gate = _sigmoid(jnp.dot(h, wg_ref[...], preferred_element_type=F32))
    pp = jnp.dot(p_ref[...].astype(BF16), wp_ref[...], preferred_element_type=F32)
    o_ref[...] = xf + gate * pp


def _ple(x, y, g, wg, p, wp, *, tm=512):
    n, d = x.shape
    pd = p.shape[1]
    tm = min(tm, n)
    nb = n // tm
    nc = d // LANES
    yspecs = [pl.BlockSpec((tm * nc, LANES), functools.partial(lambda i, k: (k * nb + i, 0), k=k)) for k in range(TOP_K)]
    return pl.pallas_call(
        _ple_kernel,
        out_shape=jax.ShapeDtypeStruct((n, d), F32),
        grid=(nb,),
        in_specs=[pl.BlockSpec((tm, d), lambda i: (i, 0))] + yspecs + [
            pl.BlockSpec((1, d), lambda i: (0, 0)),
            pl.BlockSpec((d, d), lambda i: (0, 0)),
            pl.BlockSpec((tm, pd), lambda i: (i, 0)),
            pl.BlockSpec((pd, d), lambda i: (0, 0)),
        ],
        out_specs=pl.BlockSpec((tm, d), lambda i: (i, 0)),
        compiler_params=_cparams(("parallel",)),
        name="ple",
    )(x, y, y, y, y, g.reshape(1, d).astype(F32), wg.astype(BF16), p, wp.astype(BF16))


def _cmp_kernel(z_ref, w1_ref, pos_ref, w2_ref, gain_ref, c_ref, s1_ref, s2_ref, o_ref, *, rows, half):
    j = pl.program_id(1)
    z = z_ref[0, 0, 0]
    pos = pos_ref[0]
    a1 = (z + pos[:, :half]).astype(BF16)
    a2 = (z + pos[:, half:]).astype(BF16)
    u1 = jnp.dot(a1, w1_ref[0, :half, :], preferred_element_type=F32)
    u2 = jnp.dot(a2, w1_ref[0, half:, :], preferred_element_type=F32)
    h = u1 + pltpu.roll(u2, rows - 1, 0)
    hs = h * _sigmoid(h)
    y = jnp.dot(hs.astype(BF16), w2_ref[0], preferred_element_type=F32)

    @pl.when(j == 0)
    def _():
        ss = jnp.sum(y * y, axis=-1, keepdims=True) * (1.0 / LANES)
        yn = y * lax.rsqrt(ss + EPS) * gain_ref[...]
        yn = (yn * c_ref[...] + pltpu.roll(yn, LANES - ROPE_DIM // 2, 1) * s1_ref[...]
              + pltpu.roll(yn, ROPE_DIM // 2, 1) * s2_ref[...])
        o_ref[0, 0, 0] = yn.astype(o_ref.dtype)

    @pl.when(j != 0)
    def _():
        o_ref[0, 0, 0] = y.astype(o_ref.dtype)


def _compress(zr, w1, pos, w2dup, gain_dup, tabs, *, batch):
    rows, width = zr.shape[3], zr.shape[4]
    hid = w1.shape[2]
    g = NSA_GROUPS
    return pl.pallas_call(
        functools.partial(_cmp_kernel, rows=rows, half=width),
        out_shape=jax.ShapeDtypeStruct((batch, 2, g, rows, LANES), BF16),
        grid=(batch, 2, g),
        in_specs=[
            pl.BlockSpec((1, 1, 1, rows, width), lambda b, j, gg: (b, j, gg, 0, 0)),
            pl.BlockSpec((1, 2 * width, hid), lambda b, j, gg: (j, 0, 0)),
            pl.BlockSpec((1, 1, 2 * width), lambda b, j, gg: (j, 0, 0)),
            pl.BlockSpec((1, hid, LANES), lambda b, j, gg: (j, 0, 0)),
            pl.BlockSpec((1, LANES), lambda b, j, gg: (0, 0)),
            pl.BlockSpec((rows, LANES), lambda b, j, gg: (0, 0)),
            pl.BlockSpec((rows, LANES), lambda b, j, gg: (0, 0)),
            pl.BlockSpec((rows, LANES), lambda b, j, gg: (0, 0)),
        ],
        out_specs=pl.BlockSpec((1, 1, 1, rows, LANES), lambda b, j, gg: (b, j, gg, 0, 0)),
        compiler_params=_cparams(("parallel", "parallel", "parallel")),
        name="nsa_compress",
    )(zr, w1, pos, w2dup, gain_dup, *tabs)


def _pipelined(n_items, first, second, skew):
    pending, out = {}, [None] * n_items
    for idx in range(n_items + skew):
        if idx < n_items:
            pending[idx] = first(idx)
        if idx >= skew:
            out[idx - skew] = second(idx - skew, pending.pop(idx - skew))
    return out


def _nsa_attn_kernel(q_ref, gz_ref, kc_ref, vct_ref, ovt_ref, ks_ref, vs_ref, kw_ref, vw_ref, o_ref,
                     e_ref, oc_ref, imp_ref, *, n_cmp_rows, tk_sel, skew, grp, subs):
    g = pl.program_id(1)
    i = pl.program_id(2)
    qb = NSA_QB
    mh = HEADS_PER_GROUP
    rows = mh * qb
    q0 = i * qb
    seq = ks_ref.shape[0]

    @pl.when(i == 0)
    def _():
        r = lax.broadcasted_iota(jnp.int32, (seq, LANES), 0)
        lb = lax.broadcasted_iota(jnp.int32, (seq, LANES), 1)
        e_ref[...] = jnp.where((r >> 6) == lb, NEG, 0.0).astype(BF16)

    qh = [q_ref[:, m * LANES:(m + 1) * LANES] for m in range(mh)]
    q4 = jnp.concatenate(qh, axis=0)
    jl = lax.broadcasted_iota(jnp.int32, (qb, LANES), 1)

    tlane = q0 + (lax.broadcasted_iota(jnp.int32, (1, rows), 1) & (qb - 1))

    def cmp_branch(nrow):
        st = _dot_nt(kc_ref[0, 0, 0, :nrow, :], q4)
        cend = lax.broadcasted_iota(jnp.int32, (nrow, rows), 0) * CMP_STRIDE + (CMP_BLOCK - 1)
        valid = cend <= tlane
        st = jnp.where(valid, st, NEG)
        e = jnp.where(valid, jnp.exp2(st - jnp.max(st, axis=0, keepdims=True)), 0.0)
        pt = e / jnp.maximum(jnp.sum(e, axis=0, keepdims=True), 1e-30)
        oct = jnp.dot(vct_ref[0, 0, :, :nrow], pt.astype(BF16), preferred_element_type=F32)
        for m in range(mh):
            oc_ref[m * qb:(m + 1) * qb, :] = oct[:, m * qb:(m + 1) * qb].T
        psum = (pt[:, 0:qb] + pt[:, qb:2 * qb]) + (pt[:, 2 * qb:3 * qb] + pt[:, 3 * qb:4 * qb])
        phi, plo = _split2(psum)
        ovt = ovt_ref[:, :nrow]
        imp_ref[...] = (jnp.dot(ovt, phi, preferred_element_type=F32)
                        + jnp.dot(ovt, plo, preferred_element_type=F32))

    n_chunks = n_cmp_rows // LANES
    last_visible = (q0 + qb - CMP_BLOCK) // CMP_STRIDE
    need = jnp.clip(last_visible // LANES + 1, 1, n_chunks)
    for nch in range(1, n_chunks + 1):
        pl.when(need == nch)(functools.partial(cmp_branch, nch * LANES))
    imp = imp_ref[...]

    jb = lax.broadcasted_iota(jnp.int32, (LANES, qb), 0)
    tq = q0 + lax.broadcasted_iota(jnp.int32, (LANES, qb), 1)
    cur = tq >> 6
    forced = (jb == 0) | (jb == cur) | (jb == cur - 1)
    causal = (jb << 6) <= tq
    score = jnp.where(causal, jnp.where(forced, FORCE, imp), -jnp.inf)
    notsel_t = jnp.ones((LANES, qb), F32)
    jbf = jb.astype(F32)
    for _ in range(SEL_TOPK):
        smx = jnp.max(score, axis=0, keepdims=True)
        ix = jnp.min(jnp.where(score == smx, jbf, float(LANES)), axis=0, keepdims=True)
        hit = jbf == ix
        notsel_t = jnp.where(hit, 0.0, notsel_t)
        score = jnp.where(hit, -jnp.inf, score)
    notsel_b = notsel_t.T.astype(BF16)
    n_grp = mh // grp
    q_aug = [jnp.concatenate([jnp.concatenate([qh[m], notsel_b], axis=1) for m in range(c * grp, (c + 1) * grp)], axis=0)
             for c in range(n_grp)]
    tpos = q0 + lax.broadcasted_iota(jnp.int32, (qb, 1), 0)
    tpos_g = jnp.concatenate([tpos] * grp, axis=0)

    def sel_tiles(first, n_sub):
        ks, vs = [], []
        for t in range(n_sub):
            k0 = pl.multiple_of((first + t) * tk_sel, tk_sel)
            k_aug = jnp.concatenate([ks_ref[pl.ds(k0, tk_sel), :], e_ref[pl.ds(k0, tk_sel), :]], axis=1)
            ks.append([k_aug] * n_grp)
            vs.append([vs_ref[pl.ds(k0, tk_sel), :]] * n_grp)
        return ks, vs

    def sel_step(first, n_sub, state):
        ks, vs = sel_tiles(first, n_sub)
        return _attend(q_aug, ks, vs, state, skew)

    state = tuple((jnp.full((grp * qb, 1), NEG, F32), jnp.zeros((grp * qb, LANES), F32)) for _ in range(n_grp))
    n_below = q0 // tk_sel
    state = _tile_loops(n_below, subs, sel_step, state)

    def causal(t, c, sc):
        kpos = n_below * tk_sel + lax.broadcasted_iota(jnp.int32, (grp * qb, tk_sel), 1)
        return jnp.where(kpos <= tpos_g, sc, NEG)

    ks, vs = sel_tiles(n_below, 1)
    state = _attend(q_aug, ks, vs, state, skew, mask=causal)
    sel_acc = [state[m // grp][1][(m % grp) * qb:(m % grp + 1) * qb] for m in range(mh)]

    wk = WINDOW + qb
    k0w = pl.multiple_of(jnp.maximum(i - WINDOW // qb, 0) * qb, qb)
    kwt = kw_ref[pl.ds(k0w, wk), :]
    vwt = vw_ref[pl.ds(k0w, wk), :]
    kposw = k0w + lax.broadcasted_iota(jnp.int32, (grp * qb, wk), 1)
    wmask = (kposw <= tpos_g) & (kposw > tpos_g - WINDOW)
    q_grp = [jnp.concatenate(qh[c * grp:(c + 1) * grp], axis=0) for c in range(n_grp)]

    def win_scores(c):
        return jnp.where(wmask, _dot_nt(q_grp[c], kwt), NEG)

    def win_out(c, sc):
        pe = jnp.exp2(sc - jnp.max(sc, axis=-1, keepdims=True))
        return jnp.dot(pe.astype(BF16), vwt, preferred_element_type=F32)

    win_grp = _pipelined(n_grp, win_scores, win_out, min(skew, n_grp - 1))
    win_acc = [win_grp[m // grp][(m % grp) * qb:(m % grp + 1) * qb] for m in range(mh)]

    sig = _sigmoid(gz_ref[...])
    den_lane = jl == HEAD_DIM
    heads = []
    for m in range(mh):
        col = g * mh + m
        gates = [jnp.sum(jnp.where(jl == br * N_HEADS + col, sig, 0.0), axis=-1, keepdims=True)
                 for br in range(3)]
        acc_s = sel_acc[m]
        acc_w = win_acc[m]
        den_s = jnp.sum(jnp.where(den_lane, acc_s, 0.0), axis=-1, keepdims=True)
        den_w = jnp.sum(jnp.where(den_lane, acc_w, 0.0), axis=-1, keepdims=True)
        heads.append(gates[0] * oc_ref[m * qb:(m + 1) * qb, :] + (gates[1] / den_s) * acc_s
                     + (gates[2] / den_w) * acc_w)
    lane_lo = jl < HEAD_DIM
    o_ref[:, 0:LANES] = jnp.where(lane_lo, heads[0], pltpu.roll(heads[1], HEAD_DIM, 1)).astype(o_ref.dtype)
    o_ref[:, LANES:2 * LANES] = jnp.where(lane_lo, heads[2], pltpu.roll(heads[3], HEAD_DIM, 1)).astype(o_ref.dtype)


def _nsa_attn(qn, gz, kvc, overlap, kvd, *, batch, seq, skew=2, grp=2, subs=(4, 2, 1)):
    n = qn.shape[0]
    qb = NSA_QB
    nq = seq // qb
    g = NSA_GROUPS
    rows = HEADS_PER_GROUP * qb
    n_cmp_rows = kvc.shape[3]
    tk_sel = min(512, seq)
    assert seq >= WINDOW + qb and seq % tk_sel == 0 and tk_sel % qb == 0
    in_specs = [
        pl.BlockSpec((qb, HEADS_PER_GROUP * LANES), lambda b, gg, i: (b * nq + i, gg)),
        pl.BlockSpec((qb, LANES), lambda b, gg, i: (b * nq + i, 0)),
        pl.BlockSpec((1, 1, 1, n_cmp_rows, LANES), lambda b, gg, i: (b, 0, gg, 0, 0)),
        pl.BlockSpec((1, 1, LANES, n_cmp_rows), lambda b, gg, i: (b, gg, 0, 0)),
        pl.BlockSpec((LANES, n_cmp_rows), lambda b, gg, i: (0, 0)),
        pl.BlockSpec((seq, LANES), lambda b, gg, i: (b, 0 * g + gg)),
        pl.BlockSpec((seq, LANES), lambda b, gg, i: (b, 1 * g + gg)),
        pl.BlockSpec((seq, LANES), lambda b, gg, i: (b, 2 * g + gg)),
        pl.BlockSpec((seq, LANES), lambda b, gg, i: (b, 3 * g + gg)),
    ]
    return pl.pallas_call(
        functools.partial(_nsa_attn_kernel, n_cmp_rows=n_cmp_rows, tk_sel=tk_sel, skew=skew, grp=grp, subs=subs),
        out_shape=jax.ShapeDtypeStruct((n, N_HEADS * HEAD_DIM), BF16),
        grid=(batch, g, nq),
        in_specs=in_specs,
        out_specs=pl.BlockSpec((qb, HEADS_PER_GROUP * HEAD_DIM), lambda b, gg, i: (b * nq + i, gg)),
        scratch_shapes=[pltpu.VMEM((seq, LANES), BF16), pltpu.VMEM((rows, LANES), F32), pltpu.VMEM((LANES, qb), F32)],
        compiler_params=_cparams(("parallel", "parallel", "arbitrary")),
        name="nsa_attn",
    )(qn, gz, kvc, jnp.swapaxes(kvc[:, 1], -1, -2), overlap.T, kvd, kvd, kvd, kvd)


def _pad_cols(w, width=LANES):
    d, f = w.shape
    return jnp.zeros((d, width), w.dtype).at[:, :f].set(w)


def _tile_gain(gain, n_heads, scale=1.0, dup=False):
    g = gain.astype(F32) * scale
    second = g if dup else jnp.zeros_like(g)
    return jnp.tile(jnp.concatenate([g, second]), n_heads)


def _fox_aug_tables():
    hw = N_HEADS * LANES
    piece = jnp.arange(3)[:, None]
    head = jnp.arange(N_HEADS)[None, :]
    src = (piece * N_HEADS + head).reshape(-1)
    q_dst = (head * LANES + HEAD_DIM + piece).reshape(-1)
    k_dst = (hw + head * LANES + HEAD_DIM + 3 + piece).reshape(-1)
    place = jnp.zeros((LANES, 3 * hw), F32).at[src, q_dst].set(1.0).at[src, k_dst].set(-1.0)
    lane = jnp.arange(LANES)
    q_bias = ((lane >= HEAD_DIM + 3) & (lane < HEAD_DIM + 6)).astype(F32)
    k_bias = ((lane >= HEAD_DIM) & (lane < HEAD_DIM + 3)).astype(F32)
    v_bias = (lane == HEAD_DIM).astype(F32)
    bias = jnp.concatenate([jnp.tile(q_bias, N_HEADS), jnp.tile(k_bias, N_HEADS), jnp.tile(v_bias, N_HEADS)])
    return place.astype(BF16), bias


def _fox_layer(x, batch, seq, attn_g, w_in, b_f, qk_gain, w_out):
    hd = N_HEADS * HEAD_DIM
    hw = N_HEADS * LANES
    scale = HEAD_DIM ** -0.5
    z = _rms_proj(x, attn_g, _pad_cols(w_in[:, 3 * hd:]).astype(BF16), out_dtype=F32, seq=seq)
    b_pad = jnp.zeros((1, LANES), F32).at[0, :N_HEADS].set(b_f.astype(F32))
    c3 = _fox_c(z, b_pad, batch=batch, seq=seq)
    w_bf = w_in.astype(BF16)
    gain = jnp.concatenate([_tile_gain(qk_gain[0], N_HEADS, scale * LOG2E), _tile_gain(qk_gain[1], N_HEADS),
                            jnp.zeros((hw,), F32)])
    tn = 512
    place, bias = _fox_aug_tables()
    qkv = _rms_proj(x, attn_g, w_bf, out_dtype=BF16, seq=seq, tn=tn, gain=gain, w_cols=3 * hd,
                    norm_blocks=tuple(range(2 * hw // tn)), placed=(c3, place), col_bias=bias, expand=True)
    o = _fox_attn(qkv, batch=batch, seq=seq)
    return _matmul_res(o, w_out.astype(BF16), x)


def _nsa_shared_kv(x, batch, seq, kv_norm, kv_w, kv_k_gain, cmp_pos, cmp_w1, cmp_w2):
    g = NSA_GROUPS
    gw = g * HEAD_DIM
    tabs = _rope_tables(jnp.arange(seq))
    kv_bf = kv_w.astype(BF16)
    zero = jnp.zeros((g * LANES,), F32)
    one_lane = jnp.tile((jnp.arange(LANES) == HEAD_DIM).astype(F32), g)
    gain_kvd = jnp.concatenate([_tile_gain(kv_k_gain[1], g), zero, _tile_gain(kv_k_gain[2], g), zero])
    bias_kvd = jnp.concatenate([zero, one_lane, zero, one_lane])
    kvd = _rms_proj(x, kv_norm, kv_bf, out_dtype=BF16, seq=seq, tn=g * LANES, gain=gain_kvd, w_col0=2 * gw,
                    norm_blocks=(0, 2), rope_tabs=tabs, col_bias=bias_kvd, expand=True)
    zc = _rms_proj(x, kv_norm, kv_bf, out_dtype=F32, seq=seq, w_cols=2 * gw)
    rows = seq // CMP_STRIDE
    zr = zc.reshape(batch, rows, CMP_STRIDE, 2, g, HEAD_DIM).transpose(0, 3, 4, 1, 2, 5)
    zr = zr.reshape(batch, 2, g, rows, CMP_STRIDE * HEAD_DIM)
    cmp_end = jnp.arange(rows) * CMP_STRIDE + CMP_BLOCK - 1
    ctabs = _rope_tables(cmp_end)
    w2dup = jnp.concatenate([cmp_w2, cmp_w2], axis=-1).astype(BF16)
    kvc = _compress(zr, cmp_w1.astype(BF16), cmp_pos.reshape(2, 1, CMP_BLOCK * HEAD_DIM).astype(F32), w2dup,
                    _tile_gain(kv_k_gain[0], 1, dup=True).reshape(1, LANES), ctabs, batch=batch)
    return kvc, kvd


def _nsa_layer(x, batch, seq, attn_g, w_in, q_gain, w_out, kvc, kvd):
    hd = N_HEADS * HEAD_DIM
    scale = HEAD_DIM ** -0.5
    rows = seq // CMP_STRIDE
    tabs = _rope_tables(jnp.arange(seq))
    qn = _rms_proj(x, attn_g, w_in.astype(BF16), out_dtype=BF16, seq=seq, expand=True, w_cols=hd,
                   gain=_tile_gain(q_gain, N_HEADS, scale * LOG2E), norm_blocks=tuple(range(N_HEADS * LANES // 512)),
                   rope_tabs=tabs)
    gz = _rms_proj(x, attn_g, _pad_cols(w_in[:, hd:]).astype(BF16), out_dtype=F32, seq=seq)

    n_sel = seq // SEL_BLOCK
    assert n_sel <= LANES
    cmp_start = jnp.arange(rows) * CMP_STRIDE
    sel_start = jnp.arange(LANES) * SEL_BLOCK
    overlap = jnp.clip(jnp.minimum(cmp_start[:, None] + CMP_BLOCK, sel_start[None, :] + SEL_BLOCK)
                       - jnp.maximum(cmp_start[:, None], sel_start[None, :]), 0)
    overlap = jnp.where((jnp.arange(rows) < rows - 1)[:, None] & (jnp.arange(LANES) < n_sel)[None, :], overlap, 0)
    o = _nsa_attn(qn, gz, kvc, overlap.astype(BF16), kvd, batch=batch, seq=seq)
    return _matmul_res(o, w_out.astype(BF16), x)


def kernel(x, p, attn_norm, ffn_norm, ple_norm, ple_gate_w, ple_proj_w, router_w, router_b, w_gu, b_gu, w_down, b_down, fox_w_in, fox_b_f, fox_qk_gain, fox_w_out, kv_norm, kv_w, kv_k_gain, cmp_pos, cmp_w1, cmp_w2, nsa_w_in, nsa_q_gain, nsa_w_out):
    batch, seq, d = x.shape
    depth = p.shape[0]
    n_a = fox_w_in.shape[0]
    xt = x.reshape(batch * seq, d)
    w_gu_bf = w_gu.astype(BF16)
    w_down_bf = w_down.astype(BF16)
    shared = None
    for layer in range(depth):
        if layer == n_a:
            shared = _nsa_shared_kv(xt, batch, seq, kv_norm, kv_w, kv_k_gain, cmp_pos, cmp_w1, cmp_w2)
        if layer < n_a:
            xt = _fox_layer(xt, batch, seq, attn_norm[layer], fox_w_in[layer], fox_b_f[layer],
                            fox_qk_gain[layer], fox_w_out[layer])
        else:
            i = layer - n_a
            xt = _nsa_layer(xt, batch, seq, attn_norm[layer], nsa_w_in[i], nsa_q_gain[i], nsa_w_out[i], *shared)
        y = _moe_layer(xt, ffn_norm[layer], router_w[layer], router_b[layer], w_gu_bf, b_gu[layer],
                       w_down_bf, b_down[layer], layer)
        xt = _ple(xt, y, ple_norm[layer], ple_gate_w[layer], p[layer].reshape(batch * seq, -1), ple_proj_w[layer])
    return xt.reshape(batch, seq, d)
```

```python
import functools

import jax
import jax.numpy as jnp
from jax import lax
from jax.experimental import pallas as pl
from jax.experimental.pallas import tpu as pltpu

F32 = jnp.float32
BF16 = jnp.bfloat16

N_HEADS = 16
HEAD_DIM = 64
ROPE_DIM = HEAD_DIM // 4
ROPE_THETA = 500000.0
NSA_GROUPS = 4
HEADS_PER_GROUP = N_HEADS // NSA_GROUPS
CMP_BLOCK = 32
CMP_STRIDE = 16
SEL_BLOCK = 64
SEL_TOPK = 16
WINDOW = 512
N_EXPERTS = 32
TOP_K = 4
SWIGLU_LIMIT = 7.0
SWIGLU_ALPHA = 1.702
EPS = 1e-6
NEG = -1e30
FORCE = 1e6
LOG2E = 1.4426950408889634

LANES = 128
NSA_QB = 256
MOE_BM = 256
VMEM_LIMIT = 56 * 1024 * 1024


def _cparams(sem):
    return pltpu.CompilerParams(dimension_semantics=sem, vmem_limit_bytes=VMEM_LIMIT)


def _sigmoid(x):
    return 1.0 / (1.0 + jnp.exp(-x))


def _dot_nt(a, b):
    return lax.dot_general(a, b, (((1,), (1,)), ((), ())), preferred_element_type=F32)


def _split2(x):
    hi = x.astype(BF16)
    lo = (x - hi.astype(F32)).astype(BF16)
    return hi, lo


def _proj_kernel(*refs, norm_blocks, rope, placed, biased, expand, head_div, tn, out_dtype):
    refs = list(refs)
    x_ref, g_ref, w_ref = refs[:3]
    del refs[:3]
    if norm_blocks:
        gain_ref = refs.pop(0)
    if rope:
        c_ref, s1_ref, s2_ref = refs[:3]
        del refs[:3]
    if placed:
        ex_ref, place_ref = refs[:2]
        del refs[:2]
    if biased:
        bias_ref = refs.pop(0)
    o_ref, h_ref = refs
    j = pl.program_id(1)
    aug = placed or biased

    @pl.when(j == 0)
    def _():
        xf = x_ref[...]
        ms = jnp.mean(xf * xf, axis=-1, keepdims=True)
        h_ref[...] = (xf * lax.rsqrt(ms + EPS) * g_ref[...]).astype(BF16)

    acc = jnp.dot(h_ref[...], w_ref[...], preferred_element_type=F32)
    if placed and biased:
        extra = jnp.dot(ex_ref[...], place_ref[...], preferred_element_type=F32) + bias_ref[...]
    elif biased:
        extra = jnp.broadcast_to(bias_ref[...], (acc.shape[0], tn))
    if expand:
        lo = lax.broadcasted_iota(jnp.int32, (acc.shape[0], LANES), 1) < HEAD_DIM
        blocks = []
        for c in range(acc.shape[1] // LANES):
            a = acc[:, c * LANES:(c + 1) * LANES]
            blocks += [jnp.where(lo, a, 0.0), jnp.where(lo, pltpu.roll(a, HEAD_DIM, 1), 0.0)]
    else:
        blocks = [acc[:, c * LANES:(c + 1) * LANES] for c in range(acc.shape[1] // LANES)]

    def plain():
        for c, a in enumerate(blocks):
            sl = slice(c * LANES, (c + 1) * LANES)
            o_ref[:, sl] = ((a + extra[:, sl]) if aug else a).astype(out_dtype)

    if not norm_blocks:
        plain()
        return
    is_norm = functools.reduce(jnp.logical_or, [j == c for c in norm_blocks])

    @pl.when(is_norm)
    def _():
        for c, a in enumerate(blocks):
            sl = slice(c * LANES, (c + 1) * LANES)
            ss = jnp.sum(a * a, axis=-1, keepdims=True) * (1.0 / head_div)
            y = a * lax.rsqrt(ss + EPS) * gain_ref[:, sl]
            if rope:
                y = (y * c_ref[...] + pltpu.roll(y, LANES - ROPE_DIM // 2, 1) * s1_ref[...]
                     + pltpu.roll(y, ROPE_DIM // 2, 1) * s2_ref[...])
            if aug:
                y = y + extra[:, sl]
            o_ref[:, sl] = y.astype(out_dtype)

    @pl.when(jnp.logical_not(is_norm))
    def _():
        plain()


def _rms_proj(x, g, w, *, out_dtype, seq, tm=1024, tn=512, gain=None, norm_blocks=(),
              rope_tabs=None, placed=None, col_bias=None, expand=False, head_div=HEAD_DIM,
              w_col0=0, w_cols=None):
    n, d = x.shape
    w_cols = w.shape[1] - w_col0 if w_cols is None else w_cols
    f = w_cols * (2 if expand else 1)
    tn = min(tn, f)
    tw = tn // 2 if expand else tn
    assert w_col0 % tw == 0
    j0 = w_col0 // tw
    tm = min(tm, seq)
    assert n % tm == 0 and f % tn == 0 and seq % tm == 0
    nt = seq // tm
    in_specs = [
        pl.BlockSpec((tm, d), lambda i, j: (i, 0)),
        pl.BlockSpec((1, d), lambda i, j: (0, 0)),
        pl.BlockSpec((d, tw), lambda i, j: (0, j0 + j)),
    ]
    args = [x, g.reshape(1, d).astype(F32), w]
    if norm_blocks:
        in_specs.append(pl.BlockSpec((1, tn), lambda i, j: (0, j)))
        args.append(gain.reshape(1, f).astype(F32))
    if rope_tabs is not None:
        assert norm_blocks
        for tab in rope_tabs:
            in_specs.append(pl.BlockSpec((tm, LANES), lambda i, j: (i % nt, 0)))
            args.append(tab)
    if placed is not None:
        assert col_bias is not None
        values, place = placed
        in_specs += [pl.BlockSpec((tm, LANES), lambda i, j: (i, 0)),
                     pl.BlockSpec((LANES, tn), lambda i, j: (0, j))]
        args += [values, place]
    if col_bias is not None:
        in_specs.append(pl.BlockSpec((1, tn), lambda i, j: (0, j)))
        args.append(col_bias.reshape(1, f).astype(F32))
    kern = functools.partial(_proj_kernel, norm_blocks=tuple(norm_blocks), rope=rope_tabs is not None,
                             placed=placed is not None, biased=col_bias is not None, expand=expand,
                             head_div=float(head_div), tn=tn, out_dtype=out_dtype)
    return pl.pallas_call(
        kern,
        out_shape=jax.ShapeDtypeStruct((n, f), out_dtype),
        grid=(n // tm, f // tn),
        in_specs=in_specs,
        out_specs=pl.BlockSpec((tm, tn), lambda i, j: (i, j)),
        scratch_shapes=[pltpu.VMEM((tm, d), BF16)],
        compiler_params=_cparams(("parallel", "arbitrary")),
        name="rms_proj",
    )(*args)


def _rope_tables(pos, width=LANES):
    half = ROPE_DIM // 2
    inv = jnp.power(jnp.float32(ROPE_THETA), -jnp.arange(0, ROPE_DIM, 2, dtype=F32) / ROPE_DIM)
    ang = pos.astype(F32)[:, None] * inv[None, :]
    cos, sin = jnp.cos(ang), jnp.sin(ang)
    t = pos.shape[0]
    ones = jnp.ones((t, HEAD_DIM - ROPE_DIM), F32)
    zeros_h = jnp.zeros((t, half), F32)
    zeros_r = jnp.zeros((t, HEAD_DIM - ROPE_DIM), F32)
    c = jnp.concatenate([cos, cos, ones], axis=1)
    s1 = jnp.concatenate([-sin, zeros_h, zeros_r], axis=1)
    s2 = jnp.concatenate([zeros_h, sin, zeros_r], axis=1)
    rep = width // HEAD_DIM
    return tuple(jnp.tile(a, (1, rep)) for a in (c, s1, s2))


def _matmul_res_kernel(a_ref, w_ref, x_ref, o_ref):
    o_ref[...] = x_ref[...] + jnp.dot(a_ref[...], w_ref[...], preferred_element_type=F32)


def _matmul_res(a, w, x, *, tm=512):
    n, k = a.shape
    f = w.shape[1]
    tm = min(tm, n)
    return pl.pallas_call(
        _matmul_res_kernel,
        out_shape=jax.ShapeDtypeStruct((n, f), F32),
        grid=(n // tm,),
        in_specs=[
            pl.BlockSpec((tm, k), lambda i: (i, 0)),
            pl.BlockSpec((k, f), lambda i: (0, 0)),
            pl.BlockSpec((tm, f), lambda i: (i, 0)),
        ],
        out_specs=pl.BlockSpec((tm, f), lambda i: (i, 0)),
        compiler_params=_cparams(("parallel",)),
        name="matmul_res",
    )(a, w, x)


def _fox_c_kernel(z_ref, b_ref, c_ref, carry_ref, *, tm):
    t = pl.program_id(1)

    @pl.when(t == 0)
    def _():
        carry_ref[...] = jnp.zeros_like(carry_ref)

    u = z_ref[...] + b_ref[...]
    logf = jnp.minimum(u, 0.0) - jnp.log(1.0 + jnp.exp(-jnp.abs(u)))
    row = lax.broadcasted_iota(jnp.int32, (tm, tm), 0)
    col = lax.broadcasted_iota(jnp.int32, (tm, tm), 1)
    tri = (row >= col).astype(BF16)
    hi = logf.astype(BF16)
    r1 = logf - hi.astype(F32)
    mid = r1.astype(BF16)
    lo = (r1 - mid.astype(F32)).astype(BF16)
    cs = (jnp.dot(tri, hi, preferred_element_type=F32) + jnp.dot(tri, mid, preferred_element_type=F32)
          + jnp.dot(tri, lo, preferred_element_type=F32))
    c = cs + carry_ref[...]
    carry_ref[...] = c[tm - 1:tm, :]
    c2 = c * LOG2E
    p0 = c2.astype(BF16).astype(F32)
    p1 = (c2 - p0).astype(BF16).astype(F32)
    p2 = ((c2 - p0) - p1).astype(BF16).astype(F32)
    lane = lax.broadcasted_iota(jnp.int32, c.shape, 1)
    out = jnp.where(lane < N_HEADS, p0,
                    jnp.where(lane < 2 * N_HEADS, pltpu.roll(p1, N_HEADS, 1),
                              jnp.where(lane < 3 * N_HEADS, pltpu.roll(p2, 2 * N_HEADS, 1), 0.0)))
    c_ref[...] = out.astype(BF16)


def _fox_c(z, b_pad, *, batch, seq, tm=512):
    tm = min(tm, seq)
    nt = seq // tm
    return pl.pallas_call(
        functools.partial(_fox_c_kernel, tm=tm),
        out_shape=jax.ShapeDtypeStruct(z.shape, BF16),
        grid=(batch, nt),
        in_specs=[
            pl.BlockSpec((tm, LANES), lambda b, t: (b * nt + t, 0)),
            pl.BlockSpec((1, LANES), lambda b, t: (0, 0)),
        ],
        out_specs=pl.BlockSpec((tm, LANES), lambda b, t: (b * nt + t, 0)),
        scratch_shapes=[pltpu.VMEM((1, LANES), F32)],
        compiler_params=_cparams(("parallel", "arbitrary")),
        name="fox_cumsum",
    )(z, b_pad)


def _attend(qs, ks, vs, state, skew, mask=None):
    items = [(t, c) for t in range(len(ks)) for c in range(len(qs))]
    state = list(state)
    pending = {}
    depth = min(skew, len(items) - 1)
    for idx in range(len(items) + depth):
        if idx < len(items):
            t, c = items[idx]
            sc = _dot_nt(qs[c], ks[t][c])
            pending[idx] = sc if mask is None else mask(t, c, sc)
        if idx >= depth:
            t, c = items[idx - depth]
            sc = pending.pop(idx - depth)
            m_prev, acc = state[c]
            m_new = jnp.maximum(m_prev, jnp.max(sc, axis=-1, keepdims=True))
            pe = jnp.exp2(sc - m_new)
            acc = jnp.exp2(m_prev - m_new) * acc + jnp.dot(pe.astype(BF16), vs[t][c], preferred_element_type=F32)
            state[c] = (m_new, acc)
    return tuple(state)


def _tile_loops(n_tiles, subs, step, state):
    done = 0
    for sub in subs:
        n_here = (n_tiles - done) // sub
        state = lax.fori_loop(0, n_here, lambda j, st, done=done, sub=sub: step(done + j * sub, sub, st), state)
        done = done + n_here * sub
    return state


def _fox_attn_kernel(q0_ref, q1_ref, k0_ref, k1_ref, v0_ref, v1_ref, o_ref, *, tq, tk, skew, subs):
    qi = pl.program_id(2)
    k_refs = (k0_ref, k1_ref)
    v_refs = (v0_ref, v1_ref)
    qs = [q0_ref[...], q1_ref[...]]

    def tiles(refs, first, n_sub):
        return [[r[pl.ds(pl.multiple_of((first + t) * tk, tk), tk), :] for r in refs] for t in range(n_sub)]

    def full_step(first, n_sub, state):
        return _attend(qs, tiles(k_refs, first, n_sub), tiles(v_refs, first, n_sub), state, skew)

    def causal(t, c, sc):
        tpos = lax.broadcasted_iota(jnp.int32, (tq, tk), 0)
        kpos = t * tk + lax.broadcasted_iota(jnp.int32, (tq, tk), 1)
        return jnp.where(kpos <= tpos, sc, NEG)

    state = tuple((jnp.full((tq, 1), NEG, F32), jnp.zeros((tq, LANES), F32)) for _ in qs)
    n_below = (qi * tq) // tk
    state = _tile_loops(n_below, subs, full_step, state)
    n_diag = tq // tk
    state = _attend(qs, tiles(k_refs, n_below, n_diag), tiles(v_refs, n_below, n_diag), state, skew, mask=causal)

    lane = lax.broadcasted_iota(jnp.int32, (tq, LANES), 1)
    outs = []
    for _, acc in state:
        den = jnp.sum(jnp.where(lane == HEAD_DIM, acc, 0.0), axis=-1, keepdims=True)
        outs.append(acc / den)
    o_ref[...] = jnp.where(lane < HEAD_DIM, outs[0], pltpu.roll(outs[1], HEAD_DIM, 1)).astype(o_ref.dtype)


def _fox_attn(qkv, *, batch, seq, tq=512, tk=512, skew=2, subs=(4, 2, 1)):
    n = qkv.shape[0]
    tq = min(tq, seq)
    tk = min(tk, tq)
    assert seq % tq == 0 and tq % tk == 0
    nq = seq // tq
    hp = N_HEADS // 2
    kb = N_HEADS
    vb = 2 * N_HEADS
    in_specs = [
        pl.BlockSpec((tq, LANES), lambda b, h, i: (b * nq + i, 2 * h)),
        pl.BlockSpec((tq, LANES), lambda b, h, i: (b * nq + i, 2 * h + 1)),
        pl.BlockSpec((seq, LANES), lambda b, h, i: (b, kb + 2 * h)),
        pl.BlockSpec((seq, LANES), lambda b, h, i: (b, kb + 2 * h + 1)),
        pl.BlockSpec((seq, LANES), lambda b, h, i: (b, vb + 2 * h)),
        pl.BlockSpec((seq, LANES), lambda b, h, i: (b, vb + 2 * h + 1)),
    ]
    return pl.pallas_call(
        functools.partial(_fox_attn_kernel, tq=tq, tk=tk, skew=skew, subs=subs),
        out_shape=jax.ShapeDtypeStruct((n, N_HEADS * HEAD_DIM), BF16),
        grid=(batch, hp, nq),
        in_specs=in_specs,
        out_specs=pl.BlockSpec((tq, LANES), lambda b, h, i: (b * nq + i, h)),
        compiler_params=_cparams(("parallel", "parallel", "arbitrary")),
        name="fox_attn",
    )(qkv, qkv, qkv, qkv, qkv, qkv)


def _router_kernel(x_ref, g_ref, whi_ref, wlo_ref, b_ref, h_ref, idx_ref, gate_ref, cnt_ref):
    xf = x_ref[...]
    ms = jnp.mean(xf * xf, axis=-1, keepdims=True)
    h = xf * lax.rsqrt(ms + EPS) * g_ref[...]
    nc = h.shape[1] // LANES
    for c in range(nc):
        h_ref[pl.ds(c, h.shape[0], stride=nc), :] = h[:, c * LANES:(c + 1) * LANES]
    hhi, hlo = _split2(h)
    whi = whi_ref[...]
    logits = (jnp.dot(hhi, whi, preferred_element_type=F32) + jnp.dot(hlo, whi, preferred_element_type=F32)
              + jnp.dot(hhi, wlo_ref[...], preferred_element_type=F32)) + b_ref[...]
    lane = lax.broadcasted_iota(jnp.int32, logits.shape, 1).astype(F32)
    vals, idxs = [], []
    cur = logits
    for _ in range(TOP_K):
        mx = jnp.max(cur, axis=-1, keepdims=True)
        ix = jnp.min(jnp.where(cur == mx, lane, float(LANES)), axis=-1, keepdims=True)
        vals.append(mx)
        idxs.append(ix)
        cur = jnp.where(lane == ix, -jnp.inf, cur)
    es = [jnp.exp(v - vals[0]) for v in vals]
    tot = es[0] + es[1] + es[2] + es[3]
    idx_out = jnp.zeros(logits.shape, F32)
    gate_out = jnp.zeros(logits.shape, F32)
    for r in range(TOP_K):
        idx_out = jnp.where(lane == r, idxs[r], idx_out)
        gate_out = jnp.where(lane == r, es[r] / tot, gate_out)
    idx_ref[...] = idx_out.astype(jnp.int32)
    gate_ref[...] = gate_out
    hits = sum(jnp.sum((lane == ix).astype(F32), axis=0, keepdims=True) for ix in idxs)

    @pl.when(pl.program_id(0) == 0)
    def _():
        cnt_ref[...] = jnp.zeros(cnt_ref.shape, cnt_ref.dtype)

    cnt_ref[...] += jnp.broadcast_to(hits, cnt_ref.shape)


def _router(x, g, w, b, *, tm=512):
    n, d = x.shape
    e = w.shape[1]
    tm = min(tm, n)
    w_pad = jnp.zeros((d, LANES), F32).at[:, :e].set(w)
    whi = w_pad.astype(BF16)
    wlo = (w_pad - whi.astype(F32)).astype(BF16)
    b_pad = jnp.full((1, LANES), -jnp.inf, F32).at[0, :e].set(b.astype(F32))
    return pl.pallas_call(
        _router_kernel,
        out_shape=(jax.ShapeDtypeStruct((n * (d // LANES), LANES), F32), jax.ShapeDtypeStruct((n, LANES), jnp.int32),
                   jax.ShapeDtypeStruct((n, LANES), F32), jax.ShapeDtypeStruct((8, LANES), F32)),
        grid=(n // tm,),
        in_specs=[
            pl.BlockSpec((tm, d), lambda i: (i, 0)),
            pl.BlockSpec((1, d), lambda i: (0, 0)),
            pl.BlockSpec((d, LANES), lambda i: (0, 0)),
            pl.BlockSpec((d, LANES), lambda i: (0, 0)),
            pl.BlockSpec((1, LANES), lambda i: (0, 0)),
        ],
        out_specs=(pl.BlockSpec((tm * (d // LANES), LANES), lambda i: (i, 0)), pl.BlockSpec((tm, LANES), lambda i: (i, 0)),
                   pl.BlockSpec((tm, LANES), lambda i: (i, 0)), pl.BlockSpec((8, LANES), lambda i: (0, 0))),
        compiler_params=_cparams(("arbitrary",)),
        name="moe_router",
    )(x, g.reshape(1, d).astype(F32), whi, wlo, b_pad)


def _moe_expert_kernel(blk_exp_ref, tok_ref, dst_ref, live_ref, h_hbm, ga_ref, gb_ref, wgu_a, bgu_a, wd_a, bd_a,
                       wgu_b, bgu_b, wd_b, bd_b, y_hbm, xa, xb, oa, ob, gsem, ssem, *, bm, dff):
    i = pl.program_id(0)
    live = live_ref[0]
    nc = xa.shape[0] // bm
    dump = y_hbm.shape[0] // nc - 2 * bm

    def gather(block, buf, sem):
        base = block * bm
        for r in range(bm):
            pltpu.make_async_copy(h_hbm.at[pl.ds(pl.multiple_of(tok_ref[base + r], nc), nc)], buf.at[pl.ds(r * nc, nc)], sem).start()

    def scatter(block, buf, sem):
        base = block * bm
        for r in range(bm):
            pltpu.make_async_copy(buf.at[pl.ds(r * nc, nc)], y_hbm.at[pl.ds(pl.multiple_of(dst_ref[base + r], nc), nc)], sem).start()

    def wait_gather(buf, sem):
        pltpu.make_async_copy(h_hbm.at[pl.ds(0, bm * nc)], buf, sem).wait()

    def wait_scatter(buf, sem):
        pltpu.make_async_copy(buf, y_hbm.at[pl.ds(0, bm * nc)], sem).wait()

    def experts(x_ref, o_ref, g_ref, wgu, bgu, wd, bd):
        x = jnp.concatenate([x_ref[pl.ds(c, bm, stride=nc), :] for c in range(nc)], axis=1)
        gu = jnp.dot(x.astype(BF16), wgu[0, 0], preferred_element_type=F32) + bgu[0]
        a = jnp.minimum(gu[:, :dff], SWIGLU_LIMIT)
        u = jnp.clip(gu[:, dff:], -SWIGLU_LIMIT, SWIGLU_LIMIT)
        y = (u + 1.0) * (a * _sigmoid(SWIGLU_ALPHA * a))
        out = (jnp.dot(y.astype(BF16), wd[0, 0], preferred_element_type=F32) + bd[0]) * g_ref[...]
        for c in range(nc):
            o_ref[pl.ds(c, bm, stride=nc), :] = out[:, c * LANES:(c + 1) * LANES]

    @pl.when(i == 0)
    def _():
        gather(0, xa, gsem.at[0])
        for buf, sem, base in ((oa, ssem.at[0], dump), (ob, ssem.at[1], dump + bm)):
            buf[...] = jnp.zeros(buf.shape, buf.dtype)
            for r in range(bm):
                pltpu.make_async_copy(buf.at[pl.ds(r * nc, nc)], y_hbm.at[pl.ds((base + r) * nc, nc)], sem).start()

    @pl.when(i < live)
    def _():
        blk_a = 2 * i
        blk_b = 2 * i + 1
        nxt_a = jnp.minimum(2 * i + 2, 2 * live - 1)

        gather(blk_b, xb, gsem.at[1])
        wait_gather(xa, gsem.at[0])
        wait_scatter(oa, ssem.at[0])
        experts(xa, oa, ga_ref, wgu_a, bgu_a, wd_a, bd_a)
        scatter(blk_a, oa, ssem.at[0])

        gather(nxt_a, xa, gsem.at[0])
        wait_gather(xb, gsem.at[1])
        wait_scatter(ob, ssem.at[1])
        experts(xb, ob, gb_ref, wgu_b, bgu_b, wd_b, bd_b)
        scatter(blk_b, ob, ssem.at[1])

    @pl.when(i == live - 1)
    def _():
        wait_gather(xa, gsem.at[0])
        wait_scatter(oa, ssem.at[0])
        wait_scatter(ob, ssem.at[1])


def _moe_experts(h, blk_exp, tok, dst, live, gates, w_gu, b_gu, w_down, b_down, *, bm, layer):
    _, e, d, f2 = w_gu.shape
    nc = d // LANES
    n = h.shape[0] // nc
    dff = f2 // 2
    cap = tok.shape[0]
    n_steps = cap // (2 * bm)
    assert cap == n_steps * 2 * bm

    def wspecs(off):
        return [
            pl.BlockSpec((1, 1, d, f2), lambda i, be, tk, ds, lv: (layer, be[2 * i + off], 0, 0)),
            pl.BlockSpec((1, 1, f2), lambda i, be, tk, ds, lv: (be[2 * i + off], 0, 0)),
            pl.BlockSpec((1, 1, dff, d), lambda i, be, tk, ds, lv: (layer, be[2 * i + off], 0, 0)),
            pl.BlockSpec((1, 1, d), lambda i, be, tk, ds, lv: (be[2 * i + off], 0, 0)),
        ]

    grid_spec = pltpu.PrefetchScalarGridSpec(
        num_scalar_prefetch=4,
        grid=(n_steps,),
        in_specs=[
            pl.BlockSpec(memory_space=pl.ANY),
            pl.BlockSpec((bm, 1), lambda i, be, tk, ds, lv: (2 * i, 0)),
            pl.BlockSpec((bm, 1), lambda i, be, tk, ds, lv: (2 * i + 1, 0)),
        ] + wspecs(0) + wspecs(1),
        out_specs=pl.BlockSpec(memory_space=pl.ANY),
        scratch_shapes=[pltpu.VMEM((bm * nc, LANES), F32)] * 4 + [pltpu.SemaphoreType.DMA((2,)),
                                                                  pltpu.SemaphoreType.DMA((2,))],
    )
    bgu3 = b_gu.reshape(e, 1, f2).astype(F32)
    bd3 = b_down.reshape(e, 1, d).astype(F32)
    g2 = gates.reshape(cap, 1)
    return pl.pallas_call(
        functools.partial(_moe_expert_kernel, bm=bm, dff=dff),
        out_shape=jax.ShapeDtypeStruct(((TOP_K * n + 2 * bm) * nc, LANES), F32),
        grid_spec=grid_spec,
        compiler_params=_cparams(("arbitrary",)),
        name="moe_experts",
    )(blk_exp, tok, dst, live, h, g2, g2, w_gu, bgu3, w_down, bd3, w_gu, bgu3, w_down, bd3)


def _moe_layer(x, norm_g, router_w, router_b, w_gu, b_gu, w_down, b_down, layer):
    n, d = x.shape
    bm = MOE_BM
    h, idx128, gate128, cnt = _router(x, norm_g, router_w, router_b)
    n_assign = n * TOP_K
    flat_e = idx128[:, :TOP_K].reshape(-1)
    flat_g = gate128[:, :TOP_K].reshape(-1)
    order = jnp.argsort(flat_e).astype(jnp.int32)
    counts = cnt[0, :N_EXPERTS].astype(jnp.int32)
    padded = (counts + bm - 1) // bm * bm
    start = jnp.cumsum(counts) - counts
    pend = jnp.cumsum(padded)
    pstart = pend - padded
    n_blocks = -(-n_assign // bm) + N_EXPERTS
    n_blocks += n_blocks % 2
    blk_start = jnp.arange(n_blocks, dtype=jnp.int32) * bm
    blk_exp = jnp.minimum(jnp.sum((pend[None, :] <= blk_start[:, None]).astype(jnp.int32), axis=1),
                          N_EXPERTS - 1).astype(jnp.int32)
    row = jnp.arange(bm, dtype=jnp.int32)[None, :]
    off = (blk_start - pstart[blk_exp])[:, None] + row
    valid = off < counts[blk_exp][:, None]
    a = order[jnp.clip(start[blk_exp][:, None] + off, 0, n_assign - 1).reshape(-1)].reshape(n_blocks, bm)
    nc = d // LANES
    tok = (jnp.where(valid, a // TOP_K, 0) * nc).astype(jnp.int32).reshape(-1)
    dump_row = TOP_K * n + (jnp.arange(n_blocks, dtype=jnp.int32) % 2)[:, None] * bm + row
    dst = (jnp.where(valid, (a % TOP_K) * n + a // TOP_K, dump_row) * nc).astype(jnp.int32).reshape(-1)
    live = jnp.maximum((pend[-1] // bm + 1) // 2, 1).astype(jnp.int32).reshape(1)
    gates = jnp.where(valid, flat_g[a.reshape(-1)].reshape(n_blocks, bm), 0.0).reshape(-1)
    return _moe_experts(h, blk_exp, tok, dst, live, gates, w_gu, b_gu, w_down, b_down, bm=bm, layer=layer)


def _ple_kernel(x_ref, y0_ref, y1_ref, y2_ref, y3_ref, g_ref, wg_ref, p_ref, wp_ref, o_ref):
    tm, d = x_ref.shape
    nc = d // LANES

    def plane(y_ref):
        return jnp.concatenate([y_ref[pl.ds(c, tm, stride=nc), :] for c in range(nc)], axis=1)

    xf = x_ref[...] + (((plane(y0_ref) + plane(y1_ref)) + plane(y2_ref)) + plane(y3_ref))
    ms = jnp.mean(xf * xf, axis=-1, keepdims=True)
    h = (xf * lax.rsqrt(ms + EPS) * g_ref[...]).astype(BF16)
    gate = _sigmoid(jnp.dot(h, wg_ref[...], preferred_element_type=F32))
    pp = jnp.dot(p_ref[...].astype(BF16), wp_ref[...], preferred_element_type=F32)
    o_ref[...] = xf + gate * pp


def _ple(x, y, g, wg, p, wp, *, tm=512):
    n, d = x.shape
    pd = p.shape[1]
    tm = min(tm, n)
    nb = n // tm
    nc = d // LANES
    yspecs = [pl.BlockSpec((tm * nc, LANES), functools.partial(lambda i, k: (k * nb + i, 0), k=k)) for k in range(TOP_K)]
    return pl.pallas_call(
        _ple_kernel,
        out_shape=jax.ShapeDtypeStruct((n, d), F32),
        grid=(nb,),
        in_specs=[pl.BlockSpec((tm, d), lambda i: (i, 0))] + yspecs + [
            pl.BlockSpec((1, d), lambda i: (0, 0)),
            pl.BlockSpec((d, d), lambda i: (0, 0)),
            pl.BlockSpec((tm, pd), lambda i: (i, 0)),
            pl.BlockSpec((pd, d), lambda i: (0, 0)),
        ],
        out_specs=pl.BlockSpec((tm, d), lambda i: (i, 0)),
        compiler_params=_cparams(("parallel",)),
        name="ple",
    )(x, y, y, y, y, g.reshape(1, d).astype(F32), wg.astype(BF16), p, wp.astype(BF16))


def _cmp_kernel(z_ref, w1_ref, pos_ref, w2_ref, gain_ref, c_ref, s1_ref, s2_ref, o_ref, *, rows, half):
    j = pl.program_id(1)
    z = z_ref[0, 0, 0]
    pos = pos_ref[0]
    a1 = (z + pos[:, :half]).astype(BF16)
    a2 = (z + pos[:, half:]).astype(BF16)
    u1 = jnp.dot(a1, w1_ref[0, :half, :], preferred_element_type=F32)
    u2 = jnp.dot(a2, w1_ref[0, half:, :], preferred_element_type=F32)
    h = u1 + pltpu.roll(u2, rows - 1, 0)
    hs = h * _sigmoid(h)
    y = jnp.dot(hs.astype(BF16), w2_ref[0], preferred_element_type=F32)

    @pl.when(j == 0)
    def _():
        ss = jnp.sum(y * y, axis=-1, keepdims=True) * (1.0 / LANES)
        yn = y * lax.rsqrt(ss + EPS) * gain_ref[...]
        yn = (yn * c_ref[...] + pltpu.roll(yn, LANES - ROPE_DIM // 2, 1) * s1_ref[...]
              + pltpu.roll(yn, ROPE_DIM // 2, 1) * s2_ref[...])
        o_ref[0, 0, 0] = yn.astype(o_ref.dtype)

    @pl.when(j != 0)
    def _():
        o_ref[0, 0, 0] = y.astype(o_ref.dtype)


def _compress(zr, w1, pos, w2dup, gain_dup, tabs, *, batch):
    rows, width = zr.shape[3], zr.shape[4]
    hid = w1.shape[2]
    g = NSA_GROUPS
    return pl.pallas_call(
        functools.partial(_cmp_kernel, rows=rows, half=width),
        out_shape=jax.ShapeDtypeStruct((batch, 2, g, rows, LANES), BF16),
        grid=(batch, 2, g),
        in_specs=[
            pl.BlockSpec((1, 1, 1, rows, width), lambda b, j, gg: (b, j, gg, 0, 0)),
            pl.BlockSpec((1, 2 * width, hid), lambda b, j, gg: (j, 0, 0)),
            pl.BlockSpec((1, 1, 2 * width), lambda b, j, gg: (j, 0, 0)),
            pl.BlockSpec((1, hid, LANES), lambda b, j, gg: (j, 0, 0)),
            pl.BlockSpec((1, LANES), lambda b, j, gg: (0, 0)),
            pl.BlockSpec((rows, LANES), lambda b, j, gg: (0, 0)),
            pl.BlockSpec((rows, LANES), lambda b, j, gg: (0, 0)),
            pl.BlockSpec((rows, LANES), lambda b, j, gg: (0, 0)),
        ],
        out_specs=pl.BlockSpec((1, 1, 1, rows, LANES), lambda b, j, gg: (b, j, gg, 0, 0)),
        compiler_params=_cparams(("parallel", "parallel", "parallel")),
        name="nsa_compress",
    )(zr, w1, pos, w2dup, gain_dup, *tabs)


def _pipelined(n_items, first, second, skew):
    pending, out = {}, [None] * n_items
    for idx in range(n_items + skew):
        if idx < n_items:
            pending[idx] = first(idx)
        if idx >= skew:
            out[idx - skew] = second(idx - skew, pending.pop(idx - skew))
    return out


def _nsa_attn_kernel(q_ref, gz_ref, kc_ref, vct_ref, ovt_ref, ks_ref, vs_ref, kw_ref, vw_ref, o_ref,
                     e_ref, oc_ref, imp_ref, notsel_ref, *, n_cmp_rows, tk_sel, skew, grp, subs):
    g = pl.program_id(1)
    i = pl.program_id(2)
    qb = NSA_QB
    mh = HEADS_PER_GROUP
    rows = mh * qb
    q0 = i * qb
    seq = ks_ref.shape[0]

    @pl.when(i == 0)
    def _():
        r = lax.broadcasted_iota(jnp.int32, (seq, LANES), 0)
        lb = lax.broadcasted_iota(jnp.int32, (seq, LANES), 1)
        e_ref[...] = jnp.where((r >> 6) == lb, NEG, 0.0).astype(BF16)

    qh = [q_ref[:, m * LANES:(m + 1) * LANES] for m in range(mh)]
    q4 = jnp.concatenate(qh, axis=0)
    jl = lax.broadcasted_iota(jnp.int32, (qb, LANES), 1)

    tlane = q0 + (lax.broadcasted_iota(jnp.int32, (1, rows), 1) & (qb - 1))

    def cmp_branch(nrow):
        st = _dot_nt(kc_ref[0, 0, 0, :nrow, :], q4)
        cend = lax.broadcasted_iota(jnp.int32, (nrow, rows), 0) * CMP_STRIDE + (CMP_BLOCK - 1)
        valid = cend <= tlane
        st = jnp.where(valid, st, NEG)
        e = jnp.where(valid, jnp.exp2(st - jnp.max(st, axis=0, keepdims=True)), 0.0)
        pt = e / jnp.maximum(jnp.sum(e, axis=0, keepdims=True), 1e-30)
        oct = jnp.dot(vct_ref[0, 0, :, :nrow], pt.astype(BF16), preferred_element_type=F32)
        for m in range(mh):
            oc_ref[m * qb:(m + 1) * qb, :] = oct[:, m * qb:(m + 1) * qb].T
        psum = (pt[:, 0:qb] + pt[:, qb:2 * qb]) + (pt[:, 2 * qb:3 * qb] + pt[:, 3 * qb:4 * qb])
        phi, plo = _split2(psum)
        ovt = ovt_ref[:, :nrow]
        imp_ref[...] = (jnp.dot(ovt, phi, preferred_element_type=F32)
                        + jnp.dot(ovt, plo, preferred_element_type=F32))

    n_chunks = n_cmp_rows // LANES
    last_visible = (q0 + qb - CMP_BLOCK) // CMP_STRIDE
    need = jnp.clip(last_visible // LANES + 1, 1, n_chunks)
    for nch in range(1, n_chunks + 1):
        pl.when(need == nch)(functools.partial(cmp_branch, nch * LANES))

    def select_blocks(nrow):
        jb = lax.broadcasted_iota(jnp.int32, (nrow, qb), 0)
        tq = q0 + lax.broadcasted_iota(jnp.int32, (nrow, qb), 1)
        cur = tq >> 6
        forced = (jb == 0) | (jb == cur) | (jb == cur - 1)
        causal = (jb << 6) <= tq
        score = jnp.where(causal, jnp.where(forced, FORCE, imp_ref[:nrow, :]), -jnp.inf)
        notsel_t = jnp.ones((nrow, qb), F32)
        jbf = jb.astype(F32)
        for _ in range(SEL_TOPK):
            smx = jnp.max(score, axis=0, keepdims=True)
            ix = jnp.min(jnp.where(score == smx, jbf, float(LANES)), axis=0, keepdims=True)
            hit = jbf == ix
            notsel_t = jnp.where(hit, 0.0, notsel_t)
            score = jnp.where(hit, -jnp.inf, score)
        notsel_ref[:nrow, :] = notsel_t
        if nrow < LANES:
            notsel_ref[nrow:, :] = jnp.ones((LANES - nrow, qb), F32)

    sel_chunk = 32
    n_sel_chunks = LANES // sel_chunk
    sel_need = jnp.clip(((q0 + qb - 1) // SEL_BLOCK) // sel_chunk + 1, 1, n_sel_chunks)
    for nch in range(1, n_sel_chunks + 1):
        pl.when(sel_need == nch)(functools.partial(select_blocks, nch * sel_chunk))
    notsel_b = notsel_ref[...].T.astype(BF16)
    n_grp = mh // grp
    q_aug = [jnp.concatenate([jnp.concatenate([qh[m], notsel_b], axis=1) for m in range(c * grp, (c + 1) * grp)], axis=0)
             for c in range(n_grp)]
    tpos = q0 + lax.broadcasted_iota(jnp.int32, (qb, 1), 0)
    tpos_g = jnp.concatenate([tpos] * grp, axis=0)

    def sel_tiles(first, n_sub):
        ks, vs = [], []
        for t in range(n_sub):
            k0 = pl.multiple_of((first + t) * tk_sel, tk_sel)
            k_aug = jnp.concatenate([ks_ref[pl.ds(k0, tk_sel), :], e_ref[pl.ds(k0, tk_sel), :]], axis=1)
            ks.append([k_aug] * n_grp)
            vs.append([vs_ref[pl.ds(k0, tk_sel), :]] * n_grp)
        return ks, vs

    def sel_step(first, n_sub, state):
        ks, vs = sel_tiles(first, n_sub)
        return _attend(q_aug, ks, vs, state, skew)

    state = tuple((jnp.full((grp * qb, 1), NEG, F32), jnp.zeros((grp * qb, LANES), F32)) for _ in range(n_grp))
    n_below = q0 // tk_sel
    state = _tile_loops(n_below, subs, sel_step, state)

    def causal(t, c, sc):
        kpos = n_below * tk_sel + lax.broadcasted_iota(jnp.int32, (grp * qb, tk_sel), 1)
        return jnp.where(kpos <= tpos_g, sc, NEG)

    ks, vs = sel_tiles(n_below, 1)
    state = _attend(q_aug, ks, vs, state, skew, mask=causal)
    sel_acc = [state[m // grp][1][(m % grp) * qb:(m % grp + 1) * qb] for m in range(mh)]

    wk = WINDOW + qb
    k0w = pl.multiple_of(jnp.maximum(i - WINDOW // qb, 0) * qb, qb)
    kwt = kw_ref[pl.ds(k0w, wk), :]
    vwt = vw_ref[pl.ds(k0w, wk), :]
    kposw = k0w + lax.broadcasted_iota(jnp.int32, (grp * qb, wk), 1)
    wmask = (kposw <= tpos_g) & (kposw > tpos_g - WINDOW)
    q_grp = [jnp.concatenate(qh[c * grp:(c + 1) * grp], axis=0) for c in range(n_grp)]

    def win_scores(c):
        return jnp.where(wmask, _dot_nt(q_grp[c], kwt), NEG)

    def win_out(c, sc):
        pe = jnp.exp2(sc - jnp.max(sc, axis=-1, keepdims=True))
        return jnp.dot(pe.astype(BF16), vwt, preferred_element_type=F32)

    win_grp = _pipelined(n_grp, win_scores, win_out, min(skew, n_grp - 1))
    win_acc = [win_grp[m // grp][(m % grp) * qb:(m % grp + 1) * qb] for m in range(mh)]

    sig = _sigmoid(gz_ref[...])
    den_lane = jl == HEAD_DIM
    heads = []
    for m in range(mh):
        col = g * mh + m
        gates = [jnp.sum(jnp.where(jl == br * N_HEADS + col, sig, 0.0), axis=-1, keepdims=True)
                 for br in range(3)]
        acc_s = sel_acc[m]
        acc_w = win_acc[m]
        den_s = jnp.sum(jnp.where(den_lane, acc_s, 0.0), axis=-1, keepdims=True)
        den_w = jnp.sum(jnp.where(den_lane, acc_w, 0.0), axis=-1, keepdims=True)
        heads.append(gates[0] * oc_ref[m * qb:(m + 1) * qb, :] + (gates[1] / den_s) * acc_s
                     + (gates[2] / den_w) * acc_w)
    lane_lo = jl < HEAD_DIM
    o_ref[:, 0:LANES] = jnp.where(lane_lo, heads[0], pltpu.roll(heads[1], HEAD_DIM, 1)).astype(o_ref.dtype)
    o_ref[:, LANES:2 * LANES] = jnp.where(lane_lo, heads[2], pltpu.roll(heads[3], HEAD_DIM, 1)).astype(o_ref.dtype)


def _nsa_attn(qn, gz, kvc, overlap, kvd, *, batch, seq, skew=2, grp=2, subs=(4, 2, 1)):
    n = qn.shape[0]
    qb = NSA_QB
    nq = seq // qb
    g = NSA_GROUPS
    rows = HEADS_PER_GROUP * qb
    n_cmp_rows = kvc.shape[3]
    tk_sel = min(512, seq)
    assert seq >= WINDOW + qb and seq % tk_sel == 0 and tk_sel % qb == 0
    in_specs = [
        pl.BlockSpec((qb, HEADS_PER_GROUP * LANES), lambda b, gg, i: (b * nq + i, gg)),
        pl.BlockSpec((qb, LANES), lambda b, gg, i: (b * nq + i, 0)),
        pl.BlockSpec((1, 1, 1, n_cmp_rows, LANES), lambda b, gg, i: (b, 0, gg, 0, 0)),
        pl.BlockSpec((1, 1, LANES, n_cmp_rows), lambda b, gg, i: (b, gg, 0, 0)),
        pl.BlockSpec((LANES, n_cmp_rows), lambda b, gg, i: (0, 0)),
        pl.BlockSpec((seq, LANES), lambda b, gg, i: (b, 0 * g + gg)),
        pl.BlockSpec((seq, LANES), lambda b, gg, i: (b, 1 * g + gg)),
        pl.BlockSpec((seq, LANES), lambda b, gg, i: (b, 2 * g + gg)),
        pl.BlockSpec((seq, LANES), lambda b, gg, i: (b, 3 * g + gg)),
    ]
    return pl.pallas_call(
        functools.partial(_nsa_attn_kernel, n_cmp_rows=n_cmp_rows, tk_sel=tk_sel, skew=skew, grp=grp, subs=subs),
        out_shape=jax.ShapeDtypeStruct((n, N_HEADS * HEAD_DIM), BF16),
        grid=(batch, g, nq),
        in_specs=in_specs,
        out_specs=pl.BlockSpec((qb, HEADS_PER_GROUP * HEAD_DIM), lambda b, gg, i: (b * nq + i, gg)),
        scratch_shapes=[pltpu.VMEM((seq, LANES), BF16), pltpu.VMEM((rows, LANES), F32), pltpu.VMEM((LANES, qb), F32),
                        pltpu.VMEM((LANES, qb), F32)],
        compiler_params=_cparams(("parallel", "parallel", "arbitrary")),
        name="nsa_attn",
    )(qn, gz, kvc, jnp.swapaxes(kvc[:, 1], -1, -2), overlap.T, kvd, kvd, kvd, kvd)


def _pad_cols(w, width=LANES):
    d, f = w.shape
    return jnp.zeros((d, width), w.dtype).at[:, :f].set(w)


def _tile_gain(gain, n_heads, scale=1.0, dup=False):
    g = gain.astype(F32) * scale
    second = g if dup else jnp.zeros_like(g)
    return jnp.tile(jnp.concatenate([g, second]), n_heads)


def _fox_aug_tables():
    hw = N_HEADS * LANES
    piece = jnp.arange(3)[:, None]
    head = jnp.arange(N_HEADS)[None, :]
    src = (piece * N_HEADS + head).reshape(-1)
    q_dst = (head * LANES + HEAD_DIM + piece).reshape(-1)
    k_dst = (hw + head * LANES + HEAD_DIM + 3 + piece).reshape(-1)
    place = jnp.zeros((LANES, 3 * hw), F32).at[src, q_dst].set(1.0).at[src, k_dst].set(-1.0)
    lane = jnp.arange(LANES)
    q_bias = ((lane >= HEAD_DIM + 3) & (lane < HEAD_DIM + 6)).astype(F32)
    k_bias = ((lane >= HEAD_DIM) & (lane < HEAD_DIM + 3)).astype(F32)
    v_bias = (lane == HEAD_DIM).astype(F32)
    bias = jnp.concatenate([jnp.tile(q_bias, N_HEADS), jnp.tile(k_bias, N_HEADS), jnp.tile(v_bias, N_HEADS)])
    return place.astype(BF16), bias


def _fox_layer(x, batch, seq, attn_g, w_in, b_f, qk_gain, w_out):
    hd = N_HEADS * HEAD_DIM
    hw = N_HEADS * LANES
    scale = HEAD_DIM ** -0.5
    z = _rms_proj(x, attn_g, _pad_cols(w_in[:, 3 * hd:]).astype(BF16), out_dtype=F32, seq=seq)
    b_pad = jnp.zeros((1, LANES), F32).at[0, :N_HEADS].set(b_f.astype(F32))
    c3 = _fox_c(z, b_pad, batch=batch, seq=seq)
    w_bf = w_in.astype(BF16)
    gain = jnp.concatenate([_tile_gain(qk_gain[0], N_HEADS, scale * LOG2E), _tile_gain(qk_gain[1], N_HEADS),
                            jnp.zeros((hw,), F32)])
    tn = 512
    place, bias = _fox_aug_tables()
    qkv = _rms_proj(x, attn_g, w_bf, out_dtype=BF16, seq=seq, tn=tn, gain=gain, w_cols=3 * hd,
                    norm_blocks=tuple(range(2 * hw // tn)), placed=(c3, place), col_bias=bias, expand=True)
    o = _fox_attn(qkv, batch=batch, seq=seq)
    return _matmul_res(o, w_out.astype(BF16), x)


def _nsa_shared_kv(x, batch, seq, kv_norm, kv_w, kv_k_gain, cmp_pos, cmp_w1, cmp_w2):
    g = NSA_GROUPS
    gw = g * HEAD_DIM
    tabs = _rope_tables(jnp.arange(seq))
    kv_bf = kv_w.astype(BF16)
    zero = jnp.zeros((g * LANES,), F32)
    one_lane = jnp.tile((jnp.arange(LANES) == HEAD_DIM).astype(F32), g)
    gain_kvd = jnp.concatenate([_tile_gain(kv_k_gain[1], g), zero, _tile_gain(kv_k_gain[2], g), zero])
    bias_kvd = jnp.concatenate([zero, one_lane, zero, one_lane])
    kvd = _rms_proj(x, kv_norm, kv_bf, out_dtype=BF16, seq=seq, tn=g * LANES, gain=gain_kvd, w_col0=2 * gw,
                    norm_blocks=(0, 2), rope_tabs=tabs, col_bias=bias_kvd, expand=True)
    zc = _rms_proj(x, kv_norm, kv_bf, out_dtype=F32, seq=seq, w_cols=2 * gw)
    rows = seq // CMP_STRIDE
    zr = zc.reshape(batch, rows, CMP_STRIDE, 2, g, HEAD_DIM).transpose(0, 3, 4, 1, 2, 5)
    zr = zr.reshape(batch, 2, g, rows, CMP_STRIDE * HEAD_DIM)
    cmp_end = jnp.arange(rows) * CMP_STRIDE + CMP_BLOCK - 1
    ctabs = _rope_tables(cmp_end)
    w2dup = jnp.concatenate([cmp_w2, cmp_w2], axis=-1).astype(BF16)
    kvc = _compress(zr, cmp_w1.astype(BF16), cmp_pos.reshape(2, 1, CMP_BLOCK * HEAD_DIM).astype(F32), w2dup,
                    _tile_gain(kv_k_gain[0], 1, dup=True).reshape(1, LANES), ctabs, batch=batch)
    return kvc, kvd


def _nsa_layer(x, batch, seq, attn_g, w_in, q_gain, w_out, kvc, kvd):
    hd = N_HEADS * HEAD_DIM
    scale = HEAD_DIM ** -0.5
    rows = seq // CMP_STRIDE
    tabs = _rope_tables(jnp.arange(seq))
    qn = _rms_proj(x, attn_g, w_in.astype(BF16), out_dtype=BF16, seq=seq, expand=True, w_cols=hd,
                   gain=_tile_gain(q_gain, N_HEADS, scale * LOG2E), norm_blocks=tuple(range(N_HEADS * LANES // 512)),
                   rope_tabs=tabs)
    gz = _rms_proj(x, attn_g, _pad_cols(w_in[:, hd:]).astype(BF16), out_dtype=F32, seq=seq)

    n_sel = seq // SEL_BLOCK
    assert n_sel <= LANES
    cmp_start = jnp.arange(rows) * CMP_STRIDE
    sel_start = jnp.arange(LANES) * SEL_BLOCK
    overlap = jnp.clip(jnp.minimum(cmp_start[:, None] + CMP_BLOCK, sel_start[None, :] + SEL_BLOCK)
                       - jnp.maximum(cmp_start[:, None], sel_start[None, :]), 0)
    overlap = jnp.where((jnp.arange(rows) < rows - 1)[:, None] & (jnp.arange(LANES) < n_sel)[None, :], overlap, 0)
    o = _nsa_attn(qn, gz, kvc, overlap.astype(BF16), kvd, batch=batch, seq=seq)
    return _matmul_res(o, w_out.astype(BF16), x)


def kernel(x, p, attn_norm, ffn_norm, ple_norm, ple_gate_w, ple_proj_w, router_w, router_b, w_gu, b_gu, w_down, b_down, fox_w_in, fox_b_f, fox_qk_gain, fox_w_out, kv_norm, kv_w, kv_k_gain, cmp_pos, cmp_w1, cmp_w2, nsa_w_in, nsa_q_gain, nsa_w_out):
    batch, seq, d = x.shape
    depth = p.shape[0]
    n_a = fox_w_in.shape[0]
    xt = x.reshape(batch * seq, d)
    w_gu_bf = w_gu.astype(BF16)
    w_down_bf = w_down.astype(BF16)
    shared = None
    for layer in range(depth):
        if layer == n_a:
            shared = _nsa_shared_kv(xt, batch, seq, kv_norm, kv_w, kv_k_gain, cmp_pos, cmp_w1, cmp_w2)
        if layer < n_a:
            xt = _fox_layer(xt, batch, seq, attn_norm[layer], fox_w_in[layer], fox_b_f[layer],
                            fox_qk_gain[layer], fox_w_out[layer])
        else:
            i = layer - n_a
            xt = _nsa_layer(xt, batch, seq, attn_norm[layer], nsa_w_in[i], nsa_q_gain[i], nsa_w_out[i], *shared)
        y = _moe_layer(xt, ffn_norm[layer], router_w[layer], router_b[layer], w_gu_bf, b_gu[layer],
                       w_down_bf, b_down[layer], layer)
        xt = _ple(xt, y, ple_norm[layer], ple_gate_w[layer], p[layer].reshape(batch * seq, -1), ple_proj_w[layer])
    return xt.reshape(batch, seq, d)
```

```python
import functools

import jax
import jax.numpy as jnp
from jax import lax
from jax.experimental import pallas as pl
from jax.experimental.pallas import tpu as pltpu

F32 = jnp.float32
BF16 = jnp.bfloat16

N_HEADS = 16
HEAD_DIM = 64
ROPE_DIM = HEAD_DIM // 4
ROPE_THETA = 500000.0
NSA_GROUPS = 4
HEADS_PER_GROUP = N_HEADS // NSA_GROUPS
CMP_BLOCK = 32
CMP_STRIDE = 16
SEL_BLOCK = 64
SEL_TOPK = 16
WINDOW = 512
N_EXPERTS = 32
TOP_K = 4
SWIGLU_LIMIT = 7.0
SWIGLU_ALPHA = 1.702
EPS = 1e-6
NEG = -1e30
FORCE = 1e6
LOG2E = 1.4426950408889634

LANES = 128
NSA_QB = 256
MOE_BM = 256
VMEM_LIMIT = 56 * 1024 * 1024


def _cparams(sem):
    return pltpu.CompilerParams(dimension_semantics=sem, vmem_limit_bytes=VMEM_LIMIT)


def _sigmoid(x):
    return 1.0 / (1.0 + jnp.exp(-x))


def _dot_nt(a, b):
    return lax.dot_general(a, b, (((1,), (1,)), ((), ())), preferred_element_type=F32)


def _split2(x):
    hi = x.astype(BF16)
    lo = (x - hi.astype(F32)).astype(BF16)
    return hi, lo


def _proj_kernel(*refs, norm_blocks, rope, placed, biased, expand, head_div, tn, out_dtype):
    refs = list(refs)
    x_ref, g_ref, w_ref = refs[:3]
    del refs[:3]
    if norm_blocks:
        gain_ref = refs.pop(0)
    if rope:
        c_ref, s1_ref, s2_ref = refs[:3]
        del refs[:3]
    if placed:
        ex_ref, place_ref = refs[:2]
        del refs[:2]
    if biased:
        bias_ref = refs.pop(0)
    o_ref, h_ref = refs
    j = pl.program_id(1)
    aug = placed or biased

    @pl.when(j == 0)
    def _():
        xf = x_ref[...]
        ms = jnp.mean(xf * xf, axis=-1, keepdims=True)
        h_ref[...] = (xf * lax.rsqrt(ms + EPS) * g_ref[...]).astype(BF16)

    acc = jnp.dot(h_ref[...], w_ref[...], preferred_element_type=F32)
    if placed and biased:
        extra = jnp.dot(ex_ref[...], place_ref[...], preferred_element_type=F32) + bias_ref[...]
    elif biased:
        extra = jnp.broadcast_to(bias_ref[...], (acc.shape[0], tn))
    if expand:
        lo = lax.broadcasted_iota(jnp.int32, (acc.shape[0], LANES), 1) < HEAD_DIM
        blocks = []
        for c in range(acc.shape[1] // LANES):
            a = acc[:, c * LANES:(c + 1) * LANES]
            blocks += [jnp.where(lo, a, 0.0), jnp.where(lo, pltpu.roll(a, HEAD_DIM, 1), 0.0)]
    else:
        blocks = [acc[:, c * LANES:(c + 1) * LANES] for c in range(acc.shape[1] // LANES)]

    def plain():
        for c, a in enumerate(blocks):
            sl = slice(c * LANES, (c + 1) * LANES)
            o_ref[:, sl] = ((a + extra[:, sl]) if aug else a).astype(out_dtype)

    if not norm_blocks:
        plain()
        return
    is_norm = functools.reduce(jnp.logical_or, [j == c for c in norm_blocks])

    @pl.when(is_norm)
    def _():
        for c, a in enumerate(blocks):
            sl = slice(c * LANES, (c + 1) * LANES)
            ss = jnp.sum(a * a, axis=-1, keepdims=True) * (1.0 / head_div)
            y = a * lax.rsqrt(ss + EPS) * gain_ref[:, sl]
            if rope:
                y = (y * c_ref[...] + pltpu.roll(y, LANES - ROPE_DIM // 2, 1) * s1_ref[...]
                     + pltpu.roll(y, ROPE_DIM // 2, 1) * s2_ref[...])
            if aug:
                y = y + extra[:, sl]
            o_ref[:, sl] = y.astype(out_dtype)

    @pl.when(jnp.logical_not(is_norm))
    def _():
        plain()


def _rms_proj(x, g, w, *, out_dtype, seq, tm=2048, tn=512, gain=None, norm_blocks=(),
              rope_tabs=None, placed=None, col_bias=None, expand=False, head_div=HEAD_DIM,
              w_col0=0, w_cols=None):
    n, d = x.shape
    w_cols = w.shape[1] - w_col0 if w_cols is None else w_cols
    f = w_cols * (2 if expand else 1)
    tn = min(tn, f)
    tw = tn // 2 if expand else tn
    assert w_col0 % tw == 0
    j0 = w_col0 // tw
    tm = min(tm, seq)
    assert n % tm == 0 and f % tn == 0 and seq % tm == 0
    nt = seq // tm
    in_specs = [
        pl.BlockSpec((tm, d), lambda i, j: (i, 0)),
        pl.BlockSpec((1, d), lambda i, j: (0, 0)),
        pl.BlockSpec((d, tw), lambda i, j: (0, j0 + j)),
    ]
    args = [x, g.reshape(1, d).astype(F32), w]
    if norm_blocks:
        in_specs.append(pl.BlockSpec((1, tn), lambda i, j: (0, j)))
        args.append(gain.reshape(1, f).astype(F32))
    if rope_tabs is not None:
        assert norm_blocks
        for tab in rope_tabs:
            in_specs.append(pl.BlockSpec((tm, LANES), lambda i, j: (i % nt, 0)))
            args.append(tab)
    if placed is not None:
        assert col_bias is not None
        values, place = placed
        in_specs += [pl.BlockSpec((tm, LANES), lambda i, j: (i, 0)),
                     pl.BlockSpec((LANES, tn), lambda i, j: (0, j))]
        args += [values, place]
    if col_bias is not None:
        in_specs.append(pl.BlockSpec((1, tn), lambda i, j: (0, j)))
        args.append(col_bias.reshape(1, f).astype(F32))
    kern = functools.partial(_proj_kernel, norm_blocks=tuple(norm_blocks), rope=rope_tabs is not None,
                             placed=placed is not None, biased=col_bias is not None, expand=expand,
                             head_div=float(head_div), tn=tn, out_dtype=out_dtype)
    return pl.pallas_call(
        kern,
        out_shape=jax.ShapeDtypeStruct((n, f), out_dtype),
        grid=(n // tm, f // tn),
        in_specs=in_specs,
        out_specs=pl.BlockSpec((tm, tn), lambda i, j: (i, j)),
        scratch_shapes=[pltpu.VMEM((tm, d), BF16)],
        compiler_params=_cparams(("parallel", "arbitrary")),
        name="rms_proj",
    )(*args)


def _rope_tables(pos, width=LANES):
    half = ROPE_DIM // 2
    inv = jnp.power(jnp.float32(ROPE_THETA), -jnp.arange(0, ROPE_DIM, 2, dtype=F32) / ROPE_DIM)
    ang = pos.astype(F32)[:, None] * inv[None, :]
    cos, sin = jnp.cos(ang), jnp.sin(ang)
    t = pos.shape[0]
    ones = jnp.ones((t, HEAD_DIM - ROPE_DIM), F32)
    zeros_h = jnp.zeros((t, half), F32)
    zeros_r = jnp.zeros((t, HEAD_DIM - ROPE_DIM), F32)
    c = jnp.concatenate([cos, cos, ones], axis=1)
    s1 = jnp.concatenate([-sin, zeros_h, zeros_r], axis=1)
    s2 = jnp.concatenate([zeros_h, sin, zeros_r], axis=1)
    rep = width // HEAD_DIM
    return tuple(jnp.tile(a, (1, rep)) for a in (c, s1, s2))


def _matmul_res_kernel(a_ref, w_ref, x_ref, o_ref):
    o_ref[...] = x_ref[...] + jnp.dot(a_ref[...], w_ref[...], preferred_element_type=F32)


def _matmul_res(a, w, x, *, tm=512):
    n, k = a.shape
    f = w.shape[1]
    tm = min(tm, n)
    return pl.pallas_call(
        _matmul_res_kernel,
        out_shape=jax.ShapeDtypeStruct((n, f), F32),
        grid=(n // tm,),
        in_specs=[
            pl.BlockSpec((tm, k), lambda i: (i, 0)),
            pl.BlockSpec((k, f), lambda i: (0, 0)),
            pl.BlockSpec((tm, f), lambda i: (i, 0)),
        ],
        out_specs=pl.BlockSpec((tm, f), lambda i: (i, 0)),
        compiler_params=_cparams(("parallel",)),
        name="matmul_res",
    )(a, w, x)


def _fox_c_kernel(z_ref, b_ref, c_ref, carry_ref, *, tm):
    t = pl.program_id(1)

    @pl.when(t == 0)
    def _():
        carry_ref[...] = jnp.zeros_like(carry_ref)

    u = z_ref[...] + b_ref[...]
    logf = jnp.minimum(u, 0.0) - jnp.log(1.0 + jnp.exp(-jnp.abs(u)))
    row = lax.broadcasted_iota(jnp.int32, (tm, tm), 0)
    col = lax.broadcasted_iota(jnp.int32, (tm, tm), 1)
    tri = (row >= col).astype(BF16)
    hi = logf.astype(BF16)
    r1 = logf - hi.astype(F32)
    mid = r1.astype(BF16)
    lo = (r1 - mid.astype(F32)).astype(BF16)
    cs = (jnp.dot(tri, hi, preferred_element_type=F32) + jnp.dot(tri, mid, preferred_element_type=F32)
          + jnp.dot(tri, lo, preferred_element_type=F32))
    c = cs + carry_ref[...]
    carry_ref[...] = c[tm - 1:tm, :]
    c2 = c * LOG2E
    p0 = c2.astype(BF16).astype(F32)
    p1 = (c2 - p0).astype(BF16).astype(F32)
    p2 = ((c2 - p0) - p1).astype(BF16).astype(F32)
    lane = lax.broadcasted_iota(jnp.int32, c.shape, 1)
    out = jnp.where(lane < N_HEADS, p0,
                    jnp.where(lane < 2 * N_HEADS, pltpu.roll(p1, N_HEADS, 1),
                              jnp.where(lane < 3 * N_HEADS, pltpu.roll(p2, 2 * N_HEADS, 1), 0.0)))
    c_ref[...] = out.astype(BF16)


def _fox_c(z, b_pad, *, batch, seq, tm=512):
    tm = min(tm, seq)
    nt = seq // tm
    return pl.pallas_call(
        functools.partial(_fox_c_kernel, tm=tm),
        out_shape=jax.ShapeDtypeStruct(z.shape, BF16),
        grid=(batch, nt),
        in_specs=[
            pl.BlockSpec((tm, LANES), lambda b, t: (b * nt + t, 0)),
            pl.BlockSpec((1, LANES), lambda b, t: (0, 0)),
        ],
        out_specs=pl.BlockSpec((tm, LANES), lambda b, t: (b * nt + t, 0)),
        scratch_shapes=[pltpu.VMEM((1, LANES), F32)],
        compiler_params=_cparams(("parallel", "arbitrary")),
        name="fox_cumsum",
    )(z, b_pad)


def _attend(qs, ks, vs, state, skew, mask=None):
    items = [(t, c) for t in range(len(ks)) for c in range(len(qs))]
    state = list(state)
    pending = {}
    depth = min(skew, len(items) - 1)
    for idx in range(len(items) + depth):
        if idx < len(items):
            t, c = items[idx]
            sc = _dot_nt(qs[c], ks[t][c])
            pending[idx] = sc if mask is None else mask(t, c, sc)
        if idx >= depth:
            t, c = items[idx - depth]
            sc = pending.pop(idx - depth)
            m_prev, acc = state[c]
            m_new = jnp.maximum(m_prev, jnp.max(sc, axis=-1, keepdims=True))
            pe = jnp.exp2(sc - m_new)
            acc = jnp.exp2(m_prev - m_new) * acc + jnp.dot(pe.astype(BF16), vs[t][c], preferred_element_type=F32)
            state[c] = (m_new, acc)
    return tuple(state)


def _tile_loops(n_tiles, subs, step, state):
    done = 0
    for sub in subs:
        n_here = (n_tiles - done) // sub
        state = lax.fori_loop(0, n_here, lambda j, st, done=done, sub=sub: step(done + j * sub, sub, st), state)
        done = done + n_here * sub
    return state


def _fox_attn_kernel(q0_ref, q1_ref, k0_ref, k1_ref, v0_ref, v1_ref, o_ref, *, tq, tk, skew, subs):
    qi = pl.program_id(2)
    k_refs = (k0_ref, k1_ref)
    v_refs = (v0_ref, v1_ref)
    qs = [q0_ref[...], q1_ref[...]]

    def tiles(refs, first, n_sub):
        return [[r[pl.ds(pl.multiple_of((first + t) * tk, tk), tk), :] for r in refs] for t in range(n_sub)]

    def full_step(first, n_sub, state):
        return _attend(qs, tiles(k_refs, first, n_sub), tiles(v_refs, first, n_sub), state, skew)

    def causal(t, c, sc):
        tpos = lax.broadcasted_iota(jnp.int32, (tq, tk), 0)
        kpos = t * tk + lax.broadcasted_iota(jnp.int32, (tq, tk), 1)
        return jnp.where(kpos <= tpos, sc, NEG)

    state = tuple((jnp.full((tq, 1), NEG, F32), jnp.zeros((tq, LANES), F32)) for _ in qs)
    n_below = (qi * tq) // tk
    state = _tile_loops(n_below, subs, full_step, state)
    n_diag = tq // tk
    state = _attend(qs, tiles(k_refs, n_below, n_diag), tiles(v_refs, n_below, n_diag), state, skew, mask=causal)

    lane = lax.broadcasted_iota(jnp.int32, (tq, LANES), 1)
    outs = []
    for _, acc in state:
        den = jnp.sum(jnp.where(lane == HEAD_DIM, acc, 0.0), axis=-1, keepdims=True)
        outs.append(acc / den)
    o_ref[...] = jnp.where(lane < HEAD_DIM, outs[0], pltpu.roll(outs[1], HEAD_DIM, 1)).astype(o_ref.dtype)


def _fox_attn(qkv, *, batch, seq, tq=512, tk=512, skew=2, subs=(4, 2, 1)):
    n = qkv.shape[0]
    tq = min(tq, seq)
    tk = min(tk, tq)
    assert seq % tq == 0 and tq % tk == 0
    nq = seq // tq
    hp = N_HEADS // 2
    kb = N_HEADS
    vb = 2 * N_HEADS
    in_specs = [
        pl.BlockSpec((tq, LANES), lambda b, h, i: (b * nq + i, 2 * h)),
        pl.BlockSpec((tq, LANES), lambda b, h, i: (b * nq + i, 2 * h + 1)),
        pl.BlockSpec((seq, LANES), lambda b, h, i: (b, kb + 2 * h)),
        pl.BlockSpec((seq, LANES), lambda b, h, i: (b, kb + 2 * h + 1)),
        pl.BlockSpec((seq, LANES), lambda b, h, i: (b, vb + 2 * h)),
        pl.BlockSpec((seq, LANES), lambda b, h, i: (b, vb + 2 * h + 1)),
    ]
    return pl.pallas_call(
        functools.partial(_fox_attn_kernel, tq=tq, tk=tk, skew=skew, subs=subs),
        out_shape=jax.ShapeDtypeStruct((n, N_HEADS * HEAD_DIM), BF16),
        grid=(batch, hp, nq),
        in_specs=in_specs,
        out_specs=pl.BlockSpec((tq, LANES), lambda b, h, i: (b * nq + i, h)),
        compiler_params=_cparams(("parallel", "parallel", "arbitrary")),
        name="fox_attn",
    )(qkv, qkv, qkv, qkv, qkv, qkv)


def _router_kernel(x_ref, g_ref, whi_ref, wlo_ref, b_ref, h_ref, idx_ref, gate_ref, cnt_ref):
    xf = x_ref[...]
    ms = jnp.mean(xf * xf, axis=-1, keepdims=True)
    h = xf * lax.rsqrt(ms + EPS) * g_ref[...]
    nc = h.shape[1] // LANES
    for c in range(nc):
        h_ref[pl.ds(c, h.shape[0], stride=nc), :] = h[:, c * LANES:(c + 1) * LANES]
    hhi, hlo = _split2(h)
    whi = whi_ref[...]
    logits = (jnp.dot(hhi, whi, preferred_element_type=F32) + jnp.dot(hlo, whi, preferred_element_type=F32)
              + jnp.dot(hhi, wlo_ref[...], preferred_element_type=F32)) + b_ref[...]
    lane = lax.broadcasted_iota(jnp.int32, logits.shape, 1).astype(F32)
    vals, idxs = [], []
    cur = logits
    for _ in range(TOP_K):
        mx = jnp.max(cur, axis=-1, keepdims=True)
        ix = jnp.min(jnp.where(cur == mx, lane, float(LANES)), axis=-1, keepdims=True)
        vals.append(mx)
        idxs.append(ix)
        cur = jnp.where(lane == ix, -jnp.inf, cur)
    es = [jnp.exp(v - vals[0]) for v in vals]
    tot = es[0] + es[1] + es[2] + es[3]
    idx_out = jnp.zeros(logits.shape, F32)
    gate_out = jnp.zeros(logits.shape, F32)
    for r in range(TOP_K):
        idx_out = jnp.where(lane == r, idxs[r], idx_out)
        gate_out = jnp.where(lane == r, es[r] / tot, gate_out)
    idx_ref[...] = idx_out.astype(jnp.int32)
    gate_ref[...] = gate_out
    hits = sum(jnp.sum((lane == ix).astype(F32), axis=0, keepdims=True) for ix in idxs)

    @pl.when(pl.program_id(0) == 0)
    def _():
        cnt_ref[...] = jnp.zeros(cnt_ref.shape, cnt_ref.dtype)

    cnt_ref[...] += jnp.broadcast_to(hits, cnt_ref.shape)


def _router(x, g, w, b, *, tm=512):
    n, d = x.shape
    e = w.shape[1]
    tm = min(tm, n)
    w_pad = jnp.zeros((d, LANES), F32).at[:, :e].set(w)
    whi = w_pad.astype(BF16)
    wlo = (w_pad - whi.astype(F32)).astype(BF16)
    b_pad = jnp.full((1, LANES), -jnp.inf, F32).at[0, :e].set(b.astype(F32))
    return pl.pallas_call(
        _router_kernel,
        out_shape=(jax.ShapeDtypeStruct((n * (d // LANES), LANES), F32), jax.ShapeDtypeStruct((n, LANES), jnp.int32),
                   jax.ShapeDtypeStruct((n, LANES), F32), jax.ShapeDtypeStruct((8, LANES), F32)),
        grid=(n // tm,),
        in_specs=[
            pl.BlockSpec((tm, d), lambda i: (i, 0)),
            pl.BlockSpec((1, d), lambda i: (0, 0)),
            pl.BlockSpec((d, LANES), lambda i: (0, 0)),
            pl.BlockSpec((d, LANES), lambda i: (0, 0)),
            pl.BlockSpec((1, LANES), lambda i: (0, 0)),
        ],
        out_specs=(pl.BlockSpec((tm * (d // LANES), LANES), lambda i: (i, 0)), pl.BlockSpec((tm, LANES), lambda i: (i, 0)),
                   pl.BlockSpec((tm, LANES), lambda i: (i, 0)), pl.BlockSpec((8, LANES), lambda i: (0, 0))),
        compiler_params=_cparams(("arbitrary",)),
        name="moe_router",
    )(x, g.reshape(1, d).astype(F32), whi, wlo, b_pad)


def _moe_expert_kernel(blk_exp_ref, tok_ref, dst_ref, live_ref, h_hbm, ga_ref, gb_ref, wgu_a, bgu_a, wd_a, bd_a,
                       wgu_b, bgu_b, wd_b, bd_b, y_hbm, xa, xb, oa, ob, gsem, ssem, *, bm, dff):
    i = pl.program_id(0)
    live = live_ref[0]
    nc = xa.shape[0] // bm
    dump = y_hbm.shape[0] // nc - 2 * bm

    def gather(block, buf, sem):
        base = block * bm
        for r in range(bm):
            pltpu.make_async_copy(h_hbm.at[pl.ds(pl.multiple_of(tok_ref[base + r], nc), nc)], buf.at[pl.ds(r * nc, nc)], sem).start()

    def scatter(block, buf, sem):
        base = block * bm
        for r in range(bm):
            pltpu.make_async_copy(buf.at[pl.ds(r * nc, nc)], y_hbm.at[pl.ds(pl.multiple_of(dst_ref[base + r], nc), nc)], sem).start()

    def wait_gather(buf, sem):
        pltpu.make_async_copy(h_hbm.at[pl.ds(0, bm * nc)], buf, sem).wait()

    def wait_scatter(buf, sem):
        pltpu.make_async_copy(buf, y_hbm.at[pl.ds(0, bm * nc)], sem).wait()

    def experts(x_ref, o_ref, g_ref, wgu, bgu, wd, bd):
        x = jnp.concatenate([x_ref[pl.ds(c, bm, stride=nc), :] for c in range(nc)], axis=1)
        gu = jnp.dot(x.astype(BF16), wgu[0, 0], preferred_element_type=F32) + bgu[0]
        a = jnp.minimum(gu[:, :dff], SWIGLU_LIMIT)
        u = jnp.clip(gu[:, dff:], -SWIGLU_LIMIT, SWIGLU_LIMIT)
        y = (u + 1.0) * (a * _sigmoid(SWIGLU_ALPHA * a))
        out = (jnp.dot(y.astype(BF16), wd[0, 0], preferred_element_type=F32) + bd[0]) * g_ref[...]
        for c in range(nc):
            o_ref[pl.ds(c, bm, stride=nc), :] = out[:, c * LANES:(c + 1) * LANES]

    @pl.when(i == 0)
    def _():
        gather(0, xa, gsem.at[0])
        for buf, sem, base in ((oa, ssem.at[0], dump), (ob, ssem.at[1], dump + bm)):
            buf[...] = jnp.zeros(buf.shape, buf.dtype)
            for r in range(bm):
                pltpu.make_async_copy(buf.at[pl.ds(r * nc, nc)], y_hbm.at[pl.ds((base + r) * nc, nc)], sem).start()

    @pl.when(i < live)
    def _():
        blk_a = 2 * i
        blk_b = 2 * i + 1
        nxt_a = jnp.minimum(2 * i + 2, 2 * live - 1)

        gather(blk_b, xb, gsem.at[1])
        wait_gather(xa, gsem.at[0])
        wait_scatter(oa, ssem.at[0])
        experts(xa, oa, ga_ref, wgu_a, bgu_a, wd_a, bd_a)
        scatter(blk_a, oa, ssem.at[0])

        gather(nxt_a, xa, gsem.at[0])
        wait_gather(xb, gsem.at[1])
        wait_scatter(ob, ssem.at[1])
        experts(xb, ob, gb_ref, wgu_b, bgu_b, wd_b, bd_b)
        scatter(blk_b, ob, ssem.at[1])

    @pl.when(i == live - 1)
    def _():
        wait_gather(xa, gsem.at[0])
        wait_scatter(oa, ssem.at[0])
        wait_scatter(ob, ssem.at[1])


def _moe_experts(h, blk_exp, tok, dst, live, gates, w_gu, b_gu, w_down, b_down, *, bm, layer):
    _, e, d, f2 = w_gu.shape
    nc = d // LANES
    n = h.shape[0] // nc
    dff = f2 // 2
    cap = tok.shape[0]
    n_steps = cap // (2 * bm)
    assert cap == n_steps * 2 * bm

    def wspecs(off):
        return [
            pl.BlockSpec((1, 1, d, f2), lambda i, be, tk, ds, lv: (layer, be[2 * i + off], 0, 0)),
            pl.BlockSpec((1, 1, f2), lambda i, be, tk, ds, lv: (be[2 * i + off], 0, 0)),
            pl.BlockSpec((1, 1, dff, d), lambda i, be, tk, ds, lv: (layer, be[2 * i + off], 0, 0)),
            pl.BlockSpec((1, 1, d), lambda i, be, tk, ds, lv: (be[2 * i + off], 0, 0)),
        ]

    grid_spec = pltpu.PrefetchScalarGridSpec(
        num_scalar_prefetch=4,
        grid=(n_steps,),
        in_specs=[
            pl.BlockSpec(memory_space=pl.ANY),
            pl.BlockSpec((bm, 1), lambda i, be, tk, ds, lv: (2 * i, 0)),
            pl.BlockSpec((bm, 1), lambda i, be, tk, ds, lv: (2 * i + 1, 0)),
        ] + wspecs(0) + wspecs(1),
        out_specs=pl.BlockSpec(memory_space=pl.ANY),
        scratch_shapes=[pltpu.VMEM((bm * nc, LANES), F32)] * 4 + [pltpu.SemaphoreType.DMA((2,)),
                                                                  pltpu.SemaphoreType.DMA((2,))],
    )
    bgu3 = b_gu.reshape(e, 1, f2).astype(F32)
    bd3 = b_down.reshape(e, 1, d).astype(F32)
    g2 = gates.reshape(cap, 1)
    return pl.pallas_call(
        functools.partial(_moe_expert_kernel, bm=bm, dff=dff),
        out_shape=jax.ShapeDtypeStruct(((TOP_K * n + 2 * bm) * nc, LANES), F32),
        grid_spec=grid_spec,
        compiler_params=_cparams(("arbitrary",)),
        name="moe_experts",
    )(blk_exp, tok, dst, live, h, g2, g2, w_gu, bgu3, w_down, bd3, w_gu, bgu3, w_down, bd3)


def _moe_layer(x, norm_g, router_w, router_b, w_gu, b_gu, w_down, b_down, layer):
    n, d = x.shape
    bm = MOE_BM
    h, idx128, gate128, cnt = _router(x, norm_g, router_w, router_b)
    n_assign = n * TOP_K
    flat_e = idx128[:, :TOP_K].reshape(-1)
    flat_g = gate128[:, :TOP_K].reshape(-1)
    order = jnp.argsort(flat_e).astype(jnp.int32)
    counts = cnt[0, :N_EXPERTS].astype(jnp.int32)
    padded = (counts + bm - 1) // bm * bm
    start = jnp.cumsum(counts) - counts
    pend = jnp.cumsum(padded)
    pstart = pend - padded
    n_blocks = -(-n_assign // bm) + N_EXPERTS
    n_blocks += n_blocks % 2
    blk_start = jnp.arange(n_blocks, dtype=jnp.int32) * bm
    blk_exp = jnp.minimum(jnp.sum((pend[None, :] <= blk_start[:, None]).astype(jnp.int32), axis=1),
                          N_EXPERTS - 1).astype(jnp.int32)
    row = jnp.arange(bm, dtype=jnp.int32)[None, :]
    off = (blk_start - pstart[blk_exp])[:, None] + row
    valid = off < counts[blk_exp][:, None]
    a = order[jnp.clip(start[blk_exp][:, None] + off, 0, n_assign - 1).reshape(-1)].reshape(n_blocks, bm)
    nc = d // LANES
    tok = (jnp.where(valid, a // TOP_K, 0) * nc).astype(jnp.int32).reshape(-1)
    dump_row = TOP_K * n + (jnp.arange(n_blocks, dtype=jnp.int32) % 2)[:, None] * bm + row
    dst = (jnp.where(valid, (a % TOP_K) * n + a // TOP_K, dump_row) * nc).astype(jnp.int32).reshape(-1)
    live = jnp.maximum((pend[-1] // bm + 1) // 2, 1).astype(jnp.int32).reshape(1)
    gates = jnp.where(valid, flat_g[a.reshape(-1)].reshape(n_blocks, bm), 0.0).reshape(-1)
    return _moe_experts(h, blk_exp, tok, dst, live, gates, w_gu, b_gu, w_down, b_down, bm=bm, layer=layer)


def _ple_kernel(x_ref, y0_ref, y1_ref, y2_ref, y3_ref, g_ref, wg_ref, p_ref, wp_ref, o_ref):
    tm, d = x_ref.shape
    nc = d // LANES

    def plane(y_ref):
        return jnp.concatenate([y_ref[pl.ds(c, tm, stride=nc), :] for c in range(nc)], axis=1)

    xf = x_ref[...] + (((plane(y0_ref) + plane(y1_ref)) + plane(y2_ref)) + plane(y3_ref))
    ms = jnp.mean(xf * xf, axis=-1, keepdims=True)
    h = (xf * lax.rsqrt(ms + EPS) * g_ref[...]).astype(BF16)
    gate = _sigmoid(jnp.dot(h, wg_ref[...], preferred_element_type=F32))
    pp = jnp.dot(p_ref[...].astype(BF16), wp_ref[...], preferred_element_type=F32)
    o_ref[...] = xf + gate * pp


def _ple(x, y, g, wg, p, wp, *, tm=512):
    n, d = x.shape
    pd = p.shape[1]
    tm = min(tm, n)
    nb = n // tm
    nc = d // LANES
    yspecs = [pl.BlockSpec((tm * nc, LANES), functools.partial(lambda i, k: (k * nb + i, 0), k=k)) for k in range(TOP_K)]
    return pl.pallas_call(
        _ple_kernel,
        out_shape=jax.ShapeDtypeStruct((n, d), F32),
        grid=(nb,),
        in_specs=[pl.BlockSpec((tm, d), lambda i: (i, 0))] + yspecs + [
            pl.BlockSpec((1, d), lambda i: (0, 0)),
            pl.BlockSpec((d, d), lambda i: (0, 0)),
            pl.BlockSpec((tm, pd), lambda i: (i, 0)),
            pl.BlockSpec((pd, d), lambda i: (0, 0)),
        ],
        out_specs=pl.BlockSpec((tm, d), lambda i: (i, 0)),
        compiler_params=_cparams(("parallel",)),
        name="ple",
    )(x, y, y, y, y, g.reshape(1, d).astype(F32), wg.astype(BF16), p, wp.astype(BF16))


def _cmp_kernel(z_ref, w1_ref, pos_ref, w2_ref, gain_ref, c_ref, s1_ref, s2_ref, o_ref, *, rows, half):
    j = pl.program_id(1)
    z = z_ref[0, 0, 0]
    pos = pos_ref[0]
    a1 = (z + pos[:, :half]).astype(BF16)
    a2 = (z + pos[:, half:]).astype(BF16)
    u1 = jnp.dot(a1, w1_ref[0, :half, :], preferred_element_type=F32)
    u2 = jnp.dot(a2, w1_ref[0, half:, :], preferred_element_type=F32)
    h = u1 + pltpu.roll(u2, rows - 1, 0)
    hs = h * _sigmoid(h)
    y = jnp.dot(hs.astype(BF16), w2_ref[0], preferred_element_type=F32)

    @pl.when(j == 0)
    def _():
        ss = jnp.sum(y * y, axis=-1, keepdims=True) * (1.0 / LANES)
        yn = y * lax.rsqrt(ss + EPS) * gain_ref[...]
        yn = (yn * c_ref[...] + pltpu.roll(yn, LANES - ROPE_DIM // 2, 1) * s1_ref[...]
              + pltpu.roll(yn, ROPE_DIM // 2, 1) * s2_ref[...])
        o_ref[0, 0, 0] = yn.astype(o_ref.dtype)

    @pl.when(j != 0)
    def _():
        o_ref[0, 0, 0] = y.astype(o_ref.dtype)


def _compress(zr, w1, pos, w2dup, gain_dup, tabs, *, batch):
    rows, width = zr.shape[3], zr.shape[4]
    hid = w1.shape[2]
    g = NSA_GROUPS
    return pl.pallas_call(
        functools.partial(_cmp_kernel, rows=rows, half=width),
        out_shape=jax.ShapeDtypeStruct((batch, 2, g, rows, LANES), BF16),
        grid=(batch, 2, g),
        in_specs=[
            pl.BlockSpec((1, 1, 1, rows, width), lambda b, j, gg: (b, j, gg, 0, 0)),
            pl.BlockSpec((1, 2 * width, hid), lambda b, j, gg: (j, 0, 0)),
            pl.BlockSpec((1, 1, 2 * width), lambda b, j, gg: (j, 0, 0)),
            pl.BlockSpec((1, hid, LANES), lambda b, j, gg: (j, 0, 0)),
            pl.BlockSpec((1, LANES), lambda b, j, gg: (0, 0)),
            pl.BlockSpec((rows, LANES), lambda b, j, gg: (0, 0)),
            pl.BlockSpec((rows, LANES), lambda b, j, gg: (0, 0)),
            pl.BlockSpec((rows, LANES), lambda b, j, gg: (0, 0)),
        ],
        out_specs=pl.BlockSpec((1, 1, 1, rows, LANES), lambda b, j, gg: (b, j, gg, 0, 0)),
        compiler_params=_cparams(("parallel", "parallel", "parallel")),
        name="nsa_compress",
    )(zr, w1, pos, w2dup, gain_dup, *tabs)


def _pipelined(n_items, first, second, skew):
    pending, out = {}, [None] * n_items
    for idx in range(n_items + skew):
        if idx < n_items:
            pending[idx] = first(idx)
        if idx >= skew:
            out[idx - skew] = second(idx - skew, pending.pop(idx - skew))
    return out


def _nsa_attn_kernel(q_ref, gz_ref, kc_ref, vct_ref, ovt_ref, ks_ref, vs_ref, kw_ref, vw_ref, o_ref,
                     e_ref, oc_ref, imp_ref, notsel_ref, *, n_cmp_rows, tk_sel, skew, grp, subs):
    g = pl.program_id(1)
    i = pl.program_id(2)
    qb = NSA_QB
    mh = HEADS_PER_GROUP
    rows = mh * qb
    q0 = i * qb
    seq = ks_ref.shape[0]

    @pl.when(i == 0)
    def _():
        r = lax.broadcasted_iota(jnp.int32, (seq, LANES), 0)
        lb = lax.broadcasted_iota(jnp.int32, (seq, LANES), 1)
        e_ref[...] = jnp.where((r >> 6) == lb, NEG, 0.0).astype(BF16)

    qh = [q_ref[:, m * LANES:(m + 1) * LANES] for m in range(mh)]
    q4 = jnp.concatenate(qh, axis=0)
    jl = lax.broadcasted_iota(jnp.int32, (qb, LANES), 1)

    tlane = q0 + (lax.broadcasted_iota(jnp.int32, (1, rows), 1) & (qb - 1))

    def cmp_branch(nrow):
        st = _dot_nt(kc_ref[0, 0, 0, :nrow, :], q4)
        cend = lax.broadcasted_iota(jnp.int32, (nrow, rows), 0) * CMP_STRIDE + (CMP_BLOCK - 1)
        valid = cend <= tlane
        st = jnp.where(valid, st, NEG)
        e = jnp.where(valid, jnp.exp2(st - jnp.max(st, axis=0, keepdims=True)), 0.0)
        pt = e / jnp.maximum(jnp.sum(e, axis=0, keepdims=True), 1e-30)
        oct = jnp.dot(vct_ref[0, 0, :, :nrow], pt.astype(BF16), preferred_element_type=F32)
        for m in range(mh):
            oc_ref[m * qb:(m + 1) * qb, :] = oct[:, m * qb:(m + 1) * qb].T
        psum = (pt[:, 0:qb] + pt[:, qb:2 * qb]) + (pt[:, 2 * qb:3 * qb] + pt[:, 3 * qb:4 * qb])
        phi, plo = _split2(psum)
        ovt = ovt_ref[:, :nrow]
        imp_ref[...] = (jnp.dot(ovt, phi, preferred_element_type=F32)
                        + jnp.dot(ovt, plo, preferred_element_type=F32))

    n_chunks = n_cmp_rows // LANES
    last_visible = (q0 + qb - CMP_BLOCK) // CMP_STRIDE
    need = jnp.clip(last_visible // LANES + 1, 1, n_chunks)
    for nch in range(1, n_chunks + 1):
        pl.when(need == nch)(functools.partial(cmp_branch, nch * LANES))

    def select_blocks(nrow):
        jb = lax.broadcasted_iota(jnp.int32, (nrow, qb), 0)
        tq = q0 + lax.broadcasted_iota(jnp.int32, (nrow, qb), 1)
        cur = tq >> 6
        forced = (jb == 0) | (jb == cur) | (jb == cur - 1)
        causal = (jb << 6) <= tq
        score = jnp.where(causal, jnp.where(forced, FORCE, imp_ref[:nrow, :]), -jnp.inf)
        notsel_t = jnp.ones((nrow, qb), F32)
        jbf = jb.astype(F32)
        for _ in range(SEL_TOPK):
            smx = jnp.max(score, axis=0, keepdims=True)
            ix = jnp.min(jnp.where(score == smx, jbf, float(LANES)), axis=0, keepdims=True)
            hit = jbf == ix
            notsel_t = jnp.where(hit, 0.0, notsel_t)
            score = jnp.where(hit, -jnp.inf, score)
        notsel_ref[:nrow, :] = notsel_t
        if nrow < LANES:
            notsel_ref[nrow:, :] = jnp.ones((LANES - nrow, qb), F32)

    sel_chunk = 32
    n_sel_chunks = LANES // sel_chunk
    sel_need = jnp.clip(((q0 + qb - 1) // SEL_BLOCK) // sel_chunk + 1, 1, n_sel_chunks)
    for nch in range(1, n_sel_chunks + 1):
        pl.when(sel_need == nch)(functools.partial(select_blocks, nch * sel_chunk))
    notsel_b = notsel_ref[...].T.astype(BF16)
    n_grp = mh // grp
    q_aug = [jnp.concatenate([jnp.concatenate([qh[m], notsel_b], axis=1) for m in range(c * grp, (c + 1) * grp)], axis=0)
             for c in range(n_grp)]
    tpos = q0 + lax.broadcasted_iota(jnp.int32, (qb, 1), 0)
    tpos_g = jnp.concatenate([tpos] * grp, axis=0)

    def sel_tiles(first, n_sub):
        ks, vs = [], []
        for t in range(n_sub):
            k0 = pl.multiple_of((first + t) * tk_sel, tk_sel)
            k_aug = jnp.concatenate([ks_ref[pl.ds(k0, tk_sel), :], e_ref[pl.ds(k0, tk_sel), :]], axis=1)
            ks.append([k_aug] * n_grp)
            vs.append([vs_ref[pl.ds(k0, tk_sel), :]] * n_grp)
        return ks, vs

    def sel_step(first, n_sub, state):
        ks, vs = sel_tiles(first, n_sub)
        return _attend(q_aug, ks, vs, state, skew)

    state = tuple((jnp.full((grp * qb, 1), NEG, F32), jnp.zeros((grp * qb, LANES), F32)) for _ in range(n_grp))
    n_below = q0 // tk_sel
    state = _tile_loops(n_below, subs, sel_step, state)

    def causal(t, c, sc):
        kpos = n_below * tk_sel + lax.broadcasted_iota(jnp.int32, (grp * qb, tk_sel), 1)
        return jnp.where(kpos <= tpos_g, sc, NEG)

    ks, vs = sel_tiles(n_below, 1)
    state = _attend(q_aug, ks, vs, state, skew, mask=causal)
    sel_acc = [state[m // grp][1][(m % grp) * qb:(m % grp + 1) * qb] for m in range(mh)]

    wk = WINDOW + qb
    k0w = pl.multiple_of(jnp.maximum(i - WINDOW // qb, 0) * qb, qb)
    kwt = kw_ref[pl.ds(k0w, wk), :]
    vwt = vw_ref[pl.ds(k0w, wk), :]
    kposw = k0w + lax.broadcasted_iota(jnp.int32, (grp * qb, wk), 1)
    wmask = (kposw <= tpos_g) & (kposw > tpos_g - WINDOW)
    q_grp = [jnp.concatenate(qh[c * grp:(c + 1) * grp], axis=0) for c in range(n_grp)]

    def win_scores(c):
        return jnp.where(wmask, _dot_nt(q_grp[c], kwt), NEG)

    def win_out(c, sc):
        pe = jnp.exp2(sc - jnp.max(sc, axis=-1, keepdims=True))
        return jnp.dot(pe.astype(BF16), vwt, preferred_element_type=F32)

    win_grp = _pipelined(n_grp, win_scores, win_out, min(skew, n_grp - 1))
    win_acc = [win_grp[m // grp][(m % grp) * qb:(m % grp + 1) * qb] for m in range(mh)]

    sig = _sigmoid(gz_ref[...])
    den_lane = jl == HEAD_DIM
    heads = []
    for m in range(mh):
        col = g * mh + m
        gates = [jnp.sum(jnp.where(jl == br * N_HEADS + col, sig, 0.0), axis=-1, keepdims=True)
                 for br in range(3)]
        acc_s = sel_acc[m]
        acc_w = win_acc[m]
        den_s = jnp.sum(jnp.where(den_lane, acc_s, 0.0), axis=-1, keepdims=True)
        den_w = jnp.sum(jnp.where(den_lane, acc_w, 0.0), axis=-1, keepdims=True)
        heads.append(gates[0] * oc_ref[m * qb:(m + 1) * qb, :] + (gates[1] / den_s) * acc_s
                     + (gates[2] / den_w) * acc_w)
    lane_lo = jl < HEAD_DIM
    o_ref[:, 0:LANES] = jnp.where(lane_lo, heads[0], pltpu.roll(heads[1], HEAD_DIM, 1)).astype(o_ref.dtype)
    o_ref[:, LANES:2 * LANES] = jnp.where(lane_lo, heads[2], pltpu.roll(heads[3], HEAD_DIM, 1)).astype(o_ref.dtype)


def _nsa_attn(qn, gz, kvc, overlap, kvd, *, batch, seq, skew=2, grp=2, subs=(4, 2, 1)):
    n = qn.shape[0]
    qb = NSA_QB
    nq = seq // qb
    g = NSA_GROUPS
    rows = HEADS_PER_GROUP * qb
    n_cmp_rows = kvc.shape[3]
    tk_sel = min(512, seq)
    assert seq >= WINDOW + qb and seq % tk_sel == 0 and tk_sel % qb == 0
    in_specs = [
        pl.BlockSpec((qb, HEADS_PER_GROUP * LANES), lambda b, gg, i: (b * nq + i, gg)),
        pl.BlockSpec((qb, LANES), lambda b, gg, i: (b * nq + i, 0)),
        pl.BlockSpec((1, 1, 1, n_cmp_rows, LANES), lambda b, gg, i: (b, 0, gg, 0, 0)),
        pl.BlockSpec((1, 1, LANES, n_cmp_rows), lambda b, gg, i: (b, gg, 0, 0)),
        pl.BlockSpec((LANES, n_cmp_rows), lambda b, gg, i: (0, 0)),
        pl.BlockSpec((seq, LANES), lambda b, gg, i: (b, 0 * g + gg)),
        pl.BlockSpec((seq, LANES), lambda b, gg, i: (b, 1 * g + gg)),
        pl.BlockSpec((seq, LANES), lambda b, gg, i: (b, 2 * g + gg)),
        pl.BlockSpec((seq, LANES), lambda b, gg, i: (b, 3 * g + gg)),
    ]
    return pl.pallas_call(
        functools.partial(_nsa_attn_kernel, n_cmp_rows=n_cmp_rows, tk_sel=tk_sel, skew=skew, grp=grp, subs=subs),
        out_shape=jax.ShapeDtypeStruct((n, N_HEADS * HEAD_DIM), BF16),
        grid=(batch, g, nq),
        in_specs=in_specs,
        out_specs=pl.BlockSpec((qb, HEADS_PER_GROUP * HEAD_DIM), lambda b, gg, i: (b * nq + i, gg)),
        scratch_shapes=[pltpu.VMEM((seq, LANES), BF16), pltpu.VMEM((rows, LANES), F32), pltpu.VMEM((LANES, qb), F32),
                        pltpu.VMEM((LANES, qb), F32)],
        compiler_params=_cparams(("parallel", "parallel", "arbitrary")),
        name="nsa_attn",
    )(qn, gz, kvc, jnp.swapaxes(kvc[:, 1], -1, -2), overlap.T, kvd, kvd, kvd, kvd)


def _pad_cols(w, width=LANES):
    d, f = w.shape
    return jnp.zeros((d, width), w.dtype).at[:, :f].set(w)


def _tile_gain(gain, n_heads, scale=1.0, dup=False):
    g = gain.astype(F32) * scale
    second = g if dup else jnp.zeros_like(g)
    return jnp.tile(jnp.concatenate([g, second]), n_heads)


def _fox_aug_tables():
    hw = N_HEADS * LANES
    piece = jnp.arange(3)[:, None]
    head = jnp.arange(N_HEADS)[None, :]
    src = (piece * N_HEADS + head).reshape(-1)
    q_dst = (head * LANES + HEAD_DIM + piece).reshape(-1)
    k_dst = (hw + head * LANES + HEAD_DIM + 3 + piece).reshape(-1)
    place = jnp.zeros((LANES, 3 * hw), F32).at[src, q_dst].set(1.0).at[src, k_dst].set(-1.0)
    lane = jnp.arange(LANES)
    q_bias = ((lane >= HEAD_DIM + 3) & (lane < HEAD_DIM + 6)).astype(F32)
    k_bias = ((lane >= HEAD_DIM) & (lane < HEAD_DIM + 3)).astype(F32)
    v_bias = (lane == HEAD_DIM).astype(F32)
    bias = jnp.concatenate([jnp.tile(q_bias, N_HEADS), jnp.tile(k_bias, N_HEADS), jnp.tile(v_bias, N_HEADS)])
    return place.astype(BF16), bias


def _fox_layer(x, batch, seq, attn_g, w_in, b_f, qk_gain, w_out):
    hd = N_HEADS * HEAD_DIM
    hw = N_HEADS * LANES
    scale = HEAD_DIM ** -0.5
    z = _rms_proj(x, attn_g, _pad_cols(w_in[:, 3 * hd:]).astype(BF16), out_dtype=F32, seq=seq)
    b_pad = jnp.zeros((1, LANES), F32).at[0, :N_HEADS].set(b_f.astype(F32))
    c3 = _fox_c(z, b_pad, batch=batch, seq=seq)
    w_bf = w_in.astype(BF16)
    gain = jnp.concatenate([_tile_gain(qk_gain[0], N_HEADS, scale * LOG2E), _tile_gain(qk_gain[1], N_HEADS),
                            jnp.zeros((hw,), F32)])
    tn = 1024
    place, bias = _fox_aug_tables()
    qkv = _rms_proj(x, attn_g, w_bf, out_dtype=BF16, seq=seq, tn=tn, gain=gain, w_cols=3 * hd,
                    norm_blocks=tuple(range(2 * hw // tn)), placed=(c3, place), col_bias=bias, expand=True)
    o = _fox_attn(qkv, batch=batch, seq=seq)
    return _matmul_res(o, w_out.astype(BF16), x)


def _nsa_shared_kv(x, batch, seq, kv_norm, kv_w, kv_k_gain, cmp_pos, cmp_w1, cmp_w2):
    g = NSA_GROUPS
    gw = g * HEAD_DIM
    tabs = _rope_tables(jnp.arange(seq))
    kv_bf = kv_w.astype(BF16)
    zero = jnp.zeros((g * LANES,), F32)
    one_lane = jnp.tile((jnp.arange(LANES) == HEAD_DIM).astype(F32), g)
    gain_kvd = jnp.concatenate([_tile_gain(kv_k_gain[1], g), zero, _tile_gain(kv_k_gain[2], g), zero])
    bias_kvd = jnp.concatenate([zero, one_lane, zero, one_lane])
    kvd = _rms_proj(x, kv_norm, kv_bf, out_dtype=BF16, seq=seq, tn=g * LANES, gain=gain_kvd, w_col0=2 * gw,
                    norm_blocks=(0, 2), rope_tabs=tabs, col_bias=bias_kvd, expand=True)
    zc = _rms_proj(x, kv_norm, kv_bf, out_dtype=F32, seq=seq, w_cols=2 * gw)
    rows = seq // CMP_STRIDE
    zr = zc.reshape(batch, rows, CMP_STRIDE, 2, g, HEAD_DIM).transpose(0, 3, 4, 1, 2, 5)
    zr = zr.reshape(batch, 2, g, rows, CMP_STRIDE * HEAD_DIM)
    cmp_end = jnp.arange(rows) * CMP_STRIDE + CMP_BLOCK - 1
    ctabs = _rope_tables(cmp_end)
    w2dup = jnp.concatenate([cmp_w2, cmp_w2], axis=-1).astype(BF16)
    kvc = _compress(zr, cmp_w1.astype(BF16), cmp_pos.reshape(2, 1, CMP_BLOCK * HEAD_DIM).astype(F32), w2dup,
                    _tile_gain(kv_k_gain[0], 1, dup=True).reshape(1, LANES), ctabs, batch=batch)
    return kvc, kvd


def _nsa_layer(x, batch, seq, attn_g, w_in, q_gain, w_out, kvc, kvd):
    hd = N_HEADS * HEAD_DIM
    scale = HEAD_DIM ** -0.5
    rows = seq // CMP_STRIDE
    tabs = _rope_tables(jnp.arange(seq))
    qn = _rms_proj(x, attn_g, w_in.astype(BF16), out_dtype=BF16, seq=seq, expand=True, w_cols=hd,
                   gain=_tile_gain(q_gain, N_HEADS, scale * LOG2E), norm_blocks=tuple(range(N_HEADS * LANES // 512)),
                   rope_tabs=tabs)
    gz = _rms_proj(x, attn_g, _pad_cols(w_in[:, hd:]).astype(BF16), out_dtype=F32, seq=seq)

    n_sel = seq // SEL_BLOCK
    assert n_sel <= LANES
    cmp_start = jnp.arange(rows) * CMP_STRIDE
    sel_start = jnp.arange(LANES) * SEL_BLOCK
    overlap = jnp.clip(jnp.minimum(cmp_start[:, None] + CMP_BLOCK, sel_start[None, :] + SEL_BLOCK)
                       - jnp.maximum(cmp_start[:, None], sel_start[None, :]), 0)
    overlap = jnp.where((jnp.arange(rows) < rows - 1)[:, None] & (jnp.arange(LANES) < n_sel)[None, :], overlap, 0)
    o = _nsa_attn(qn, gz, kvc, overlap.astype(BF16), kvd, batch=batch, seq=seq)
    return _matmul_res(o, w_out.astype(BF16), x)


def kernel(x, p, attn_norm, ffn_norm, ple_norm, ple_gate_w, ple_proj_w, router_w, router_b, w_gu, b_gu, w_down, b_down, fox_w_in, fox_b_f, fox_qk_gain, fox_w_out, kv_norm, kv_w, kv_k_gain, cmp_pos, cmp_w1, cmp_w2, nsa_w_in, nsa_q_gain, nsa_w_out):
    batch, seq, d = x.shape
    depth = p.shape[0]
    n_a = fox_w_in.shape[0]
    xt = x.reshape(batch * seq, d)
    w_gu_bf = w_gu.astype(BF16)
    w_down_bf = w_down.astype(BF16)
    shared = None
    for layer in range(depth):
        if layer == n_a:
            shared = _nsa_shared_kv(xt, batch, seq, kv_norm, kv_w, kv_k_gain, cmp_pos, cmp_w1, cmp_w2)
        if layer < n_a:
            xt = _fox_layer(xt, batch, seq, attn_norm[layer], fox_w_in[layer], fox_b_f[layer],
                            fox_qk_gain[layer], fox_w_out[layer])
        else:
            i = layer - n_a
            xt = _nsa_layer(xt, batch, seq, attn_norm[layer], nsa_w_in[i], nsa_q_gain[i], nsa_w_out[i], *shared)
        y = _moe_layer(xt, ffn_norm[layer], router_w[layer], router_b[layer], w_gu_bf, b_gu[layer],
                       w_down_bf, b_down[layer], layer)
        xt = _ple(xt, y, ple_norm[layer], ple_gate_w[layer], p[layer].reshape(batch * seq, -1), ple_proj_w[layer])
    return xt.reshape(batch, seq, d)
```

```python
import functools

import jax
import jax.numpy as jnp
from jax import lax
from jax.experimental import pallas as pl
from jax.experimental.pallas import tpu as pltpu

F32 = jnp.float32
BF16 = jnp.bfloat16

N_HEADS = 16
HEAD_DIM = 64
ROPE_DIM = HEAD_DIM // 4
ROPE_THETA = 500000.0
NSA_GROUPS = 4
HEADS_PER_GROUP = N_HEADS // NSA_GROUPS
CMP_BLOCK = 32
CMP_STRIDE = 16
SEL_BLOCK = 64
SEL_TOPK = 16
WINDOW = 512
N_EXPERTS = 32
TOP_K = 4
SWIGLU_LIMIT = 7.0
SWIGLU_ALPHA = 1.702
EPS = 1e-6
NEG = -1e30
FORCE = 1e6
LOG2E = 1.4426950408889634

LANES = 128
NSA_QB = 256
MOE_BM = 256
VMEM_LIMIT = 56 * 1024 * 1024


def _cparams(sem):
    return pltpu.CompilerParams(dimension_semantics=sem, vmem_limit_bytes=VMEM_LIMIT)


def _sigmoid(x):
    return 1.0 / (1.0 + jnp.exp(-x))


def _dot_nt(a, b):
    return lax.dot_general(a, b, (((1,), (1,)), ((), ())), preferred_element_type=F32)


def _split2(x):
    hi = x.astype(BF16)
    lo = (x - hi.astype(F32)).astype(BF16)
    return hi, lo


def _proj_kernel(*refs, norm_blocks, rope, placed, biased, expand, head_div, tn, out_dtype):
    refs = list(refs)
    x_ref, g_ref, w_ref = refs[:3]
    del refs[:3]
    if norm_blocks:
        gain_ref = refs.pop(0)
    if rope:
        c_ref, s1_ref, s2_ref = refs[:3]
        del refs[:3]
    if placed:
        ex_ref, place_ref = refs[:2]
        del refs[:2]
    if biased:
        bias_ref = refs.pop(0)
    o_ref, h_ref = refs
    j = pl.program_id(1)
    aug = placed or biased

    @pl.when(j == 0)
    def _():
        xf = x_ref[...]
        ms = jnp.mean(xf * xf, axis=-1, keepdims=True)
        h_ref[...] = (xf * lax.rsqrt(ms + EPS) * g_ref[...]).astype(BF16)

    acc = jnp.dot(h_ref[...], w_ref[...], preferred_element_type=F32)
    if placed and biased:
        extra = jnp.dot(ex_ref[...], place_ref[...], preferred_element_type=F32) + bias_ref[...]
    elif biased:
        extra = jnp.broadcast_to(bias_ref[...], (acc.shape[0], tn))
    if expand:
        lo = lax.broadcasted_iota(jnp.int32, (acc.shape[0], LANES), 1) < HEAD_DIM
        blocks = []
        for c in range(acc.shape[1] // LANES):
            a = acc[:, c * LANES:(c + 1) * LANES]
            blocks += [jnp.where(lo, a, 0.0), jnp.where(lo, pltpu.roll(a, HEAD_DIM, 1), 0.0)]
    else:
        blocks = [acc[:, c * LANES:(c + 1) * LANES] for c in range(acc.shape[1] // LANES)]

    def plain():
        for c, a in enumerate(blocks):
            sl = slice(c * LANES, (c + 1) * LANES)
            o_ref[:, sl] = ((a + extra[:, sl]) if aug else a).astype(out_dtype)

    if not norm_blocks:
        plain()
        return
    is_norm = functools.reduce(jnp.logical_or, [j == c for c in norm_blocks])

    @pl.when(is_norm)
    def _():
        for c, a in enumerate(blocks):
            sl = slice(c * LANES, (c + 1) * LANES)
            ss = jnp.sum(a * a, axis=-1, keepdims=True) * (1.0 / head_div)
            y = a * lax.rsqrt(ss + EPS) * gain_ref[:, sl]
            if rope:
                y = (y * c_ref[...] + pltpu.roll(y, LANES - ROPE_DIM // 2, 1) * s1_ref[...]
                     + pltpu.roll(y, ROPE_DIM // 2, 1) * s2_ref[...])
            if aug:
                y = y + extra[:, sl]
            o_ref[:, sl] = y.astype(out_dtype)

    @pl.when(jnp.logical_not(is_norm))
    def _():
        plain()


def _rms_proj(x, g, w, *, out_dtype, seq, tm=2048, tn=512, gain=None, norm_blocks=(),
              rope_tabs=None, placed=None, col_bias=None, expand=False, head_div=HEAD_DIM,
              w_col0=0, w_cols=None):
    n, d = x.shape
    w_cols = w.shape[1] - w_col0 if w_cols is None else w_cols
    f = w_cols * (2 if expand else 1)
    tn = min(tn, f)
    tw = tn // 2 if expand else tn
    assert w_col0 % tw == 0
    j0 = w_col0 // tw
    tm = min(tm, seq)
    assert n % tm == 0 and f % tn == 0 and seq % tm == 0
    nt = seq // tm
    in_specs = [
        pl.BlockSpec((tm, d), lambda i, j: (i, 0)),
        pl.BlockSpec((1, d), lambda i, j: (0, 0)),
        pl.BlockSpec((d, tw), lambda i, j: (0, j0 + j)),
    ]
    args = [x, g.reshape(1, d).astype(F32), w]
    if norm_blocks:
        in_specs.append(pl.BlockSpec((1, tn), lambda i, j: (0, j)))
        args.append(gain.reshape(1, f).astype(F32))
    if rope_tabs is not None:
        assert norm_blocks
        for tab in rope_tabs:
            in_specs.append(pl.BlockSpec((tm, LANES), lambda i, j: (i % nt, 0)))
            args.append(tab)
    if placed is not None:
        assert col_bias is not None
        values, place = placed
        in_specs += [pl.BlockSpec((tm, LANES), lambda i, j: (i, 0)),
                     pl.BlockSpec((LANES, tn), lambda i, j: (0, j))]
        args += [values, place]
    if col_bias is not None:
        in_specs.append(pl.BlockSpec((1, tn), lambda i, j: (0, j)))
        args.append(col_bias.reshape(1, f).astype(F32))
    kern = functools.partial(_proj_kernel, norm_blocks=tuple(norm_blocks), rope=rope_tabs is not None,
                             placed=placed is not None, biased=col_bias is not None, expand=expand,
                             head_div=float(head_div), tn=tn, out_dtype=out_dtype)
    return pl.pallas_call(
        kern,
        out_shape=jax.ShapeDtypeStruct((n, f), out_dtype),
        grid=(n // tm, f // tn),
        in_specs=in_specs,
        out_specs=pl.BlockSpec((tm, tn), lambda i, j: (i, j)),
        scratch_shapes=[pltpu.VMEM((tm, d), BF16)],
        compiler_params=_cparams(("parallel", "arbitrary")),
        name="rms_proj",
    )(*args)


def _rope_tables(pos, width=LANES):
    half = ROPE_DIM // 2
    inv = jnp.power(jnp.float32(ROPE_THETA), -jnp.arange(0, ROPE_DIM, 2, dtype=F32) / ROPE_DIM)
    ang = pos.astype(F32)[:, None] * inv[None, :]
    cos, sin = jnp.cos(ang), jnp.sin(ang)
    t = pos.shape[0]
    ones = jnp.ones((t, HEAD_DIM - ROPE_DIM), F32)
    zeros_h = jnp.zeros((t, half), F32)
    zeros_r = jnp.zeros((t, HEAD_DIM - ROPE_DIM), F32)
    c = jnp.concatenate([cos, cos, ones], axis=1)
    s1 = jnp.concatenate([-sin, zeros_h, zeros_r], axis=1)
    s2 = jnp.concatenate([zeros_h, sin, zeros_r], axis=1)
    rep = width // HEAD_DIM
    return tuple(jnp.tile(a, (1, rep)) for a in (c, s1, s2))


def _matmul_res_kernel(a_ref, w_ref, x_ref, o_ref):
    o_ref[...] = x_ref[...] + jnp.dot(a_ref[...], w_ref[...], preferred_element_type=F32)


def _matmul_res(a, w, x, *, tm=512):
    n, k = a.shape
    f = w.shape[1]
    tm = min(tm, n)
    return pl.pallas_call(
        _matmul_res_kernel,
        out_shape=jax.ShapeDtypeStruct((n, f), F32),
        grid=(n // tm,),
        in_specs=[
            pl.BlockSpec((tm, k), lambda i: (i, 0)),
            pl.BlockSpec((k, f), lambda i: (0, 0)),
            pl.BlockSpec((tm, f), lambda i: (i, 0)),
        ],
        out_specs=pl.BlockSpec((tm, f), lambda i: (i, 0)),
        compiler_params=_cparams(("parallel",)),
        name="matmul_res",
    )(a, w, x)


def _fox_c_kernel(z_ref, b_ref, c_ref, carry_ref, *, tm):
    t = pl.program_id(1)

    @pl.when(t == 0)
    def _():
        carry_ref[...] = jnp.zeros_like(carry_ref)

    u = z_ref[...] + b_ref[...]
    logf = jnp.minimum(u, 0.0) - jnp.log(1.0 + jnp.exp(-jnp.abs(u)))
    row = lax.broadcasted_iota(jnp.int32, (tm, tm), 0)
    col = lax.broadcasted_iota(jnp.int32, (tm, tm), 1)
    tri = (row >= col).astype(BF16)
    hi = logf.astype(BF16)
    r1 = logf - hi.astype(F32)
    mid = r1.astype(BF16)
    lo = (r1 - mid.astype(F32)).astype(BF16)
    cs = (jnp.dot(tri, hi, preferred_element_type=F32) + jnp.dot(tri, mid, preferred_element_type=F32)
          + jnp.dot(tri, lo, preferred_element_type=F32))
    c = cs + carry_ref[...]
    carry_ref[...] = c[tm - 1:tm, :]
    c2 = c * LOG2E
    p0 = c2.astype(BF16).astype(F32)
    p1 = (c2 - p0).astype(BF16).astype(F32)
    p2 = ((c2 - p0) - p1).astype(BF16).astype(F32)
    lane = lax.broadcasted_iota(jnp.int32, c.shape, 1)
    out = jnp.where(lane < N_HEADS, p0,
                    jnp.where(lane < 2 * N_HEADS, pltpu.roll(p1, N_HEADS, 1),
                              jnp.where(lane < 3 * N_HEADS, pltpu.roll(p2, 2 * N_HEADS, 1), 0.0)))
    c_ref[...] = out.astype(BF16)


def _fox_c(z, b_pad, *, batch, seq, tm=512):
    tm = min(tm, seq)
    nt = seq // tm
    return pl.pallas_call(
        functools.partial(_fox_c_kernel, tm=tm),
        out_shape=jax.ShapeDtypeStruct(z.shape, BF16),
        grid=(batch, nt),
        in_specs=[
            pl.BlockSpec((tm, LANES), lambda b, t: (b * nt + t, 0)),
            pl.BlockSpec((1, LANES), lambda b, t: (0, 0)),
        ],
        out_specs=pl.BlockSpec((tm, LANES), lambda b, t: (b * nt + t, 0)),
        scratch_shapes=[pltpu.VMEM((1, LANES), F32)],
        compiler_params=_cparams(("parallel", "arbitrary")),
        name="fox_cumsum",
    )(z, b_pad)


def _attend(qs, ks, vs, state, skew, mask=None):
    items = [(t, c) for t in range(len(ks)) for c in range(len(qs))]
    state = list(state)
    pending = {}
    depth = min(skew, len(items) - 1)
    for idx in range(len(items) + depth):
        if idx < len(items):
            t, c = items[idx]
            sc = _dot_nt(qs[c], ks[t][c])
            pending[idx] = sc if mask is None else mask(t, c, sc)
        if idx >= depth:
            t, c = items[idx - depth]
            sc = pending.pop(idx - depth)
            m_prev, acc = state[c]
            m_new = jnp.maximum(m_prev, jnp.max(sc, axis=-1, keepdims=True))
            pe = jnp.exp2(sc - m_new)
            acc = jnp.exp2(m_prev - m_new) * acc + jnp.dot(pe.astype(BF16), vs[t][c], preferred_element_type=F32)
            state[c] = (m_new, acc)
    return tuple(state)


def _tile_loops(n_tiles, sub, step, tail, state):
    n_full = n_tiles // sub
    state = lax.fori_loop(0, n_full, lambda j, st: step(j * sub, sub, st), state)
    first = n_full * sub
    return lax.switch(n_tiles - first, [functools.partial(tail, first, r) for r in range(sub)], state)


def _fox_attn_kernel(q0_ref, q1_ref, k0_ref, k1_ref, v0_ref, v1_ref, o_ref, *, tq, tk, skew, sub):
    qi = pl.program_id(2)
    k_refs = (k0_ref, k1_ref)
    v_refs = (v0_ref, v1_ref)
    qs = [q0_ref[...], q1_ref[...]]

    def tiles(refs, first, n_sub):
        return [[r[pl.ds(pl.multiple_of((first + t) * tk, tk), tk), :] for r in refs] for t in range(n_sub)]

    def full_step(first, n_sub, state):
        return _attend(qs, tiles(k_refs, first, n_sub), tiles(v_refs, first, n_sub), state, skew)

    def causal(t, c, sc):
        tpos = lax.broadcasted_iota(jnp.int32, (tq, tk), 0)
        kpos = t * tk + lax.broadcasted_iota(jnp.int32, (tq, tk), 1)
        return jnp.where(kpos <= tpos, sc, NEG)

    n_diag = tq // tk

    def tail_step(first, n_rem, state):
        def mask(t, c, sc):
            return causal(t - n_rem, c, sc) if t >= n_rem else sc

        n = n_rem + n_diag
        return _attend(qs, tiles(k_refs, first, n), tiles(v_refs, first, n), state, skew, mask=mask)

    state = tuple((jnp.full((tq, 1), NEG, F32), jnp.zeros((tq, LANES), F32)) for _ in qs)
    n_below = (qi * tq) // tk
    state = _tile_loops(n_below, sub, full_step, tail_step, state)

    lane = lax.broadcasted_iota(jnp.int32, (tq, LANES), 1)
    outs = []
    for _, acc in state:
        den = jnp.sum(jnp.where(lane == HEAD_DIM, acc, 0.0), axis=-1, keepdims=True)
        outs.append(acc / den)
    o_ref[...] = jnp.where(lane < HEAD_DIM, outs[0], pltpu.roll(outs[1], HEAD_DIM, 1)).astype(o_ref.dtype)


def _fox_attn(qkv, *, batch, seq, tq=512, tk=512, skew=2, sub=4):
    n = qkv.shape[0]
    tq = min(tq, seq)
    tk = min(tk, tq)
    assert seq % tq == 0 and tq % tk == 0
    nq = seq // tq
    hp = N_HEADS // 2
    kb = N_HEADS
    vb = 2 * N_HEADS
    in_specs = [
        pl.BlockSpec((tq, LANES), lambda b, h, i: (b * nq + i, 2 * h)),
        pl.BlockSpec((tq, LANES), lambda b, h, i: (b * nq + i, 2 * h + 1)),
        pl.BlockSpec((seq, LANES), lambda b, h, i: (b, kb + 2 * h)),
        pl.BlockSpec((seq, LANES), lambda b, h, i: (b, kb + 2 * h + 1)),
        pl.BlockSpec((seq, LANES), lambda b, h, i: (b, vb + 2 * h)),
        pl.BlockSpec((seq, LANES), lambda b, h, i: (b, vb + 2 * h + 1)),
    ]
    return pl.pallas_call(
        functools.partial(_fox_attn_kernel, tq=tq, tk=tk, skew=skew, sub=sub),
        out_shape=jax.ShapeDtypeStruct((n, N_HEADS * HEAD_DIM), BF16),
        grid=(batch, hp, nq),
        in_specs=in_specs,
        out_specs=pl.BlockSpec((tq, LANES), lambda b, h, i: (b * nq + i, h)),
        compiler_params=_cparams(("parallel", "parallel", "arbitrary")),
        name="fox_attn",
    )(qkv, qkv, qkv, qkv, qkv, qkv)


def _router_kernel(x_ref, g_ref, whi_ref, wlo_ref, b_ref, h_ref, idx_ref, gate_ref, cnt_ref):
    xf = x_ref[...]
    ms = jnp.mean(xf * xf, axis=-1, keepdims=True)
    h = xf * lax.rsqrt(ms + EPS) * g_ref[...]
    nc = h.shape[1] // LANES
    for c in range(nc):
        h_ref[pl.ds(c, h.shape[0], stride=nc), :] = h[:, c * LANES:(c + 1) * LANES]
    hhi, hlo = _split2(h)
    whi = whi_ref[...]
    logits = (jnp.dot(hhi, whi, preferred_element_type=F32) + jnp.dot(hlo, whi, preferred_element_type=F32)
              + jnp.dot(hhi, wlo_ref[...], preferred_element_type=F32)) + b_ref[...]
    lane = lax.broadcasted_iota(jnp.int32, logits.shape, 1).astype(F32)
    vals, idxs = [], []
    cur = logits
    for _ in range(TOP_K):
        mx = jnp.max(cur, axis=-1, keepdims=True)
        ix = jnp.min(jnp.where(cur == mx, lane, float(LANES)), axis=-1, keepdims=True)
        vals.append(mx)
        idxs.append(ix)
        cur = jnp.where(lane == ix, -jnp.inf, cur)
    es = [jnp.exp(v - vals[0]) for v in vals]
    tot = es[0] + es[1] + es[2] + es[3]
    idx_out = jnp.zeros(logits.shape, F32)
    gate_out = jnp.zeros(logits.shape, F32)
    for r in range(TOP_K):
        idx_out = jnp.where(lane == r, idxs[r], idx_out)
        gate_out = jnp.where(lane == r, es[r] / tot, gate_out)
    idx_ref[...] = idx_out.astype(jnp.int32)
    gate_ref[...] = gate_out
    hits = sum(jnp.sum((lane == ix).astype(F32), axis=0, keepdims=True) for ix in idxs)

    @pl.when(pl.program_id(0) == 0)
    def _():
        cnt_ref[...] = jnp.zeros(cnt_ref.shape, cnt_ref.dtype)

    cnt_ref[...] += jnp.broadcast_to(hits, cnt_ref.shape)


def _router(x, g, w, b, *, tm=512):
    n, d = x.shape
    e = w.shape[1]
    tm = min(tm, n)
    w_pad = jnp.zeros((d, LANES), F32).at[:, :e].set(w)
    whi = w_pad.astype(BF16)
    wlo = (w_pad - whi.astype(F32)).astype(BF16)
    b_pad = jnp.full((1, LANES), -jnp.inf, F32).at[0, :e].set(b.astype(F32))
    return pl.pallas_call(
        _router_kernel,
        out_shape=(jax.ShapeDtypeStruct((n * (d // LANES), LANES), F32), jax.ShapeDtypeStruct((n, LANES), jnp.int32),
                   jax.ShapeDtypeStruct((n, LANES), F32), jax.ShapeDtypeStruct((8, LANES), F32)),
        grid=(n // tm,),
        in_specs=[
            pl.BlockSpec((tm, d), lambda i: (i, 0)),
            pl.BlockSpec((1, d), lambda i: (0, 0)),
            pl.BlockSpec((d, LANES), lambda i: (0, 0)),
            pl.BlockSpec((d, LANES), lambda i: (0, 0)),
            pl.BlockSpec((1, LANES), lambda i: (0, 0)),
        ],
        out_specs=(pl.BlockSpec((tm * (d // LANES), LANES), lambda i: (i, 0)), pl.BlockSpec((tm, LANES), lambda i: (i, 0)),
                   pl.BlockSpec((tm, LANES), lambda i: (i, 0)), pl.BlockSpec((8, LANES), lambda i: (0, 0))),
        compiler_params=_cparams(("arbitrary",)),
        name="moe_router",
    )(x, g.reshape(1, d).astype(F32), whi, wlo, b_pad)


def _moe_expert_kernel(blk_exp_ref, tok_ref, dst_ref, live_ref, h_hbm, ga_ref, gb_ref, wgu_a, bgu_a, wd_a, bd_a,
                       wgu_b, bgu_b, wd_b, bd_b, y_hbm, xa, xb, oa, ob, gsem, ssem, *, bm, dff):
    i = pl.program_id(0)
    live = live_ref[0]
    nc = xa.shape[0] // bm
    dump = y_hbm.shape[0] // nc - 2 * bm

    def gather(block, buf, sem):
        base = block * bm
        for r in range(bm):
            pltpu.make_async_copy(h_hbm.at[pl.ds(pl.multiple_of(tok_ref[base + r], nc), nc)], buf.at[pl.ds(r * nc, nc)], sem).start()

    def scatter(block, buf, sem):
        base = block * bm
        for r in range(bm):
            pltpu.make_async_copy(buf.at[pl.ds(r * nc, nc)], y_hbm.at[pl.ds(pl.multiple_of(dst_ref[base + r], nc), nc)], sem).start()

    def wait_gather(buf, sem):
        pltpu.make_async_copy(h_hbm.at[pl.ds(0, bm * nc)], buf, sem).wait()

    def wait_scatter(buf, sem):
        pltpu.make_async_copy(buf, y_hbm.at[pl.ds(0, bm * nc)], sem).wait()

    def experts(x_ref, o_ref, g_ref, wgu, bgu, wd, bd):
        x = jnp.concatenate([x_ref[pl.ds(c, bm, stride=nc), :] for c in range(nc)], axis=1)
        gu = jnp.dot(x.astype(BF16), wgu[0, 0], preferred_element_type=F32) + bgu[0]
        a = jnp.minimum(gu[:, :dff], SWIGLU_LIMIT)
        u = jnp.clip(gu[:, dff:], -SWIGLU_LIMIT, SWIGLU_LIMIT)
        y = (u + 1.0) * (a * _sigmoid(SWIGLU_ALPHA * a))
        out = (jnp.dot(y.astype(BF16), wd[0, 0], preferred_element_type=F32) + bd[0]) * g_ref[...]
        for c in range(nc):
            o_ref[pl.ds(c, bm, stride=nc), :] = out[:, c * LANES:(c + 1) * LANES]

    @pl.when(i == 0)
    def _():
        gather(0, xa, gsem.at[0])
        for buf, sem, base in ((oa, ssem.at[0], dump), (ob, ssem.at[1], dump + bm)):
            buf[...] = jnp.zeros(buf.shape, buf.dtype)
            for r in range(bm):
                pltpu.make_async_copy(buf.at[pl.ds(r * nc, nc)], y_hbm.at[pl.ds((base + r) * nc, nc)], sem).start()

    @pl.when(i < live)
    def _():
        blk_a = 2 * i
        blk_b = 2 * i + 1
        nxt_a = jnp.minimum(2 * i + 2, 2 * live - 1)

        gather(blk_b, xb, gsem.at[1])
        wait_gather(xa, gsem.at[0])
        wait_scatter(oa, ssem.at[0])
        experts(xa, oa, ga_ref, wgu_a, bgu_a, wd_a, bd_a)
        scatter(blk_a, oa, ssem.at[0])

        gather(nxt_a, xa, gsem.at[0])
        wait_gather(xb, gsem.at[1])
        wait_scatter(ob, ssem.at[1])
        experts(xb, ob, gb_ref, wgu_b, bgu_b, wd_b, bd_b)
        scatter(blk_b, ob, ssem.at[1])

    @pl.when(i == live - 1)
    def _():
        wait_gather(xa, gsem.at[0])
        wait_scatter(oa, ssem.at[0])
        wait_scatter(ob, ssem.at[1])


def _moe_experts(h, blk_exp, tok, dst, live, gates, w_gu, b_gu, w_down, b_down, *, bm, layer):
    _, e, d, f2 = w_gu.shape
    nc = d // LANES
    n = h.shape[0] // nc
    dff = f2 // 2
    cap = tok.shape[0]
    n_steps = cap // (2 * bm)
    assert cap == n_steps * 2 * bm

    def wspecs(off):
        return [
            pl.BlockSpec((1, 1, d, f2), lambda i, be, tk, ds, lv: (layer, be[2 * i + off], 0, 0)),
            pl.BlockSpec((1, 1, f2), lambda i, be, tk, ds, lv: (be[2 * i + off], 0, 0)),
            pl.BlockSpec((1, 1, dff, d), lambda i, be, tk, ds, lv: (layer, be[2 * i + off], 0, 0)),
            pl.BlockSpec((1, 1, d), lambda i, be, tk, ds, lv: (be[2 * i + off], 0, 0)),
        ]

    grid_spec = pltpu.PrefetchScalarGridSpec(
        num_scalar_prefetch=4,
        grid=(n_steps,),
        in_specs=[
            pl.BlockSpec(memory_space=pl.ANY),
            pl.BlockSpec((bm, 1), lambda i, be, tk, ds, lv: (2 * i, 0)),
            pl.BlockSpec((bm, 1), lambda i, be, tk, ds, lv: (2 * i + 1, 0)),
        ] + wspecs(0) + wspecs(1),
        out_specs=pl.BlockSpec(memory_space=pl.ANY),
        scratch_shapes=[pltpu.VMEM((bm * nc, LANES), F32)] * 4 + [pltpu.SemaphoreType.DMA((2,)),
                                                                  pltpu.SemaphoreType.DMA((2,))],
    )
    bgu3 = b_gu.reshape(e, 1, f2).astype(F32)
    bd3 = b_down.reshape(e, 1, d).astype(F32)
    g2 = gates.reshape(cap, 1)
    return pl.pallas_call(
        functools.partial(_moe_expert_kernel, bm=bm, dff=dff),
        out_shape=jax.ShapeDtypeStruct(((TOP_K * n + 2 * bm) * nc, LANES), F32),
        grid_spec=grid_spec,
        compiler_params=_cparams(("arbitrary",)),
        name="moe_experts",
    )(blk_exp, tok, dst, live, h, g2, g2, w_gu, bgu3, w_down, bd3, w_gu, bgu3, w_down, bd3)


def _moe_layer(x, norm_g, router_w, router_b, w_gu, b_gu, w_down, b_down, layer):
    n, d = x.shape
    bm = MOE_BM
    h, idx128, gate128, cnt = _router(x, norm_g, router_w, router_b)
    n_assign = n * TOP_K
    flat_e = idx128[:, :TOP_K].reshape(-1)
    flat_g = gate128[:, :TOP_K].reshape(-1)
    order = jnp.argsort(flat_e).astype(jnp.int32)
    counts = cnt[0, :N_EXPERTS].astype(jnp.int32)
    padded = (counts + bm - 1) // bm * bm
    start = jnp.cumsum(counts) - counts
    pend = jnp.cumsum(padded)
    pstart = pend - padded
    n_blocks = -(-n_assign // bm) + N_EXPERTS
    n_blocks += n_blocks % 2
    blk_start = jnp.arange(n_blocks, dtype=jnp.int32) * bm
    blk_exp = jnp.minimum(jnp.sum((pend[None, :] <= blk_start[:, None]).astype(jnp.int32), axis=1),
                          N_EXPERTS - 1).astype(jnp.int32)
    row = jnp.arange(bm, dtype=jnp.int32)[None, :]
    off = (blk_start - pstart[blk_exp])[:, None] + row
    valid = off < counts[blk_exp][:, None]
    a = order[jnp.clip(start[blk_exp][:, None] + off, 0, n_assign - 1).reshape(-1)].reshape(n_blocks, bm)
    nc = d // LANES
    tok = (jnp.where(valid, a // TOP_K, 0) * nc).astype(jnp.int32).reshape(-1)
    dump_row = TOP_K * n + (jnp.arange(n_blocks, dtype=jnp.int32) % 2)[:, None] * bm + row
    dst = (jnp.where(valid, (a % TOP_K) * n + a // TOP_K, dump_row) * nc).astype(jnp.int32).reshape(-1)
    live = jnp.maximum((pend[-1] // bm + 1) // 2, 1).astype(jnp.int32).reshape(1)
    gates = jnp.where(valid, flat_g[a.reshape(-1)].reshape(n_blocks, bm), 0.0).reshape(-1)
    return _moe_experts(h, blk_exp, tok, dst, live, gates, w_gu, b_gu, w_down, b_down, bm=bm, layer=layer)


def _ple_kernel(x_ref, y0_ref, y1_ref, y2_ref, y3_ref, g_ref, wg_ref, p_ref, wp_ref, o_ref):
    tm, d = x_ref.shape
    nc = d // LANES

    def plane(y_ref):
        return jnp.concatenate([y_ref[pl.ds(c, tm, stride=nc), :] for c in range(nc)], axis=1)

    xf = x_ref[...] + (((plane(y0_ref) + plane(y1_ref)) + plane(y2_ref)) + plane(y3_ref))
    ms = jnp.mean(xf * xf, axis=-1, keepdims=True)
    h = (xf * lax.rsqrt(ms + EPS) * g_ref[...]).astype(BF16)
    gate = _sigmoid(jnp.dot(h, wg_ref[...], preferred_element_type=F32))
    pp = jnp.dot(p_ref[...].astype(BF16), wp_ref[...], preferred_element_type=F32)
    o_ref[...] = xf + gate * pp


def _ple(x, y, g, wg, p, wp, *, tm=512):
    n, d = x.shape
    pd = p.shape[1]
    tm = min(tm, n)
    nb = n // tm
    nc = d // LANES
    yspecs = [pl.BlockSpec((tm * nc, LANES), functools.partial(lambda i, k: (k * nb + i, 0), k=k)) for k in range(TOP_K)]
    return pl.pallas_call(
        _ple_kernel,
        out_shape=jax.ShapeDtypeStruct((n, d), F32),
        grid=(nb,),
        in_specs=[pl.BlockSpec((tm, d), lambda i: (i, 0))] + yspecs + [
            pl.BlockSpec((1, d), lambda i: (0, 0)),
            pl.BlockSpec((d, d), lambda i: (0, 0)),
            pl.BlockSpec((tm, pd), lambda i: (i, 0)),
            pl.BlockSpec((pd, d), lambda i: (0, 0)),
        ],
        out_specs=pl.BlockSpec((tm, d), lambda i: (i, 0)),
        compiler_params=_cparams(("parallel",)),
        name="ple",
    )(x, y, y, y, y, g.reshape(1, d).astype(F32), wg.astype(BF16), p, wp.astype(BF16))


def _cmp_kernel(z_ref, w1_ref, pos_ref, w2_ref, gain_ref, c_ref, s1_ref, s2_ref, o_ref, *, rows, half):
    j = pl.program_id(1)
    z = z_ref[0, 0, 0]
    pos = pos_ref[0]
    a1 = (z + pos[:, :half]).astype(BF16)
    a2 = (z + pos[:, half:]).astype(BF16)
    u1 = jnp.dot(a1, w1_ref[0, :half, :], preferred_element_type=F32)
    u2 = jnp.dot(a2, w1_ref[0, half:, :], preferred_element_type=F32)
    h = u1 + pltpu.roll(u2, rows - 1, 0)
    hs = h * _sigmoid(h)
    y = jnp.dot(hs.astype(BF16), w2_ref[0], preferred_element_type=F32)

    @pl.when(j == 0)
    def _():
        ss = jnp.sum(y * y, axis=-1, keepdims=True) * (1.0 / LANES)
        yn = y * lax.rsqrt(ss + EPS) * gain_ref[...]
        yn = (yn * c_ref[...] + pltpu.roll(yn, LANES - ROPE_DIM // 2, 1) * s1_ref[...]
              + pltpu.roll(yn, ROPE_DIM // 2, 1) * s2_ref[...])
        o_ref[0, 0, 0] = yn.astype(o_ref.dtype)

    @pl.when(j != 0)
    def _():
        o_ref[0, 0, 0] = y.astype(o_ref.dtype)


def _compress(zr, w1, pos, w2dup, gain_dup, tabs, *, batch):
    rows, width = zr.shape[3], zr.shape[4]
    hid = w1.shape[2]
    g = NSA_GROUPS
    return pl.pallas_call(
        functools.partial(_cmp_kernel, rows=rows, half=width),
        out_shape=jax.ShapeDtypeStruct((batch, 2, g, rows, LANES), BF16),
        grid=(batch, 2, g),
        in_specs=[
            pl.BlockSpec((1, 1, 1, rows, width), lambda b, j, gg: (b, j, gg, 0, 0)),
            pl.BlockSpec((1, 2 * width, hid), lambda b, j, gg: (j, 0, 0)),
            pl.BlockSpec((1, 1, 2 * width), lambda b, j, gg: (j, 0, 0)),
            pl.BlockSpec((1, hid, LANES), lambda b, j, gg: (j, 0, 0)),
            pl.BlockSpec((1, LANES), lambda b, j, gg: (0, 0)),
            pl.BlockSpec((rows, LANES), lambda b, j, gg: (0, 0)),
            pl.BlockSpec((rows, LANES), lambda b, j, gg: (0, 0)),
            pl.BlockSpec((rows, LANES), lambda b, j, gg: (0, 0)),
        ],
        out_specs=pl.BlockSpec((1, 1, 1, rows, LANES), lambda b, j, gg: (b, j, gg, 0, 0)),
        compiler_params=_cparams(("parallel", "parallel", "parallel")),
        name="nsa_compress",
    )(zr, w1, pos, w2dup, gain_dup, *tabs)


def _pipelined(n_items, first, second, skew):
    pending, out = {}, [None] * n_items
    for idx in range(n_items + skew):
        if idx < n_items:
            pending[idx] = first(idx)
        if idx >= skew:
            out[idx - skew] = second(idx - skew, pending.pop(idx - skew))
    return out


def _nsa_attn_kernel(q_ref, gz_ref, kc_ref, vct_ref, ovt_ref, ks_ref, vs_ref, kw_ref, vw_ref, o_ref,
                     e_ref, oc_ref, imp_ref, notsel_ref, *, n_cmp_rows, tk_sel, skew, grp, sub):
    g = pl.program_id(1)
    i = pl.program_id(2)
    qb = NSA_QB
    mh = HEADS_PER_GROUP
    rows = mh * qb
    q0 = i * qb
    seq = ks_ref.shape[0]

    @pl.when(i == 0)
    def _():
        r = lax.broadcasted_iota(jnp.int32, (seq, LANES), 0)
        lb = lax.broadcasted_iota(jnp.int32, (seq, LANES), 1)
        e_ref[...] = jnp.where((r >> 6) == lb, NEG, 0.0).astype(BF16)

    qh = [q_ref[:, m * LANES:(m + 1) * LANES] for m in range(mh)]
    q4 = jnp.concatenate(qh, axis=0)
    jl = lax.broadcasted_iota(jnp.int32, (qb, LANES), 1)

    tlane = q0 + (lax.broadcasted_iota(jnp.int32, (1, rows), 1) & (qb - 1))

    def cmp_branch(nrow):
        st = _dot_nt(kc_ref[0, 0, 0, :nrow, :], q4)
        cend = lax.broadcasted_iota(jnp.int32, (nrow, rows), 0) * CMP_STRIDE + (CMP_BLOCK - 1)
        valid = cend <= tlane
        st = jnp.where(valid, st, NEG)
        e = jnp.where(valid, jnp.exp2(st - jnp.max(st, axis=0, keepdims=True)), 0.0)
        pt = e / jnp.maximum(jnp.sum(e, axis=0, keepdims=True), 1e-30)
        oct = jnp.dot(vct_ref[0, 0, :, :nrow], pt.astype(BF16), preferred_element_type=F32)
        for m in range(mh):
            oc_ref[m * qb:(m + 1) * qb, :] = oct[:, m * qb:(m + 1) * qb].T
        psum = (pt[:, 0:qb] + pt[:, qb:2 * qb]) + (pt[:, 2 * qb:3 * qb] + pt[:, 3 * qb:4 * qb])
        phi, plo = _split2(psum)
        ovt = ovt_ref[:, :nrow]
        imp_ref[...] = (jnp.dot(ovt, phi, preferred_element_type=F32)
                        + jnp.dot(ovt, plo, preferred_element_type=F32))

    n_chunks = n_cmp_rows // LANES
    last_visible = (q0 + qb - CMP_BLOCK) // CMP_STRIDE
    need = jnp.clip(last_visible // LANES + 1, 1, n_chunks)
    for nch in range(1, n_chunks + 1):
        pl.when(need == nch)(functools.partial(cmp_branch, nch * LANES))

    def select_blocks(nrow):
        jb = lax.broadcasted_iota(jnp.int32, (nrow, qb), 0)
        tq = q0 + lax.broadcasted_iota(jnp.int32, (nrow, qb), 1)
        cur = tq >> 6
        forced = (jb == 0) | (jb == cur) | (jb == cur - 1)
        causal = (jb << 6) <= tq
        score = jnp.where(causal, jnp.where(forced, FORCE, imp_ref[:nrow, :]), -jnp.inf)
        notsel_t = jnp.ones((nrow, qb), F32)
        jbf = jb.astype(F32)
        for _ in range(SEL_TOPK):
            smx = jnp.max(score, axis=0, keepdims=True)
            ix = jnp.min(jnp.where(score == smx, jbf, float(LANES)), axis=0, keepdims=True)
            hit = jbf == ix
            notsel_t = jnp.where(hit, 0.0, notsel_t)
            score = jnp.where(hit, -jnp.inf, score)
        notsel_ref[:nrow, :] = notsel_t
        if nrow < LANES:
            notsel_ref[nrow:, :] = jnp.ones((LANES - nrow, qb), F32)

    sel_chunk = 32
    n_sel_chunks = LANES // sel_chunk
    sel_need = jnp.clip(((q0 + qb - 1) // SEL_BLOCK) // sel_chunk + 1, 1, n_sel_chunks)
    for nch in range(1, n_sel_chunks + 1):
        pl.when(sel_need == nch)(functools.partial(select_blocks, nch * sel_chunk))
    notsel_b = notsel_ref[...].T.astype(BF16)
    n_grp = mh // grp
    q_aug = [jnp.concatenate([jnp.concatenate([qh[m], notsel_b], axis=1) for m in range(c * grp, (c + 1) * grp)], axis=0)
             for c in range(n_grp)]
    tpos = q0 + lax.broadcasted_iota(jnp.int32, (qb, 1), 0)
    tpos_g = jnp.concatenate([tpos] * grp, axis=0)

    def sel_tiles(first, n_sub):
        ks, vs = [], []
        for t in range(n_sub):
            k0 = pl.multiple_of((first + t) * tk_sel, tk_sel)
            k_aug = jnp.concatenate([ks_ref[pl.ds(k0, tk_sel), :], e_ref[pl.ds(k0, tk_sel), :]], axis=1)
            ks.append([k_aug] * n_grp)
            vs.append([vs_ref[pl.ds(k0, tk_sel), :]] * n_grp)
        return ks, vs

    def sel_step(first, n_sub, state):
        ks, vs = sel_tiles(first, n_sub)
        return _attend(q_aug, ks, vs, state, skew)

    def sel_tail(first, n_rem, state):
        def mask(t, c, sc):
            if t < n_rem:
                return sc
            kpos = (first + n_rem) * tk_sel + lax.broadcasted_iota(jnp.int32, (grp * qb, tk_sel), 1)
            return jnp.where(kpos <= tpos_g, sc, NEG)

        ks, vs = sel_tiles(first, n_rem + 1)
        return _attend(q_aug, ks, vs, state, skew, mask=mask)

    state = tuple((jnp.full((grp * qb, 1), NEG, F32), jnp.zeros((grp * qb, LANES), F32)) for _ in range(n_grp))
    n_below = q0 // tk_sel
    state = _tile_loops(n_below, sub, sel_step, sel_tail, state)
    sel_acc = [state[m // grp][1][(m % grp) * qb:(m % grp + 1) * qb] for m in range(mh)]

    wk = WINDOW + qb
    k0w = pl.multiple_of(jnp.maximum(i - WINDOW // qb, 0) * qb, qb)
    kwt = kw_ref[pl.ds(k0w, wk), :]
    vwt = vw_ref[pl.ds(k0w, wk), :]
    kposw = k0w + lax.broadcasted_iota(jnp.int32, (grp * qb, wk), 1)
    wmask = (kposw <= tpos_g) & (kposw > tpos_g - WINDOW)
    q_grp = [jnp.concatenate(qh[c * grp:(c + 1) * grp], axis=0) for c in range(n_grp)]

    def win_scores(c):
        return jnp.where(wmask, _dot_nt(q_grp[c], kwt), NEG)

    def win_out(c, sc):
        pe = jnp.exp2(sc - jnp.max(sc, axis=-1, keepdims=True))
        return jnp.dot(pe.astype(BF16), vwt, preferred_element_type=F32)

    win_grp = _pipelined(n_grp, win_scores, win_out, min(skew, n_grp - 1))
    win_acc = [win_grp[m // grp][(m % grp) * qb:(m % grp + 1) * qb] for m in range(mh)]

    sig = _sigmoid(gz_ref[...])
    den_lane = jl == HEAD_DIM
    heads = []
    for m in range(mh):
        col = g * mh + m
        gates = [jnp.sum(jnp.where(jl == br * N_HEADS + col, sig, 0.0), axis=-1, keepdims=True)
                 for br in range(3)]
        acc_s = sel_acc[m]
        acc_w = win_acc[m]
        den_s = jnp.sum(jnp.where(den_lane, acc_s, 0.0), axis=-1, keepdims=True)
        den_w = jnp.sum(jnp.where(den_lane, acc_w, 0.0), axis=-1, keepdims=True)
        heads.append(gates[0] * oc_ref[m * qb:(m + 1) * qb, :] + (gates[1] / den_s) * acc_s
                     + (gates[2] / den_w) * acc_w)
    lane_lo = jl < HEAD_DIM
    o_ref[:, 0:LANES] = jnp.where(lane_lo, heads[0], pltpu.roll(heads[1], HEAD_DIM, 1)).astype(o_ref.dtype)
    o_ref[:, LANES:2 * LANES] = jnp.where(lane_lo, heads[2], pltpu.roll(heads[3], HEAD_DIM, 1)).astype(o_ref.dtype)


def _nsa_attn(qn, gz, kvc, overlap, kvd, *, batch, seq, skew=2, grp=2, sub=4):
    n = qn.shape[0]
    qb = NSA_QB
    nq = seq // qb
    g = NSA_GROUPS
    rows = HEADS_PER_GROUP * qb
    n_cmp_rows = kvc.shape[3]
    tk_sel = min(512, seq)
    assert seq >= WINDOW + qb and seq % tk_sel == 0 and tk_sel % qb == 0
    in_specs = [
        pl.BlockSpec((qb, HEADS_PER_GROUP * LANES), lambda b, gg, i: (b * nq + i, gg)),
        pl.BlockSpec((qb, LANES), lambda b, gg, i: (b * nq + i, 0)),
        pl.BlockSpec((1, 1, 1, n_cmp_rows, LANES), lambda b, gg, i: (b, 0, gg, 0, 0)),
        pl.BlockSpec((1, 1, LANES, n_cmp_rows), lambda b, gg, i: (b, gg, 0, 0)),
        pl.BlockSpec((LANES, n_cmp_rows), lambda b, gg, i: (0, 0)),
        pl.BlockSpec((seq, LANES), lambda b, gg, i: (b, 0 * g + gg)),
        pl.BlockSpec((seq, LANES), lambda b, gg, i: (b, 1 * g + gg)),
        pl.BlockSpec((seq, LANES), lambda b, gg, i: (b, 2 * g + gg)),
        pl.BlockSpec((seq, LANES), lambda b, gg, i: (b, 3 * g + gg)),
    ]
    return pl.pallas_call(
        functools.partial(_nsa_attn_kernel, n_cmp_rows=n_cmp_rows, tk_sel=tk_sel, skew=skew, grp=grp, sub=sub),
        out_shape=jax.ShapeDtypeStruct((n, N_HEADS * HEAD_DIM), BF16),
        grid=(batch, g, nq),
        in_specs=in_specs,
        out_specs=pl.BlockSpec((qb, HEADS_PER_GROUP * HEAD_DIM), lambda b, gg, i: (b * nq + i, gg)),
        scratch_shapes=[pltpu.VMEM((seq, LANES), BF16), pltpu.VMEM((rows, LANES), F32), pltpu.VMEM((LANES, qb), F32),
                        pltpu.VMEM((LANES, qb), F32)],
        compiler_params=_cparams(("parallel", "parallel", "arbitrary")),
        name="nsa_attn",
    )(qn, gz, kvc, jnp.swapaxes(kvc[:, 1], -1, -2), overlap.T, kvd, kvd, kvd, kvd)


def _pad_cols(w, width=LANES):
    d, f = w.shape
    return jnp.zeros((d, width), w.dtype).at[:, :f].set(w)


def _tile_gain(gain, n_heads, scale=1.0, dup=False):
    g = gain.astype(F32) * scale
    second = g if dup else jnp.zeros_like(g)
    return jnp.tile(jnp.concatenate([g, second]), n_heads)


def _fox_aug_tables():
    hw = N_HEADS * LANES
    piece = jnp.arange(3)[:, None]
    head = jnp.arange(N_HEADS)[None, :]
    src = (piece * N_HEADS + head).reshape(-1)
    q_dst = (head * LANES + HEAD_DIM + piece).reshape(-1)
    k_dst = (hw + head * LANES + HEAD_DIM + 3 + piece).reshape(-1)
    place = jnp.zeros((LANES, 3 * hw), F32).at[src, q_dst].set(1.0).at[src, k_dst].set(-1.0)
    lane = jnp.arange(LANES)
    q_bias = ((lane >= HEAD_DIM + 3) & (lane < HEAD_DIM + 6)).astype(F32)
    k_bias = ((lane >= HEAD_DIM) & (lane < HEAD_DIM + 3)).astype(F32)
    v_bias = (lane == HEAD_DIM).astype(F32)
    bias = jnp.concatenate([jnp.tile(q_bias, N_HEADS), jnp.tile(k_bias, N_HEADS), jnp.tile(v_bias, N_HEADS)])
    return place.astype(BF16), bias


def _fox_layer(x, batch, seq, attn_g, w_in, b_f, qk_gain, w_out):
    hd = N_HEADS * HEAD_DIM
    hw = N_HEADS * LANES
    scale = HEAD_DIM ** -0.5
    z = _rms_proj(x, attn_g, _pad_cols(w_in[:, 3 * hd:]).astype(BF16), out_dtype=F32, seq=seq)
    b_pad = jnp.zeros((1, LANES), F32).at[0, :N_HEADS].set(b_f.astype(F32))
    c3 = _fox_c(z, b_pad, batch=batch, seq=seq)
    w_bf = w_in.astype(BF16)
    gain = jnp.concatenate([_tile_gain(qk_gain[0], N_HEADS, scale * LOG2E), _tile_gain(qk_gain[1], N_HEADS),
                            jnp.zeros((hw,), F32)])
    tn = 1024
    place, bias = _fox_aug_tables()
    qkv = _rms_proj(x, attn_g, w_bf, out_dtype=BF16, seq=seq, tn=tn, gain=gain, w_cols=3 * hd,
                    norm_blocks=tuple(range(2 * hw // tn)), placed=(c3, place), col_bias=bias, expand=True)
    o = _fox_attn(qkv, batch=batch, seq=seq)
    return _matmul_res(o, w_out.astype(BF16), x)


def _nsa_shared_kv(x, batch, seq, kv_norm, kv_w, kv_k_gain, cmp_pos, cmp_w1, cmp_w2):
    g = NSA_GROUPS
    gw = g * HEAD_DIM
    tabs = _rope_tables(jnp.arange(seq))
    kv_bf = kv_w.astype(BF16)
    zero = jnp.zeros((g * LANES,), F32)
    one_lane = jnp.tile((jnp.arange(LANES) == HEAD_DIM).astype(F32), g)
    gain_kvd = jnp.concatenate([_tile_gain(kv_k_gain[1], g), zero, _tile_gain(kv_k_gain[2], g), zero])
    bias_kvd = jnp.concatenate([zero, one_lane, zero, one_lane])
    kvd = _rms_proj(x, kv_norm, kv_bf, out_dtype=BF16, seq=seq, tn=g * LANES, gain=gain_kvd, w_col0=2 * gw,
                    norm_blocks=(0, 2), rope_tabs=tabs, col_bias=bias_kvd, expand=True)
    zc = _rms_proj(x, kv_norm, kv_bf, out_dtype=F32, seq=seq, w_cols=2 * gw)
    rows = seq // CMP_STRIDE
    zr = zc.reshape(batch, rows, CMP_STRIDE, 2, g, HEAD_DIM).transpose(0, 3, 4, 1, 2, 5)
    zr = zr.reshape(batch, 2, g, rows, CMP_STRIDE * HEAD_DIM)
    cmp_end = jnp.arange(rows) * CMP_STRIDE + CMP_BLOCK - 1
    ctabs = _rope_tables(cmp_end)
    w2dup = jnp.concatenate([cmp_w2, cmp_w2], axis=-1).astype(BF16)
    kvc = _compress(zr, cmp_w1.astype(BF16), cmp_pos.reshape(2, 1, CMP_BLOCK * HEAD_DIM).astype(F32), w2dup,
                    _tile_gain(kv_k_gain[0], 1, dup=True).reshape(1, LANES), ctabs, batch=batch)
    return kvc, kvd


def _nsa_layer(x, batch, seq, attn_g, w_in, q_gain, w_out, kvc, kvd):
    hd = N_HEADS * HEAD_DIM
    scale = HEAD_DIM ** -0.5
    rows = seq // CMP_STRIDE
    tabs = _rope_tables(jnp.arange(seq))
    qn = _rms_proj(x, attn_g, w_in.astype(BF16), out_dtype=BF16, seq=seq, expand=True, w_cols=hd,
                   gain=_tile_gain(q_gain, N_HEADS, scale * LOG2E), norm_blocks=tuple(range(N_HEADS * LANES // 512)),
                   rope_tabs=tabs)
    gz = _rms_proj(x, attn_g, _pad_cols(w_in[:, hd:]).astype(BF16), out_dtype=F32, seq=seq)

    n_sel = seq // SEL_BLOCK
    assert n_sel <= LANES
    cmp_start = jnp.arange(rows) * CMP_STRIDE
    sel_start = jnp.arange(LANES) * SEL_BLOCK
    overlap = jnp.clip(jnp.minimum(cmp_start[:, None] + CMP_BLOCK, sel_start[None, :] + SEL_BLOCK)
                       - jnp.maximum(cmp_start[:, None], sel_start[None, :]), 0)
    overlap = jnp.where((jnp.arange(rows) < rows - 1)[:, None] & (jnp.arange(LANES) < n_sel)[None, :], overlap, 0)
    o = _nsa_attn(qn, gz, kvc, overlap.astype(BF16), kvd, batch=batch, seq=seq)
    return _matmul_res(o, w_out.astype(BF16), x)


def kernel(x, p, attn_norm, ffn_norm, ple_norm, ple_gate_w, ple_proj_w, router_w, router_b, w_gu, b_gu, w_down, b_down, fox_w_in, fox_b_f, fox_qk_gain, fox_w_out, kv_norm, kv_w, kv_k_gain, cmp_pos, cmp_w1, cmp_w2, nsa_w_in, nsa_q_gain, nsa_w_out):
    batch, seq, d = x.shape
    depth = p.shape[0]
    n_a = fox_w_in.shape[0]
    xt = x.reshape(batch * seq, d)
    w_gu_bf = w_gu.astype(BF16)
    w_down_bf = w_down.astype(BF16)
    shared = None
    for layer in range(depth):
        if layer == n_a:
            shared = _nsa_shared_kv(xt, batch, seq, kv_norm, kv_w, kv_k_gain, cmp_pos, cmp_w1, cmp_w2)
        if layer < n_a:
            xt = _fox_layer(xt, batch, seq, attn_norm[layer], fox_w_in[layer], fox_b_f[layer],
                            fox_qk_gain[layer], fox_w_out[layer])
        else:
            i = layer - n_a
            xt = _nsa_layer(xt, batch, seq, attn_norm[layer], nsa_w_in[i], nsa_q_gain[i], nsa_w_out[i], *shared)
        y = _moe_layer(xt, ffn_norm[layer], router_w[layer], router_b[layer], w_gu_bf, b_gu[layer],
                       w_down_bf, b_down[layer], layer)
        xt = _ple(xt, y, ple_norm[layer], ple_gate_w[layer], p[layer].reshape(batch * seq, -1), ple_proj_w[layer])
    return xt.reshape(batch, seq, d)
```

```python
import functools

import jax
import jax.numpy as jnp
from jax import lax
from jax.experimental import pallas as pl
from jax.experimental.pallas import tpu as pltpu

F32 = jnp.float32
BF16 = jnp.bfloat16

N_HEADS = 16
HEAD_DIM = 64
ROPE_DIM = HEAD_DIM // 4
ROPE_THETA = 500000.0
NSA_GROUPS = 4
HEADS_PER_GROUP = N_HEADS // NSA_GROUPS
CMP_BLOCK = 32
CMP_STRIDE = 16
SEL_BLOCK = 64
SEL_TOPK = 16
WINDOW = 512
N_EXPERTS = 32
TOP_K = 4
SWIGLU_LIMIT = 7.0
SWIGLU_ALPHA = 1.702
EPS = 1e-6
NEG = -1e30
FORCE = 1e6
LOG2E = 1.4426950408889634

LANES = 128
NSA_QB = 512
MOE_BM = 256
VMEM_LIMIT = 56 * 1024 * 1024


def _cparams(sem):
    return pltpu.CompilerParams(dimension_semantics=sem, vmem_limit_bytes=VMEM_LIMIT)


def _sigmoid(x):
    return 1.0 / (1.0 + jnp.exp(-x))


def _dot_nt(a, b):
    return lax.dot_general(a, b, (((1,), (1,)), ((), ())), preferred_element_type=F32)


def _split2(x):
    hi = x.astype(BF16)
    lo = (x - hi.astype(F32)).astype(BF16)
    return hi, lo


def _proj_kernel(*refs, norm_blocks, rope, placed, biased, expand, head_div, tn, out_dtype):
    refs = list(refs)
    x_ref, g_ref, w_ref = refs[:3]
    del refs[:3]
    if norm_blocks:
        gain_ref = refs.pop(0)
    if rope:
        c_ref, s1_ref, s2_ref = refs[:3]
        del refs[:3]
    if placed:
        ex_ref, place_ref = refs[:2]
        del refs[:2]
    if biased:
        bias_ref = refs.pop(0)
    o_ref, h_ref = refs
    j = pl.program_id(1)
    aug = placed or biased

    @pl.when(j == 0)
    def _():
        xf = x_ref[...]
        ms = jnp.mean(xf * xf, axis=-1, keepdims=True)
        h_ref[...] = (xf * lax.rsqrt(ms + EPS) * g_ref[...]).astype(BF16)

    acc = jnp.dot(h_ref[...], w_ref[...], preferred_element_type=F32)
    if placed and biased:
        extra = jnp.dot(ex_ref[...], place_ref[...], preferred_element_type=F32) + bias_ref[...]
    elif biased:
        extra = jnp.broadcast_to(bias_ref[...], (acc.shape[0], tn))
    if expand:
        lo = lax.broadcasted_iota(jnp.int32, (acc.shape[0], LANES), 1) < HEAD_DIM
        blocks = []
        for c in range(acc.shape[1] // LANES):
            a = acc[:, c * LANES:(c + 1) * LANES]
            blocks += [jnp.where(lo, a, 0.0), jnp.where(lo, pltpu.roll(a, HEAD_DIM, 1), 0.0)]
    else:
        blocks = [acc[:, c * LANES:(c + 1) * LANES] for c in range(acc.shape[1] // LANES)]

    def plain():
        for c, a in enumerate(blocks):
            sl = slice(c * LANES, (c + 1) * LANES)
            o_ref[:, sl] = ((a + extra[:, sl]) if aug else a).astype(out_dtype)

    if not norm_blocks:
        plain()
        return
    is_norm = functools.reduce(jnp.logical_or, [j == c for c in norm_blocks])

    @pl.when(is_norm)
    def _():
        for c, a in enumerate(blocks):
            sl = slice(c * LANES, (c + 1) * LANES)
            ss = jnp.sum(a * a, axis=-1, keepdims=True) * (1.0 / head_div)
            y = a * lax.rsqrt(ss + EPS) * gain_ref[:, sl]
            if rope:
                y = (y * c_ref[...] + pltpu.roll(y, LANES - ROPE_DIM // 2, 1) * s1_ref[...]
                     + pltpu.roll(y, ROPE_DIM // 2, 1) * s2_ref[...])
            if aug:
                y = y + extra[:, sl]
            o_ref[:, sl] = y.astype(out_dtype)

    @pl.when(jnp.logical_not(is_norm))
    def _():
        plain()


def _rms_proj(x, g, w, *, out_dtype, seq, tm=2048, tn=512, gain=None, norm_blocks=(),
              rope_tabs=None, placed=None, col_bias=None, expand=False, head_div=HEAD_DIM,
              w_col0=0, w_cols=None):
    n, d = x.shape
    w_cols = w.shape[1] - w_col0 if w_cols is None else w_cols
    f = w_cols * (2 if expand else 1)
    tn = min(tn, f)
    tw = tn // 2 if expand else tn
    assert w_col0 % tw == 0
    j0 = w_col0 // tw
    tm = min(tm, seq)
    assert n % tm == 0 and f % tn == 0 and seq % tm == 0
    nt = seq // tm
    in_specs = [
        pl.BlockSpec((tm, d), lambda i, j: (i, 0)),
        pl.BlockSpec((1, d), lambda i, j: (0, 0)),
        pl.BlockSpec((d, tw), lambda i, j: (0, j0 + j)),
    ]
    args = [x, g.reshape(1, d).astype(F32), w]
    if norm_blocks:
        in_specs.append(pl.BlockSpec((1, tn), lambda i, j: (0, j)))
        args.append(gain.reshape(1, f).astype(F32))
    if rope_tabs is not None:
        assert norm_blocks
        for tab in rope_tabs:
            in_specs.append(pl.BlockSpec((tm, LANES), lambda i, j: (i % nt, 0)))
            args.append(tab)
    if placed is not None:
        assert col_bias is not None
        values, place = placed
        in_specs += [pl.BlockSpec((tm, LANES), lambda i, j: (i, 0)),
                     pl.BlockSpec((LANES, tn), lambda i, j: (0, j))]
        args += [values, place]
    if col_bias is not None:
        in_specs.append(pl.BlockSpec((1, tn), lambda i, j: (0, j)))
        args.append(col_bias.reshape(1, f).astype(F32))
    kern = functools.partial(_proj_kernel, norm_blocks=tuple(norm_blocks), rope=rope_tabs is not None,
                             placed=placed is not None, biased=col_bias is not None, expand=expand,
                             head_div=float(head_div), tn=tn, out_dtype=out_dtype)
    return pl.pallas_call(
        kern,
        out_shape=jax.ShapeDtypeStruct((n, f), out_dtype),
        grid=(n // tm, f // tn),
        in_specs=in_specs,
        out_specs=pl.BlockSpec((tm, tn), lambda i, j: (i, j)),
        scratch_shapes=[pltpu.VMEM((tm, d), BF16)],
        compiler_params=_cparams(("parallel", "arbitrary")),
        name="rms_proj",
    )(*args)


def _rope_tables(pos, width=LANES):
    half = ROPE_DIM // 2
    inv = jnp.power(jnp.float32(ROPE_THETA), -jnp.arange(0, ROPE_DIM, 2, dtype=F32) / ROPE_DIM)
    ang = pos.astype(F32)[:, None] * inv[None, :]
    cos, sin = jnp.cos(ang), jnp.sin(ang)
    t = pos.shape[0]
    ones = jnp.ones((t, HEAD_DIM - ROPE_DIM), F32)
    zeros_h = jnp.zeros((t, half), F32)
    zeros_r = jnp.zeros((t, HEAD_DIM - ROPE_DIM), F32)
    c = jnp.concatenate([cos, cos, ones], axis=1)
    s1 = jnp.concatenate([-sin, zeros_h, zeros_r], axis=1)
    s2 = jnp.concatenate([zeros_h, sin, zeros_r], axis=1)
    rep = width // HEAD_DIM
    return tuple(jnp.tile(a, (1, rep)) for a in (c, s1, s2))


def _matmul_res_kernel(a_ref, w_ref, x_ref, o_ref):
    o_ref[...] = x_ref[...] + jnp.dot(a_ref[...], w_ref[...], preferred_element_type=F32)


def _matmul_res(a, w, x, *, tm=1024):
    n, k = a.shape
    f = w.shape[1]
    tm = min(tm, n)
    return pl.pallas_call(
        _matmul_res_kernel,
        out_shape=jax.ShapeDtypeStruct((n, f), F32),
        grid=(n // tm,),
        in_specs=[
            pl.BlockSpec((tm, k), lambda i: (i, 0)),
            pl.BlockSpec((k, f), lambda i: (0, 0)),
            pl.BlockSpec((tm, f), lambda i: (i, 0)),
        ],
        out_specs=pl.BlockSpec((tm, f), lambda i: (i, 0)),
        compiler_params=_cparams(("parallel",)),
        name="matmul_res",
    )(a, w, x)


def _fox_c_kernel(z_ref, b_ref, c_ref, carry_ref, *, tm):
    t = pl.program_id(1)

    @pl.when(t == 0)
    def _():
        carry_ref[...] = jnp.zeros_like(carry_ref)

    u = z_ref[...] + b_ref[...]
    logf = jnp.minimum(u, 0.0) - jnp.log(1.0 + jnp.exp(-jnp.abs(u)))
    row = lax.broadcasted_iota(jnp.int32, (tm, tm), 0)
    col = lax.broadcasted_iota(jnp.int32, (tm, tm), 1)
    tri = (row >= col).astype(BF16)
    hi = logf.astype(BF16)
    r1 = logf - hi.astype(F32)
    mid = r1.astype(BF16)
    lo = (r1 - mid.astype(F32)).astype(BF16)
    cs = (jnp.dot(tri, hi, preferred_element_type=F32) + jnp.dot(tri, mid, preferred_element_type=F32)
          + jnp.dot(tri, lo, preferred_element_type=F32))
    c = cs + carry_ref[...]
    carry_ref[...] = c[tm - 1:tm, :]
    c2 = c * LOG2E
    p0 = c2.astype(BF16).astype(F32)
    p1 = (c2 - p0).astype(BF16).astype(F32)
    p2 = ((c2 - p0) - p1).astype(BF16).astype(F32)
    lane = lax.broadcasted_iota(jnp.int32, c.shape, 1)
    out = jnp.where(lane < N_HEADS, p0,
                    jnp.where(lane < 2 * N_HEADS, pltpu.roll(p1, N_HEADS, 1),
                              jnp.where(lane < 3 * N_HEADS, pltpu.roll(p2, 2 * N_HEADS, 1), 0.0)))
    c_ref[...] = out.astype(BF16)


def _fox_c(z, b_pad, *, batch, seq, tm=512):
    tm = min(tm, seq)
    nt = seq // tm
    return pl.pallas_call(
        functools.partial(_fox_c_kernel, tm=tm),
        out_shape=jax.ShapeDtypeStruct(z.shape, BF16),
        grid=(batch, nt),
        in_specs=[
            pl.BlockSpec((tm, LANES), lambda b, t: (b * nt + t, 0)),
            pl.BlockSpec((1, LANES), lambda b, t: (0, 0)),
        ],
        out_specs=pl.BlockSpec((tm, LANES), lambda b, t: (b * nt + t, 0)),
        scratch_shapes=[pltpu.VMEM((1, LANES), F32)],
        compiler_params=_cparams(("parallel", "arbitrary")),
        name="fox_cumsum",
    )(z, b_pad)


def _attend(qs, ks, vs, state, skew, mask=None):
    items = [(t, c) for t in range(len(ks)) for c in range(len(qs))]
    state = list(state)
    pending = {}
    depth = min(skew, len(items) - 1)
    for idx in range(len(items) + depth):
        if idx < len(items):
            t, c = items[idx]
            sc = _dot_nt(qs[c], ks[t][c])
            pending[idx] = sc if mask is None else mask(t, c, sc)
        if idx >= depth:
            t, c = items[idx - depth]
            sc = pending.pop(idx - depth)
            m_prev, acc = state[c]
            m_new = jnp.maximum(m_prev, jnp.max(sc, axis=-1, keepdims=True))
            pe = jnp.exp2(sc - m_new)
            acc = jnp.exp2(m_prev - m_new) * acc + jnp.dot(pe.astype(BF16), vs[t][c], preferred_element_type=F32)
            state[c] = (m_new, acc)
    return tuple(state)


def _tile_loops(n_tiles, sub, step, tail, state):
    n_full = n_tiles // sub
    state = lax.fori_loop(0, n_full, lambda j, st: step(j * sub, sub, st), state)
    first = n_full * sub
    return lax.switch(n_tiles - first, [functools.partial(tail, first, r) for r in range(sub)], state)


def _fox_attn_kernel(q0_ref, q1_ref, k0_ref, k1_ref, v0_ref, v1_ref, o_ref, *, tq, tk, skew, sub):
    qi = pl.program_id(2)
    k_refs = (k0_ref, k1_ref)
    v_refs = (v0_ref, v1_ref)
    qs = [q0_ref[...], q1_ref[...]]

    def tiles(refs, first, n_sub):
        return [[r[pl.ds(pl.multiple_of((first + t) * tk, tk), tk), :] for r in refs] for t in range(n_sub)]

    def full_step(first, n_sub, state):
        return _attend(qs, tiles(k_refs, first, n_sub), tiles(v_refs, first, n_sub), state, skew)

    def causal(t, c, sc):
        tpos = lax.broadcasted_iota(jnp.int32, (tq, tk), 0)
        kpos = t * tk + lax.broadcasted_iota(jnp.int32, (tq, tk), 1)
        return jnp.where(kpos <= tpos, sc, NEG)

    n_diag = tq // tk

    def tail_step(first, n_rem, state):
        def mask(t, c, sc):
            return causal(t - n_rem, c, sc) if t >= n_rem else sc

        n = n_rem + n_diag
        return _attend(qs, tiles(k_refs, first, n), tiles(v_refs, first, n), state, skew, mask=mask)

    state = tuple((jnp.full((tq, 1), NEG, F32), jnp.zeros((tq, LANES), F32)) for _ in qs)
    n_below = (qi * tq) // tk
    state = _tile_loops(n_below, sub, full_step, tail_step, state)

    lane = lax.broadcasted_iota(jnp.int32, (tq, LANES), 1)
    outs = []
    for _, acc in state:
        den = jnp.sum(jnp.where(lane == HEAD_DIM, acc, 0.0), axis=-1, keepdims=True)
        outs.append(acc / den)
    o_ref[...] = jnp.where(lane < HEAD_DIM, outs[0], pltpu.roll(outs[1], HEAD_DIM, 1)).astype(o_ref.dtype)


def _fox_attn(qkv, *, batch, seq, tq=512, tk=512, skew=2, sub=4):
    n = qkv.shape[0]
    tq = min(tq, seq)
    tk = min(tk, tq)
    assert seq % tq == 0 and tq % tk == 0
    nq = seq // tq
    hp = N_HEADS // 2
    kb = N_HEADS
    vb = 2 * N_HEADS
    in_specs = [
        pl.BlockSpec((tq, LANES), lambda b, h, i: (b * nq + i, 2 * h)),
        pl.BlockSpec((tq, LANES), lambda b, h, i: (b * nq + i, 2 * h + 1)),
        pl.BlockSpec((seq, LANES), lambda b, h, i: (b, kb + 2 * h)),
        pl.BlockSpec((seq, LANES), lambda b, h, i: (b, kb + 2 * h + 1)),
        pl.BlockSpec((seq, LANES), lambda b, h, i: (b, vb + 2 * h)),
        pl.BlockSpec((seq, LANES), lambda b, h, i: (b, vb + 2 * h + 1)),
    ]
    return pl.pallas_call(
        functools.partial(_fox_attn_kernel, tq=tq, tk=tk, skew=skew, sub=sub),
        out_shape=jax.ShapeDtypeStruct((n, N_HEADS * HEAD_DIM), BF16),
        grid=(batch, hp, nq),
        in_specs=in_specs,
        out_specs=pl.BlockSpec((tq, LANES), lambda b, h, i: (b * nq + i, h)),
        compiler_params=_cparams(("parallel", "parallel", "arbitrary")),
        name="fox_attn",
    )(qkv, qkv, qkv, qkv, qkv, qkv)


def _router_kernel(x_ref, g_ref, whi_ref, wlo_ref, b_ref, h_ref, idx_ref, gate_ref, cnt_ref):
    xf = x_ref[...]
    ms = jnp.mean(xf * xf, axis=-1, keepdims=True)
    h = xf * lax.rsqrt(ms + EPS) * g_ref[...]
    nc = h.shape[1] // LANES
    for c in range(nc):
        h_ref[pl.ds(c, h.shape[0], stride=nc), :] = h[:, c * LANES:(c + 1) * LANES]
    hhi, hlo = _split2(h)
    whi = whi_ref[...]
    logits = (jnp.dot(hhi, whi, preferred_element_type=F32) + jnp.dot(hlo, whi, preferred_element_type=F32)
              + jnp.dot(hhi, wlo_ref[...], preferred_element_type=F32)) + b_ref[...]
    lane = lax.broadcasted_iota(jnp.int32, logits.shape, 1).astype(F32)
    vals, idxs = [], []
    cur = logits
    for _ in range(TOP_K):
        mx = jnp.max(cur, axis=-1, keepdims=True)
        ix = jnp.min(jnp.where(cur == mx, lane, float(LANES)), axis=-1, keepdims=True)
        vals.append(mx)
        idxs.append(ix)
        cur = jnp.where(lane == ix, -jnp.inf, cur)
    es = [jnp.exp(v - vals[0]) for v in vals]
    tot = es[0] + es[1] + es[2] + es[3]
    idx_out = jnp.zeros(logits.shape, F32)
    gate_out = jnp.zeros(logits.shape, F32)
    for r in range(TOP_K):
        idx_out = jnp.where(lane == r, idxs[r], idx_out)
        gate_out = jnp.where(lane == r, es[r] / tot, gate_out)
    idx_ref[...] = idx_out.astype(jnp.int32)
    gate_ref[...] = gate_out
    hits = sum(jnp.sum((lane == ix).astype(F32), axis=0, keepdims=True) for ix in idxs)

    @pl.when(pl.program_id(0) == 0)
    def _():
        cnt_ref[...] = jnp.zeros(cnt_ref.shape, cnt_ref.dtype)

    cnt_ref[...] += jnp.broadcast_to(hits, cnt_ref.shape)


def _router(x, g, w, b, *, tm=1024):
    n, d = x.shape
    e = w.shape[1]
    tm = min(tm, n)
    w_pad = jnp.zeros((d, LANES), F32).at[:, :e].set(w)
    whi = w_pad.astype(BF16)
    wlo = (w_pad - whi.astype(F32)).astype(BF16)
    b_pad = jnp.full((1, LANES), -jnp.inf, F32).at[0, :e].set(b.astype(F32))
    return pl.pallas_call(
        _router_kernel,
        out_shape=(jax.ShapeDtypeStruct((n * (d // LANES), LANES), F32), jax.ShapeDtypeStruct((n, LANES), jnp.int32),
                   jax.ShapeDtypeStruct((n, LANES), F32), jax.ShapeDtypeStruct((8, LANES), F32)),
        grid=(n // tm,),
        in_specs=[
            pl.BlockSpec((tm, d), lambda i: (i, 0)),
            pl.BlockSpec((1, d), lambda i: (0, 0)),
            pl.BlockSpec((d, LANES), lambda i: (0, 0)),
            pl.BlockSpec((d, LANES), lambda i: (0, 0)),
            pl.BlockSpec((1, LANES), lambda i: (0, 0)),
        ],
        out_specs=(pl.BlockSpec((tm * (d // LANES), LANES), lambda i: (i, 0)), pl.BlockSpec((tm, LANES), lambda i: (i, 0)),
                   pl.BlockSpec((tm, LANES), lambda i: (i, 0)), pl.BlockSpec((8, LANES), lambda i: (0, 0))),
        compiler_params=_cparams(("arbitrary",)),
        name="moe_router",
    )(x, g.reshape(1, d).astype(F32), whi, wlo, b_pad)


def _moe_expert_kernel(blk_exp_ref, tok_ref, dst_ref, live_ref, h_hbm, ga_ref, gb_ref, wgu_a, bgu_a, wd_a, bd_a,
                       wgu_b, bgu_b, wd_b, bd_b, y_hbm, xa, xb, oa, ob, gsem, ssem, *, bm, dff):
    i = pl.program_id(0)
    live = live_ref[0]
    nc = xa.shape[0] // bm
    dump = y_hbm.shape[0] // nc - 2 * bm

    def gather(block, buf, sem):
        base = block * bm
        for r in range(bm):
            pltpu.make_async_copy(h_hbm.at[pl.ds(pl.multiple_of(tok_ref[base + r], nc), nc)], buf.at[pl.ds(r * nc, nc)], sem).start()

    def scatter(block, buf, sem):
        base = block * bm
        for r in range(bm):
            pltpu.make_async_copy(buf.at[pl.ds(r * nc, nc)], y_hbm.at[pl.ds(pl.multiple_of(dst_ref[base + r], nc), nc)], sem).start()

    def wait_gather(buf, sem):
        pltpu.make_async_copy(h_hbm.at[pl.ds(0, bm * nc)], buf, sem).wait()

    def wait_scatter(buf, sem):
        pltpu.make_async_copy(buf, y_hbm.at[pl.ds(0, bm * nc)], sem).wait()

    def experts(x_ref, o_ref, g_ref, wgu, bgu, wd, bd):
        x = jnp.concatenate([x_ref[pl.ds(c, bm, stride=nc), :] for c in range(nc)], axis=1)
        gu = jnp.dot(x.astype(BF16), wgu[0, 0], preferred_element_type=F32) + bgu[0]
        a = jnp.minimum(gu[:, :dff], SWIGLU_LIMIT)
        u = jnp.clip(gu[:, dff:], -SWIGLU_LIMIT, SWIGLU_LIMIT)
        y = (u + 1.0) * (a * _sigmoid(SWIGLU_ALPHA * a))
        out = (jnp.dot(y.astype(BF16), wd[0, 0], preferred_element_type=F32) + bd[0]) * g_ref[...]
        for c in range(nc):
            o_ref[pl.ds(c, bm, stride=nc), :] = out[:, c * LANES:(c + 1) * LANES]

    @pl.when(i == 0)
    def _():
        gather(0, xa, gsem.at[0])
        for buf, sem, base in ((oa, ssem.at[0], dump), (ob, ssem.at[1], dump + bm)):
            buf[...] = jnp.zeros(buf.shape, buf.dtype)
            for r in range(bm):
                pltpu.make_async_copy(buf.at[pl.ds(r * nc, nc)], y_hbm.at[pl.ds((base + r) * nc, nc)], sem).start()

    @pl.when(i < live)
    def _():
        blk_a = 2 * i
        blk_b = 2 * i + 1
        nxt_a = jnp.minimum(2 * i + 2, 2 * live - 1)

        gather(blk_b, xb, gsem.at[1])
        wait_gather(xa, gsem.at[0])
        wait_scatter(oa, ssem.at[0])
        experts(xa, oa, ga_ref, wgu_a, bgu_a, wd_a, bd_a)
        scatter(blk_a, oa, ssem.at[0])

        gather(nxt_a, xa, gsem.at[0])
        wait_gather(xb, gsem.at[1])
        wait_scatter(ob, ssem.at[1])
        experts(xb, ob, gb_ref, wgu_b, bgu_b, wd_b, bd_b)
        scatter(blk_b, ob, ssem.at[1])

    @pl.when(i == live - 1)
    def _():
        wait_gather(xa, gsem.at[0])
        wait_scatter(oa, ssem.at[0])
        wait_scatter(ob, ssem.at[1])


def _moe_experts(h, blk_exp, tok, dst, live, gates, w_gu, b_gu, w_down, b_down, *, bm, layer):
    _, e, d, f2 = w_gu.shape
    nc = d // LANES
    n = h.shape[0] // nc
    dff = f2 // 2
    cap = tok.shape[0]
    n_steps = cap // (2 * bm)
    assert cap == n_steps * 2 * bm

    def wspecs(off):
        return [
            pl.BlockSpec((1, 1, d, f2), lambda i, be, tk, ds, lv: (layer, be[2 * i + off], 0, 0)),
            pl.BlockSpec((1, 1, f2), lambda i, be, tk, ds, lv: (be[2 * i + off], 0, 0)),
            pl.BlockSpec((1, 1, dff, d), lambda i, be, tk, ds, lv: (layer, be[2 * i + off], 0, 0)),
            pl.BlockSpec((1, 1, d), lambda i, be, tk, ds, lv: (be[2 * i + off], 0, 0)),
        ]

    grid_spec = pltpu.PrefetchScalarGridSpec(
        num_scalar_prefetch=4,
        grid=(n_steps,),
        in_specs=[
            pl.BlockSpec(memory_space=pl.ANY),
            pl.BlockSpec((bm, 1), lambda i, be, tk, ds, lv: (2 * i, 0)),
            pl.BlockSpec((bm, 1), lambda i, be, tk, ds, lv: (2 * i + 1, 0)),
        ] + wspecs(0) + wspecs(1),
        out_specs=pl.BlockSpec(memory_space=pl.ANY),
        scratch_shapes=[pltpu.VMEM((bm * nc, LANES), F32)] * 4 + [pltpu.SemaphoreType.DMA((2,)),
                                                                  pltpu.SemaphoreType.DMA((2,))],
    )
    bgu3 = b_gu.reshape(e, 1, f2).astype(F32)
    bd3 = b_down.reshape(e, 1, d).astype(F32)
    g2 = gates.reshape(cap, 1)
    return pl.pallas_call(
        functools.partial(_moe_expert_kernel, bm=bm, dff=dff),
        out_shape=jax.ShapeDtypeStruct(((TOP_K * n + 2 * bm) * nc, LANES), F32),
        grid_spec=grid_spec,
        compiler_params=_cparams(("arbitrary",)),
        name="moe_experts",
    )(blk_exp, tok, dst, live, h, g2, g2, w_gu, bgu3, w_down, bd3, w_gu, bgu3, w_down, bd3)


def _moe_layer(x, norm_g, router_w, router_b, w_gu, b_gu, w_down, b_down, layer):
    n, d = x.shape
    bm = MOE_BM
    h, idx128, gate128, cnt = _router(x, norm_g, router_w, router_b)
    n_assign = n * TOP_K
    flat_e = idx128[:, :TOP_K].reshape(-1)
    flat_g = gate128[:, :TOP_K].reshape(-1)
    order = jnp.argsort(flat_e).astype(jnp.int32)
    counts = cnt[0, :N_EXPERTS].astype(jnp.int32)
    padded = (counts + bm - 1) // bm * bm
    start = jnp.cumsum(counts) - counts
    pend = jnp.cumsum(padded)
    pstart = pend - padded
    n_blocks = -(-n_assign // bm) + N_EXPERTS
    n_blocks += n_blocks % 2
    blk_start = jnp.arange(n_blocks, dtype=jnp.int32) * bm
    blk_exp = jnp.minimum(jnp.sum((pend[None, :] <= blk_start[:, None]).astype(jnp.int32), axis=1),
                          N_EXPERTS - 1).astype(jnp.int32)
    row = jnp.arange(bm, dtype=jnp.int32)[None, :]
    off = (blk_start - pstart[blk_exp])[:, None] + row
    valid = off < counts[blk_exp][:, None]
    a = order[jnp.clip(start[blk_exp][:, None] + off, 0, n_assign - 1).reshape(-1)].reshape(n_blocks, bm)
    nc = d // LANES
    tok = (jnp.where(valid, a // TOP_K, 0) * nc).astype(jnp.int32).reshape(-1)
    dump_row = TOP_K * n + (jnp.arange(n_blocks, dtype=jnp.int32) % 2)[:, None] * bm + row
    dst = (jnp.where(valid, (a % TOP_K) * n + a // TOP_K, dump_row) * nc).astype(jnp.int32).reshape(-1)
    live = jnp.maximum((pend[-1] // bm + 1) // 2, 1).astype(jnp.int32).reshape(1)
    gates = jnp.where(valid, flat_g[a.reshape(-1)].reshape(n_blocks, bm), 0.0).reshape(-1)
    return _moe_experts(h, blk_exp, tok, dst, live, gates, w_gu, b_gu, w_down, b_down, bm=bm, layer=layer)


def _ple_kernel(x_ref, y0_ref, y1_ref, y2_ref, y3_ref, g_ref, wg_ref, p_ref, wp_ref, o_ref):
    tm, d = x_ref.shape
    nc = d // LANES

    def plane(y_ref):
        return jnp.concatenate([y_ref[pl.ds(c, tm, stride=nc), :] for c in range(nc)], axis=1)

    xf = x_ref[...] + (((plane(y0_ref) + plane(y1_ref)) + plane(y2_ref)) + plane(y3_ref))
    ms = jnp.mean(xf * xf, axis=-1, keepdims=True)
    h = (xf * lax.rsqrt(ms + EPS) * g_ref[...]).astype(BF16)
    gate = _sigmoid(jnp.dot(h, wg_ref[...], preferred_element_type=F32))
    pp = jnp.dot(p_ref[...].astype(BF16), wp_ref[...], preferred_element_type=F32)
    o_ref[...] = xf + gate * pp


def _ple(x, y, g, wg, p, wp, *, tm=512):
    n, d = x.shape
    pd = p.shape[1]
    tm = min(tm, n)
    nb = n // tm
    nc = d // LANES
    yspecs = [pl.BlockSpec((tm * nc, LANES), functools.partial(lambda i, k: (k * nb + i, 0), k=k)) for k in range(TOP_K)]
    return pl.pallas_call(
        _ple_kernel,
        out_shape=jax.ShapeDtypeStruct((n, d), F32),
        grid=(nb,),
        in_specs=[pl.BlockSpec((tm, d), lambda i: (i, 0))] + yspecs + [
            pl.BlockSpec((1, d), lambda i: (0, 0)),
            pl.BlockSpec((d, d), lambda i: (0, 0)),
            pl.BlockSpec((tm, pd), lambda i: (i, 0)),
            pl.BlockSpec((pd, d), lambda i: (0, 0)),
        ],
        out_specs=pl.BlockSpec((tm, d), lambda i: (i, 0)),
        compiler_params=_cparams(("parallel",)),
        name="ple",
    )(x, y, y, y, y, g.reshape(1, d).astype(F32), wg.astype(BF16), p, wp.astype(BF16))


def _cmp_kernel(z_ref, w1_ref, pos_ref, w2_ref, gain_ref, c_ref, s1_ref, s2_ref, o_ref, *, rows, half):
    j = pl.program_id(1)
    z = z_ref[0, 0, 0]
    pos = pos_ref[0]
    a1 = (z + pos[:, :half]).astype(BF16)
    a2 = (z + pos[:, half:]).astype(BF16)
    u1 = jnp.dot(a1, w1_ref[0, :half, :], preferred_element_type=F32)
    u2 = jnp.dot(a2, w1_ref[0, half:, :], preferred_element_type=F32)
    h = u1 + pltpu.roll(u2, rows - 1, 0)
    hs = h * _sigmoid(h)
    y = jnp.dot(hs.astype(BF16), w2_ref[0], preferred_element_type=F32)

    @pl.when(j == 0)
    def _():
        ss = jnp.sum(y * y, axis=-1, keepdims=True) * (1.0 / LANES)
        yn = y * lax.rsqrt(ss + EPS) * gain_ref[...]
        yn = (yn * c_ref[...] + pltpu.roll(yn, LANES - ROPE_DIM // 2, 1) * s1_ref[...]
              + pltpu.roll(yn, ROPE_DIM // 2, 1) * s2_ref[...])
        o_ref[0, 0, 0] = yn.astype(o_ref.dtype)

    @pl.when(j != 0)
    def _():
        o_ref[0, 0, 0] = y.astype(o_ref.dtype)


def _compress(zr, w1, pos, w2dup, gain_dup, tabs, *, batch):
    rows, width = zr.shape[3], zr.shape[4]
    hid = w1.shape[2]
    g = NSA_GROUPS
    return pl.pallas_call(
        functools.partial(_cmp_kernel, rows=rows, half=width),
        out_shape=jax.ShapeDtypeStruct((batch, 2, g, rows, LANES), BF16),
        grid=(batch, 2, g),
        in_specs=[
            pl.BlockSpec((1, 1, 1, rows, width), lambda b, j, gg: (b, j, gg, 0, 0)),
            pl.BlockSpec((1, 2 * width, hid), lambda b, j, gg: (j, 0, 0)),
            pl.BlockSpec((1, 1, 2 * width), lambda b, j, gg: (j, 0, 0)),
            pl.BlockSpec((1, hid, LANES), lambda b, j, gg: (j, 0, 0)),
            pl.BlockSpec((1, LANES), lambda b, j, gg: (0, 0)),
            pl.BlockSpec((rows, LANES), lambda b, j, gg: (0, 0)),
            pl.BlockSpec((rows, LANES), lambda b, j, gg: (0, 0)),
            pl.BlockSpec((rows, LANES), lambda b, j, gg: (0, 0)),
        ],
        out_specs=pl.BlockSpec((1, 1, 1, rows, LANES), lambda b, j, gg: (b, j, gg, 0, 0)),
        compiler_params=_cparams(("parallel", "parallel", "parallel")),
        name="nsa_compress",
    )(zr, w1, pos, w2dup, gain_dup, *tabs)


def _pipelined(n_items, first, second, skew):
    pending, out = {}, [None] * n_items
    for idx in range(n_items + skew):
        if idx < n_items:
            pending[idx] = first(idx)
        if idx >= skew:
            out[idx - skew] = second(idx - skew, pending.pop(idx - skew))
    return out


def _nsa_attn_kernel(q_ref, gz_ref, kc_ref, vct_ref, ovt_ref, ks_ref, vs_ref, kw_ref, vw_ref, o_ref,
                     e_ref, oc_ref, imp_ref, notsel_ref, *, n_cmp_rows, tk_sel, skew, grp, sub):
    g = pl.program_id(1)
    i = pl.program_id(2)
    qb = NSA_QB
    mh = HEADS_PER_GROUP
    rows = mh * qb
    q0 = i * qb
    seq = ks_ref.shape[0]

    @pl.when(i == 0)
    def _():
        r = lax.broadcasted_iota(jnp.int32, (seq, LANES), 0)
        lb = lax.broadcasted_iota(jnp.int32, (seq, LANES), 1)
        e_ref[...] = jnp.where((r >> 6) == lb, NEG, 0.0).astype(BF16)

    qh = [q_ref[:, m * LANES:(m + 1) * LANES] for m in range(mh)]
    q4 = jnp.concatenate(qh, axis=0)
    jl = lax.broadcasted_iota(jnp.int32, (qb, LANES), 1)

    tlane = q0 + (lax.broadcasted_iota(jnp.int32, (1, rows), 1) & (qb - 1))

    def cmp_branch(nrow):
        st = _dot_nt(kc_ref[0, 0, 0, :nrow, :], q4)
        cend = lax.broadcasted_iota(jnp.int32, (nrow, rows), 0) * CMP_STRIDE + (CMP_BLOCK - 1)
        valid = cend <= tlane
        st = jnp.where(valid, st, NEG)
        e = jnp.where(valid, jnp.exp2(st - jnp.max(st, axis=0, keepdims=True)), 0.0)
        pt = e / jnp.maximum(jnp.sum(e, axis=0, keepdims=True), 1e-30)
        oct = jnp.dot(vct_ref[0, 0, :, :nrow], pt.astype(BF16), preferred_element_type=F32)
        for m in range(mh):
            oc_ref[m * qb:(m + 1) * qb, :] = oct[:, m * qb:(m + 1) * qb].T
        psum = (pt[:, 0:qb] + pt[:, qb:2 * qb]) + (pt[:, 2 * qb:3 * qb] + pt[:, 3 * qb:4 * qb])
        phi, plo = _split2(psum)
        ovt = ovt_ref[:, :nrow]
        imp_ref[...] = (jnp.dot(ovt, phi, preferred_element_type=F32)
                        + jnp.dot(ovt, plo, preferred_element_type=F32))

    n_chunks = n_cmp_rows // LANES
    last_visible = (q0 + qb - CMP_BLOCK) // CMP_STRIDE
    need = jnp.clip(last_visible // LANES + 1, 1, n_chunks)
    for nch in range(1, n_chunks + 1):
        pl.when(need == nch)(functools.partial(cmp_branch, nch * LANES))

    def select_blocks(nrow):
        jb = lax.broadcasted_iota(jnp.int32, (nrow, qb), 0)
        tq = q0 + lax.broadcasted_iota(jnp.int32, (nrow, qb), 1)
        cur = tq >> 6
        forced = (jb == 0) | (jb == cur) | (jb == cur - 1)
        causal = (jb << 6) <= tq
        score = jnp.where(causal, jnp.where(forced, FORCE, imp_ref[:nrow, :]), -jnp.inf)
        notsel_t = jnp.ones((nrow, qb), F32)
        jbf = jb.astype(F32)
        for _ in range(SEL_TOPK):
            smx = jnp.max(score, axis=0, keepdims=True)
            ix = jnp.min(jnp.where(score == smx, jbf, float(LANES)), axis=0, keepdims=True)
            hit = jbf == ix
            notsel_t = jnp.where(hit, 0.0, notsel_t)
            score = jnp.where(hit, -jnp.inf, score)
        notsel_ref[:nrow, :] = notsel_t
        if nrow < LANES:
            notsel_ref[nrow:, :] = jnp.ones((LANES - nrow, qb), F32)

    sel_chunk = 32
    n_sel_chunks = LANES // sel_chunk
    sel_need = jnp.clip(((q0 + qb - 1) // SEL_BLOCK) // sel_chunk + 1, 1, n_sel_chunks)
    for nch in range(1, n_sel_chunks + 1):
        pl.when(sel_need == nch)(functools.partial(select_blocks, nch * sel_chunk))
    notsel_b = notsel_ref[...].T.astype(BF16)
    n_grp = mh // grp
    q_aug = [jnp.concatenate([jnp.concatenate([qh[m], notsel_b], axis=1) for m in range(c * grp, (c + 1) * grp)], axis=0)
             for c in range(n_grp)]
    tpos = q0 + lax.broadcasted_iota(jnp.int32, (qb, 1), 0)
    tpos_g = jnp.concatenate([tpos] * grp, axis=0)

    def sel_tiles(first, n_sub):
        ks, vs = [], []
        for t in range(n_sub):
            k0 = pl.multiple_of((first + t) * tk_sel, tk_sel)
            k_aug = jnp.concatenate([ks_ref[pl.ds(k0, tk_sel), :], e_ref[pl.ds(k0, tk_sel), :]], axis=1)
            ks.append([k_aug] * n_grp)
            vs.append([vs_ref[pl.ds(k0, tk_sel), :]] * n_grp)
        return ks, vs

    def sel_step(first, n_sub, state):
        ks, vs = sel_tiles(first, n_sub)
        return _attend(q_aug, ks, vs, state, skew)

    def sel_tail(first, n_rem, state):
        def mask(t, c, sc):
            if t < n_rem:
                return sc
            kpos = (first + n_rem) * tk_sel + lax.broadcasted_iota(jnp.int32, (grp * qb, tk_sel), 1)
            return jnp.where(kpos <= tpos_g, sc, NEG)

        ks, vs = sel_tiles(first, n_rem + 1)
        return _attend(q_aug, ks, vs, state, skew, mask=mask)

    state = tuple((jnp.full((grp * qb, 1), NEG, F32), jnp.zeros((grp * qb, LANES), F32)) for _ in range(n_grp))
    n_below = q0 // tk_sel
    state = _tile_loops(n_below, sub, sel_step, sel_tail, state)
    sel_acc = [state[m // grp][1][(m % grp) * qb:(m % grp + 1) * qb] for m in range(mh)]

    wk = WINDOW + qb
    k0w = pl.multiple_of(jnp.maximum(i - WINDOW // qb, 0) * qb, qb)
    kwt = kw_ref[pl.ds(k0w, wk), :]
    vwt = vw_ref[pl.ds(k0w, wk), :]
    kposw = k0w + lax.broadcasted_iota(jnp.int32, (grp * qb, wk), 1)
    wmask = (kposw <= tpos_g) & (kposw > tpos_g - WINDOW)
    q_grp = [jnp.concatenate(qh[c * grp:(c + 1) * grp], axis=0) for c in range(n_grp)]

    def win_scores(c):
        return jnp.where(wmask, _dot_nt(q_grp[c], kwt), NEG)

    def win_out(c, sc):
        pe = jnp.exp2(sc - jnp.max(sc, axis=-1, keepdims=True))
        return jnp.dot(pe.astype(BF16), vwt, preferred_element_type=F32)

    win_grp = _pipelined(n_grp, win_scores, win_out, min(skew, n_grp - 1))
    win_acc = [win_grp[m // grp][(m % grp) * qb:(m % grp + 1) * qb] for m in range(mh)]

    sig = _sigmoid(gz_ref[...])
    den_lane = jl == HEAD_DIM
    heads = []
    for m in range(mh):
        col = g * mh + m
        gates = [jnp.sum(jnp.where(jl == br * N_HEADS + col, sig, 0.0), axis=-1, keepdims=True)
                 for br in range(3)]
        acc_s = sel_acc[m]
        acc_w = win_acc[m]
        den_s = jnp.sum(jnp.where(den_lane, acc_s, 0.0), axis=-1, keepdims=True)
        den_w = jnp.sum(jnp.where(den_lane, acc_w, 0.0), axis=-1, keepdims=True)
        heads.append(gates[0] * oc_ref[m * qb:(m + 1) * qb, :] + (gates[1] / den_s) * acc_s
                     + (gates[2] / den_w) * acc_w)
    lane_lo = jl < HEAD_DIM
    o_ref[:, 0:LANES] = jnp.where(lane_lo, heads[0], pltpu.roll(heads[1], HEAD_DIM, 1)).astype(o_ref.dtype)
    o_ref[:, LANES:2 * LANES] = jnp.where(lane_lo, heads[2], pltpu.roll(heads[3], HEAD_DIM, 1)).astype(o_ref.dtype)


def _nsa_attn(qn, gz, kvc, overlap, kvd, *, batch, seq, skew=2, grp=2, sub=4):
    n = qn.shape[0]
    qb = NSA_QB
    nq = seq // qb
    g = NSA_GROUPS
    rows = HEADS_PER_GROUP * qb
    n_cmp_rows = kvc.shape[3]
    tk_sel = min(512, seq)
    assert seq >= WINDOW + qb and seq % tk_sel == 0 and tk_sel % qb == 0
    in_specs = [
        pl.BlockSpec((qb, HEADS_PER_GROUP * LANES), lambda b, gg, i: (b * nq + i, gg)),
        pl.BlockSpec((qb, LANES), lambda b, gg, i: (b * nq + i, 0)),
        pl.BlockSpec((1, 1, 1, n_cmp_rows, LANES), lambda b, gg, i: (b, 0, gg, 0, 0)),
        pl.BlockSpec((1, 1, LANES, n_cmp_rows), lambda b, gg, i: (b, gg, 0, 0)),
        pl.BlockSpec((LANES, n_cmp_rows), lambda b, gg, i: (0, 0)),
        pl.BlockSpec((seq, LANES), lambda b, gg, i: (b, 0 * g + gg)),
        pl.BlockSpec((seq, LANES), lambda b, gg, i: (b, 1 * g + gg)),
        pl.BlockSpec((seq, LANES), lambda b, gg, i: (b, 2 * g + gg)),
        pl.BlockSpec((seq, LANES), lambda b, gg, i: (b, 3 * g + gg)),
    ]
    return pl.pallas_call(
        functools.partial(_nsa_attn_kernel, n_cmp_rows=n_cmp_rows, tk_sel=tk_sel, skew=skew, grp=grp, sub=sub),
        out_shape=jax.ShapeDtypeStruct((n, N_HEADS * HEAD_DIM), BF16),
        grid=(batch, g, nq),
        in_specs=in_specs,
        out_specs=pl.BlockSpec((qb, HEADS_PER_GROUP * HEAD_DIM), lambda b, gg, i: (b * nq + i, gg)),
        scratch_shapes=[pltpu.VMEM((seq, LANES), BF16), pltpu.VMEM((rows, LANES), F32), pltpu.VMEM((LANES, qb), F32),
                        pltpu.VMEM((LANES, qb), F32)],
        compiler_params=_cparams(("parallel", "parallel", "arbitrary")),
        name="nsa_attn",
    )(qn, gz, kvc, jnp.swapaxes(kvc[:, 1], -1, -2), overlap.T, kvd, kvd, kvd, kvd)


def _pad_cols(w, width=LANES):
    d, f = w.shape
    return jnp.zeros((d, width), w.dtype).at[:, :f].set(w)


def _tile_gain(gain, n_heads, scale=1.0, dup=False):
    g = gain.astype(F32) * scale
    second = g if dup else jnp.zeros_like(g)
    return jnp.tile(jnp.concatenate([g, second]), n_heads)


def _fox_aug_tables():
    hw = N_HEADS * LANES
    piece = jnp.arange(3)[:, None]
    head = jnp.arange(N_HEADS)[None, :]
    src = (piece * N_HEADS + head).reshape(-1)
    q_dst = (head * LANES + HEAD_DIM + piece).reshape(-1)
    k_dst = (hw + head * LANES + HEAD_DIM + 3 + piece).reshape(-1)
    place = jnp.zeros((LANES, 3 * hw), F32).at[src, q_dst].set(1.0).at[src, k_dst].set(-1.0)
    lane = jnp.arange(LANES)
    q_bias = ((lane >= HEAD_DIM + 3) & (lane < HEAD_DIM + 6)).astype(F32)
    k_bias = ((lane >= HEAD_DIM) & (lane < HEAD_DIM + 3)).astype(F32)
    v_bias = (lane == HEAD_DIM).astype(F32)
    bias = jnp.concatenate([jnp.tile(q_bias, N_HEADS), jnp.tile(k_bias, N_HEADS), jnp.tile(v_bias, N_HEADS)])
    return place.astype(BF16), bias


def _fox_layer(x, batch, seq, attn_g, w_in, b_f, qk_gain, w_out):
    hd = N_HEADS * HEAD_DIM
    hw = N_HEADS * LANES
    scale = HEAD_DIM ** -0.5
    z = _rms_proj(x, attn_g, _pad_cols(w_in[:, 3 * hd:]).astype(BF16), out_dtype=F32, seq=seq)
    b_pad = jnp.zeros((1, LANES), F32).at[0, :N_HEADS].set(b_f.astype(F32))
    c3 = _fox_c(z, b_pad, batch=batch, seq=seq)
    w_bf = w_in.astype(BF16)
    gain = jnp.concatenate([_tile_gain(qk_gain[0], N_HEADS, scale * LOG2E), _tile_gain(qk_gain[1], N_HEADS),
                            jnp.zeros((hw,), F32)])
    tn = 1024
    place, bias = _fox_aug_tables()
    qkv = _rms_proj(x, attn_g, w_bf, out_dtype=BF16, seq=seq, tn=tn, gain=gain, w_cols=3 * hd,
                    norm_blocks=tuple(range(2 * hw // tn)), placed=(c3, place), col_bias=bias, expand=True)
    o = _fox_attn(qkv, batch=batch, seq=seq)
    return _matmul_res(o, w_out.astype(BF16), x)


def _nsa_shared_kv(x, batch, seq, kv_norm, kv_w, kv_k_gain, cmp_pos, cmp_w1, cmp_w2):
    g = NSA_GROUPS
    gw = g * HEAD_DIM
    tabs = _rope_tables(jnp.arange(seq))
    kv_bf = kv_w.astype(BF16)
    zero = jnp.zeros((g * LANES,), F32)
    one_lane = jnp.tile((jnp.arange(LANES) == HEAD_DIM).astype(F32), g)
    gain_kvd = jnp.concatenate([_tile_gain(kv_k_gain[1], g), zero, _tile_gain(kv_k_gain[2], g), zero])
    bias_kvd = jnp.concatenate([zero, one_lane, zero, one_lane])
    kvd = _rms_proj(x, kv_norm, kv_bf, out_dtype=BF16, seq=seq, tn=g * LANES, gain=gain_kvd, w_col0=2 * gw,
                    norm_blocks=(0, 2), rope_tabs=tabs, col_bias=bias_kvd, expand=True)
    zc = _rms_proj(x, kv_norm, kv_bf, out_dtype=F32, seq=seq, w_cols=2 * gw)
    rows = seq // CMP_STRIDE
    zr = zc.reshape(batch, rows, CMP_STRIDE, 2, g, HEAD_DIM).transpose(0, 3, 4, 1, 2, 5)
    zr = zr.reshape(batch, 2, g, rows, CMP_STRIDE * HEAD_DIM)
    cmp_end = jnp.arange(rows) * CMP_STRIDE + CMP_BLOCK - 1
    ctabs = _rope_tables(cmp_end)
    w2dup = jnp.concatenate([cmp_w2, cmp_w2], axis=-1).astype(BF16)
    kvc = _compress(zr, cmp_w1.astype(BF16), cmp_pos.reshape(2, 1, CMP_BLOCK * HEAD_DIM).astype(F32), w2dup,
                    _tile_gain(kv_k_gain[0], 1, dup=True).reshape(1, LANES), ctabs, batch=batch)
    return kvc, kvd


def _nsa_layer(x, batch, seq, attn_g, w_in, q_gain, w_out, kvc, kvd):
    hd = N_HEADS * HEAD_DIM
    scale = HEAD_DIM ** -0.5
    rows = seq // CMP_STRIDE
    tabs = _rope_tables(jnp.arange(seq))
    qn = _rms_proj(x, attn_g, w_in.astype(BF16), out_dtype=BF16, seq=seq, expand=True, w_cols=hd,
                   gain=_tile_gain(q_gain, N_HEADS, scale * LOG2E), norm_blocks=tuple(range(N_HEADS * LANES // 512)),
                   rope_tabs=tabs)
    gz = _rms_proj(x, attn_g, _pad_cols(w_in[:, hd:]).astype(BF16), out_dtype=F32, seq=seq)

    n_sel = seq // SEL_BLOCK
    assert n_sel <= LANES
    cmp_start = jnp.arange(rows) * CMP_STRIDE
    sel_start = jnp.arange(LANES) * SEL_BLOCK
    overlap = jnp.clip(jnp.minimum(cmp_start[:, None] + CMP_BLOCK, sel_start[None, :] + SEL_BLOCK)
                       - jnp.maximum(cmp_start[:, None], sel_start[None, :]), 0)
    overlap = jnp.where((jnp.arange(rows) < rows - 1)[:, None] & (jnp.arange(LANES) < n_sel)[None, :], overlap, 0)
    o = _nsa_attn(qn, gz, kvc, overlap.astype(BF16), kvd, batch=batch, seq=seq)
    return _matmul_res(o, w_out.astype(BF16), x)


def kernel(x, p, attn_norm, ffn_norm, ple_norm, ple_gate_w, ple_proj_w, router_w, router_b, w_gu, b_gu, w_down, b_down, fox_w_in, fox_b_f, fox_qk_gain, fox_w_out, kv_norm, kv_w, kv_k_gain, cmp_pos, cmp_w1, cmp_w2, nsa_w_in, nsa_q_gain, nsa_w_out):
    batch, seq, d = x.shape
    depth = p.shape[0]
    n_a = fox_w_in.shape[0]
    xt = x.reshape(batch * seq, d)
    w_gu_bf = w_gu.astype(BF16)
    w_down_bf = w_down.astype(BF16)
    shared = None
    for layer in range(depth):
        if layer == n_a:
            shared = _nsa_shared_kv(xt, batch, seq, kv_norm, kv_w, kv_k_gain, cmp_pos, cmp_w1, cmp_w2)
        if layer < n_a:
            xt = _fox_layer(xt, batch, seq, attn_norm[layer], fox_w_in[layer], fox_b_f[layer],
                            fox_qk_gain[layer], fox_w_out[layer])
        else:
            i = layer - n_a
            xt = _nsa_layer(xt, batch, seq, attn_norm[layer], nsa_w_in[i], nsa_q_gain[i], nsa_w_out[i], *shared)
        y = _moe_layer(xt, ffn_norm[layer], router_w[layer], router_b[layer], w_gu_bf, b_gu[layer],
                       w_down_bf, b_down[layer], layer)
        xt = _ple(xt, y, ple_norm[layer], ple_gate_w[layer], p[layer].reshape(batch * seq, -1), ple_proj_w[layer])
    return xt.reshape(batch, seq, d)
```

```python
import functools

import jax
import jax.numpy as jnp
from jax import lax
from jax.experimental import pallas as pl
from jax.experimental.pallas import tpu as pltpu

F32 = jnp.float32
BF16 = jnp.bfloat16

N_HEADS = 16
HEAD_DIM = 64
ROPE_DIM = HEAD_DIM // 4
ROPE_THETA = 500000.0
NSA_GROUPS = 4
HEADS_PER_GROUP = N_HEADS // NSA_GROUPS
CMP_BLOCK = 32
CMP_STRIDE = 16
SEL_BLOCK = 64
SEL_TOPK = 16
WINDOW = 512
N_EXPERTS = 32
TOP_K = 4
SWIGLU_LIMIT = 7.0
SWIGLU_ALPHA = 1.702
EPS = 1e-6
NEG = -1e30
FORCE = 1e6
LOG2E = 1.4426950408889634

LANES = 128
NSA_QB = 256
MOE_BM = 256
VMEM_LIMIT = 56 * 1024 * 1024


def _cparams(sem):
    return pltpu.CompilerParams(dimension_semantics=sem, vmem_limit_bytes=VMEM_LIMIT)


def _sigmoid(x):
    return 1.0 / (1.0 + jnp.exp(-x))


def _dot_nt(a, b):
    return lax.dot_general(a, b, (((1,), (1,)), ((), ())), preferred_element_type=F32)


def _split2(x):
    hi = x.astype(BF16)
    lo = (x - hi.astype(F32)).astype(BF16)
    return hi, lo


def _proj_kernel(*refs, norm_blocks, rope, placed, biased, expand, head_div, tn, out_dtype):
    refs = list(refs)
    x_ref, g_ref, w_ref = refs[:3]
    del refs[:3]
    if norm_blocks:
        gain_ref = refs.pop(0)
    if rope:
        c_ref, s1_ref, s2_ref = refs[:3]
        del refs[:3]
    if placed:
        ex_ref, place_ref = refs[:2]
        del refs[:2]
    if biased:
        bias_ref = refs.pop(0)
    o_ref, h_ref = refs
    j = pl.program_id(1)
    aug = placed or biased

    @pl.when(j == 0)
    def _():
        xf = x_ref[...]
        ms = jnp.mean(xf * xf, axis=-1, keepdims=True)
        h_ref[...] = (xf * lax.rsqrt(ms + EPS) * g_ref[...]).astype(BF16)

    acc = jnp.dot(h_ref[...], w_ref[...], preferred_element_type=F32)
    if placed and biased:
        extra = jnp.dot(ex_ref[...], place_ref[...], preferred_element_type=F32) + bias_ref[...]
    elif biased:
        extra = jnp.broadcast_to(bias_ref[...], (acc.shape[0], tn))
    if expand:
        lo = lax.broadcasted_iota(jnp.int32, (acc.shape[0], LANES), 1) < HEAD_DIM
        blocks = []
        for c in range(acc.shape[1] // LANES):
            a = acc[:, c * LANES:(c + 1) * LANES]
            blocks += [jnp.where(lo, a, 0.0), jnp.where(lo, pltpu.roll(a, HEAD_DIM, 1), 0.0)]
    else:
        blocks = [acc[:, c * LANES:(c + 1) * LANES] for c in range(acc.shape[1] // LANES)]

    def plain():
        for c, a in enumerate(blocks):
            sl = slice(c * LANES, (c + 1) * LANES)
            o_ref[:, sl] = ((a + extra[:, sl]) if aug else a).astype(out_dtype)

    if not norm_blocks:
        plain()
        return
    is_norm = functools.reduce(jnp.logical_or, [j == c for c in norm_blocks])

    @pl.when(is_norm)
    def _():
        for c, a in enumerate(blocks):
            sl = slice(c * LANES, (c + 1) * LANES)
            ss = jnp.sum(a * a, axis=-1, keepdims=True) * (1.0 / head_div)
            y = a * lax.rsqrt(ss + EPS) * gain_ref[:, sl]
            if rope:
                y = (y * c_ref[...] + pltpu.roll(y, LANES - ROPE_DIM // 2, 1) * s1_ref[...]
                     + pltpu.roll(y, ROPE_DIM // 2, 1) * s2_ref[...])
            if aug:
                y = y + extra[:, sl]
            o_ref[:, sl] = y.astype(out_dtype)

    @pl.when(jnp.logical_not(is_norm))
    def _():
        plain()


def _rms_proj(x, g, w, *, out_dtype, seq, tm=2048, tn=512, gain=None, norm_blocks=(),
              rope_tabs=None, placed=None, col_bias=None, expand=False, head_div=HEAD_DIM,
              w_col0=0, w_cols=None):
    n, d = x.shape
    w_cols = w.shape[1] - w_col0 if w_cols is None else w_cols
    f = w_cols * (2 if expand else 1)
    tn = min(tn, f)
    tw = tn // 2 if expand else tn
    assert w_col0 % tw == 0
    j0 = w_col0 // tw
    tm = min(tm, seq)
    assert n % tm == 0 and f % tn == 0 and seq % tm == 0
    nt = seq // tm
    in_specs = [
        pl.BlockSpec((tm, d), lambda i, j: (i, 0)),
        pl.BlockSpec((1, d), lambda i, j: (0, 0)),
        pl.BlockSpec((d, tw), lambda i, j: (0, j0 + j)),
    ]
    args = [x, g.reshape(1, d).astype(F32), w]
    if norm_blocks:
        in_specs.append(pl.BlockSpec((1, tn), lambda i, j: (0, j)))
        args.append(gain.reshape(1, f).astype(F32))
    if rope_tabs is not None:
        assert norm_blocks
        for tab in rope_tabs:
            in_specs.append(pl.BlockSpec((tm, LANES), lambda i, j: (i % nt, 0)))
            args.append(tab)
    if placed is not None:
        assert col_bias is not None
        values, place = placed
        in_specs += [pl.BlockSpec((tm, LANES), lambda i, j: (i, 0)),
                     pl.BlockSpec((LANES, tn), lambda i, j: (0, j))]
        args += [values, place]
    if col_bias is not None:
        in_specs.append(pl.BlockSpec((1, tn), lambda i, j: (0, j)))
        args.append(col_bias.reshape(1, f).astype(F32))
    kern = functools.partial(_proj_kernel, norm_blocks=tuple(norm_blocks), rope=rope_tabs is not None,
                             placed=placed is not None, biased=col_bias is not None, expand=expand,
                             head_div=float(head_div), tn=tn, out_dtype=out_dtype)
    return pl.pallas_call(
        kern,
        out_shape=jax.ShapeDtypeStruct((n, f), out_dtype),
        grid=(n // tm, f // tn),
        in_specs=in_specs,
        out_specs=pl.BlockSpec((tm, tn), lambda i, j: (i, j)),
        scratch_shapes=[pltpu.VMEM((tm, d), BF16)],
        compiler_params=_cparams(("parallel", "arbitrary")),
        name="rms_proj",
    )(*args)


def _rope_tables(pos, width=LANES):
    half = ROPE_DIM // 2
    inv = jnp.power(jnp.float32(ROPE_THETA), -jnp.arange(0, ROPE_DIM, 2, dtype=F32) / ROPE_DIM)
    ang = pos.astype(F32)[:, None] * inv[None, :]
    cos, sin = jnp.cos(ang), jnp.sin(ang)
    t = pos.shape[0]
    ones = jnp.ones((t, HEAD_DIM - ROPE_DIM), F32)
    zeros_h = jnp.zeros((t, half), F32)
    zeros_r = jnp.zeros((t, HEAD_DIM - ROPE_DIM), F32)
    c = jnp.concatenate([cos, cos, ones], axis=1)
    s1 = jnp.concatenate([-sin, zeros_h, zeros_r], axis=1)
    s2 = jnp.concatenate([zeros_h, sin, zeros_r], axis=1)
    rep = width // HEAD_DIM
    return tuple(jnp.tile(a, (1, rep)) for a in (c, s1, s2))


def _matmul_res_kernel(a_ref, w_ref, x_ref, o_ref):
    o_ref[...] = x_ref[...] + jnp.dot(a_ref[...], w_ref[...], preferred_element_type=F32)


def _matmul_res(a, w, x, *, tm=512):
    n, k = a.shape
    f = w.shape[1]
    tm = min(tm, n)
    return pl.pallas_call(
        _matmul_res_kernel,
        out_shape=jax.ShapeDtypeStruct((n, f), F32),
        grid=(n // tm,),
        in_specs=[
            pl.BlockSpec((tm, k), lambda i: (i, 0)),
            pl.BlockSpec((k, f), lambda i: (0, 0)),
            pl.BlockSpec((tm, f), lambda i: (i, 0)),
        ],
        out_specs=pl.BlockSpec((tm, f), lambda i: (i, 0)),
        compiler_params=_cparams(("parallel",)),
        name="matmul_res",
    )(a, w, x)


def _fox_c_kernel(z_ref, b_ref, c_ref, carry_ref, *, tm):
    t = pl.program_id(1)

    @pl.when(t == 0)
    def _():
        carry_ref[...] = jnp.zeros_like(carry_ref)

    u = z_ref[...] + b_ref[...]
    logf = jnp.minimum(u, 0.0) - jnp.log(1.0 + jnp.exp(-jnp.abs(u)))
    row = lax.broadcasted_iota(jnp.int32, (tm, tm), 0)
    col = lax.broadcasted_iota(jnp.int32, (tm, tm), 1)
    tri = (row >= col).astype(BF16)
    hi = logf.astype(BF16)
    r1 = logf - hi.astype(F32)
    mid = r1.astype(BF16)
    lo = (r1 - mid.astype(F32)).astype(BF16)
    cs = (jnp.dot(tri, hi, preferred_element_type=F32) + jnp.dot(tri, mid, preferred_element_type=F32)
          + jnp.dot(tri, lo, preferred_element_type=F32))
    c = cs + carry_ref[...]
    carry_ref[...] = c[tm - 1:tm, :]
    c2 = c * LOG2E
    p0 = c2.astype(BF16).astype(F32)
    p1 = (c2 - p0).astype(BF16).astype(F32)
    p2 = ((c2 - p0) - p1).astype(BF16).astype(F32)
    lane = lax.broadcasted_iota(jnp.int32, c.shape, 1)
    out = jnp.where(lane < N_HEADS, p0,
                    jnp.where(lane < 2 * N_HEADS, pltpu.roll(p1, N_HEADS, 1),
                              jnp.where(lane < 3 * N_HEADS, pltpu.roll(p2, 2 * N_HEADS, 1), 0.0)))
    c_ref[...] = out.astype(BF16)


def _fox_c(z, b_pad, *, batch, seq, tm=512):
    tm = min(tm, seq)
    nt = seq // tm
    return pl.pallas_call(
        functools.partial(_fox_c_kernel, tm=tm),
        out_shape=jax.ShapeDtypeStruct(z.shape, BF16),
        grid=(batch, nt),
        in_specs=[
            pl.BlockSpec((tm, LANES), lambda b, t: (b * nt + t, 0)),
            pl.BlockSpec((1, LANES), lambda b, t: (0, 0)),
        ],
        out_specs=pl.BlockSpec((tm, LANES), lambda b, t: (b * nt + t, 0)),
        scratch_shapes=[pltpu.VMEM((1, LANES), F32)],
        compiler_params=_cparams(("parallel", "arbitrary")),
        name="fox_cumsum",
    )(z, b_pad)


def _attend(qs, ks, vs, state, skew, mask=None):
    items = [(t, c) for t in range(len(ks)) for c in range(len(qs))]
    state = list(state)
    pending = {}
    depth = min(skew, len(items) - 1)
    for idx in range(len(items) + depth):
        if idx < len(items):
            t, c = items[idx]
            sc = _dot_nt(qs[c], ks[t][c])
            pending[idx] = sc if mask is None else mask(t, c, sc)
        if idx >= depth:
            t, c = items[idx - depth]
            sc = pending.pop(idx - depth)
            m_prev, acc = state[c]
            m_new = jnp.maximum(m_prev, jnp.max(sc, axis=-1, keepdims=True))
            pe = jnp.exp2(sc - m_new)
            acc = jnp.exp2(m_prev - m_new) * acc + jnp.dot(pe.astype(BF16), vs[t][c], preferred_element_type=F32)
            state[c] = (m_new, acc)
    return tuple(state)


def _tile_loops(n_tiles, sub, step, tail, state):
    n_full = n_tiles // sub
    state = lax.fori_loop(0, n_full, lambda j, st: step(j * sub, sub, st), state)
    first = n_full * sub
    return lax.switch(n_tiles - first, [functools.partial(tail, first, r) for r in range(sub)], state)


def _fox_attn_kernel(q0_ref, q1_ref, k0_ref, k1_ref, v0_ref, v1_ref, o_ref, *, tq, tk, skew, sub):
    qi = pl.program_id(2)
    k_refs = (k0_ref, k1_ref)
    v_refs = (v0_ref, v1_ref)
    qs = [q0_ref[...], q1_ref[...]]

    def tiles(refs, first, n_sub):
        return [[r[pl.ds(pl.multiple_of((first + t) * tk, tk), tk), :] for r in refs] for t in range(n_sub)]

    def full_step(first, n_sub, state):
        return _attend(qs, tiles(k_refs, first, n_sub), tiles(v_refs, first, n_sub), state, skew)

    def causal(t, c, sc):
        tpos = lax.broadcasted_iota(jnp.int32, (tq, tk), 0)
        kpos = t * tk + lax.broadcasted_iota(jnp.int32, (tq, tk), 1)
        return jnp.where(kpos <= tpos, sc, NEG)

    n_diag = tq // tk

    def tail_step(first, n_rem, state):
        def mask(t, c, sc):
            return causal(t - n_rem, c, sc) if t >= n_rem else sc

        n = n_rem + n_diag
        return _attend(qs, tiles(k_refs, first, n), tiles(v_refs, first, n), state, skew, mask=mask)

    state = tuple((jnp.full((tq, 1), NEG, F32), jnp.zeros((tq, LANES), F32)) for _ in qs)
    n_below = (qi * tq) // tk
    state = _tile_loops(n_below, sub, full_step, tail_step, state)

    lane = lax.broadcasted_iota(jnp.int32, (tq, LANES), 1)
    outs = []
    for _, acc in state:
        den = jnp.sum(jnp.where(lane == HEAD_DIM, acc, 0.0), axis=-1, keepdims=True)
        outs.append(acc / den)
    o_ref[...] = jnp.where(lane < HEAD_DIM, outs[0], pltpu.roll(outs[1], HEAD_DIM, 1)).astype(o_ref.dtype)


def _fox_attn(qkv, *, batch, seq, tq=512, tk=512, skew=2, sub=4):
    n = qkv.shape[0]
    tq = min(tq, seq)
    tk = min(tk, tq)
    assert seq % tq == 0 and tq % tk == 0
    nq = seq // tq
    hp = N_HEADS // 2
    kb = N_HEADS
    vb = 2 * N_HEADS
    in_specs = [
        pl.BlockSpec((tq, LANES), lambda b, h, i: (b * nq + i, 2 * h)),
        pl.BlockSpec((tq, LANES), lambda b, h, i: (b * nq + i, 2 * h + 1)),
        pl.BlockSpec((seq, LANES), lambda b, h, i: (b, kb + 2 * h)),
        pl.BlockSpec((seq, LANES), lambda b, h, i: (b, kb + 2 * h + 1)),
        pl.BlockSpec((seq, LANES), lambda b, h, i: (b, vb + 2 * h)),
        pl.BlockSpec((seq, LANES), lambda b, h, i: (b, vb + 2 * h + 1)),
    ]
    return pl.pallas_call(
        functools.partial(_fox_attn_kernel, tq=tq, tk=tk, skew=skew, sub=sub),
        out_shape=jax.ShapeDtypeStruct((n, N_HEADS * HEAD_DIM), BF16),
        grid=(batch, hp, nq),
        in_specs=in_specs,
        out_specs=pl.BlockSpec((tq, LANES), lambda b, h, i: (b * nq + i, h)),
        compiler_params=_cparams(("parallel", "parallel", "arbitrary")),
        name="fox_attn",
    )(qkv, qkv, qkv, qkv, qkv, qkv)


def _router_kernel(x_ref, g_ref, whi_ref, wlo_ref, b_ref, h_ref, idx_ref, gate_ref, cnt_ref):
    xf = x_ref[...]
    ms = jnp.mean(xf * xf, axis=-1, keepdims=True)
    h = xf * lax.rsqrt(ms + EPS) * g_ref[...]
    nc = h.shape[1] // LANES
    for c in range(nc):
        h_ref[pl.ds(c, h.shape[0], stride=nc), :] = h[:, c * LANES:(c + 1) * LANES]
    hhi, hlo = _split2(h)
    whi = whi_ref[...]
    logits = (jnp.dot(hhi, whi, preferred_element_type=F32) + jnp.dot(hlo, whi, preferred_element_type=F32)
              + jnp.dot(hhi, wlo_ref[...], preferred_element_type=F32)) + b_ref[...]
    lane = lax.broadcasted_iota(jnp.int32, logits.shape, 1).astype(F32)
    vals, idxs = [], []
    cur = logits
    for _ in range(TOP_K):
        mx = jnp.max(cur, axis=-1, keepdims=True)
        ix = jnp.min(jnp.where(cur == mx, lane, float(LANES)), axis=-1, keepdims=True)
        vals.append(mx)
        idxs.append(ix)
        cur = jnp.where(lane == ix, -jnp.inf, cur)
    es = [jnp.exp(v - vals[0]) for v in vals]
    tot = es[0] + es[1] + es[2] + es[3]
    idx_out = jnp.zeros(logits.shape, F32)
    gate_out = jnp.zeros(logits.shape, F32)
    for r in range(TOP_K):
        idx_out = jnp.where(lane == r, idxs[r], idx_out)
        gate_out = jnp.where(lane == r, es[r] / tot, gate_out)
    idx_ref[...] = idx_out.astype(jnp.int32)
    gate_ref[...] = gate_out
    hits = sum(jnp.sum((lane == ix).astype(F32), axis=0, keepdims=True) for ix in idxs)

    @pl.when(pl.program_id(0) == 0)
    def _():
        cnt_ref[...] = jnp.zeros(cnt_ref.shape, cnt_ref.dtype)

    cnt_ref[...] += jnp.broadcast_to(hits, cnt_ref.shape)


def _router(x, g, w, b, *, tm=512):
    n, d = x.shape
    e = w.shape[1]
    tm = min(tm, n)
    w_pad = jnp.zeros((d, LANES), F32).at[:, :e].set(w)
    whi = w_pad.astype(BF16)
    wlo = (w_pad - whi.astype(F32)).astype(BF16)
    b_pad = jnp.full((1, LANES), -jnp.inf, F32).at[0, :e].set(b.astype(F32))
    return pl.pallas_call(
        _router_kernel,
        out_shape=(jax.ShapeDtypeStruct((n * (d // LANES), LANES), F32), jax.ShapeDtypeStruct((n, LANES), jnp.int32),
                   jax.ShapeDtypeStruct((n, LANES), F32), jax.ShapeDtypeStruct((8, LANES), F32)),
        grid=(n // tm,),
        in_specs=[
            pl.BlockSpec((tm, d), lambda i: (i, 0)),
            pl.BlockSpec((1, d), lambda i: (0, 0)),
            pl.BlockSpec((d, LANES), lambda i: (0, 0)),
            pl.BlockSpec((d, LANES), lambda i: (0, 0)),
            pl.BlockSpec((1, LANES), lambda i: (0, 0)),
        ],
        out_specs=(pl.BlockSpec((tm * (d // LANES), LANES), lambda i: (i, 0)), pl.BlockSpec((tm, LANES), lambda i: (i, 0)),
                   pl.BlockSpec((tm, LANES), lambda i: (i, 0)), pl.BlockSpec((8, LANES), lambda i: (0, 0))),
        compiler_params=_cparams(("arbitrary",)),
        name="moe_router",
    )(x, g.reshape(1, d).astype(F32), whi, wlo, b_pad)


def _moe_expert_kernel(blk_exp_ref, tok_ref, dst_ref, live_ref, h_hbm, ga_ref, gb_ref, wgu_a, bgu_a, wd_a, bd_a,
                       wgu_b, bgu_b, wd_b, bd_b, y_hbm, xa, xb, oa, ob, gsem, ssem, *, bm, dff, n_blocks):
    i = pl.program_id(0)
    live = live_ref[0]
    nc = xa.shape[0] // bm
    dump = y_hbm.shape[0] // nc - 2 * bm

    def gather_rows(block, buf, sem, lo, hi):
        base = block * bm
        for r in range(lo, hi):
            pltpu.make_async_copy(h_hbm.at[pl.ds(pl.multiple_of(tok_ref[base + r], nc), nc)], buf.at[pl.ds(r * nc, nc)], sem).start()

    def scatter_rows(block, buf, sem, lo, hi):
        base = block * bm
        for r in range(lo, hi):
            pltpu.make_async_copy(buf.at[pl.ds(r * nc, nc)], y_hbm.at[pl.ds(pl.multiple_of(dst_ref[base + r], nc), nc)], sem).start()

    def wait_gather(buf, sem):
        pltpu.make_async_copy(h_hbm.at[pl.ds(0, bm * nc)], buf, sem).wait()

    def wait_scatter(buf, sem):
        pltpu.make_async_copy(buf, y_hbm.at[pl.ds(0, bm * nc)], sem).wait()

    n_piece = 12

    def experts(x_ref, o_ref, g_ref, wgu, bgu, wd, bd, between):
        x = jnp.concatenate([x_ref[pl.ds(c, bm, stride=nc), :] for c in range(nc)], axis=1).astype(BF16)
        ct = 2 * dff // 8
        gus = []
        for k in range(8):
            gus.append(jnp.dot(x, wgu[0, 0, :, k * ct:(k + 1) * ct], preferred_element_type=F32) + bgu[0][:, k * ct:(k + 1) * ct])
            between(k)
        gu = jnp.concatenate(gus, axis=1)
        a = jnp.minimum(gu[:, :dff], SWIGLU_LIMIT)
        u = jnp.clip(gu[:, dff:], -SWIGLU_LIMIT, SWIGLU_LIMIT)
        y = ((u + 1.0) * (a * _sigmoid(SWIGLU_ALPHA * a))).astype(BF16)
        co = (nc * LANES) // 4
        for k in range(4):
            out = (jnp.dot(y, wd[0, 0, :, k * co:(k + 1) * co], preferred_element_type=F32) + bd[0][:, k * co:(k + 1) * co]) * g_ref[...]
            for c in range(co // LANES):
                o_ref[pl.ds(k * (co // LANES) + c, bm, stride=nc), :] = out[:, c * LANES:(c + 1) * LANES]
            between(8 + k)

    def spread(k):
        per = -(-bm // n_piece)
        return min(k * per, bm), min((k + 1) * per, bm)

    @pl.when(i == 0)
    def _():
        gather_rows(0, xa, gsem.at[0], 0, bm)
        oa[...] = jnp.zeros(oa.shape, oa.dtype)
        ob[...] = jnp.zeros(ob.shape, ob.dtype)
        for r in range(bm):
            pltpu.make_async_copy(oa.at[pl.ds(r * nc, nc)], y_hbm.at[pl.ds((dump + r) * nc, nc)], ssem.at[0]).start()

    @pl.when(i < live)
    def _():
        blk_a = 2 * i
        blk_b = 2 * i + 1
        prev_b = jnp.where(i == 0, n_blocks, 2 * i - 1)
        nxt_a = jnp.minimum(2 * i + 2, 2 * live - 1)

        wait_gather(xa, gsem.at[0])
        wait_scatter(oa, ssem.at[0])

        def ride_a(k):
            lo, hi = spread(k)
            gather_rows(blk_b, xb, gsem.at[1], lo, hi)
            scatter_rows(prev_b, ob, ssem.at[1], lo, hi)

        experts(xa, oa, ga_ref, wgu_a, bgu_a, wd_a, bd_a, ride_a)

        wait_gather(xb, gsem.at[1])
        wait_scatter(ob, ssem.at[1])

        def ride_b(k):
            lo, hi = spread(k)
            scatter_rows(blk_a, oa, ssem.at[0], lo, hi)
            gather_rows(nxt_a, xa, gsem.at[0], lo, hi)

        experts(xb, ob, gb_ref, wgu_b, bgu_b, wd_b, bd_b, ride_b)

    @pl.when(i == live - 1)
    def _():
        scatter_rows(2 * i + 1, ob, ssem.at[1], 0, bm)
        wait_gather(xa, gsem.at[0])
        wait_scatter(oa, ssem.at[0])
        wait_scatter(ob, ssem.at[1])


def _moe_experts(h, blk_exp, tok, dst, live, gates, w_gu, b_gu, w_down, b_down, *, bm, layer):
    _, e, d, f2 = w_gu.shape
    nc = d // LANES
    n = h.shape[0] // nc
    dff = f2 // 2
    n_steps = (tok.shape[0] // bm - 1) // 2
    cap = n_steps * 2 * bm
    assert tok.shape[0] == cap + bm

    def wspecs(off):
        return [
            pl.BlockSpec((1, 1, d, f2), lambda i, be, tk, ds, lv: (layer, be[2 * i + off], 0, 0)),
            pl.BlockSpec((1, 1, f2), lambda i, be, tk, ds, lv: (be[2 * i + off], 0, 0)),
            pl.BlockSpec((1, 1, dff, d), lambda i, be, tk, ds, lv: (layer, be[2 * i + off], 0, 0)),
            pl.BlockSpec((1, 1, d), lambda i, be, tk, ds, lv: (be[2 * i + off], 0, 0)),
        ]

    grid_spec = pltpu.PrefetchScalarGridSpec(
        num_scalar_prefetch=4,
        grid=(n_steps,),
        in_specs=[
            pl.BlockSpec(memory_space=pl.ANY),
            pl.BlockSpec((bm, 1), lambda i, be, tk, ds, lv: (2 * i, 0)),
            pl.BlockSpec((bm, 1), lambda i, be, tk, ds, lv: (2 * i + 1, 0)),
        ] + wspecs(0) + wspecs(1),
        out_specs=pl.BlockSpec(memory_space=pl.ANY),
        scratch_shapes=[pltpu.VMEM((bm * nc, LANES), F32)] * 4 + [pltpu.SemaphoreType.DMA((2,)),
                                                                  pltpu.SemaphoreType.DMA((2,))],
    )
    bgu3 = b_gu.reshape(e, 1, f2).astype(F32)
    bd3 = b_down.reshape(e, 1, d).astype(F32)
    g2 = gates.reshape(cap, 1)
    return pl.pallas_call(
        functools.partial(_moe_expert_kernel, bm=bm, dff=dff, n_blocks=2 * n_steps),
        out_shape=jax.ShapeDtypeStruct(((TOP_K * n + 2 * bm) * nc, LANES), F32),
        grid_spec=grid_spec,
        compiler_params=_cparams(("arbitrary",)),
        name="moe_experts",
    )(blk_exp, tok, dst, live, h, g2, g2, w_gu, bgu3, w_down, bd3, w_gu, bgu3, w_down, bd3)


def _moe_layer(x, norm_g, router_w, router_b, w_gu, b_gu, w_down, b_down, layer):
    n, d = x.shape
    bm = MOE_BM
    h, idx128, gate128, cnt = _router(x, norm_g, router_w, router_b)
    n_assign = n * TOP_K
    flat_e = idx128[:, :TOP_K].reshape(-1)
    flat_g = gate128[:, :TOP_K].reshape(-1)
    order = jnp.argsort(flat_e).astype(jnp.int32)
    counts = cnt[0, :N_EXPERTS].astype(jnp.int32)
    padded = (counts + bm - 1) // bm * bm
    start = jnp.cumsum(counts) - counts
    pend = jnp.cumsum(padded)
    pstart = pend - padded
    n_blocks = -(-n_assign // bm) + N_EXPERTS
    n_blocks += n_blocks % 2
    blk_start = jnp.arange(n_blocks, dtype=jnp.int32) * bm
    blk_exp = jnp.minimum(jnp.sum((pend[None, :] <= blk_start[:, None]).astype(jnp.int32), axis=1),
                          N_EXPERTS - 1).astype(jnp.int32)
    row = jnp.arange(bm, dtype=jnp.int32)[None, :]
    off = (blk_start - pstart[blk_exp])[:, None] + row
    valid = off < counts[blk_exp][:, None]
    a = order[jnp.clip(start[blk_exp][:, None] + off, 0, n_assign - 1).reshape(-1)].reshape(n_blocks, bm)
    nc = d // LANES
    dump_row = TOP_K * n + (jnp.arange(n_blocks, dtype=jnp.int32) % 2)[:, None] * bm + row
    pad_b = (dump_row[1:2] * nc).astype(jnp.int32)
    tok = jnp.concatenate([(jnp.where(valid, a // TOP_K, 0) * nc).astype(jnp.int32), jnp.zeros_like(pad_b)]).reshape(-1)
    dst = jnp.concatenate([(jnp.where(valid, (a % TOP_K) * n + a // TOP_K, dump_row) * nc).astype(jnp.int32),
                           pad_b]).reshape(-1)
    live = jnp.maximum((pend[-1] // bm + 1) // 2, 1).astype(jnp.int32).reshape(1)
    gates = jnp.where(valid, flat_g[a.reshape(-1)].reshape(n_blocks, bm), 0.0).reshape(-1)
    return _moe_experts(h, blk_exp, tok, dst, live, gates, w_gu, b_gu, w_down, b_down, bm=bm, layer=layer)


def _ple_kernel(x_ref, y0_ref, y1_ref, y2_ref, y3_ref, g_ref, wg_ref, p_ref, wp_ref, o_ref):
    tm, d = x_ref.shape
    nc = d // LANES

    def plane(y_ref):
        return jnp.concatenate([y_ref[pl.ds(c, tm, stride=nc), :] for c in range(nc)], axis=1)

    xf = x_ref[...] + (((plane(y0_ref) + plane(y1_ref)) + plane(y2_ref)) + plane(y3_ref))
    ms = jnp.mean(xf * xf, axis=-1, keepdims=True)
    h = (xf * lax.rsqrt(ms + EPS) * g_ref[...]).astype(BF16)
    gate = _sigmoid(jnp.dot(h, wg_ref[...], preferred_element_type=F32))
    pp = jnp.dot(p_ref[...].astype(BF16), wp_ref[...], preferred_element_type=F32)
    o_ref[...] = xf + gate * pp


def _ple(x, y, g, wg, p, wp, *, tm=512):
    n, d = x.shape
    pd = p.shape[1]
    tm = min(tm, n)
    nb = n // tm
    nc = d // LANES
    yspecs = [pl.BlockSpec((tm * nc, LANES), functools.partial(lambda i, k: (k * nb + i, 0), k=k)) for k in range(TOP_K)]
    return pl.pallas_call(
        _ple_kernel,
        out_shape=jax.ShapeDtypeStruct((n, d), F32),
        grid=(nb,),
        in_specs=[pl.BlockSpec((tm, d), lambda i: (i, 0))] + yspecs + [
            pl.BlockSpec((1, d), lambda i: (0, 0)),
            pl.BlockSpec((d, d), lambda i: (0, 0)),
            pl.BlockSpec((tm, pd), lambda i: (i, 0)),
            pl.BlockSpec((pd, d), lambda i: (0, 0)),
        ],
        out_specs=pl.BlockSpec((tm, d), lambda i: (i, 0)),
        compiler_params=_cparams(("parallel",)),
        name="ple",
    )(x, y, y, y, y, g.reshape(1, d).astype(F32), wg.astype(BF16), p, wp.astype(BF16))


def _cmp_kernel(z_ref, w1_ref, pos_ref, w2_ref, gain_ref, c_ref, s1_ref, s2_ref, o_ref, *, rows, half):
    j = pl.program_id(1)
    z = z_ref[0, 0, 0]
    pos = pos_ref[0]
    a1 = (z + pos[:, :half]).astype(BF16)
    a2 = (z + pos[:, half:]).astype(BF16)
    u1 = jnp.dot(a1, w1_ref[0, :half, :], preferred_element_type=F32)
    u2 = jnp.dot(a2, w1_ref[0, half:, :], preferred_element_type=F32)
    h = u1 + pltpu.roll(u2, rows - 1, 0)
    hs = h * _sigmoid(h)
    y = jnp.dot(hs.astype(BF16), w2_ref[0], preferred_element_type=F32)

    @pl.when(j == 0)
    def _():
        ss = jnp.sum(y * y, axis=-1, keepdims=True) * (1.0 / LANES)
        yn = y * lax.rsqrt(ss + EPS) * gain_ref[...]
        yn = (yn * c_ref[...] + pltpu.roll(yn, LANES - ROPE_DIM // 2, 1) * s1_ref[...]
              + pltpu.roll(yn, ROPE_DIM // 2, 1) * s2_ref[...])
        o_ref[0, 0, 0] = yn.astype(o_ref.dtype)

    @pl.when(j != 0)
    def _():
        o_ref[0, 0, 0] = y.astype(o_ref.dtype)


def _compress(zr, w1, pos, w2dup, gain_dup, tabs, *, batch):
    rows, width = zr.shape[3], zr.shape[4]
    hid = w1.shape[2]
    g = NSA_GROUPS
    return pl.pallas_call(
        functools.partial(_cmp_kernel, rows=rows, half=width),
        out_shape=jax.ShapeDtypeStruct((batch, 2, g, rows, LANES), BF16),
        grid=(batch, 2, g),
        in_specs=[
            pl.BlockSpec((1, 1, 1, rows, width), lambda b, j, gg: (b, j, gg, 0, 0)),
            pl.BlockSpec((1, 2 * width, hid), lambda b, j, gg: (j, 0, 0)),
            pl.BlockSpec((1, 1, 2 * width), lambda b, j, gg: (j, 0, 0)),
            pl.BlockSpec((1, hid, LANES), lambda b, j, gg: (j, 0, 0)),
            pl.BlockSpec((1, LANES), lambda b, j, gg: (0, 0)),
            pl.BlockSpec((rows, LANES), lambda b, j, gg: (0, 0)),
            pl.BlockSpec((rows, LANES), lambda b, j, gg: (0, 0)),
            pl.BlockSpec((rows, LANES), lambda b, j, gg: (0, 0)),
        ],
        out_specs=pl.BlockSpec((1, 1, 1, rows, LANES), lambda b, j, gg: (b, j, gg, 0, 0)),
        compiler_params=_cparams(("parallel", "parallel", "parallel")),
        name="nsa_compress",
    )(zr, w1, pos, w2dup, gain_dup, *tabs)


def _pipelined(n_items, first, second, skew):
    pending, out = {}, [None] * n_items
    for idx in range(n_items + skew):
        if idx < n_items:
            pending[idx] = first(idx)
        if idx >= skew:
            out[idx - skew] = second(idx - skew, pending.pop(idx - skew))
    return out


def _nsa_attn_kernel(q_ref, gz_ref, kc_ref, vct_ref, ovt_ref, ks_ref, vs_ref, kw_ref, vw_ref, o_ref,
                     e_ref, oc_ref, imp_ref, notsel_ref, *, n_cmp_rows, tk_sel, skew, grp, sub):
    g = pl.program_id(1)
    i = pl.program_id(2)
    qb = NSA_QB
    mh = HEADS_PER_GROUP
    rows = mh * qb
    q0 = i * qb
    seq = ks_ref.shape[0]

    @pl.when(i == 0)
    def _():
        r = lax.broadcasted_iota(jnp.int32, (seq, LANES), 0)
        lb = lax.broadcasted_iota(jnp.int32, (seq, LANES), 1)
        e_ref[...] = jnp.where((r >> 6) == lb, NEG, 0.0).astype(BF16)

    qh = [q_ref[:, m * LANES:(m + 1) * LANES] for m in range(mh)]
    q4 = jnp.concatenate(qh, axis=0)
    jl = lax.broadcasted_iota(jnp.int32, (qb, LANES), 1)

    tlane = q0 + (lax.broadcasted_iota(jnp.int32, (1, rows), 1) & (qb - 1))

    def cmp_branch(nrow):
        st = _dot_nt(kc_ref[0, 0, 0, :nrow, :], q4)
        cend = lax.broadcasted_iota(jnp.int32, (nrow, rows), 0) * CMP_STRIDE + (CMP_BLOCK - 1)
        valid = cend <= tlane
        st = jnp.where(valid, st, NEG)
        e = jnp.where(valid, jnp.exp2(st - jnp.max(st, axis=0, keepdims=True)), 0.0)
        pt = e / jnp.maximum(jnp.sum(e, axis=0, keepdims=True), 1e-30)
        oct = jnp.dot(vct_ref[0, 0, :, :nrow], pt.astype(BF16), preferred_element_type=F32)
        for m in range(mh):
            oc_ref[m * qb:(m + 1) * qb, :] = oct[:, m * qb:(m + 1) * qb].T
        psum = (pt[:, 0:qb] + pt[:, qb:2 * qb]) + (pt[:, 2 * qb:3 * qb] + pt[:, 3 * qb:4 * qb])
        phi, plo = _split2(psum)
        ovt = ovt_ref[:, :nrow]
        imp_ref[...] = (jnp.dot(ovt, phi, preferred_element_type=F32)
                        + jnp.dot(ovt, plo, preferred_element_type=F32))

    n_chunks = n_cmp_rows // LANES
    last_visible = (q0 + qb - CMP_BLOCK) // CMP_STRIDE
    need = jnp.clip(last_visible // LANES + 1, 1, n_chunks)
    for nch in range(1, n_chunks + 1):
        pl.when(need == nch)(functools.partial(cmp_branch, nch * LANES))

    def select_blocks(nrow):
        jb = lax.broadcasted_iota(jnp.int32, (nrow, qb), 0)
        tq = q0 + lax.broadcasted_iota(jnp.int32, (nrow, qb), 1)
        cur = tq >> 6
        forced = (jb == 0) | (jb == cur) | (jb == cur - 1)
        causal = (jb << 6) <= tq
        score = jnp.where(causal, jnp.where(forced, FORCE, imp_ref[:nrow, :]), -jnp.inf)
        notsel_t = jnp.ones((nrow, qb), F32)
        jbf = jb.astype(F32)
        for _ in range(SEL_TOPK):
            smx = jnp.max(score, axis=0, keepdims=True)
            ix = jnp.min(jnp.where(score == smx, jbf, float(LANES)), axis=0, keepdims=True)
            hit = jbf == ix
            notsel_t = jnp.where(hit, 0.0, notsel_t)
            score = jnp.where(hit, -jnp.inf, score)
        notsel_ref[:nrow, :] = notsel_t
        if nrow < LANES:
            notsel_ref[nrow:, :] = jnp.ones((LANES - nrow, qb), F32)

    sel_chunk = 32
    n_sel_chunks = LANES // sel_chunk
    sel_need = jnp.clip(((q0 + qb - 1) // SEL_BLOCK) // sel_chunk + 1, 1, n_sel_chunks)
    for nch in range(1, n_sel_chunks + 1):
        pl.when(sel_need == nch)(functools.partial(select_blocks, nch * sel_chunk))
    notsel_b = notsel_ref[...].T.astype(BF16)
    n_grp = mh // grp
    q_aug = [jnp.concatenate([jnp.concatenate([qh[m], notsel_b], axis=1) for m in range(c * grp, (c + 1) * grp)], axis=0)
             for c in range(n_grp)]
    tpos = q0 + lax.broadcasted_iota(jnp.int32, (qb, 1), 0)
    tpos_g = jnp.concatenate([tpos] * grp, axis=0)

    def sel_tiles(first, n_sub):
        ks, vs = [], []
        for t in range(n_sub):
            k0 = pl.multiple_of((first + t) * tk_sel, tk_sel)
            k_aug = jnp.concatenate([ks_ref[pl.ds(k0, tk_sel), :], e_ref[pl.ds(k0, tk_sel), :]], axis=1)
            ks.append([k_aug] * n_grp)
            vs.append([vs_ref[pl.ds(k0, tk_sel), :]] * n_grp)
        return ks, vs

    def sel_step(first, n_sub, state):
        ks, vs = sel_tiles(first, n_sub)
        return _attend(q_aug, ks, vs, state, skew)

    def sel_tail(first, n_rem, state):
        def mask(t, c, sc):
            if t < n_rem:
                return sc
            kpos = (first + n_rem) * tk_sel + lax.broadcasted_iota(jnp.int32, (grp * qb, tk_sel), 1)
            return jnp.where(kpos <= tpos_g, sc, NEG)

        ks, vs = sel_tiles(first, n_rem + 1)
        return _attend(q_aug, ks, vs, state, skew, mask=mask)

    state = tuple((jnp.full((grp * qb, 1), NEG, F32), jnp.zeros((grp * qb, LANES), F32)) for _ in range(n_grp))
    n_below = q0 // tk_sel
    state = _tile_loops(n_below, sub, sel_step, sel_tail, state)
    sel_acc = [state[m // grp][1][(m % grp) * qb:(m % grp + 1) * qb] for m in range(mh)]

    wk = WINDOW + qb
    k0w = pl.multiple_of(jnp.maximum(i - WINDOW // qb, 0) * qb, qb)
    kwt = kw_ref[pl.ds(k0w, wk), :]
    vwt = vw_ref[pl.ds(k0w, wk), :]
    kposw = k0w + lax.broadcasted_iota(jnp.int32, (grp * qb, wk), 1)
    wmask = (kposw <= tpos_g) & (kposw > tpos_g - WINDOW)
    q_grp = [jnp.concatenate(qh[c * grp:(c + 1) * grp], axis=0) for c in range(n_grp)]

    def win_scores(c):
        return jnp.where(wmask, _dot_nt(q_grp[c], kwt), NEG)

    def win_out(c, sc):
        pe = jnp.exp2(sc - jnp.max(sc, axis=-1, keepdims=True))
        return jnp.dot(pe.astype(BF16), vwt, preferred_element_type=F32)

    win_grp = _pipelined(n_grp, win_scores, win_out, min(skew, n_grp - 1))
    win_acc = [win_grp[m // grp][(m % grp) * qb:(m % grp + 1) * qb] for m in range(mh)]

    sig = _sigmoid(gz_ref[...])
    den_lane = jl == HEAD_DIM
    heads = []
    for m in range(mh):
        col = g * mh + m
        gates = [jnp.sum(jnp.where(jl == br * N_HEADS + col, sig, 0.0), axis=-1, keepdims=True)
                 for br in range(3)]
        acc_s = sel_acc[m]
        acc_w = win_acc[m]
        den_s = jnp.sum(jnp.where(den_lane, acc_s, 0.0), axis=-1, keepdims=True)
        den_w = jnp.sum(jnp.where(den_lane, acc_w, 0.0), axis=-1, keepdims=True)
        heads.append(gates[0] * oc_ref[m * qb:(m + 1) * qb, :] + (gates[1] / den_s) * acc_s
                     + (gates[2] / den_w) * acc_w)
    lane_lo = jl < HEAD_DIM
    o_ref[:, 0:LANES] = jnp.where(lane_lo, heads[0], pltpu.roll(heads[1], HEAD_DIM, 1)).astype(o_ref.dtype)
    o_ref[:, LANES:2 * LANES] = jnp.where(lane_lo, heads[2], pltpu.roll(heads[3], HEAD_DIM, 1)).astype(o_ref.dtype)


def _nsa_attn(qn, gz, kvc, overlap, kvd, *, batch, seq, skew=2, grp=2, sub=4):
    n = qn.shape[0]
    qb = NSA_QB
    nq = seq // qb
    g = NSA_GROUPS
    rows = HEADS_PER_GROUP * qb
    n_cmp_rows = kvc.shape[3]
    tk_sel = min(512, seq)
    assert seq >= WINDOW + qb and seq % tk_sel == 0 and tk_sel % qb == 0
    in_specs = [
        pl.BlockSpec((qb, HEADS_PER_GROUP * LANES), lambda b, gg, i: (b * nq + i, gg)),
        pl.BlockSpec((qb, LANES), lambda b, gg, i: (b * nq + i, 0)),
        pl.BlockSpec((1, 1, 1, n_cmp_rows, LANES), lambda b, gg, i: (b, 0, gg, 0, 0)),
        pl.BlockSpec((1, 1, LANES, n_cmp_rows), lambda b, gg, i: (b, gg, 0, 0)),
        pl.BlockSpec((LANES, n_cmp_rows), lambda b, gg, i: (0, 0)),
        pl.BlockSpec((seq, LANES), lambda b, gg, i: (b, 0 * g + gg)),
        pl.BlockSpec((seq, LANES), lambda b, gg, i: (b, 1 * g + gg)),
        pl.BlockSpec((seq, LANES), lambda b, gg, i: (b, 2 * g + gg)),
        pl.BlockSpec((seq, LANES), lambda b, gg, i: (b, 3 * g + gg)),
    ]
    return pl.pallas_call(
        functools.partial(_nsa_attn_kernel, n_cmp_rows=n_cmp_rows, tk_sel=tk_sel, skew=skew, grp=grp, sub=sub),
        out_shape=jax.ShapeDtypeStruct((n, N_HEADS * HEAD_DIM), BF16),
        grid=(batch, g, nq),
        in_specs=in_specs,
        out_specs=pl.BlockSpec((qb, HEADS_PER_GROUP * HEAD_DIM), lambda b, gg, i: (b * nq + i, gg)),
        scratch_shapes=[pltpu.VMEM((seq, LANES), BF16), pltpu.VMEM((rows, LANES), F32), pltpu.VMEM((LANES, qb), F32),
                        pltpu.VMEM((LANES, qb), F32)],
        compiler_params=_cparams(("parallel", "parallel", "arbitrary")),
        name="nsa_attn",
    )(qn, gz, kvc, jnp.swapaxes(kvc[:, 1], -1, -2), overlap.T, kvd, kvd, kvd, kvd)


def _pad_cols(w, width=LANES):
    d, f = w.shape
    return jnp.zeros((d, width), w.dtype).at[:, :f].set(w)


def _tile_gain(gain, n_heads, scale=1.0, dup=False):
    g = gain.astype(F32) * scale
    second = g if dup else jnp.zeros_like(g)
    return jnp.tile(jnp.concatenate([g, second]), n_heads)


def _fox_aug_tables():
    hw = N_HEADS * LANES
    piece = jnp.arange(3)[:, None]
    head = jnp.arange(N_HEADS)[None, :]
    src = (piece * N_HEADS + head).reshape(-1)
    q_dst = (head * LANES + HEAD_DIM + piece).reshape(-1)
    k_dst = (hw + head * LANES + HEAD_DIM + 3 + piece).reshape(-1)
    place = jnp.zeros((LANES, 3 * hw), F32).at[src, q_dst].set(1.0).at[src, k_dst].set(-1.0)
    lane = jnp.arange(LANES)
    q_bias = ((lane >= HEAD_DIM + 3) & (lane < HEAD_DIM + 6)).astype(F32)
    k_bias = ((lane >= HEAD_DIM) & (lane < HEAD_DIM + 3)).astype(F32)
    v_bias = (lane == HEAD_DIM).astype(F32)
    bias = jnp.concatenate([jnp.tile(q_bias, N_HEADS), jnp.tile(k_bias, N_HEADS), jnp.tile(v_bias, N_HEADS)])
    return place.astype(BF16), bias


def _fox_layer(x, batch, seq, attn_g, w_in, b_f, qk_gain, w_out):
    hd = N_HEADS * HEAD_DIM
    hw = N_HEADS * LANES
    scale = HEAD_DIM ** -0.5
    z = _rms_proj(x, attn_g, _pad_cols(w_in[:, 3 * hd:]).astype(BF16), out_dtype=F32, seq=seq)
    b_pad = jnp.zeros((1, LANES), F32).at[0, :N_HEADS].set(b_f.astype(F32))
    c3 = _fox_c(z, b_pad, batch=batch, seq=seq)
    w_bf = w_in.astype(BF16)
    gain = jnp.concatenate([_tile_gain(qk_gain[0], N_HEADS, scale * LOG2E), _tile_gain(qk_gain[1], N_HEADS),
                            jnp.zeros((hw,), F32)])
    tn = 1024
    place, bias = _fox_aug_tables()
    qkv = _rms_proj(x, attn_g, w_bf, out_dtype=BF16, seq=seq, tn=tn, gain=gain, w_cols=3 * hd,
                    norm_blocks=tuple(range(2 * hw // tn)), placed=(c3, place), col_bias=bias, expand=True)
    o = _fox_attn(qkv, batch=batch, seq=seq)
    return _matmul_res(o, w_out.astype(BF16), x)


def _nsa_shared_kv(x, batch, seq, kv_norm, kv_w, kv_k_gain, cmp_pos, cmp_w1, cmp_w2):
    g = NSA_GROUPS
    gw = g * HEAD_DIM
    tabs = _rope_tables(jnp.arange(seq))
    kv_bf = kv_w.astype(BF16)
    zero = jnp.zeros((g * LANES,), F32)
    one_lane = jnp.tile((jnp.arange(LANES) == HEAD_DIM).astype(F32), g)
    gain_kvd = jnp.concatenate([_tile_gain(kv_k_gain[1], g), zero, _tile_gain(kv_k_gain[2], g), zero])
    bias_kvd = jnp.concatenate([zero, one_lane, zero, one_lane])
    kvd = _rms_proj(x, kv_norm, kv_bf, out_dtype=BF16, seq=seq, tn=g * LANES, gain=gain_kvd, w_col0=2 * gw,
                    norm_blocks=(0, 2), rope_tabs=tabs, col_bias=bias_kvd, expand=True)
    zc = _rms_proj(x, kv_norm, kv_bf, out_dtype=F32, seq=seq, w_cols=2 * gw)
    rows = seq // CMP_STRIDE
    zr = zc.reshape(batch, rows, CMP_STRIDE, 2, g, HEAD_DIM).transpose(0, 3, 4, 1, 2, 5)
    zr = zr.reshape(batch, 2, g, rows, CMP_STRIDE * HEAD_DIM)
    cmp_end = jnp.arange(rows) * CMP_STRIDE + CMP_BLOCK - 1
    ctabs = _rope_tables(cmp_end)
    w2dup = jnp.concatenate([cmp_w2, cmp_w2], axis=-1).astype(BF16)
    kvc = _compress(zr, cmp_w1.astype(BF16), cmp_pos.reshape(2, 1, CMP_BLOCK * HEAD_DIM).astype(F32), w2dup,
                    _tile_gain(kv_k_gain[0], 1, dup=True).reshape(1, LANES), ctabs, batch=batch)
    return kvc, kvd


def _nsa_layer(x, batch, seq, attn_g, w_in, q_gain, w_out, kvc, kvd):
    hd = N_HEADS * HEAD_DIM
    scale = HEAD_DIM ** -0.5
    rows = seq // CMP_STRIDE
    tabs = _rope_tables(jnp.arange(seq))
    qn = _rms_proj(x, attn_g, w_in.astype(BF16), out_dtype=BF16, seq=seq, expand=True, w_cols=hd,
                   gain=_tile_gain(q_gain, N_HEADS, scale * LOG2E), norm_blocks=tuple(range(N_HEADS * LANES // 512)),
                   rope_tabs=tabs)
    gz = _rms_proj(x, attn_g, _pad_cols(w_in[:, hd:]).astype(BF16), out_dtype=F32, seq=seq)

    n_sel = seq // SEL_BLOCK
    assert n_sel <= LANES
    cmp_start = jnp.arange(rows) * CMP_STRIDE
    sel_start = jnp.arange(LANES) * SEL_BLOCK
    overlap = jnp.clip(jnp.minimum(cmp_start[:, None] + CMP_BLOCK, sel_start[None, :] + SEL_BLOCK)
                       - jnp.maximum(cmp_start[:, None], sel_start[None, :]), 0)
    overlap = jnp.where((jnp.arange(rows) < rows - 1)[:, None] & (jnp.arange(LANES) < n_sel)[None, :], overlap, 0)
    o = _nsa_attn(qn, gz, kvc, overlap.astype(BF16), kvd, batch=batch, seq=seq)
    return _matmul_res(o, w_out.astype(BF16), x)


def kernel(x, p, attn_norm, ffn_norm, ple_norm, ple_gate_w, ple_proj_w, router_w, router_b, w_gu, b_gu, w_down, b_down, fox_w_in, fox_b_f, fox_qk_gain, fox_w_out, kv_norm, kv_w, kv_k_gain, cmp_pos, cmp_w1, cmp_w2, nsa_w_in, nsa_q_gain, nsa_w_out):
    batch, seq, d = x.shape
    depth = p.shape[0]
    n_a = fox_w_in.shape[0]
    xt = x.reshape(batch * seq, d)
    w_gu_bf = w_gu.astype(BF16)
    w_down_bf = w_down.astype(BF16)
    shared = None
    for layer in range(depth):
        if layer == n_a:
            shared = _nsa_shared_kv(xt, batch, seq, kv_norm, kv_w, kv_k_gain, cmp_pos, cmp_w1, cmp_w2)
        if layer < n_a:
            xt = _fox_layer(xt, batch, seq, attn_norm[layer], fox_w_in[layer], fox_b_f[layer],
                            fox_qk_gain[layer], fox_w_out[layer])
        else:
            i = layer - n_a
            xt = _nsa_layer(xt, batch, seq, attn_norm[layer], nsa_w_in[i], nsa_q_gain[i], nsa_w_out[i], *shared)
        y = _moe_layer(xt, ffn_norm[layer], router_w[layer], router_b[layer], w_gu_bf, b_gu[layer],
                       w_down_bf, b_down[layer], layer)
        xt = _ple(xt, y, ple_norm[layer], ple_gate_w[layer], p[layer].reshape(batch * seq, -1), ple_proj_w[layer])
    return xt.reshape(batch, seq, d)
```

```python
import functools

import jax
import jax.numpy as jnp
from jax import lax
from jax.experimental import pallas as pl
from jax.experimental.pallas import tpu as pltpu

F32 = jnp.float32
BF16 = jnp.bfloat16

N_HEADS = 16
HEAD_DIM = 64
ROPE_DIM = HEAD_DIM // 4
ROPE_THETA = 500000.0
NSA_GROUPS = 4
HEADS_PER_GROUP = N_HEADS // NSA_GROUPS
CMP_BLOCK = 32
CMP_STRIDE = 16
SEL_BLOCK = 64
SEL_TOPK = 16
WINDOW = 512
N_EXPERTS = 32
TOP_K = 4
SWIGLU_LIMIT = 7.0
SWIGLU_ALPHA = 1.702
EPS = 1e-6
NEG = -1e30
FORCE = 1e6
LOG2E = 1.4426950408889634

LANES = 128
NSA_QB = 256
MOE_BM = 256
VMEM_LIMIT = 56 * 1024 * 1024


def _cparams(sem):
    return pltpu.CompilerParams(dimension_semantics=sem, vmem_limit_bytes=VMEM_LIMIT)


def _sigmoid(x):
    return 1.0 / (1.0 + jnp.exp(-x))


def _dot_nt(a, b):
    return lax.dot_general(a, b, (((1,), (1,)), ((), ())), preferred_element_type=F32)


def _split2(x):
    hi = x.astype(BF16)
    lo = (x - hi.astype(F32)).astype(BF16)
    return hi, lo


def _proj_kernel(*refs, norm_blocks, rope, placed, biased, expand, head_div, tn, out_dtype):
    refs = list(refs)
    x_ref, g_ref, w_ref = refs[:3]
    del refs[:3]
    if norm_blocks:
        gain_ref = refs.pop(0)
    if rope:
        c_ref, s1_ref, s2_ref = refs[:3]
        del refs[:3]
    if placed:
        ex_ref, place_ref = refs[:2]
        del refs[:2]
    if biased:
        bias_ref = refs.pop(0)
    o_ref, h_ref = refs
    j = pl.program_id(1)
    aug = placed or biased

    @pl.when(j == 0)
    def _():
        xf = x_ref[...]
        ms = jnp.mean(xf * xf, axis=-1, keepdims=True)
        h_ref[...] = (xf * lax.rsqrt(ms + EPS) * g_ref[...]).astype(BF16)

    acc = jnp.dot(h_ref[...], w_ref[...], preferred_element_type=F32)
    if placed and biased:
        extra = jnp.dot(ex_ref[...], place_ref[...], preferred_element_type=F32) + bias_ref[...]
    elif biased:
        extra = jnp.broadcast_to(bias_ref[...], (acc.shape[0], tn))
    if expand:
        lo = lax.broadcasted_iota(jnp.int32, (acc.shape[0], LANES), 1) < HEAD_DIM
        blocks = []
        for c in range(acc.shape[1] // LANES):
            a = acc[:, c * LANES:(c + 1) * LANES]
            blocks += [jnp.where(lo, a, 0.0), jnp.where(lo, pltpu.roll(a, HEAD_DIM, 1), 0.0)]
    else:
        blocks = [acc[:, c * LANES:(c + 1) * LANES] for c in range(acc.shape[1] // LANES)]

    def plain():
        for c, a in enumerate(blocks):
            sl = slice(c * LANES, (c + 1) * LANES)
            o_ref[:, sl] = ((a + extra[:, sl]) if aug else a).astype(out_dtype)

    if not norm_blocks:
        plain()
        return
    is_norm = functools.reduce(jnp.logical_or, [j == c for c in norm_blocks])

    @pl.when(is_norm)
    def _():
        for c, a in enumerate(blocks):
            sl = slice(c * LANES, (c + 1) * LANES)
            ss = jnp.sum(a * a, axis=-1, keepdims=True) * (1.0 / head_div)
            y = a * lax.rsqrt(ss + EPS) * gain_ref[:, sl]
            if rope:
                y = (y * c_ref[...] + pltpu.roll(y, LANES - ROPE_DIM // 2, 1) * s1_ref[...]
                     + pltpu.roll(y, ROPE_DIM // 2, 1) * s2_ref[...])
            if aug:
                y = y + extra[:, sl]
            o_ref[:, sl] = y.astype(out_dtype)

    @pl.when(jnp.logical_not(is_norm))
    def _():
        plain()


def _rms_proj(x, g, w, *, out_dtype, seq, tm=2048, tn=512, gain=None, norm_blocks=(),
              rope_tabs=None, placed=None, col_bias=None, expand=False, head_div=HEAD_DIM,
              w_col0=0, w_cols=None):
    n, d = x.shape
    w_cols = w.shape[1] - w_col0 if w_cols is None else w_cols
    f = w_cols * (2 if expand else 1)
    tn = min(tn, f)
    tw = tn // 2 if expand else tn
    assert w_col0 % tw == 0
    j0 = w_col0 // tw
    tm = min(tm, seq)
    assert n % tm == 0 and f % tn == 0 and seq % tm == 0
    nt = seq // tm
    in_specs = [
        pl.BlockSpec((tm, d), lambda i, j: (i, 0)),
        pl.BlockSpec((1, d), lambda i, j: (0, 0)),
        pl.BlockSpec((d, tw), lambda i, j: (0, j0 + j)),
    ]
    args = [x, g.reshape(1, d).astype(F32), w]
    if norm_blocks:
        in_specs.append(pl.BlockSpec((1, tn), lambda i, j: (0, j)))
        args.append(gain.reshape(1, f).astype(F32))
    if rope_tabs is not None:
        assert norm_blocks
        for tab in rope_tabs:
            in_specs.append(pl.BlockSpec((tm, LANES), lambda i, j: (i % nt, 0)))
            args.append(tab)
    if placed is not None:
        assert col_bias is not None
        values, place = placed
        in_specs += [pl.BlockSpec((tm, LANES), lambda i, j: (i, 0)),
                     pl.BlockSpec((LANES, tn), lambda i, j: (0, j))]
        args += [values, place]
    if col_bias is not None:
        in_specs.append(pl.BlockSpec((1, tn), lambda i, j: (0, j)))
        args.append(col_bias.reshape(1, f).astype(F32))
    kern = functools.partial(_proj_kernel, norm_blocks=tuple(norm_blocks), rope=rope_tabs is not None,
                             placed=placed is not None, biased=col_bias is not None, expand=expand,
                             head_div=float(head_div), tn=tn, out_dtype=out_dtype)
    return pl.pallas_call(
        kern,
        out_shape=jax.ShapeDtypeStruct((n, f), out_dtype),
        grid=(n // tm, f // tn),
        in_specs=in_specs,
        out_specs=pl.BlockSpec((tm, tn), lambda i, j: (i, j)),
        scratch_shapes=[pltpu.VMEM((tm, d), BF16)],
        compiler_params=_cparams(("parallel", "arbitrary")),
        name="rms_proj",
    )(*args)


def _rope_tables(pos, width=LANES):
    half = ROPE_DIM // 2
    inv = jnp.power(jnp.float32(ROPE_THETA), -jnp.arange(0, ROPE_DIM, 2, dtype=F32) / ROPE_DIM)
    ang = pos.astype(F32)[:, None] * inv[None, :]
    cos, sin = jnp.cos(ang), jnp.sin(ang)
    t = pos.shape[0]
    ones = jnp.ones((t, HEAD_DIM - ROPE_DIM), F32)
    zeros_h = jnp.zeros((t, half), F32)
    zeros_r = jnp.zeros((t, HEAD_DIM - ROPE_DIM), F32)
    c = jnp.concatenate([cos, cos, ones], axis=1)
    s1 = jnp.concatenate([-sin, zeros_h, zeros_r], axis=1)
    s2 = jnp.concatenate([zeros_h, sin, zeros_r], axis=1)
    rep = width // HEAD_DIM
    return tuple(jnp.tile(a, (1, rep)) for a in (c, s1, s2))


def _matmul_res_kernel(a_ref, w_ref, x_ref, o_ref):
    o_ref[...] = x_ref[...] + jnp.dot(a_ref[...], w_ref[...], preferred_element_type=F32)


def _matmul_res(a, w, x, *, tm=512):
    n, k = a.shape
    f = w.shape[1]
    tm = min(tm, n)
    return pl.pallas_call(
        _matmul_res_kernel,
        out_shape=jax.ShapeDtypeStruct((n, f), F32),
        grid=(n // tm,),
        in_specs=[
            pl.BlockSpec((tm, k), lambda i: (i, 0)),
            pl.BlockSpec((k, f), lambda i: (0, 0)),
            pl.BlockSpec((tm, f), lambda i: (i, 0)),
        ],
        out_specs=pl.BlockSpec((tm, f), lambda i: (i, 0)),
        compiler_params=_cparams(("parallel",)),
        name="matmul_res",
    )(a, w, x)


def _fox_c_kernel(z_ref, b_ref, c_ref, carry_ref, *, tm):
    t = pl.program_id(1)

    @pl.when(t == 0)
    def _():
        carry_ref[...] = jnp.zeros_like(carry_ref)

    u = z_ref[...] + b_ref[...]
    logf = jnp.minimum(u, 0.0) - jnp.log(1.0 + jnp.exp(-jnp.abs(u)))
    row = lax.broadcasted_iota(jnp.int32, (tm, tm), 0)
    col = lax.broadcasted_iota(jnp.int32, (tm, tm), 1)
    tri = (row >= col).astype(BF16)
    hi = logf.astype(BF16)
    r1 = logf - hi.astype(F32)
    mid = r1.astype(BF16)
    lo = (r1 - mid.astype(F32)).astype(BF16)
    cs = (jnp.dot(tri, hi, preferred_element_type=F32) + jnp.dot(tri, mid, preferred_element_type=F32)
          + jnp.dot(tri, lo, preferred_element_type=F32))
    c = cs + carry_ref[...]
    carry_ref[...] = c[tm - 1:tm, :]
    c2 = c * LOG2E
    p0 = c2.astype(BF16).astype(F32)
    p1 = (c2 - p0).astype(BF16).astype(F32)
    p2 = ((c2 - p0) - p1).astype(BF16).astype(F32)
    lane = lax.broadcasted_iota(jnp.int32, c.shape, 1)
    out = jnp.where(lane < N_HEADS, p0,
                    jnp.where(lane < 2 * N_HEADS, pltpu.roll(p1, N_HEADS, 1),
                              jnp.where(lane < 3 * N_HEADS, pltpu.roll(p2, 2 * N_HEADS, 1), 0.0)))
    c_ref[...] = out.astype(BF16)


def _fox_c(z, b_pad, *, batch, seq, tm=512):
    tm = min(tm, seq)
    nt = seq // tm
    return pl.pallas_call(
        functools.partial(_fox_c_kernel, tm=tm),
        out_shape=jax.ShapeDtypeStruct(z.shape, BF16),
        grid=(batch, nt),
        in_specs=[
            pl.BlockSpec((tm, LANES), lambda b, t: (b * nt + t, 0)),
            pl.BlockSpec((1, LANES), lambda b, t: (0, 0)),
        ],
        out_specs=pl.BlockSpec((tm, LANES), lambda b, t: (b * nt + t, 0)),
        scratch_shapes=[pltpu.VMEM((1, LANES), F32)],
        compiler_params=_cparams(("parallel", "arbitrary")),
        name="fox_cumsum",
    )(z, b_pad)


def _attend(qs, ks, vs, state, skew, mask=None):
    items = [(t, c) for t in range(len(ks)) for c in range(len(qs))]
    state = list(state)
    pending = {}
    depth = min(skew, len(items) - 1)
    for idx in range(len(items) + depth):
        if idx < len(items):
            t, c = items[idx]
            sc = _dot_nt(qs[c], ks[t][c])
            pending[idx] = sc if mask is None else mask(t, c, sc)
        if idx >= depth:
            t, c = items[idx - depth]
            sc = pending.pop(idx - depth)
            m_prev, acc = state[c]
            m_new = jnp.maximum(m_prev, jnp.max(sc, axis=-1, keepdims=True))
            pe = jnp.exp2(sc - m_new)
            acc = jnp.exp2(m_prev - m_new) * acc + jnp.dot(pe.astype(BF16), vs[t][c], preferred_element_type=F32)
            state[c] = (m_new, acc)
    return tuple(state)


def _tile_loops(n_tiles, sub, step, tail, state):
    n_full = n_tiles // sub
    state = lax.fori_loop(0, n_full, lambda j, st: step(j * sub, sub, st), state)
    first = n_full * sub
    return lax.switch(n_tiles - first, [functools.partial(tail, first, r) for r in range(sub)], state)


def _fox_attn_kernel(q0_ref, q1_ref, k0_ref, k1_ref, v0_ref, v1_ref, o_ref, *, tq, tk, skew, sub):
    qi = pl.program_id(2)
    k_refs = (k0_ref, k1_ref)
    v_refs = (v0_ref, v1_ref)
    qs = [q0_ref[...], q1_ref[...]]

    def tiles(refs, first, n_sub):
        return [[r[pl.ds(pl.multiple_of((first + t) * tk, tk), tk), :] for r in refs] for t in range(n_sub)]

    def full_step(first, n_sub, state):
        return _attend(qs, tiles(k_refs, first, n_sub), tiles(v_refs, first, n_sub), state, skew)

    def causal(t, c, sc):
        tpos = lax.broadcasted_iota(jnp.int32, (tq, tk), 0)
        kpos = t * tk + lax.broadcasted_iota(jnp.int32, (tq, tk), 1)
        return jnp.where(kpos <= tpos, sc, NEG)

    n_diag = tq // tk

    def tail_step(first, n_rem, state):
        def mask(t, c, sc):
            return causal(t - n_rem, c, sc) if t >= n_rem else sc

        n = n_rem + n_diag
        return _attend(qs, tiles(k_refs, first, n), tiles(v_refs, first, n), state, skew, mask=mask)

    state = tuple((jnp.full((tq, 1), NEG, F32), jnp.zeros((tq, LANES), F32)) for _ in qs)
    n_below = (qi * tq) // tk
    state = _tile_loops(n_below, sub, full_step, tail_step, state)

    lane = lax.broadcasted_iota(jnp.int32, (tq, LANES), 1)
    outs = []
    for _, acc in state:
        den = jnp.sum(jnp.where(lane == HEAD_DIM, acc, 0.0), axis=-1, keepdims=True)
        outs.append(acc / den)
    o_ref[...] = jnp.where(lane < HEAD_DIM, outs[0], pltpu.roll(outs[1], HEAD_DIM, 1)).astype(o_ref.dtype)


def _fox_attn(qkv, *, batch, seq, tq=512, tk=512, skew=2, sub=4):
    n = qkv.shape[0]
    tq = min(tq, seq)
    tk = min(tk, tq)
    assert seq % tq == 0 and tq % tk == 0
    nq = seq // tq
    hp = N_HEADS // 2
    kb = N_HEADS
    vb = 2 * N_HEADS
    in_specs = [
        pl.BlockSpec((tq, LANES), lambda b, h, i: (b * nq + i, 2 * h)),
        pl.BlockSpec((tq, LANES), lambda b, h, i: (b * nq + i, 2 * h + 1)),
        pl.BlockSpec((seq, LANES), lambda b, h, i: (b, kb + 2 * h)),
        pl.BlockSpec((seq, LANES), lambda b, h, i: (b, kb + 2 * h + 1)),
        pl.BlockSpec((seq, LANES), lambda b, h, i: (b, vb + 2 * h)),
        pl.BlockSpec((seq, LANES), lambda b, h, i: (b, vb + 2 * h + 1)),
    ]
    return pl.pallas_call(
        functools.partial(_fox_attn_kernel, tq=tq, tk=tk, skew=skew, sub=sub),
        out_shape=jax.ShapeDtypeStruct((n, N_HEADS * HEAD_DIM), BF16),
        grid=(batch, hp, nq),
        in_specs=in_specs,
        out_specs=pl.BlockSpec((tq, LANES), lambda b, h, i: (b * nq + i, h)),
        compiler_params=_cparams(("parallel", "parallel", "arbitrary")),
        name="fox_attn",
    )(qkv, qkv, qkv, qkv, qkv, qkv)


def _router_kernel(x_ref, g_ref, whi_ref, wlo_ref, b_ref, h_ref, idx_ref, gate_ref, cnt_ref):
    xf = x_ref[...]
    ms = jnp.mean(xf * xf, axis=-1, keepdims=True)
    h = xf * lax.rsqrt(ms + EPS) * g_ref[...]
    nc = h.shape[1] // LANES
    for c in range(nc):
        h_ref[pl.ds(c, h.shape[0], stride=nc), :] = h[:, c * LANES:(c + 1) * LANES]
    hhi, hlo = _split2(h)
    whi = whi_ref[...]
    logits = (jnp.dot(hhi, whi, preferred_element_type=F32) + jnp.dot(hlo, whi, preferred_element_type=F32)
              + jnp.dot(hhi, wlo_ref[...], preferred_element_type=F32)) + b_ref[...]
    lane = lax.broadcasted_iota(jnp.int32, logits.shape, 1).astype(F32)
    vals, idxs = [], []
    cur = logits
    for _ in range(TOP_K):
        mx = jnp.max(cur, axis=-1, keepdims=True)
        ix = jnp.min(jnp.where(cur == mx, lane, float(LANES)), axis=-1, keepdims=True)
        vals.append(mx)
        idxs.append(ix)
        cur = jnp.where(lane == ix, -jnp.inf, cur)
    es = [jnp.exp(v - vals[0]) for v in vals]
    tot = es[0] + es[1] + es[2] + es[3]
    idx_out = jnp.zeros(logits.shape, F32)
    gate_out = jnp.zeros(logits.shape, F32)
    for r in range(TOP_K):
        idx_out = jnp.where(lane == r, idxs[r], idx_out)
        gate_out = jnp.where(lane == r, es[r] / tot, gate_out)
    idx_ref[...] = idx_out.astype(jnp.int32)
    gate_ref[...] = gate_out
    hits = sum(jnp.sum((lane == ix).astype(F32), axis=0, keepdims=True) for ix in idxs)

    @pl.when(pl.program_id(0) == 0)
    def _():
        cnt_ref[...] = jnp.zeros(cnt_ref.shape, cnt_ref.dtype)

    cnt_ref[...] += jnp.broadcast_to(hits, cnt_ref.shape)


def _router(x, g, w, b, *, tm=512):
    n, d = x.shape
    e = w.shape[1]
    tm = min(tm, n)
    w_pad = jnp.zeros((d, LANES), F32).at[:, :e].set(w)
    whi = w_pad.astype(BF16)
    wlo = (w_pad - whi.astype(F32)).astype(BF16)
    b_pad = jnp.full((1, LANES), -jnp.inf, F32).at[0, :e].set(b.astype(F32))
    return pl.pallas_call(
        _router_kernel,
        out_shape=(jax.ShapeDtypeStruct((n * (d // LANES), LANES), F32), jax.ShapeDtypeStruct((n, LANES), jnp.int32),
                   jax.ShapeDtypeStruct((n, LANES), F32), jax.ShapeDtypeStruct((8, LANES), F32)),
        grid=(n // tm,),
        in_specs=[
            pl.BlockSpec((tm, d), lambda i: (i, 0)),
            pl.BlockSpec((1, d), lambda i: (0, 0)),
            pl.BlockSpec((d, LANES), lambda i: (0, 0)),
            pl.BlockSpec((d, LANES), lambda i: (0, 0)),
            pl.BlockSpec((1, LANES), lambda i: (0, 0)),
        ],
        out_specs=(pl.BlockSpec((tm * (d // LANES), LANES), lambda i: (i, 0)), pl.BlockSpec((tm, LANES), lambda i: (i, 0)),
                   pl.BlockSpec((tm, LANES), lambda i: (i, 0)), pl.BlockSpec((8, LANES), lambda i: (0, 0))),
        compiler_params=_cparams(("arbitrary",)),
        name="moe_router",
    )(x, g.reshape(1, d).astype(F32), whi, wlo, b_pad)


def _moe_expert_kernel(blk_exp_ref, tok_ref, dst_ref, live_ref, h_hbm, ga_ref, gb_ref, wgu_a, bgu_a, wd_a, bd_a,
                       wgu_b, bgu_b, wd_b, bd_b, y_hbm, xa, xb, oa, ob, gsem, ssem, *, bm, dff, n_blocks):
    i = pl.program_id(0)
    live = live_ref[0]
    nc = xa.shape[0] // bm
    dump = y_hbm.shape[0] // nc - 2 * bm

    def gather_rows(block, buf, sem, lo, hi):
        base = block * bm
        for r in range(lo, hi):
            pltpu.make_async_copy(h_hbm.at[pl.ds(pl.multiple_of(tok_ref[base + r], nc), nc)], buf.at[pl.ds(r * nc, nc)], sem).start()

    def scatter_rows(block, buf, sem, lo, hi):
        base = block * bm
        for r in range(lo, hi):
            pltpu.make_async_copy(buf.at[pl.ds(r * nc, nc)], y_hbm.at[pl.ds(pl.multiple_of(dst_ref[base + r], nc), nc)], sem).start()

    def wait_gather(buf, sem):
        pltpu.make_async_copy(h_hbm.at[pl.ds(0, bm * nc)], buf, sem).wait()

    def wait_scatter(buf, sem):
        pltpu.make_async_copy(buf, y_hbm.at[pl.ds(0, bm * nc)], sem).wait()

    def experts(x_ref, o_ref, g_ref, wgu, bgu, wd, bd, issue, o_sem):
        x = jnp.concatenate([x_ref[pl.ds(c, bm, stride=nc), :] for c in range(nc)], axis=1).astype(BF16)
        issue()
        gu = jnp.dot(x, wgu[0, 0], preferred_element_type=F32) + bgu[0]
        a = jnp.minimum(gu[:, :dff], SWIGLU_LIMIT)
        u = jnp.clip(gu[:, dff:], -SWIGLU_LIMIT, SWIGLU_LIMIT)
        y = (u + 1.0) * (a * _sigmoid(SWIGLU_ALPHA * a))
        wait_scatter(o_ref, o_sem)
        out = (jnp.dot(y.astype(BF16), wd[0, 0], preferred_element_type=F32) + bd[0]) * g_ref[...]
        for c in range(nc):
            o_ref[pl.ds(c, bm, stride=nc), :] = out[:, c * LANES:(c + 1) * LANES]

    @pl.when(i == 0)
    def _():
        gather_rows(0, xa, gsem.at[0], 0, bm)
        gather_rows(1, xb, gsem.at[1], 0, bm)
        oa[...] = jnp.zeros(oa.shape, oa.dtype)
        ob[...] = jnp.zeros(ob.shape, ob.dtype)
        for r in range(bm):
            pltpu.make_async_copy(oa.at[pl.ds(r * nc, nc)], y_hbm.at[pl.ds((dump + r) * nc, nc)], ssem.at[0]).start()

    @pl.when(i < live)
    def _():
        blk_a = 2 * i
        prev_b = jnp.where(i == 0, n_blocks, 2 * i - 1)
        last = 2 * live - 1
        nxt_a = jnp.minimum(2 * i + 2, last)
        nxt_b = jnp.minimum(2 * i + 3, last)

        def ride_a():
            gather_rows(nxt_a, xa, gsem.at[0], 0, bm)
            scatter_rows(prev_b, ob, ssem.at[1], 0, bm)

        wait_gather(xa, gsem.at[0])
        experts(xa, oa, ga_ref, wgu_a, bgu_a, wd_a, bd_a, ride_a, ssem.at[0])

        def ride_b():
            gather_rows(nxt_b, xb, gsem.at[1], 0, bm)
            scatter_rows(blk_a, oa, ssem.at[0], 0, bm)

        wait_gather(xb, gsem.at[1])
        experts(xb, ob, gb_ref, wgu_b, bgu_b, wd_b, bd_b, ride_b, ssem.at[1])

    @pl.when(i == live - 1)
    def _():
        scatter_rows(2 * i + 1, ob, ssem.at[1], 0, bm)
        wait_gather(xa, gsem.at[0])
        wait_gather(xb, gsem.at[1])
        wait_scatter(oa, ssem.at[0])
        wait_scatter(ob, ssem.at[1])


def _moe_experts(h, blk_exp, tok, dst, live, gates, w_gu, b_gu, w_down, b_down, *, bm, layer):
    _, e, d, f2 = w_gu.shape
    nc = d // LANES
    n = h.shape[0] // nc
    dff = f2 // 2
    n_steps = (tok.shape[0] // bm - 1) // 2
    cap = n_steps * 2 * bm
    assert tok.shape[0] == cap + bm

    def wspecs(off):
        return [
            pl.BlockSpec((1, 1, d, f2), lambda i, be, tk, ds, lv: (layer, be[2 * i + off], 0, 0)),
            pl.BlockSpec((1, 1, f2), lambda i, be, tk, ds, lv: (be[2 * i + off], 0, 0)),
            pl.BlockSpec((1, 1, dff, d), lambda i, be, tk, ds, lv: (layer, be[2 * i + off], 0, 0)),
            pl.BlockSpec((1, 1, d), lambda i, be, tk, ds, lv: (be[2 * i + off], 0, 0)),
        ]

    grid_spec = pltpu.PrefetchScalarGridSpec(
        num_scalar_prefetch=4,
        grid=(n_steps,),
        in_specs=[
            pl.BlockSpec(memory_space=pl.ANY),
            pl.BlockSpec((bm, 1), lambda i, be, tk, ds, lv: (2 * i, 0)),
            pl.BlockSpec((bm, 1), lambda i, be, tk, ds, lv: (2 * i + 1, 0)),
        ] + wspecs(0) + wspecs(1),
        out_specs=pl.BlockSpec(memory_space=pl.ANY),
        scratch_shapes=[pltpu.VMEM((bm * nc, LANES), F32)] * 4 + [pltpu.SemaphoreType.DMA((2,)),
                                                                  pltpu.SemaphoreType.DMA((2,))],
    )
    bgu3 = b_gu.reshape(e, 1, f2).astype(F32)
    bd3 = b_down.reshape(e, 1, d).astype(F32)
    g2 = gates.reshape(cap, 1)
    return pl.pallas_call(
        functools.partial(_moe_expert_kernel, bm=bm, dff=dff, n_blocks=2 * n_steps),
        out_shape=jax.ShapeDtypeStruct(((TOP_K * n + 2 * bm) * nc, LANES), F32),
        grid_spec=grid_spec,
        compiler_params=_cparams(("arbitrary",)),
        name="moe_experts",
    )(blk_exp, tok, dst, live, h, g2, g2, w_gu, bgu3, w_down, bd3, w_gu, bgu3, w_down, bd3)


def _moe_layer(x, norm_g, router_w, router_b, w_gu, b_gu, w_down, b_down, layer):
    n, d = x.shape
    bm = MOE_BM
    h, idx128, gate128, cnt = _router(x, norm_g, router_w, router_b)
    n_assign = n * TOP_K
    flat_e = idx128[:, :TOP_K].reshape(-1)
    flat_g = gate128[:, :TOP_K].reshape(-1)
    order = jnp.argsort(flat_e).astype(jnp.int32)
    counts = cnt[0, :N_EXPERTS].astype(jnp.int32)
    padded = (counts + bm - 1) // bm * bm
    start = jnp.cumsum(counts) - counts
    pend = jnp.cumsum(padded)
    pstart = pend - padded
    n_blocks = -(-n_assign // bm) + N_EXPERTS
    n_blocks += n_blocks % 2
    blk_start = jnp.arange(n_blocks, dtype=jnp.int32) * bm
    blk_exp = jnp.minimum(jnp.sum((pend[None, :] <= blk_start[:, None]).astype(jnp.int32), axis=1),
                          N_EXPERTS - 1).astype(jnp.int32)
    row = jnp.arange(bm, dtype=jnp.int32)[None, :]
    off = (blk_start - pstart[blk_exp])[:, None] + row
    valid = off < counts[blk_exp][:, None]
    a = order[jnp.clip(start[blk_exp][:, None] + off, 0, n_assign - 1).reshape(-1)].reshape(n_blocks, bm)
    nc = d // LANES
    dump_row = TOP_K * n + (jnp.arange(n_blocks, dtype=jnp.int32) % 2)[:, None] * bm + row
    pad_b = (dump_row[1:2] * nc).astype(jnp.int32)
    tok = jnp.concatenate([(jnp.where(valid, a // TOP_K, 0) * nc).astype(jnp.int32), jnp.zeros_like(pad_b)]).reshape(-1)
    dst = jnp.concatenate([(jnp.where(valid, (a % TOP_K) * n + a // TOP_K, dump_row) * nc).astype(jnp.int32),
                           pad_b]).reshape(-1)
    live = jnp.maximum((pend[-1] // bm + 1) // 2, 1).astype(jnp.int32).reshape(1)
    gates = jnp.where(valid, flat_g[a.reshape(-1)].reshape(n_blocks, bm), 0.0).reshape(-1)
    return _moe_experts(h, blk_exp, tok, dst, live, gates, w_gu, b_gu, w_down, b_down, bm=bm, layer=layer)


def _ple_kernel(x_ref, y0_ref, y1_ref, y2_ref, y3_ref, g_ref, wg_ref, p_ref, wp_ref, o_ref):
    tm, d = x_ref.shape
    nc = d // LANES

    def plane(y_ref):
        return jnp.concatenate([y_ref[pl.ds(c, tm, stride=nc), :] for c in range(nc)], axis=1)

    xf = x_ref[...] + (((plane(y0_ref) + plane(y1_ref)) + plane(y2_ref)) + plane(y3_ref))
    ms = jnp.mean(xf * xf, axis=-1, keepdims=True)
    h = (xf * lax.rsqrt(ms + EPS) * g_ref[...]).astype(BF16)
    gate = _sigmoid(jnp.dot(h, wg_ref[...], preferred_element_type=F32))
    pp = jnp.dot(p_ref[...].astype(BF16), wp_ref[...], preferred_element_type=F32)
    o_ref[...] = xf + gate * pp


def _ple(x, y, g, wg, p, wp, *, tm=512):
    n, d = x.shape
    pd = p.shape[1]
    tm = min(tm, n)
    nb = n // tm
    nc = d // LANES
    yspecs = [pl.BlockSpec((tm * nc, LANES), functools.partial(lambda i, k: (k * nb + i, 0), k=k)) for k in range(TOP_K)]
    return pl.pallas_call(
        _ple_kernel,
        out_shape=jax.ShapeDtypeStruct((n, d), F32),
        grid=(nb,),
        in_specs=[pl.BlockSpec((tm, d), lambda i: (i, 0))] + yspecs + [
            pl.BlockSpec((1, d), lambda i: (0, 0)),
            pl.BlockSpec((d, d), lambda i: (0, 0)),
            pl.BlockSpec((tm, pd), lambda i: (i, 0)),
            pl.BlockSpec((pd, d), lambda i: (0, 0)),
        ],
        out_specs=pl.BlockSpec((tm, d), lambda i: (i, 0)),
        compiler_params=_cparams(("parallel",)),
        name="ple",
    )(x, y, y, y, y, g.reshape(1, d).astype(F32), wg.astype(BF16), p, wp.astype(BF16))


def _cmp_kernel(z_ref, w1_ref, pos_ref, w2_ref, gain_ref, c_ref, s1_ref, s2_ref, o_ref, *, rows, half):
    j = pl.program_id(1)
    z = z_ref[0, 0, 0]
    pos = pos_ref[0]
    a1 = (z + pos[:, :half]).astype(BF16)
    a2 = (z + pos[:, half:]).astype(BF16)
    u1 = jnp.dot(a1, w1_ref[0, :half, :], preferred_element_type=F32)
    u2 = jnp.dot(a2, w1_ref[0, half:, :], preferred_element_type=F32)
    h = u1 + pltpu.roll(u2, rows - 1, 0)
    hs = h * _sigmoid(h)
    y = jnp.dot(hs.astype(BF16), w2_ref[0], preferred_element_type=F32)

    @pl.when(j == 0)
    def _():
        ss = jnp.sum(y * y, axis=-1, keepdims=True) * (1.0 / LANES)
        yn = y * lax.rsqrt(ss + EPS) * gain_ref[...]
        yn = (yn * c_ref[...] + pltpu.roll(yn, LANES - ROPE_DIM // 2, 1) * s1_ref[...]
              + pltpu.roll(yn, ROPE_DIM // 2, 1) * s2_ref[...])
        o_ref[0, 0, 0] = yn.astype(o_ref.dtype)

    @pl.when(j != 0)
    def _():
        o_ref[0, 0, 0] = y.astype(o_ref.dtype)


def _compress(zr, w1, pos, w2dup, gain_dup, tabs, *, batch):
    rows, width = zr.shape[3], zr.shape[4]
    hid = w1.shape[2]
    g = NSA_GROUPS
    return pl.pallas_call(
        functools.partial(_cmp_kernel, rows=rows, half=width),
        out_shape=jax.ShapeDtypeStruct((batch, 2, g, rows, LANES), BF16),
        grid=(batch, 2, g),
        in_specs=[
            pl.BlockSpec((1, 1, 1, rows, width), lambda b, j, gg: (b, j, gg, 0, 0)),
            pl.BlockSpec((1, 2 * width, hid), lambda b, j, gg: (j, 0, 0)),
            pl.BlockSpec((1, 1, 2 * width), lambda b, j, gg: (j, 0, 0)),
            pl.BlockSpec((1, hid, LANES), lambda b, j, gg: (j, 0, 0)),
            pl.BlockSpec((1, LANES), lambda b, j, gg: (0, 0)),
            pl.BlockSpec((rows, LANES), lambda b, j, gg: (0, 0)),
            pl.BlockSpec((rows, LANES), lambda b, j, gg: (0, 0)),
            pl.BlockSpec((rows, LANES), lambda b, j, gg: (0, 0)),
        ],
        out_specs=pl.BlockSpec((1, 1, 1, rows, LANES), lambda b, j, gg: (b, j, gg, 0, 0)),
        compiler_params=_cparams(("parallel", "parallel", "parallel")),
        name="nsa_compress",
    )(zr, w1, pos, w2dup, gain_dup, *tabs)


def _pipelined(n_items, first, second, skew):
    pending, out = {}, [None] * n_items
    for idx in range(n_items + skew):
        if idx < n_items:
            pending[idx] = first(idx)
        if idx >= skew:
            out[idx - skew] = second(idx - skew, pending.pop(idx - skew))
    return out


def _nsa_attn_kernel(q_ref, gz_ref, kc_ref, vct_ref, ovt_ref, ks_ref, vs_ref, kw_ref, vw_ref, o_ref,
                     e_ref, oc_ref, imp_ref, notsel_ref, *, n_cmp_rows, tk_sel, skew, grp, sub):
    g = pl.program_id(1)
    i = pl.program_id(2)
    qb = NSA_QB
    mh = HEADS_PER_GROUP
    rows = mh * qb
    q0 = i * qb
    seq = ks_ref.shape[0]

    @pl.when(i == 0)
    def _():
        r = lax.broadcasted_iota(jnp.int32, (seq, LANES), 0)
        lb = lax.broadcasted_iota(jnp.int32, (seq, LANES), 1)
        e_ref[...] = jnp.where((r >> 6) == lb, NEG, 0.0).astype(BF16)

    qh = [q_ref[:, m * LANES:(m + 1) * LANES] for m in range(mh)]
    q4 = jnp.concatenate(qh, axis=0)
    jl = lax.broadcasted_iota(jnp.int32, (qb, LANES), 1)

    tlane = q0 + (lax.broadcasted_iota(jnp.int32, (1, rows), 1) & (qb - 1))

    def cmp_branch(nrow):
        st = _dot_nt(kc_ref[0, 0, 0, :nrow, :], q4)
        cend = lax.broadcasted_iota(jnp.int32, (nrow, rows), 0) * CMP_STRIDE + (CMP_BLOCK - 1)
        valid = cend <= tlane
        st = jnp.where(valid, st, NEG)
        e = jnp.where(valid, jnp.exp2(st - jnp.max(st, axis=0, keepdims=True)), 0.0)
        pt = e / jnp.maximum(jnp.sum(e, axis=0, keepdims=True), 1e-30)
        oct = jnp.dot(vct_ref[0, 0, :, :nrow], pt.astype(BF16), preferred_element_type=F32)
        for m in range(mh):
            oc_ref[m * qb:(m + 1) * qb, :] = oct[:, m * qb:(m + 1) * qb].T
        psum = (pt[:, 0:qb] + pt[:, qb:2 * qb]) + (pt[:, 2 * qb:3 * qb] + pt[:, 3 * qb:4 * qb])
        phi, plo = _split2(psum)
        ovt = ovt_ref[:, :nrow]
        imp_ref[...] = (jnp.dot(ovt, phi, preferred_element_type=F32)
                        + jnp.dot(ovt, plo, preferred_element_type=F32))

    n_chunks = n_cmp_rows // LANES
    last_visible = (q0 + qb - CMP_BLOCK) // CMP_STRIDE
    need = jnp.clip(last_visible // LANES + 1, 1, n_chunks)
    for nch in range(1, n_chunks + 1):
        pl.when(need == nch)(functools.partial(cmp_branch, nch * LANES))

    def select_blocks(nrow):
        jb = lax.broadcasted_iota(jnp.int32, (nrow, qb), 0)
        tq = q0 + lax.broadcasted_iota(jnp.int32, (nrow, qb), 1)
        cur = tq >> 6
        forced = (jb == 0) | (jb == cur) | (jb == cur - 1)
        causal = (jb << 6) <= tq
        score = jnp.where(causal, jnp.where(forced, FORCE, imp_ref[:nrow, :]), -jnp.inf)
        notsel_t = jnp.ones((nrow, qb), F32)
        jbf = jb.astype(F32)
        for _ in range(SEL_TOPK):
            smx = jnp.max(score, axis=0, keepdims=True)
            ix = jnp.min(jnp.where(score == smx, jbf, float(LANES)), axis=0, keepdims=True)
            hit = jbf == ix
            notsel_t = jnp.where(hit, 0.0, notsel_t)
            score = jnp.where(hit, -jnp.inf, score)
        notsel_ref[:nrow, :] = notsel_t
        if nrow < LANES:
            notsel_ref[nrow:, :] = jnp.ones((LANES - nrow, qb), F32)

    sel_chunk = 32
    n_sel_chunks = LANES // sel_chunk
    sel_need = jnp.clip(((q0 + qb - 1) // SEL_BLOCK) // sel_chunk + 1, 1, n_sel_chunks)
    for nch in range(1, n_sel_chunks + 1):
        pl.when(sel_need == nch)(functools.partial(select_blocks, nch * sel_chunk))
    notsel_b = notsel_ref[...].T.astype(BF16)
    n_grp = mh // grp
    q_aug = [jnp.concatenate([jnp.concatenate([qh[m], notsel_b], axis=1) for m in range(c * grp, (c + 1) * grp)], axis=0)
             for c in range(n_grp)]
    tpos = q0 + lax.broadcasted_iota(jnp.int32, (qb, 1), 0)
    tpos_g = jnp.concatenate([tpos] * grp, axis=0)

    def sel_tiles(first, n_sub):
        ks, vs = [], []
        for t in range(n_sub):
            k0 = pl.multiple_of((first + t) * tk_sel, tk_sel)
            k_aug = jnp.concatenate([ks_ref[pl.ds(k0, tk_sel), :], e_ref[pl.ds(k0, tk_sel), :]], axis=1)
            ks.append([k_aug] * n_grp)
            vs.append([vs_ref[pl.ds(k0, tk_sel), :]] * n_grp)
        return ks, vs

    def sel_step(first, n_sub, state):
        ks, vs = sel_tiles(first, n_sub)
        return _attend(q_aug, ks, vs, state, skew)

    def sel_tail(first, n_rem, state):
        def mask(t, c, sc):
            if t < n_rem:
                return sc
            kpos = (first + n_rem) * tk_sel + lax.broadcasted_iota(jnp.int32, (grp * qb, tk_sel), 1)
            return jnp.where(kpos <= tpos_g, sc, NEG)

        ks, vs = sel_tiles(first, n_rem + 1)
        return _attend(q_aug, ks, vs, state, skew, mask=mask)

    state = tuple((jnp.full((grp * qb, 1), NEG, F32), jnp.zeros((grp * qb, LANES), F32)) for _ in range(n_grp))
    n_below = q0 // tk_sel
    state = _tile_loops(n_below, sub, sel_step, sel_tail, state)
    sel_acc = [state[m // grp][1][(m % grp) * qb:(m % grp + 1) * qb] for m in range(mh)]

    wk = WINDOW + qb
    k0w = pl.multiple_of(jnp.maximum(i - WINDOW // qb, 0) * qb, qb)
    kwt = kw_ref[pl.ds(k0w, wk), :]
    vwt = vw_ref[pl.ds(k0w, wk), :]
    kposw = k0w + lax.broadcasted_iota(jnp.int32, (grp * qb, wk), 1)
    wmask = (kposw <= tpos_g) & (kposw > tpos_g - WINDOW)
    q_grp = [jnp.concatenate(qh[c * grp:(c + 1) * grp], axis=0) for c in range(n_grp)]

    def win_scores(c):
        return jnp.where(wmask, _dot_nt(q_grp[c], kwt), NEG)

    def win_out(c, sc):
        pe = jnp.exp2(sc - jnp.max(sc, axis=-1, keepdims=True))
        return jnp.dot(pe.astype(BF16), vwt, preferred_element_type=F32)

    win_grp = _pipelined(n_grp, win_scores, win_out, min(skew, n_grp - 1))
    win_acc = [win_grp[m // grp][(m % grp) * qb:(m % grp + 1) * qb] for m in range(mh)]

    sig = _sigmoid(gz_ref[...])
    den_lane = jl == HEAD_DIM
    heads = []
    for m in range(mh):
        col = g * mh + m
        gates = [jnp.sum(jnp.where(jl == br * N_HEADS + col, sig, 0.0), axis=-1, keepdims=True)
                 for br in range(3)]
        acc_s = sel_acc[m]
        acc_w = win_acc[m]
        den_s = jnp.sum(jnp.where(den_lane, acc_s, 0.0), axis=-1, keepdims=True)
        den_w = jnp.sum(jnp.where(den_lane, acc_w, 0.0), axis=-1, keepdims=True)
        heads.append(gates[0] * oc_ref[m * qb:(m + 1) * qb, :] + (gates[1] / den_s) * acc_s
                     + (gates[2] / den_w) * acc_w)
    lane_lo = jl < HEAD_DIM
    o_ref[:, 0:LANES] = jnp.where(lane_lo, heads[0], pltpu.roll(heads[1], HEAD_DIM, 1)).astype(o_ref.dtype)
    o_ref[:, LANES:2 * LANES] = jnp.where(lane_lo, heads[2], pltpu.roll(heads[3], HEAD_DIM, 1)).astype(o_ref.dtype)


def _nsa_attn(qn, gz, kvc, overlap, kvd, *, batch, seq, skew=2, grp=2, sub=4):
    n = qn.shape[0]
    qb = NSA_QB
    nq = seq // qb
    g = NSA_GROUPS
    rows = HEADS_PER_GROUP * qb
    n_cmp_rows = kvc.shape[3]
    tk_sel = min(512, seq)
    assert seq >= WINDOW + qb and seq % tk_sel == 0 and tk_sel % qb == 0
    in_specs = [
        pl.BlockSpec((qb, HEADS_PER_GROUP * LANES), lambda b, gg, i: (b * nq + i, gg)),
        pl.BlockSpec((qb, LANES), lambda b, gg, i: (b * nq + i, 0)),
        pl.BlockSpec((1, 1, 1, n_cmp_rows, LANES), lambda b, gg, i: (b, 0, gg, 0, 0)),
        pl.BlockSpec((1, 1, LANES, n_cmp_rows), lambda b, gg, i: (b, gg, 0, 0)),
        pl.BlockSpec((LANES, n_cmp_rows), lambda b, gg, i: (0, 0)),
        pl.BlockSpec((seq, LANES), lambda b, gg, i: (b, 0 * g + gg)),
        pl.BlockSpec((seq, LANES), lambda b, gg, i: (b, 1 * g + gg)),
        pl.BlockSpec((seq, LANES), lambda b, gg, i: (b, 2 * g + gg)),
        pl.BlockSpec((seq, LANES), lambda b, gg, i: (b, 3 * g + gg)),
    ]
    return pl.pallas_call(
        functools.partial(_nsa_attn_kernel, n_cmp_rows=n_cmp_rows, tk_sel=tk_sel, skew=skew, grp=grp, sub=sub),
        out_shape=jax.ShapeDtypeStruct((n, N_HEADS * HEAD_DIM), BF16),
        grid=(batch, g, nq),
        in_specs=in_specs,
        out_specs=pl.BlockSpec((qb, HEADS_PER_GROUP * HEAD_DIM), lambda b, gg, i: (b * nq + i, gg)),
        scratch_shapes=[pltpu.VMEM((seq, LANES), BF16), pltpu.VMEM((rows, LANES), F32), pltpu.VMEM((LANES, qb), F32),
                        pltpu.VMEM((LANES, qb), F32)],
        compiler_params=_cparams(("parallel", "parallel", "arbitrary")),
        name="nsa_attn",
    )(qn, gz, kvc, jnp.swapaxes(kvc[:, 1], -1, -2), overlap.T, kvd, kvd, kvd, kvd)


def _pad_cols(w, width=LANES):
    d, f = w.shape
    return jnp.zeros((d, width), w.dtype).at[:, :f].set(w)


def _tile_gain(gain, n_heads, scale=1.0, dup=False):
    g = gain.astype(F32) * scale
    second = g if dup else jnp.zeros_like(g)
    return jnp.tile(jnp.concatenate([g, second]), n_heads)


def _fox_aug_tables():
    hw = N_HEADS * LANES
    piece = jnp.arange(3)[:, None]
    head = jnp.arange(N_HEADS)[None, :]
    src = (piece * N_HEADS + head).reshape(-1)
    q_dst = (head * LANES + HEAD_DIM + piece).reshape(-1)
    k_dst = (hw + head * LANES + HEAD_DIM + 3 + piece).reshape(-1)
    place = jnp.zeros((LANES, 3 * hw), F32).at[src, q_dst].set(1.0).at[src, k_dst].set(-1.0)
    lane = jnp.arange(LANES)
    q_bias = ((lane >= HEAD_DIM + 3) & (lane < HEAD_DIM + 6)).astype(F32)
    k_bias = ((lane >= HEAD_DIM) & (lane < HEAD_DIM + 3)).astype(F32)
    v_bias = (lane == HEAD_DIM).astype(F32)
    bias = jnp.concatenate([jnp.tile(q_bias, N_HEADS), jnp.tile(k_bias, N_HEADS), jnp.tile(v_bias, N_HEADS)])
    return place.astype(BF16), bias


def _fox_layer(x, batch, seq, attn_g, w_in, b_f, qk_gain, w_out):
    hd = N_HEADS * HEAD_DIM
    hw = N_HEADS * LANES
    scale = HEAD_DIM ** -0.5
    z = _rms_proj(x, attn_g, _pad_cols(w_in[:, 3 * hd:]).astype(BF16), out_dtype=F32, seq=seq)
    b_pad = jnp.zeros((1, LANES), F32).at[0, :N_HEADS].set(b_f.astype(F32))
    c3 = _fox_c(z, b_pad, batch=batch, seq=seq)
    w_bf = w_in.astype(BF16)
    gain = jnp.concatenate([_tile_gain(qk_gain[0], N_HEADS, scale * LOG2E), _tile_gain(qk_gain[1], N_HEADS),
                            jnp.zeros((hw,), F32)])
    tn = 1024
    place, bias = _fox_aug_tables()
    qkv = _rms_proj(x, attn_g, w_bf, out_dtype=BF16, seq=seq, tn=tn, gain=gain, w_cols=3 * hd,
                    norm_blocks=tuple(range(2 * hw // tn)), placed=(c3, place), col_bias=bias, expand=True)
    o = _fox_attn(qkv, batch=batch, seq=seq)
    return _matmul_res(o, w_out.astype(BF16), x)


def _nsa_shared_kv(x, batch, seq, kv_norm, kv_w, kv_k_gain, cmp_pos, cmp_w1, cmp_w2):
    g = NSA_GROUPS
    gw = g * HEAD_DIM
    tabs = _rope_tables(jnp.arange(seq))
    kv_bf = kv_w.astype(BF16)
    zero = jnp.zeros((g * LANES,), F32)
    one_lane = jnp.tile((jnp.arange(LANES) == HEAD_DIM).astype(F32), g)
    gain_kvd = jnp.concatenate([_tile_gain(kv_k_gain[1], g), zero, _tile_gain(kv_k_gain[2], g), zero])
    bias_kvd = jnp.concatenate([zero, one_lane, zero, one_lane])
    kvd = _rms_proj(x, kv_norm, kv_bf, out_dtype=BF16, seq=seq, tn=g * LANES, gain=gain_kvd, w_col0=2 * gw,
                    norm_blocks=(0, 2), rope_tabs=tabs, col_bias=bias_kvd, expand=True)
    zc = _rms_proj(x, kv_norm, kv_bf, out_dtype=F32, seq=seq, w_cols=2 * gw)
    rows = seq // CMP_STRIDE
    zr = zc.reshape(batch, rows, CMP_STRIDE, 2, g, HEAD_DIM).transpose(0, 3, 4, 1, 2, 5)
    zr = zr.reshape(batch, 2, g, rows, CMP_STRIDE * HEAD_DIM)
    cmp_end = jnp.arange(rows) * CMP_STRIDE + CMP_BLOCK - 1
    ctabs = _rope_tables(cmp_end)
    w2dup = jnp.concatenate([cmp_w2, cmp_w2], axis=-1).astype(BF16)
    kvc = _compress(zr, cmp_w1.astype(BF16), cmp_pos.reshape(2, 1, CMP_BLOCK * HEAD_DIM).astype(F32), w2dup,
                    _tile_gain(kv_k_gain[0], 1, dup=True).reshape(1, LANES), ctabs, batch=batch)
    return kvc, kvd


def _nsa_layer(x, batch, seq, attn_g, w_in, q_gain, w_out, kvc, kvd):
    hd = N_HEADS * HEAD_DIM
    scale = HEAD_DIM ** -0.5
    rows = seq // CMP_STRIDE
    tabs = _rope_tables(jnp.arange(seq))
    qn = _rms_proj(x, attn_g, w_in.astype(BF16), out_dtype=BF16, seq=seq, expand=True, w_cols=hd,
                   gain=_tile_gain(q_gain, N_HEADS, scale * LOG2E), norm_blocks=tuple(range(N_HEADS * LANES // 512)),
                   rope_tabs=tabs)
    gz = _rms_proj(x, attn_g, _pad_cols(w_in[:, hd:]).astype(BF16), out_dtype=F32, seq=seq)

    n_sel = seq // SEL_BLOCK
    assert n_sel <= LANES
    cmp_start = jnp.arange(rows) * CMP_STRIDE
    sel_start = jnp.arange(LANES) * SEL_BLOCK
    overlap = jnp.clip(jnp.minimum(cmp_start[:, None] + CMP_BLOCK, sel_start[None, :] + SEL_BLOCK)
                       - jnp.maximum(cmp_start[:, None], sel_start[None, :]), 0)
    overlap = jnp.where((jnp.arange(rows) < rows - 1)[:, None] & (jnp.arange(LANES) < n_sel)[None, :], overlap, 0)
    o = _nsa_attn(qn, gz, kvc, overlap.astype(BF16), kvd, batch=batch, seq=seq)
    return _matmul_res(o, w_out.astype(BF16), x)


def kernel(x, p, attn_norm, ffn_norm, ple_norm, ple_gate_w, ple_proj_w, router_w, router_b, w_gu, b_gu, w_down, b_down, fox_w_in, fox_b_f, fox_qk_gain, fox_w_out, kv_norm, kv_w, kv_k_gain, cmp_pos, cmp_w1, cmp_w2, nsa_w_in, nsa_q_gain, nsa_w_out):
    batch, seq, d = x.shape
    depth = p.shape[0]
    n_a = fox_w_in.shape[0]
    xt = x.reshape(batch * seq, d)
    w_gu_bf = w_gu.astype(BF16)
    w_down_bf = w_down.astype(BF16)
    shared = None
    for layer in range(depth):
        if layer == n_a:
            shared = _nsa_shared_kv(xt, batch, seq, kv_norm, kv_w, kv_k_gain, cmp_pos, cmp_w1, cmp_w2)
        if layer < n_a:
            xt = _fox_layer(xt, batch, seq, attn_norm[layer], fox_w_in[layer], fox_b_f[layer],
                            fox_qk_gain[layer], fox_w_out[layer])
        else:
            i = layer - n_a
            xt = _nsa_layer(xt, batch, seq, attn_norm[layer], nsa_w_in[i], nsa_q_gain[i], nsa_w_out[i], *shared)
        y = _moe_layer(xt, ffn_norm[layer], router_w[layer], router_b[layer], w_gu_bf, b_gu[layer],
                       w_down_bf, b_down[layer], layer)
        xt = _ple(xt, y, ple_norm[layer], ple_gate_w[layer], p[layer].reshape(batch * seq, -1), ple_proj_w[layer])
    return xt.reshape(batch, seq, d)
```

```python
import functools

import jax
import jax.numpy as jnp
from jax import lax
from jax.experimental import pallas as pl
from jax.experimental.pallas import tpu as pltpu

F32 = jnp.float32
BF16 = jnp.bfloat16

N_HEADS = 16
HEAD_DIM = 64
ROPE_DIM = HEAD_DIM // 4
ROPE_THETA = 500000.0
NSA_GROUPS = 4
HEADS_PER_GROUP = N_HEADS // NSA_GROUPS
CMP_BLOCK = 32
CMP_STRIDE = 16
SEL_BLOCK = 64
SEL_TOPK = 16
WINDOW = 512
N_EXPERTS = 32
TOP_K = 4
SWIGLU_LIMIT = 7.0
SWIGLU_ALPHA = 1.702
EPS = 1e-6
NEG = -1e30
FORCE = 1e6
LOG2E = 1.4426950408889634

LANES = 128
NSA_QB = 256
MOE_BM = 256
VMEM_LIMIT = 56 * 1024 * 1024


def _cparams(sem):
    return pltpu.CompilerParams(dimension_semantics=sem, vmem_limit_bytes=VMEM_LIMIT)


def _sigmoid(x):
    return 1.0 / (1.0 + jnp.exp(-x))


def _dot_nt(a, b):
    return lax.dot_general(a, b, (((1,), (1,)), ((), ())), preferred_element_type=F32)


def _split2(x):
    hi = x.astype(BF16)
    lo = (x - hi.astype(F32)).astype(BF16)
    return hi, lo


def _proj_kernel(*refs, norm_blocks, rope, placed, biased, expand, head_div, tn, out_dtype):
    refs = list(refs)
    x_ref, g_ref, w_ref = refs[:3]
    del refs[:3]
    if norm_blocks:
        gain_ref = refs.pop(0)
    if rope:
        c_ref, s1_ref, s2_ref = refs[:3]
        del refs[:3]
    if placed:
        ex_ref, place_ref = refs[:2]
        del refs[:2]
    if biased:
        bias_ref = refs.pop(0)
    o_ref, h_ref = refs
    j = pl.program_id(1)
    aug = placed or biased

    @pl.when(j == 0)
    def _():
        xf = x_ref[...]
        ms = jnp.mean(xf * xf, axis=-1, keepdims=True)
        h_ref[...] = (xf * lax.rsqrt(ms + EPS) * g_ref[...]).astype(BF16)

    acc = jnp.dot(h_ref[...], w_ref[...], preferred_element_type=F32)
    if placed and biased:
        extra = jnp.dot(ex_ref[...], place_ref[...], preferred_element_type=F32) + bias_ref[...]
    elif biased:
        extra = jnp.broadcast_to(bias_ref[...], (acc.shape[0], tn))
    if expand:
        lo = lax.broadcasted_iota(jnp.int32, (acc.shape[0], LANES), 1) < HEAD_DIM
        blocks = []
        for c in range(acc.shape[1] // LANES):
            a = acc[:, c * LANES:(c + 1) * LANES]
            blocks += [jnp.where(lo, a, 0.0), jnp.where(lo, pltpu.roll(a, HEAD_DIM, 1), 0.0)]
    else:
        blocks = [acc[:, c * LANES:(c + 1) * LANES] for c in range(acc.shape[1] // LANES)]

    def plain():
        for c, a in enumerate(blocks):
            sl = slice(c * LANES, (c + 1) * LANES)
            o_ref[:, sl] = ((a + extra[:, sl]) if aug else a).astype(out_dtype)

    if not norm_blocks:
        plain()
        return
    is_norm = functools.reduce(jnp.logical_or, [j == c for c in norm_blocks])

    @pl.when(is_norm)
    def _():
        for c, a in enumerate(blocks):
            sl = slice(c * LANES, (c + 1) * LANES)
            ss = jnp.sum(a * a, axis=-1, keepdims=True) * (1.0 / head_div)
            y = a * lax.rsqrt(ss + EPS) * gain_ref[:, sl]
            if rope:
                y = (y * c_ref[...] + pltpu.roll(y, LANES - ROPE_DIM // 2, 1) * s1_ref[...]
                     + pltpu.roll(y, ROPE_DIM // 2, 1) * s2_ref[...])
            if aug:
                y = y + extra[:, sl]
            o_ref[:, sl] = y.astype(out_dtype)

    @pl.when(jnp.logical_not(is_norm))
    def _():
        plain()


def _rms_proj(x, g, w, *, out_dtype, seq, tm=2048, tn=512, gain=None, norm_blocks=(),
              rope_tabs=None, placed=None, col_bias=None, expand=False, head_div=HEAD_DIM,
              w_col0=0, w_cols=None):
    n, d = x.shape
    w_cols = w.shape[1] - w_col0 if w_cols is None else w_cols
    f = w_cols * (2 if expand else 1)
    tn = min(tn, f)
    tw = tn // 2 if expand else tn
    assert w_col0 % tw == 0
    j0 = w_col0 // tw
    tm = min(tm, seq)
    assert n % tm == 0 and f % tn == 0 and seq % tm == 0
    nt = seq // tm
    in_specs = [
        pl.BlockSpec((tm, d), lambda i, j: (i, 0)),
        pl.BlockSpec((1, d), lambda i, j: (0, 0)),
        pl.BlockSpec((d, tw), lambda i, j: (0, j0 + j)),
    ]
    args = [x, g.reshape(1, d).astype(F32), w]
    if norm_blocks:
        in_specs.append(pl.BlockSpec((1, tn), lambda i, j: (0, j)))
        args.append(gain.reshape(1, f).astype(F32))
    if rope_tabs is not None:
        assert norm_blocks
        for tab in rope_tabs:
            in_specs.append(pl.BlockSpec((tm, LANES), lambda i, j: (i % nt, 0)))
            args.append(tab)
    if placed is not None:
        assert col_bias is not None
        values, place = placed
        in_specs += [pl.BlockSpec((tm, LANES), lambda i, j: (i, 0)),
                     pl.BlockSpec((LANES, tn), lambda i, j: (0, j))]
        args += [values, place]
    if col_bias is not None:
        in_specs.append(pl.BlockSpec((1, tn), lambda i, j: (0, j)))
        args.append(col_bias.reshape(1, f).astype(F32))
    kern = functools.partial(_proj_kernel, norm_blocks=tuple(norm_blocks), rope=rope_tabs is not None,
                             placed=placed is not None, biased=col_bias is not None, expand=expand,
                             head_div=float(head_div), tn=tn, out_dtype=out_dtype)
    return pl.pallas_call(
        kern,
        out_shape=jax.ShapeDtypeStruct((n, f), out_dtype),
        grid=(n // tm, f // tn),
        in_specs=in_specs,
        out_specs=pl.BlockSpec((tm, tn), lambda i, j: (i, j)),
        scratch_shapes=[pltpu.VMEM((tm, d), BF16)],
        compiler_params=_cparams(("parallel", "arbitrary")),
        name="rms_proj",
    )(*args)


def _rope_tables(pos, width=LANES):
    half = ROPE_DIM // 2
    inv = jnp.power(jnp.float32(ROPE_THETA), -jnp.arange(0, ROPE_DIM, 2, dtype=F32) / ROPE_DIM)
    ang = pos.astype(F32)[:, None] * inv[None, :]
    cos, sin = jnp.cos(ang), jnp.sin(ang)
    t = pos.shape[0]
    ones = jnp.ones((t, HEAD_DIM - ROPE_DIM), F32)
    zeros_h = jnp.zeros((t, half), F32)
    zeros_r = jnp.zeros((t, HEAD_DIM - ROPE_DIM), F32)
    c = jnp.concatenate([cos, cos, ones], axis=1)
    s1 = jnp.concatenate([-sin, zeros_h, zeros_r], axis=1)
    s2 = jnp.concatenate([zeros_h, sin, zeros_r], axis=1)
    rep = width // HEAD_DIM
    return tuple(jnp.tile(a, (1, rep)) for a in (c, s1, s2))


def _matmul_res_kernel(a_ref, w_ref, x_ref, o_ref):
    o_ref[...] = x_ref[...] + jnp.dot(a_ref[...], w_ref[...], preferred_element_type=F32)


def _matmul_res(a, w, x, *, tm=512):
    n, k = a.shape
    f = w.shape[1]
    tm = min(tm, n)
    return pl.pallas_call(
        _matmul_res_kernel,
        out_shape=jax.ShapeDtypeStruct((n, f), F32),
        grid=(n // tm,),
        in_specs=[
            pl.BlockSpec((tm, k), lambda i: (i, 0)),
            pl.BlockSpec((k, f), lambda i: (0, 0)),
            pl.BlockSpec((tm, f), lambda i: (i, 0)),
        ],
        out_specs=pl.BlockSpec((tm, f), lambda i: (i, 0)),
        compiler_params=_cparams(("parallel",)),
        name="matmul_res",
    )(a, w, x)


def _fox_c_kernel(z_ref, b_ref, c_ref, carry_ref, *, tm):
    t = pl.program_id(1)

    @pl.when(t == 0)
    def _():
        carry_ref[...] = jnp.zeros_like(carry_ref)

    u = z_ref[...] + b_ref[...]
    logf = jnp.minimum(u, 0.0) - jnp.log(1.0 + jnp.exp(-jnp.abs(u)))
    row = lax.broadcasted_iota(jnp.int32, (tm, tm), 0)
    col = lax.broadcasted_iota(jnp.int32, (tm, tm), 1)
    tri = (row >= col).astype(BF16)
    hi = logf.astype(BF16)
    r1 = logf - hi.astype(F32)
    mid = r1.astype(BF16)
    lo = (r1 - mid.astype(F32)).astype(BF16)
    cs = (jnp.dot(tri, hi, preferred_element_type=F32) + jnp.dot(tri, mid, preferred_element_type=F32)
          + jnp.dot(tri, lo, preferred_element_type=F32))
    c = cs + carry_ref[...]
    carry_ref[...] = c[tm - 1:tm, :]
    c2 = c * LOG2E
    p0 = c2.astype(BF16).astype(F32)
    p1 = (c2 - p0).astype(BF16).astype(F32)
    p2 = ((c2 - p0) - p1).astype(BF16).astype(F32)
    lane = lax.broadcasted_iota(jnp.int32, c.shape, 1)
    out = jnp.where(lane < N_HEADS, p0,
                    jnp.where(lane < 2 * N_HEADS, pltpu.roll(p1, N_HEADS, 1),
                              jnp.where(lane < 3 * N_HEADS, pltpu.roll(p2, 2 * N_HEADS, 1), 0.0)))
    c_ref[...] = out.astype(BF16)


def _fox_c(z, b_pad, *, batch, seq, tm=512):
    tm = min(tm, seq)
    nt = seq // tm
    return pl.pallas_call(
        functools.partial(_fox_c_kernel, tm=tm),
        out_shape=jax.ShapeDtypeStruct(z.shape, BF16),
        grid=(batch, nt),
        in_specs=[
            pl.BlockSpec((tm, LANES), lambda b, t: (b * nt + t, 0)),
            pl.BlockSpec((1, LANES), lambda b, t: (0, 0)),
        ],
        out_specs=pl.BlockSpec((tm, LANES), lambda b, t: (b * nt + t, 0)),
        scratch_shapes=[pltpu.VMEM((1, LANES), F32)],
        compiler_params=_cparams(("parallel", "arbitrary")),
        name="fox_cumsum",
    )(z, b_pad)


def _attend(qs, ks, vs, state, skew, mask=None):
    items = [(t, c) for t in range(len(ks)) for c in range(len(qs))]
    state = list(state)
    pending = {}
    depth = min(skew, len(items) - 1)
    for idx in range(len(items) + depth):
        if idx < len(items):
            t, c = items[idx]
            sc = _dot_nt(qs[c], ks[t][c])
            pending[idx] = sc if mask is None else mask(t, c, sc)
        if idx >= depth:
            t, c = items[idx - depth]
            sc = pending.pop(idx - depth)
            m_prev, acc = state[c]
            m_new = jnp.maximum(m_prev, jnp.max(sc, axis=-1, keepdims=True))
            pe = jnp.exp2(sc - m_new)
            acc = jnp.exp2(m_prev - m_new) * acc + jnp.dot(pe.astype(BF16), vs[t][c], preferred_element_type=F32)
            state[c] = (m_new, acc)
    return tuple(state)


def _tile_loops(n_tiles, sub, step, tail, state):
    n_full = n_tiles // sub
    state = lax.fori_loop(0, n_full, lambda j, st: step(j * sub, sub, st), state)
    first = n_full * sub
    return lax.switch(n_tiles - first, [functools.partial(tail, first, r) for r in range(sub)], state)


def _fox_attn_kernel(q0_ref, q1_ref, k0_ref, k1_ref, v0_ref, v1_ref, o_ref, *, tq, tk, skew, sub):
    qi = pl.program_id(2)
    k_refs = (k0_ref, k1_ref)
    v_refs = (v0_ref, v1_ref)
    qs = [q0_ref[...], q1_ref[...]]

    def tiles(refs, first, n_sub):
        return [[r[pl.ds(pl.multiple_of((first + t) * tk, tk), tk), :] for r in refs] for t in range(n_sub)]

    def full_step(first, n_sub, state):
        return _attend(qs, tiles(k_refs, first, n_sub), tiles(v_refs, first, n_sub), state, skew)

    def causal(t, c, sc):
        tpos = lax.broadcasted_iota(jnp.int32, (tq, tk), 0)
        kpos = t * tk + lax.broadcasted_iota(jnp.int32, (tq, tk), 1)
        return jnp.where(kpos <= tpos, sc, NEG)

    n_diag = tq // tk

    def tail_step(first, n_rem, state):
        def mask(t, c, sc):
            return causal(t - n_rem, c, sc) if t >= n_rem else sc

        n = n_rem + n_diag
        return _attend(qs, tiles(k_refs, first, n), tiles(v_refs, first, n), state, skew, mask=mask)

    state = tuple((jnp.full((tq, 1), NEG, F32), jnp.zeros((tq, LANES), F32)) for _ in qs)
    n_below = (qi * tq) // tk
    state = _tile_loops(n_below, sub, full_step, tail_step, state)

    lane = lax.broadcasted_iota(jnp.int32, (tq, LANES), 1)
    outs = []
    for _, acc in state:
        den = jnp.sum(jnp.where(lane == HEAD_DIM, acc, 0.0), axis=-1, keepdims=True)
        outs.append(acc / den)
    o_ref[...] = jnp.where(lane < HEAD_DIM, outs[0], pltpu.roll(outs[1], HEAD_DIM, 1)).astype(o_ref.dtype)


def _fox_attn(qkv, *, batch, seq, tq=512, tk=512, skew=2, sub=4):
    n = qkv.shape[0]
    tq = min(tq, seq)
    tk = min(tk, tq)
    assert seq % tq == 0 and tq % tk == 0
    nq = seq // tq
    hp = N_HEADS // 2
    kb = N_HEADS
    vb = 2 * N_HEADS
    in_specs = [
        pl.BlockSpec((tq, LANES), lambda b, h, i: (b * nq + i, 2 * h)),
        pl.BlockSpec((tq, LANES), lambda b, h, i: (b * nq + i, 2 * h + 1)),
        pl.BlockSpec((seq, LANES), lambda b, h, i: (b, kb + 2 * h)),
        pl.BlockSpec((seq, LANES), lambda b, h, i: (b, kb + 2 * h + 1)),
        pl.BlockSpec((seq, LANES), lambda b, h, i: (b, vb + 2 * h)),
        pl.BlockSpec((seq, LANES), lambda b, h, i: (b, vb + 2 * h + 1)),
    ]
    return pl.pallas_call(
        functools.partial(_fox_attn_kernel, tq=tq, tk=tk, skew=skew, sub=sub),
        out_shape=jax.ShapeDtypeStruct((n, N_HEADS * HEAD_DIM), BF16),
        grid=(batch, hp, nq),
        in_specs=in_specs,
        out_specs=pl.BlockSpec((tq, LANES), lambda b, h, i: (b * nq + i, h)),
        compiler_params=_cparams(("parallel", "parallel", "arbitrary")),
        name="fox_attn",
    )(qkv, qkv, qkv, qkv, qkv, qkv)


def _router_kernel(x_ref, g_ref, whi_ref, wlo_ref, b_ref, h_ref, idx_ref, gate_ref, cnt_ref):
    xf = x_ref[...]
    ms = jnp.mean(xf * xf, axis=-1, keepdims=True)
    h = xf * lax.rsqrt(ms + EPS) * g_ref[...]
    nc = h.shape[1] // LANES
    for c in range(nc):
        h_ref[pl.ds(c, h.shape[0], stride=nc), :] = h[:, c * LANES:(c + 1) * LANES]
    hhi, hlo = _split2(h)
    whi = whi_ref[...]
    logits = (jnp.dot(hhi, whi, preferred_element_type=F32) + jnp.dot(hlo, whi, preferred_element_type=F32)
              + jnp.dot(hhi, wlo_ref[...], preferred_element_type=F32)) + b_ref[...]
    lane = lax.broadcasted_iota(jnp.int32, logits.shape, 1).astype(F32)
    vals, idxs = [], []
    cur = logits
    for _ in range(TOP_K):
        mx = jnp.max(cur, axis=-1, keepdims=True)
        ix = jnp.min(jnp.where(cur == mx, lane, float(LANES)), axis=-1, keepdims=True)
        vals.append(mx)
        idxs.append(ix)
        cur = jnp.where(lane == ix, -jnp.inf, cur)
    es = [jnp.exp(v - vals[0]) for v in vals]
    tot = es[0] + es[1] + es[2] + es[3]
    idx_out = jnp.zeros(logits.shape, F32)
    gate_out = jnp.zeros(logits.shape, F32)
    for r in range(TOP_K):
        idx_out = jnp.where(lane == r, idxs[r], idx_out)
        gate_out = jnp.where(lane == r, es[r] / tot, gate_out)
    idx_ref[...] = idx_out.astype(jnp.int32)
    gate_ref[...] = gate_out
    hits = sum(jnp.sum((lane == ix).astype(F32), axis=0, keepdims=True) for ix in idxs)

    @pl.when(pl.program_id(0) == 0)
    def _():
        cnt_ref[...] = jnp.zeros(cnt_ref.shape, cnt_ref.dtype)

    cnt_ref[...] += jnp.broadcast_to(hits, cnt_ref.shape)


def _router(x, g, w, b, *, tm=512):
    n, d = x.shape
    e = w.shape[1]
    tm = min(tm, n)
    w_pad = jnp.zeros((d, LANES), F32).at[:, :e].set(w)
    whi = w_pad.astype(BF16)
    wlo = (w_pad - whi.astype(F32)).astype(BF16)
    b_pad = jnp.full((1, LANES), -jnp.inf, F32).at[0, :e].set(b.astype(F32))
    return pl.pallas_call(
        _router_kernel,
        out_shape=(jax.ShapeDtypeStruct((n * (d // LANES), LANES), F32), jax.ShapeDtypeStruct((n, LANES), jnp.int32),
                   jax.ShapeDtypeStruct((n, LANES), F32), jax.ShapeDtypeStruct((8, LANES), F32)),
        grid=(n // tm,),
        in_specs=[
            pl.BlockSpec((tm, d), lambda i: (i, 0)),
            pl.BlockSpec((1, d), lambda i: (0, 0)),
            pl.BlockSpec((d, LANES), lambda i: (0, 0)),
            pl.BlockSpec((d, LANES), lambda i: (0, 0)),
            pl.BlockSpec((1, LANES), lambda i: (0, 0)),
        ],
        out_specs=(pl.BlockSpec((tm * (d // LANES), LANES), lambda i: (i, 0)), pl.BlockSpec((tm, LANES), lambda i: (i, 0)),
                   pl.BlockSpec((tm, LANES), lambda i: (i, 0)), pl.BlockSpec((8, LANES), lambda i: (0, 0))),
        compiler_params=_cparams(("arbitrary",)),
        name="moe_router",
    )(x, g.reshape(1, d).astype(F32), whi, wlo, b_pad)


def _moe_expert_kernel(blk_exp_ref, tok_ref, dst_ref, live_ref, h_hbm, ga_ref, gb_ref, wgu_a, bgu_a, wd_a, bd_a,
                       wgu_b, bgu_b, wd_b, bd_b, y_hbm, xa, xb, oa, ob, gsem, ssem, *, bm, dff):
    i = pl.program_id(0)
    live = live_ref[0]
    nc = xa.shape[0] // bm
    dump = y_hbm.shape[0] // nc - 2 * bm

    def gather(block, buf, sem):
        base = block * bm
        for r in range(bm):
            pltpu.make_async_copy(h_hbm.at[pl.ds(pl.multiple_of(tok_ref[base + r], nc), nc)], buf.at[pl.ds(r * nc, nc)], sem).start(priority=r % 2)

    def scatter(block, buf, sem):
        base = block * bm
        for r in range(bm):
            pltpu.make_async_copy(buf.at[pl.ds(r * nc, nc)], y_hbm.at[pl.ds(pl.multiple_of(dst_ref[base + r], nc), nc)], sem).start(priority=r % 2)

    def wait_gather(buf, sem):
        pltpu.make_async_copy(h_hbm.at[pl.ds(0, bm * nc)], buf, sem).wait()

    def wait_scatter(buf, sem):
        pltpu.make_async_copy(buf, y_hbm.at[pl.ds(0, bm * nc)], sem).wait()

    def experts(x_ref, o_ref, g_ref, wgu, bgu, wd, bd):
        x = jnp.concatenate([x_ref[pl.ds(c, bm, stride=nc), :] for c in range(nc)], axis=1)
        gu = jnp.dot(x.astype(BF16), wgu[0, 0], preferred_element_type=F32) + bgu[0]
        a = jnp.minimum(gu[:, :dff], SWIGLU_LIMIT)
        u = jnp.clip(gu[:, dff:], -SWIGLU_LIMIT, SWIGLU_LIMIT)
        y = (u + 1.0) * (a * _sigmoid(SWIGLU_ALPHA * a))
        out = (jnp.dot(y.astype(BF16), wd[0, 0], preferred_element_type=F32) + bd[0]) * g_ref[...]
        for c in range(nc):
            o_ref[pl.ds(c, bm, stride=nc), :] = out[:, c * LANES:(c + 1) * LANES]

    @pl.when(i == 0)
    def _():
        gather(0, xa, gsem.at[0])
        for buf, sem, base in ((oa, ssem.at[0], dump), (ob, ssem.at[1], dump + bm)):
            buf[...] = jnp.zeros(buf.shape, buf.dtype)
            for r in range(bm):
                pltpu.make_async_copy(buf.at[pl.ds(r * nc, nc)], y_hbm.at[pl.ds((base + r) * nc, nc)], sem).start()

    @pl.when(i < live)
    def _():
        blk_a = 2 * i
        blk_b = 2 * i + 1
        nxt_a = jnp.minimum(2 * i + 2, 2 * live - 1)

        gather(blk_b, xb, gsem.at[1])
        wait_gather(xa, gsem.at[0])
        wait_scatter(oa, ssem.at[0])
        experts(xa, oa, ga_ref, wgu_a, bgu_a, wd_a, bd_a)
        scatter(blk_a, oa, ssem.at[0])

        gather(nxt_a, xa, gsem.at[0])
        wait_gather(xb, gsem.at[1])
        wait_scatter(ob, ssem.at[1])
        experts(xb, ob, gb_ref, wgu_b, bgu_b, wd_b, bd_b)
        scatter(blk_b, ob, ssem.at[1])

    @pl.when(i == live - 1)
    def _():
        wait_gather(xa, gsem.at[0])
        wait_scatter(oa, ssem.at[0])
        wait_scatter(ob, ssem.at[1])


def _moe_experts(h, blk_exp, tok, dst, live, gates, w_gu, b_gu, w_down, b_down, *, bm, layer):
    _, e, d, f2 = w_gu.shape
    nc = d // LANES
    n = h.shape[0] // nc
    dff = f2 // 2
    cap = tok.shape[0]
    n_steps = cap // (2 * bm)
    assert cap == n_steps * 2 * bm

    def wspecs(off):
        return [
            pl.BlockSpec((1, 1, d, f2), lambda i, be, tk, ds, lv: (layer, be[2 * i + off], 0, 0)),
            pl.BlockSpec((1, 1, f2), lambda i, be, tk, ds, lv: (be[2 * i + off], 0, 0)),
            pl.BlockSpec((1, 1, dff, d), lambda i, be, tk, ds, lv: (layer, be[2 * i + off], 0, 0)),
            pl.BlockSpec((1, 1, d), lambda i, be, tk, ds, lv: (be[2 * i + off], 0, 0)),
        ]

    grid_spec = pltpu.PrefetchScalarGridSpec(
        num_scalar_prefetch=4,
        grid=(n_steps,),
        in_specs=[
            pl.BlockSpec(memory_space=pl.ANY),
            pl.BlockSpec((bm, 1), lambda i, be, tk, ds, lv: (2 * i, 0)),
            pl.BlockSpec((bm, 1), lambda i, be, tk, ds, lv: (2 * i + 1, 0)),
        ] + wspecs(0) + wspecs(1),
        out_specs=pl.BlockSpec(memory_space=pl.ANY),
        scratch_shapes=[pltpu.VMEM((bm * nc, LANES), F32)] * 4 + [pltpu.SemaphoreType.DMA((2,)),
                                                                  pltpu.SemaphoreType.DMA((2,))],
    )
    bgu3 = b_gu.reshape(e, 1, f2).astype(F32)
    bd3 = b_down.reshape(e, 1, d).astype(F32)
    g2 = gates.reshape(cap, 1)
    return pl.pallas_call(
        functools.partial(_moe_expert_kernel, bm=bm, dff=dff),
        out_shape=jax.ShapeDtypeStruct(((TOP_K * n + 2 * bm) * nc, LANES), F32),
        grid_spec=grid_spec,
        compiler_params=_cparams(("arbitrary",)),
        name="moe_experts",
    )(blk_exp, tok, dst, live, h, g2, g2, w_gu, bgu3, w_down, bd3, w_gu, bgu3, w_down, bd3)


def _moe_layer(x, norm_g, router_w, router_b, w_gu, b_gu, w_down, b_down, layer):
    n, d = x.shape
    bm = MOE_BM
    h, idx128, gate128, cnt = _router(x, norm_g, router_w, router_b)
    n_assign = n * TOP_K
    flat_e = idx128[:, :TOP_K].reshape(-1)
    flat_g = gate128[:, :TOP_K].reshape(-1)
    order = jnp.argsort(flat_e).astype(jnp.int32)
    counts = cnt[0, :N_EXPERTS].astype(jnp.int32)
    padded = (counts + bm - 1) // bm * bm
    start = jnp.cumsum(counts) - counts
    pend = jnp.cumsum(padded)
    pstart = pend - padded
    n_blocks = -(-n_assign // bm) + N_EXPERTS
    n_blocks += n_blocks % 2
    blk_start = jnp.arange(n_blocks, dtype=jnp.int32) * bm
    blk_exp = jnp.minimum(jnp.sum((pend[None, :] <= blk_start[:, None]).astype(jnp.int32), axis=1),
                          N_EXPERTS - 1).astype(jnp.int32)
    row = jnp.arange(bm, dtype=jnp.int32)[None, :]
    off = (blk_start - pstart[blk_exp])[:, None] + row
    valid = off < counts[blk_exp][:, None]
    a = order[jnp.clip(start[blk_exp][:, None] + off, 0, n_assign - 1).reshape(-1)].reshape(n_blocks, bm)
    nc = d // LANES
    tok = (jnp.where(valid, a // TOP_K, 0) * nc).astype(jnp.int32).reshape(-1)
    dump_row = TOP_K * n + (jnp.arange(n_blocks, dtype=jnp.int32) % 2)[:, None] * bm + row
    dst = (jnp.where(valid, (a % TOP_K) * n + a // TOP_K, dump_row) * nc).astype(jnp.int32).reshape(-1)
    live = jnp.maximum((pend[-1] // bm + 1) // 2, 1).astype(jnp.int32).reshape(1)
    gates = jnp.where(valid, flat_g[a.reshape(-1)].reshape(n_blocks, bm), 0.0).reshape(-1)
    return _moe_experts(h, blk_exp, tok, dst, live, gates, w_gu, b_gu, w_down, b_down, bm=bm, layer=layer)


def _ple_kernel(x_ref, y0_ref, y1_ref, y2_ref, y3_ref, g_ref, wg_ref, p_ref, wp_ref, o_ref):
    tm, d = x_ref.shape
    nc = d // LANES

    def plane(y_ref):
        return jnp.concatenate([y_ref[pl.ds(c, tm, stride=nc), :] for c in range(nc)], axis=1)

    xf = x_ref[...] + (((plane(y0_ref) + plane(y1_ref)) + plane(y2_ref)) + plane(y3_ref))
    ms = jnp.mean(xf * xf, axis=-1, keepdims=True)
    h = (xf * lax.rsqrt(ms + EPS) * g_ref[...]).astype(BF16)
    gate = _sigmoid(jnp.dot(h, wg_ref[...], preferred_element_type=F32))
    pp = jnp.dot(p_ref[...].astype(BF16), wp_ref[...], preferred_element_type=F32)
    o_ref[...] = xf + gate * pp


def _ple(x, y, g, wg, p, wp, *, tm=512):
    n, d = x.shape
    pd = p.shape[1]
    tm = min(tm, n)
    nb = n // tm
    nc = d // LANES
    yspecs = [pl.BlockSpec((tm * nc, LANES), functools.partial(lambda i, k: (k * nb + i, 0), k=k)) for k in range(TOP_K)]
    return pl.pallas_call(
        _ple_kernel,
        out_shape=jax.ShapeDtypeStruct((n, d), F32),
        grid=(nb,),
        in_specs=[pl.BlockSpec((tm, d), lambda i: (i, 0))] + yspecs + [
            pl.BlockSpec((1, d), lambda i: (0, 0)),
            pl.BlockSpec((d, d), lambda i: (0, 0)),
            pl.BlockSpec((tm, pd), lambda i: (i, 0)),
            pl.BlockSpec((pd, d), lambda i: (0, 0)),
        ],
        out_specs=pl.BlockSpec((tm, d), lambda i: (i, 0)),
        compiler_params=_cparams(("parallel",)),
        name="ple",
    )(x, y, y, y, y, g.reshape(1, d).astype(F32), wg.astype(BF16), p, wp.astype(BF16))


def _cmp_kernel(z_ref, w1_ref, pos_ref, w2_ref, gain_ref, c_ref, s1_ref, s2_ref, o_ref, *, rows, half):
    j = pl.program_id(1)
    z = z_ref[0, 0, 0]
    pos = pos_ref[0]
    a1 = (z + pos[:, :half]).astype(BF16)
    a2 = (z + pos[:, half:]).astype(BF16)
    u1 = jnp.dot(a1, w1_ref[0, :half, :], preferred_element_type=F32)
    u2 = jnp.dot(a2, w1_ref[0, half:, :], preferred_element_type=F32)
    h = u1 + pltpu.roll(u2, rows - 1, 0)
    hs = h * _sigmoid(h)
    y = jnp.dot(hs.astype(BF16), w2_ref[0], preferred_element_type=F32)

    @pl.when(j == 0)
    def _():
        ss = jnp.sum(y * y, axis=-1, keepdims=True) * (1.0 / LANES)
        yn = y * lax.rsqrt(ss + EPS) * gain_ref[...]
        yn = (yn * c_ref[...] + pltpu.roll(yn, LANES - ROPE_DIM // 2, 1) * s1_ref[...]
              + pltpu.roll(yn, ROPE_DIM // 2, 1) * s2_ref[...])
        o_ref[0, 0, 0] = yn.astype(o_ref.dtype)

    @pl.when(j != 0)
    def _():
        o_ref[0, 0, 0] = y.astype(o_ref.dtype)


def _compress(zr, w1, pos, w2dup, gain_dup, tabs, *, batch):
    rows, width = zr.shape[3], zr.shape[4]
    hid = w1.shape[2]
    g = NSA_GROUPS
    return pl.pallas_call(
        functools.partial(_cmp_kernel, rows=rows, half=width),
        out_shape=jax.ShapeDtypeStruct((batch, 2, g, rows, LANES), BF16),
        grid=(batch, 2, g),
        in_specs=[
            pl.BlockSpec((1, 1, 1, rows, width), lambda b, j, gg: (b, j, gg, 0, 0)),
            pl.BlockSpec((1, 2 * width, hid), lambda b, j, gg: (j, 0, 0)),
            pl.BlockSpec((1, 1, 2 * width), lambda b, j, gg: (j, 0, 0)),
            pl.BlockSpec((1, hid, LANES), lambda b, j, gg: (j, 0, 0)),
            pl.BlockSpec((1, LANES), lambda b, j, gg: (0, 0)),
            pl.BlockSpec((rows, LANES), lambda b, j, gg: (0, 0)),
            pl.BlockSpec((rows, LANES), lambda b, j, gg: (0, 0)),
            pl.BlockSpec((rows, LANES), lambda b, j, gg: (0, 0)),
        ],
        out_specs=pl.BlockSpec((1, 1, 1, rows, LANES), lambda b, j, gg: (b, j, gg, 0, 0)),
        compiler_params=_cparams(("parallel", "parallel", "parallel")),
        name="nsa_compress",
    )(zr, w1, pos, w2dup, gain_dup, *tabs)


def _pipelined(n_items, first, second, skew):
    pending, out = {}, [None] * n_items
    for idx in range(n_items + skew):
        if idx < n_items:
            pending[idx] = first(idx)
        if idx >= skew:
            out[idx - skew] = second(idx - skew, pending.pop(idx - skew))
    return out


def _nsa_attn_kernel(q_ref, gz_ref, kc_ref, vct_ref, ovt_ref, ks_ref, vs_ref, kw_ref, vw_ref, o_ref,
                     e_ref, oc_ref, imp_ref, notsel_ref, *, n_cmp_rows, tk_sel, skew, grp, sub):
    g = pl.program_id(1)
    i = pl.program_id(2)
    qb = NSA_QB
    mh = HEADS_PER_GROUP
    rows = mh * qb
    q0 = i * qb
    seq = ks_ref.shape[0]

    @pl.when(i == 0)
    def _():
        r = lax.broadcasted_iota(jnp.int32, (seq, LANES), 0)
        lb = lax.broadcasted_iota(jnp.int32, (seq, LANES), 1)
        e_ref[...] = jnp.where((r >> 6) == lb, NEG, 0.0).astype(BF16)

    qh = [q_ref[:, m * LANES:(m + 1) * LANES] for m in range(mh)]
    q4 = jnp.concatenate(qh, axis=0)
    jl = lax.broadcasted_iota(jnp.int32, (qb, LANES), 1)

    tlane = q0 + (lax.broadcasted_iota(jnp.int32, (1, rows), 1) & (qb - 1))

    def cmp_branch(nrow):
        st = _dot_nt(kc_ref[0, 0, 0, :nrow, :], q4)
        cend = lax.broadcasted_iota(jnp.int32, (nrow, rows), 0) * CMP_STRIDE + (CMP_BLOCK - 1)
        valid = cend <= tlane
        st = jnp.where(valid, st, NEG)
        e = jnp.where(valid, jnp.exp2(st - jnp.max(st, axis=0, keepdims=True)), 0.0)
        pt = e / jnp.maximum(jnp.sum(e, axis=0, keepdims=True), 1e-30)
        oct = jnp.dot(vct_ref[0, 0, :, :nrow], pt.astype(BF16), preferred_element_type=F32)
        for m in range(mh):
            oc_ref[m * qb:(m + 1) * qb, :] = oct[:, m * qb:(m + 1) * qb].T
        psum = (pt[:, 0:qb] + pt[:, qb:2 * qb]) + (pt[:, 2 * qb:3 * qb] + pt[:, 3 * qb:4 * qb])
        phi, plo = _split2(psum)
        ovt = ovt_ref[:, :nrow]
        imp_ref[...] = (jnp.dot(ovt, phi, preferred_element_type=F32)
                        + jnp.dot(ovt, plo, preferred_element_type=F32))

    n_chunks = n_cmp_rows // LANES
    last_visible = (q0 + qb - CMP_BLOCK) // CMP_STRIDE
    need = jnp.clip(last_visible // LANES + 1, 1, n_chunks)
    for nch in range(1, n_chunks + 1):
        pl.when(need == nch)(functools.partial(cmp_branch, nch * LANES))

    def select_blocks(nrow):
        jb = lax.broadcasted_iota(jnp.int32, (nrow, qb), 0)
        tq = q0 + lax.broadcasted_iota(jnp.int32, (nrow, qb), 1)
        cur = tq >> 6
        forced = (jb == 0) | (jb == cur) | (jb == cur - 1)
        causal = (jb << 6) <= tq
        score = jnp.where(causal, jnp.where(forced, FORCE, imp_ref[:nrow, :]), -jnp.inf)
        notsel_t = jnp.ones((nrow, qb), F32)
        jbf = jb.astype(F32)
        for _ in range(SEL_TOPK):
            smx = jnp.max(score, axis=0, keepdims=True)
            ix = jnp.min(jnp.where(score == smx, jbf, float(LANES)), axis=0, keepdims=True)
            hit = jbf == ix
            notsel_t = jnp.where(hit, 0.0, notsel_t)
            score = jnp.where(hit, -jnp.inf, score)
        notsel_ref[:nrow, :] = notsel_t
        if nrow < LANES:
            notsel_ref[nrow:, :] = jnp.ones((LANES - nrow, qb), F32)

    sel_chunk = 32
    n_sel_chunks = LANES // sel_chunk
    sel_need = jnp.clip(((q0 + qb - 1) // SEL_BLOCK) // sel_chunk + 1, 1, n_sel_chunks)
    for nch in range(1, n_sel_chunks + 1):
        pl.when(sel_need == nch)(functools.partial(select_blocks, nch * sel_chunk))
    notsel_b = notsel_ref[...].T.astype(BF16)
    n_grp = mh // grp
    q_aug = [jnp.concatenate([jnp.concatenate([qh[m], notsel_b], axis=1) for m in range(c * grp, (c + 1) * grp)], axis=0)
             for c in range(n_grp)]
    tpos = q0 + lax.broadcasted_iota(jnp.int32, (qb, 1), 0)
    tpos_g = jnp.concatenate([tpos] * grp, axis=0)

    def sel_tiles(first, n_sub):
        ks, vs = [], []
        for t in range(n_sub):
            k0 = pl.multiple_of((first + t) * tk_sel, tk_sel)
            k_aug = jnp.concatenate([ks_ref[pl.ds(k0, tk_sel), :], e_ref[pl.ds(k0, tk_sel), :]], axis=1)
            ks.append([k_aug] * n_grp)
            vs.append([vs_ref[pl.ds(k0, tk_sel), :]] * n_grp)
        return ks, vs

    def sel_step(first, n_sub, state):
        ks, vs = sel_tiles(first, n_sub)
        return _attend(q_aug, ks, vs, state, skew)

    def sel_tail(first, n_rem, state):
        def mask(t, c, sc):
            if t < n_rem:
                return sc
            kpos = (first + n_rem) * tk_sel + lax.broadcasted_iota(jnp.int32, (grp * qb, tk_sel), 1)
            return jnp.where(kpos <= tpos_g, sc, NEG)

        ks, vs = sel_tiles(first, n_rem + 1)
        return _attend(q_aug, ks, vs, state, skew, mask=mask)

    state = tuple((jnp.full((grp * qb, 1), NEG, F32), jnp.zeros((grp * qb, LANES), F32)) for _ in range(n_grp))
    n_below = q0 // tk_sel
    state = _tile_loops(n_below, sub, sel_step, sel_tail, state)
    sel_acc = [state[m // grp][1][(m % grp) * qb:(m % grp + 1) * qb] for m in range(mh)]

    wk = WINDOW + qb
    k0w = pl.multiple_of(jnp.maximum(i - WINDOW // qb, 0) * qb, qb)
    kwt = kw_ref[pl.ds(k0w, wk), :]
    vwt = vw_ref[pl.ds(k0w, wk), :]
    kposw = k0w + lax.broadcasted_iota(jnp.int32, (grp * qb, wk), 1)
    wmask = (kposw <= tpos_g) & (kposw > tpos_g - WINDOW)
    q_grp = [jnp.concatenate(qh[c * grp:(c + 1) * grp], axis=0) for c in range(n_grp)]

    def win_scores(c):
        return jnp.where(wmask, _dot_nt(q_grp[c], kwt), NEG)

    def win_out(c, sc):
        pe = jnp.exp2(sc - jnp.max(sc, axis=-1, keepdims=True))
        return jnp.dot(pe.astype(BF16), vwt, preferred_element_type=F32)

    win_grp = _pipelined(n_grp, win_scores, win_out, min(skew, n_grp - 1))
    win_acc = [win_grp[m // grp][(m % grp) * qb:(m % grp + 1) * qb] for m in range(mh)]

    sig = _sigmoid(gz_ref[...])
    den_lane = jl == HEAD_DIM
    heads = []
    for m in range(mh):
        col = g * mh + m
        gates = [jnp.sum(jnp.where(jl == br * N_HEADS + col, sig, 0.0), axis=-1, keepdims=True)
                 for br in range(3)]
        acc_s = sel_acc[m]
        acc_w = win_acc[m]
        den_s = jnp.sum(jnp.where(den_lane, acc_s, 0.0), axis=-1, keepdims=True)
        den_w = jnp.sum(jnp.where(den_lane, acc_w, 0.0), axis=-1, keepdims=True)
        heads.append(gates[0] * oc_ref[m * qb:(m + 1) * qb, :] + (gates[1] / den_s) * acc_s
                     + (gates[2] / den_w) * acc_w)
    lane_lo = jl < HEAD_DIM
    o_ref[:, 0:LANES] = jnp.where(lane_lo, heads[0], pltpu.roll(heads[1], HEAD_DIM, 1)).astype(o_ref.dtype)
    o_ref[:, LANES:2 * LANES] = jnp.where(lane_lo, heads[2], pltpu.roll(heads[3], HEAD_DIM, 1)).astype(o_ref.dtype)


def _nsa_attn(qn, gz, kvc, overlap, kvd, *, batch, seq, skew=2, grp=2, sub=4):
    n = qn.shape[0]
    qb = NSA_QB
    nq = seq // qb
    g = NSA_GROUPS
    rows = HEADS_PER_GROUP * qb
    n_cmp_rows = kvc.shape[3]
    tk_sel = min(512, seq)
    assert seq >= WINDOW + qb and seq % tk_sel == 0 and tk_sel % qb == 0
    in_specs = [
        pl.BlockSpec((qb, HEADS_PER_GROUP * LANES), lambda b, gg, i: (b * nq + i, gg)),
        pl.BlockSpec((qb, LANES), lambda b, gg, i: (b * nq + i, 0)),
        pl.BlockSpec((1, 1, 1, n_cmp_rows, LANES), lambda b, gg, i: (b, 0, gg, 0, 0)),
        pl.BlockSpec((1, 1, LANES, n_cmp_rows), lambda b, gg, i: (b, gg, 0, 0)),
        pl.BlockSpec((LANES, n_cmp_rows), lambda b, gg, i: (0, 0)),
        pl.BlockSpec((seq, LANES), lambda b, gg, i: (b, 0 * g + gg)),
        pl.BlockSpec((seq, LANES), lambda b, gg, i: (b, 1 * g + gg)),
        pl.BlockSpec((seq, LANES), lambda b, gg, i: (b, 2 * g + gg)),
        pl.BlockSpec((seq, LANES), lambda b, gg, i: (b, 3 * g + gg)),
    ]
    return pl.pallas_call(
        functools.partial(_nsa_attn_kernel, n_cmp_rows=n_cmp_rows, tk_sel=tk_sel, skew=skew, grp=grp, sub=sub),
        out_shape=jax.ShapeDtypeStruct((n, N_HEADS * HEAD_DIM), BF16),
        grid=(batch, g, nq),
        in_specs=in_specs,
        out_specs=pl.BlockSpec((qb, HEADS_PER_GROUP * HEAD_DIM), lambda b, gg, i: (b * nq + i, gg)),
        scratch_shapes=[pltpu.VMEM((seq, LANES), BF16), pltpu.VMEM((rows, LANES), F32), pltpu.VMEM((LANES, qb), F32),
                        pltpu.VMEM((LANES, qb), F32)],
        compiler_params=_cparams(("parallel", "parallel", "arbitrary")),
        name="nsa_attn",
    )(qn, gz, kvc, jnp.swapaxes(kvc[:, 1], -1, -2), overlap.T, kvd, kvd, kvd, kvd)


def _pad_cols(w, width=LANES):
    d, f = w.shape
    return jnp.zeros((d, width), w.dtype).at[:, :f].set(w)


def _tile_gain(gain, n_heads, scale=1.0, dup=False):
    g = gain.astype(F32) * scale
    second = g if dup else jnp.zeros_like(g)
    return jnp.tile(jnp.concatenate([g, second]), n_heads)


def _fox_aug_tables():
    hw = N_HEADS * LANES
    piece = jnp.arange(3)[:, None]
    head = jnp.arange(N_HEADS)[None, :]
    src = (piece * N_HEADS + head).reshape(-1)
    q_dst = (head * LANES + HEAD_DIM + piece).reshape(-1)
    k_dst = (hw + head * LANES + HEAD_DIM + 3 + piece).reshape(-1)
    place = jnp.zeros((LANES, 3 * hw), F32).at[src, q_dst].set(1.0).at[src, k_dst].set(-1.0)
    lane = jnp.arange(LANES)
    q_bias = ((lane >= HEAD_DIM + 3) & (lane < HEAD_DIM + 6)).astype(F32)
    k_bias = ((lane >= HEAD_DIM) & (lane < HEAD_DIM + 3)).astype(F32)
    v_bias = (lane == HEAD_DIM).astype(F32)
    bias = jnp.concatenate([jnp.tile(q_bias, N_HEADS), jnp.tile(k_bias, N_HEADS), jnp.tile(v_bias, N_HEADS)])
    return place.astype(BF16), bias


def _fox_layer(x, batch, seq, attn_g, w_in, b_f, qk_gain, w_out):
    hd = N_HEADS * HEAD_DIM
    hw = N_HEADS * LANES
    scale = HEAD_DIM ** -0.5
    z = _rms_proj(x, attn_g, _pad_cols(w_in[:, 3 * hd:]).astype(BF16), out_dtype=F32, seq=seq)
    b_pad = jnp.zeros((1, LANES), F32).at[0, :N_HEADS].set(b_f.astype(F32))
    c3 = _fox_c(z, b_pad, batch=batch, seq=seq)
    w_bf = w_in.astype(BF16)
    gain = jnp.concatenate([_tile_gain(qk_gain[0], N_HEADS, scale * LOG2E), _tile_gain(qk_gain[1], N_HEADS),
                            jnp.zeros((hw,), F32)])
    tn = 1024
    place, bias = _fox_aug_tables()
    qkv = _rms_proj(x, attn_g, w_bf, out_dtype=BF16, seq=seq, tn=tn, gain=gain, w_cols=3 * hd,
                    norm_blocks=tuple(range(2 * hw // tn)), placed=(c3, place), col_bias=bias, expand=True)
    o = _fox_attn(qkv, batch=batch, seq=seq)
    return _matmul_res(o, w_out.astype(BF16), x)


def _nsa_shared_kv(x, batch, seq, kv_norm, kv_w, kv_k_gain, cmp_pos, cmp_w1, cmp_w2):
    g = NSA_GROUPS
    gw = g * HEAD_DIM
    tabs = _rope_tables(jnp.arange(seq))
    kv_bf = kv_w.astype(BF16)
    zero = jnp.zeros((g * LANES,), F32)
    one_lane = jnp.tile((jnp.arange(LANES) == HEAD_DIM).astype(F32), g)
    gain_kvd = jnp.concatenate([_tile_gain(kv_k_gain[1], g), zero, _tile_gain(kv_k_gain[2], g), zero])
    bias_kvd = jnp.concatenate([zero, one_lane, zero, one_lane])
    kvd = _rms_proj(x, kv_norm, kv_bf, out_dtype=BF16, seq=seq, tn=g * LANES, gain=gain_kvd, w_col0=2 * gw,
                    norm_blocks=(0, 2), rope_tabs=tabs, col_bias=bias_kvd, expand=True)
    zc = _rms_proj(x, kv_norm, kv_bf, out_dtype=F32, seq=seq, w_cols=2 * gw)
    rows = seq // CMP_STRIDE
    zr = zc.reshape(batch, rows, CMP_STRIDE, 2, g, HEAD_DIM).transpose(0, 3, 4, 1, 2, 5)
    zr = zr.reshape(batch, 2, g, rows, CMP_STRIDE * HEAD_DIM)
    cmp_end = jnp.arange(rows) * CMP_STRIDE + CMP_BLOCK - 1
    ctabs = _rope_tables(cmp_end)
    w2dup = jnp.concatenate([cmp_w2, cmp_w2], axis=-1).astype(BF16)
    kvc = _compress(zr, cmp_w1.astype(BF16), cmp_pos.reshape(2, 1, CMP_BLOCK * HEAD_DIM).astype(F32), w2dup,
                    _tile_gain(kv_k_gain[0], 1, dup=True).reshape(1, LANES), ctabs, batch=batch)
    return kvc, kvd


def _nsa_layer(x, batch, seq, attn_g, w_in, q_gain, w_out, kvc, kvd):
    hd = N_HEADS * HEAD_DIM
    scale = HEAD_DIM ** -0.5
    rows = seq // CMP_STRIDE
    tabs = _rope_tables(jnp.arange(seq))
    qn = _rms_proj(x, attn_g, w_in.astype(BF16), out_dtype=BF16, seq=seq, expand=True, w_cols=hd,
                   gain=_tile_gain(q_gain, N_HEADS, scale * LOG2E), norm_blocks=tuple(range(N_HEADS * LANES // 512)),
                   rope_tabs=tabs)
    gz = _rms_proj(x, attn_g, _pad_cols(w_in[:, hd:]).astype(BF16), out_dtype=F32, seq=seq)

    n_sel = seq // SEL_BLOCK
    assert n_sel <= LANES
    cmp_start = jnp.arange(rows) * CMP_STRIDE
    sel_start = jnp.arange(LANES) * SEL_BLOCK
    overlap = jnp.clip(jnp.minimum(cmp_start[:, None] + CMP_BLOCK, sel_start[None, :] + SEL_BLOCK)
                       - jnp.maximum(cmp_start[:, None], sel_start[None, :]), 0)
    overlap = jnp.where((jnp.arange(rows) < rows - 1)[:, None] & (jnp.arange(LANES) < n_sel)[None, :], overlap, 0)
    o = _nsa_attn(qn, gz, kvc, overlap.astype(BF16), kvd, batch=batch, seq=seq)
    return _matmul_res(o, w_out.astype(BF16), x)


def kernel(x, p, attn_norm, ffn_norm, ple_norm, ple_gate_w, ple_proj_w, router_w, router_b, w_gu, b_gu, w_down, b_down, fox_w_in, fox_b_f, fox_qk_gain, fox_w_out, kv_norm, kv_w, kv_k_gain, cmp_pos, cmp_w1, cmp_w2, nsa_w_in, nsa_q_gain, nsa_w_out):
    batch, seq, d = x.shape
    depth = p.shape[0]
    n_a = fox_w_in.shape[0]
    xt = x.reshape(batch * seq, d)
    w_gu_bf = w_gu.astype(BF16)
    w_down_bf = w_down.astype(BF16)
    shared = None
    for layer in range(depth):
        if layer == n_a:
            shared = _nsa_shared_kv(xt, batch, seq, kv_norm, kv_w, kv_k_gain, cmp_pos, cmp_w1, cmp_w2)
        if layer < n_a:
            xt = _fox_layer(xt, batch, seq, attn_norm[layer], fox_w_in[layer], fox_b_f[layer],
                            fox_qk_gain[layer], fox_w_out[layer])
        else:
            i = layer - n_a
            xt = _nsa_layer(xt, batch, seq, attn_norm[layer], nsa_w_in[i], nsa_q_gain[i], nsa_w_out[i], *shared)
        y = _moe_layer(xt, ffn_norm[layer], router_w[layer], router_b[layer], w_gu_bf, b_gu[layer],
                       w_down_bf, b_down[layer], layer)
        xt = _ple(xt, y, ple_norm[layer], ple_gate_w[layer], p[layer].reshape(batch * seq, -1), ple_proj_w[layer])
    return xt.reshape(batch, seq, d)
```

```python
import functools

import jax
import jax.numpy as jnp
from jax import lax
from jax.experimental import pallas as pl
from jax.experimental.pallas import tpu as pltpu

F32 = jnp.float32
BF16 = jnp.bfloat16

N_HEADS = 16
HEAD_DIM = 64
ROPE_DIM = HEAD_DIM // 4
ROPE_THETA = 500000.0
NSA_GROUPS = 4
HEADS_PER_GROUP = N_HEADS // NSA_GROUPS
CMP_BLOCK = 32
CMP_STRIDE = 16
SEL_BLOCK = 64
SEL_TOPK = 16
WINDOW = 512
N_EXPERTS = 32
TOP_K = 4
SWIGLU_LIMIT = 7.0
SWIGLU_ALPHA = 1.702
EPS = 1e-6
NEG = -1e30
FORCE = 1e6
LOG2E = 1.4426950408889634

LANES = 128
NSA_QB = 256
MOE_BM = 256
VMEM_LIMIT = 56 * 1024 * 1024


def _cparams(sem):
    return pltpu.CompilerParams(dimension_semantics=sem, vmem_limit_bytes=VMEM_LIMIT)


def _sigmoid(x):
    return 1.0 / (1.0 + jnp.exp(-x))


def _dot_nt(a, b):
    return lax.dot_general(a, b, (((1,), (1,)), ((), ())), preferred_element_type=F32)


def _split2(x):
    hi = x.astype(BF16)
    lo = (x - hi.astype(F32)).astype(BF16)
    return hi, lo


def _proj_kernel(*refs, norm_blocks, rope, placed, biased, expand, head_div, tn, out_dtype):
    refs = list(refs)
    x_ref, g_ref, w_ref = refs[:3]
    del refs[:3]
    if norm_blocks:
        gain_ref = refs.pop(0)
    if rope:
        c_ref, s1_ref, s2_ref = refs[:3]
        del refs[:3]
    if placed:
        ex_ref, place_ref = refs[:2]
        del refs[:2]
    if biased:
        bias_ref = refs.pop(0)
    o_ref, h_ref = refs
    j = pl.program_id(1)
    aug = placed or biased

    @pl.when(j == 0)
    def _():
        xf = x_ref[...]
        ms = jnp.mean(xf * xf, axis=-1, keepdims=True)
        h_ref[...] = (xf * lax.rsqrt(ms + EPS) * g_ref[...]).astype(BF16)

    acc = jnp.dot(h_ref[...], w_ref[...], preferred_element_type=F32)
    if placed and biased:
        extra = jnp.dot(ex_ref[...], place_ref[...], preferred_element_type=F32) + bias_ref[...]
    elif biased:
        extra = jnp.broadcast_to(bias_ref[...], (acc.shape[0], tn))
    if expand:
        lo = lax.broadcasted_iota(jnp.int32, (acc.shape[0], LANES), 1) < HEAD_DIM
        blocks = []
        for c in range(acc.shape[1] // LANES):
            a = acc[:, c * LANES:(c + 1) * LANES]
            blocks += [jnp.where(lo, a, 0.0), jnp.where(lo, pltpu.roll(a, HEAD_DIM, 1), 0.0)]
    else:
        blocks = [acc[:, c * LANES:(c + 1) * LANES] for c in range(acc.shape[1] // LANES)]

    def plain():
        for c, a in enumerate(blocks):
            sl = slice(c * LANES, (c + 1) * LANES)
            o_ref[:, sl] = ((a + extra[:, sl]) if aug else a).astype(out_dtype)

    if not norm_blocks:
        plain()
        return
    is_norm = functools.reduce(jnp.logical_or, [j == c for c in norm_blocks])

    @pl.when(is_norm)
    def _():
        for c, a in enumerate(blocks):
            sl = slice(c * LANES, (c + 1) * LANES)
            ss = jnp.sum(a * a, axis=-1, keepdims=True) * (1.0 / head_div)
            y = a * lax.rsqrt(ss + EPS) * gain_ref[:, sl]
            if rope:
                y = (y * c_ref[...] + pltpu.roll(y, LANES - ROPE_DIM // 2, 1) * s1_ref[...]
                     + pltpu.roll(y, ROPE_DIM // 2, 1) * s2_ref[...])
            if aug:
                y = y + extra[:, sl]
            o_ref[:, sl] = y.astype(out_dtype)

    @pl.when(jnp.logical_not(is_norm))
    def _():
        plain()


def _rms_proj(x, g, w, *, out_dtype, seq, tm=2048, tn=512, gain=None, norm_blocks=(),
              rope_tabs=None, placed=None, col_bias=None, expand=False, head_div=HEAD_DIM,
              w_col0=0, w_cols=None):
    n, d = x.shape
    w_cols = w.shape[1] - w_col0 if w_cols is None else w_cols
    f = w_cols * (2 if expand else 1)
    tn = min(tn, f)
    tw = tn // 2 if expand else tn
    assert w_col0 % tw == 0
    j0 = w_col0 // tw
    tm = min(tm, seq)
    assert n % tm == 0 and f % tn == 0 and seq % tm == 0
    nt = seq // tm
    in_specs = [
        pl.BlockSpec((tm, d), lambda i, j: (i, 0)),
        pl.BlockSpec((1, d), lambda i, j: (0, 0)),
        pl.BlockSpec((d, tw), lambda i, j: (0, j0 + j)),
    ]
    args = [x, g.reshape(1, d).astype(F32), w]
    if norm_blocks:
        in_specs.append(pl.BlockSpec((1, tn), lambda i, j: (0, j)))
        args.append(gain.reshape(1, f).astype(F32))
    if rope_tabs is not None:
        assert norm_blocks
        for tab in rope_tabs:
            in_specs.append(pl.BlockSpec((tm, LANES), lambda i, j: (i % nt, 0)))
            args.append(tab)
    if placed is not None:
        assert col_bias is not None
        values, place = placed
        in_specs += [pl.BlockSpec((tm, LANES), lambda i, j: (i, 0)),
                     pl.BlockSpec((LANES, tn), lambda i, j: (0, j))]
        args += [values, place]
    if col_bias is not None:
        in_specs.append(pl.BlockSpec((1, tn), lambda i, j: (0, j)))
        args.append(col_bias.reshape(1, f).astype(F32))
    kern = functools.partial(_proj_kernel, norm_blocks=tuple(norm_blocks), rope=rope_tabs is not None,
                             placed=placed is not None, biased=col_bias is not None, expand=expand,
                             head_div=float(head_div), tn=tn, out_dtype=out_dtype)
    return pl.pallas_call(
        kern,
        out_shape=jax.ShapeDtypeStruct((n, f), out_dtype),
        grid=(n // tm, f // tn),
        in_specs=in_specs,
        out_specs=pl.BlockSpec((tm, tn), lambda i, j: (i, j)),
        scratch_shapes=[pltpu.VMEM((tm, d), BF16)],
        compiler_params=_cparams(("parallel", "arbitrary")),
        name="rms_proj",
    )(*args)


def _rope_tables(pos, width=LANES):
    half = ROPE_DIM // 2
    inv = jnp.power(jnp.float32(ROPE_THETA), -jnp.arange(0, ROPE_DIM, 2, dtype=F32) / ROPE_DIM)
    ang = pos.astype(F32)[:, None] * inv[None, :]
    cos, sin = jnp.cos(ang), jnp.sin(ang)
    t = pos.shape[0]
    ones = jnp.ones((t, HEAD_DIM - ROPE_DIM), F32)
    zeros_h = jnp.zeros((t, half), F32)
    zeros_r = jnp.zeros((t, HEAD_DIM - ROPE_DIM), F32)
    c = jnp.concatenate([cos, cos, ones], axis=1)
    s1 = jnp.concatenate([-sin, zeros_h, zeros_r], axis=1)
    s2 = jnp.concatenate([zeros_h, sin, zeros_r], axis=1)
    rep = width // HEAD_DIM
    return tuple(jnp.tile(a, (1, rep)) for a in (c, s1, s2))


def _fox_c_kernel(z_ref, b_ref, c_ref, carry_ref, *, tm):
    t = pl.program_id(1)

    @pl.when(t == 0)
    def _():
        carry_ref[...] = jnp.zeros_like(carry_ref)

    u = z_ref[...] + b_ref[...]
    logf = jnp.minimum(u, 0.0) - jnp.log(1.0 + jnp.exp(-jnp.abs(u)))
    row = lax.broadcasted_iota(jnp.int32, (tm, tm), 0)
    col = lax.broadcasted_iota(jnp.int32, (tm, tm), 1)
    tri = (row >= col).astype(BF16)
    hi = logf.astype(BF16)
    r1 = logf - hi.astype(F32)
    mid = r1.astype(BF16)
    lo = (r1 - mid.astype(F32)).astype(BF16)
    cs = (jnp.dot(tri, hi, preferred_element_type=F32) + jnp.dot(tri, mid, preferred_element_type=F32)
          + jnp.dot(tri, lo, preferred_element_type=F32))
    c = cs + carry_ref[...]
    carry_ref[...] = c[tm - 1:tm, :]
    c2 = c * LOG2E
    p0 = c2.astype(BF16).astype(F32)
    p1 = (c2 - p0).astype(BF16).astype(F32)
    p2 = ((c2 - p0) - p1).astype(BF16).astype(F32)
    lane = lax.broadcasted_iota(jnp.int32, c.shape, 1)
    out = jnp.where(lane < N_HEADS, p0,
                    jnp.where(lane < 2 * N_HEADS, pltpu.roll(p1, N_HEADS, 1),
                              jnp.where(lane < 3 * N_HEADS, pltpu.roll(p2, 2 * N_HEADS, 1), 0.0)))
    c_ref[...] = out.astype(BF16)


def _fox_c(z, b_pad, *, batch, seq, tm=512):
    tm = min(tm, seq)
    nt = seq // tm
    return pl.pallas_call(
        functools.partial(_fox_c_kernel, tm=tm),
        out_shape=jax.ShapeDtypeStruct(z.shape, BF16),
        grid=(batch, nt),
        in_specs=[
            pl.BlockSpec((tm, LANES), lambda b, t: (b * nt + t, 0)),
            pl.BlockSpec((1, LANES), lambda b, t: (0, 0)),
        ],
        out_specs=pl.BlockSpec((tm, LANES), lambda b, t: (b * nt + t, 0)),
        scratch_shapes=[pltpu.VMEM((1, LANES), F32)],
        compiler_params=_cparams(("parallel", "arbitrary")),
        name="fox_cumsum",
    )(z, b_pad)


def _attend(qs, ks, vs, state, skew, mask=None):
    items = [(t, c) for t in range(len(ks)) for c in range(len(qs))]
    state = list(state)
    pending = {}
    depth = min(skew, len(items) - 1)
    for idx in range(len(items) + depth):
        if idx < len(items):
            t, c = items[idx]
            sc = _dot_nt(qs[c], ks[t][c])
            pending[idx] = sc if mask is None else mask(t, c, sc)
        if idx >= depth:
            t, c = items[idx - depth]
            sc = pending.pop(idx - depth)
            m_prev, acc = state[c]
            m_new = jnp.maximum(m_prev, jnp.max(sc, axis=-1, keepdims=True))
            pe = jnp.exp2(sc - m_new)
            acc = jnp.exp2(m_prev - m_new) * acc + jnp.dot(pe.astype(BF16), vs[t][c], preferred_element_type=F32)
            state[c] = (m_new, acc)
    return tuple(state)


def _tile_loops(n_tiles, sub, step, tail, state):
    n_full = n_tiles // sub
    state = lax.fori_loop(0, n_full, lambda j, st: step(j * sub, sub, st), state)
    first = n_full * sub
    return lax.switch(n_tiles - first, [functools.partial(tail, first, r) for r in range(sub)], state)


def _fox_attn_kernel(q0_ref, q1_ref, k0_ref, k1_ref, v0_ref, v1_ref, o_ref, *, tq, tk, skew, sub):
    qi = pl.program_id(2)
    k_refs = (k0_ref, k1_ref)
    v_refs = (v0_ref, v1_ref)
    qs = [q0_ref[...], q1_ref[...]]

    def tiles(refs, first, n_sub):
        return [[r[pl.ds(pl.multiple_of((first + t) * tk, tk), tk), :] for r in refs] for t in range(n_sub)]

    def full_step(first, n_sub, state):
        return _attend(qs, tiles(k_refs, first, n_sub), tiles(v_refs, first, n_sub), state, skew)

    def causal(t, c, sc):
        tpos = lax.broadcasted_iota(jnp.int32, (tq, tk), 0)
        kpos = t * tk + lax.broadcasted_iota(jnp.int32, (tq, tk), 1)
        return jnp.where(kpos <= tpos, sc, NEG)

    n_diag = tq // tk

    def tail_step(first, n_rem, state):
        def mask(t, c, sc):
            return causal(t - n_rem, c, sc) if t >= n_rem else sc

        n = n_rem + n_diag
        return _attend(qs, tiles(k_refs, first, n), tiles(v_refs, first, n), state, skew, mask=mask)

    state = tuple((jnp.full((tq, 1), NEG, F32), jnp.zeros((tq, LANES), F32)) for _ in qs)
    n_below = (qi * tq) // tk
    state = _tile_loops(n_below, sub, full_step, tail_step, state)

    lane = lax.broadcasted_iota(jnp.int32, (tq, LANES), 1)
    outs = []
    for _, acc in state:
        den = jnp.sum(jnp.where(lane == HEAD_DIM, acc, 0.0), axis=-1, keepdims=True)
        outs.append(acc / den)
    o_ref[...] = jnp.where(lane < HEAD_DIM, outs[0], pltpu.roll(outs[1], HEAD_DIM, 1)).astype(o_ref.dtype)


def _fox_attn(qkv, *, batch, seq, tq=512, tk=512, skew=2, sub=4):
    n = qkv.shape[0]
    tq = min(tq, seq)
    tk = min(tk, tq)
    assert seq % tq == 0 and tq % tk == 0
    nq = seq // tq
    hp = N_HEADS // 2
    kb = N_HEADS
    vb = 2 * N_HEADS
    in_specs = [
        pl.BlockSpec((tq, LANES), lambda b, h, i: (b * nq + i, 2 * h)),
        pl.BlockSpec((tq, LANES), lambda b, h, i: (b * nq + i, 2 * h + 1)),
        pl.BlockSpec((seq, LANES), lambda b, h, i: (b, kb + 2 * h)),
        pl.BlockSpec((seq, LANES), lambda b, h, i: (b, kb + 2 * h + 1)),
        pl.BlockSpec((seq, LANES), lambda b, h, i: (b, vb + 2 * h)),
        pl.BlockSpec((seq, LANES), lambda b, h, i: (b, vb + 2 * h + 1)),
    ]
    return pl.pallas_call(
        functools.partial(_fox_attn_kernel, tq=tq, tk=tk, skew=skew, sub=sub),
        out_shape=jax.ShapeDtypeStruct((n, N_HEADS * HEAD_DIM), BF16),
        grid=(batch, hp, nq),
        in_specs=in_specs,
        out_specs=pl.BlockSpec((tq, LANES), lambda b, h, i: (b * nq + i, h)),
        compiler_params=_cparams(("parallel", "parallel", "arbitrary")),
        name="fox_attn",
    )(qkv, qkv, qkv, qkv, qkv, qkv)


def _router_kernel(a_ref, wo_ref, x_ref, g_ref, whi_ref, wlo_ref, b_ref, x1_ref, h_ref, idx_ref, gate_ref, cnt_ref):
    xf = x_ref[...] + jnp.dot(a_ref[...], wo_ref[...], preferred_element_type=F32)
    x1_ref[...] = xf
    ms = jnp.mean(xf * xf, axis=-1, keepdims=True)
    h = xf * lax.rsqrt(ms + EPS) * g_ref[...]
    nc = h.shape[1] // LANES
    for c in range(nc):
        h_ref[pl.ds(c, h.shape[0], stride=nc), :] = h[:, c * LANES:(c + 1) * LANES]
    hhi, hlo = _split2(h)
    whi = whi_ref[...]
    logits = (jnp.dot(hhi, whi, preferred_element_type=F32) + jnp.dot(hlo, whi, preferred_element_type=F32)
              + jnp.dot(hhi, wlo_ref[...], preferred_element_type=F32)) + b_ref[...]
    lane = lax.broadcasted_iota(jnp.int32, logits.shape, 1).astype(F32)
    vals, idxs = [], []
    cur = logits
    for _ in range(TOP_K):
        mx = jnp.max(cur, axis=-1, keepdims=True)
        ix = jnp.min(jnp.where(cur == mx, lane, float(LANES)), axis=-1, keepdims=True)
        vals.append(mx)
        idxs.append(ix)
        cur = jnp.where(lane == ix, -jnp.inf, cur)
    es = [jnp.exp(v - vals[0]) for v in vals]
    tot = es[0] + es[1] + es[2] + es[3]
    idx_out = jnp.zeros(logits.shape, F32)
    gate_out = jnp.zeros(logits.shape, F32)
    for r in range(TOP_K):
        idx_out = jnp.where(lane == r, idxs[r], idx_out)
        gate_out = jnp.where(lane == r, es[r] / tot, gate_out)
    idx_ref[...] = idx_out.astype(jnp.int32)
    gate_ref[...] = gate_out
    hits = sum(jnp.sum((lane == ix).astype(F32), axis=0, keepdims=True) for ix in idxs)

    @pl.when(pl.program_id(0) == 0)
    def _():
        cnt_ref[...] = jnp.zeros(cnt_ref.shape, cnt_ref.dtype)

    cnt_ref[...] += jnp.broadcast_to(hits, cnt_ref.shape)


def _router(a, wo, x, g, w, b, *, tm=512):
    n, d = x.shape
    ka = a.shape[1]
    e = w.shape[1]
    tm = min(tm, n)
    w_pad = jnp.zeros((d, LANES), F32).at[:, :e].set(w)
    whi = w_pad.astype(BF16)
    wlo = (w_pad - whi.astype(F32)).astype(BF16)
    b_pad = jnp.full((1, LANES), -jnp.inf, F32).at[0, :e].set(b.astype(F32))
    return pl.pallas_call(
        _router_kernel,
        out_shape=(jax.ShapeDtypeStruct((n, d), F32),
                   jax.ShapeDtypeStruct((n * (d // LANES), LANES), F32), jax.ShapeDtypeStruct((n, LANES), jnp.int32),
                   jax.ShapeDtypeStruct((n, LANES), F32), jax.ShapeDtypeStruct((8, LANES), F32)),
        grid=(n // tm,),
        in_specs=[
            pl.BlockSpec((tm, ka), lambda i: (i, 0)),
            pl.BlockSpec((ka, d), lambda i: (0, 0)),
            pl.BlockSpec((tm, d), lambda i: (i, 0)),
            pl.BlockSpec((1, d), lambda i: (0, 0)),
            pl.BlockSpec((d, LANES), lambda i: (0, 0)),
            pl.BlockSpec((d, LANES), lambda i: (0, 0)),
            pl.BlockSpec((1, LANES), lambda i: (0, 0)),
        ],
        out_specs=(pl.BlockSpec((tm, d), lambda i: (i, 0)),
                   pl.BlockSpec((tm * (d // LANES), LANES), lambda i: (i, 0)), pl.BlockSpec((tm, LANES), lambda i: (i, 0)),
                   pl.BlockSpec((tm, LANES), lambda i: (i, 0)), pl.BlockSpec((8, LANES), lambda i: (0, 0))),
        compiler_params=_cparams(("arbitrary",)),
        name="moe_router",
    )(a, wo, x, g.reshape(1, d).astype(F32), whi, wlo, b_pad)


def _moe_expert_kernel(blk_exp_ref, tok_ref, dst_ref, live_ref, h_hbm, ga_ref, gb_ref, wgu_a, bgu_a, wd_a, bd_a,
                       wgu_b, bgu_b, wd_b, bd_b, y_hbm, xa, xb, oa, ob, gsem, ssem, *, bm, dff):
    i = pl.program_id(0)
    live = live_ref[0]
    nc = xa.shape[0] // bm
    dump = y_hbm.shape[0] // nc - 2 * bm

    def gather(block, buf, sem):
        base = block * bm
        for r in range(bm):
            pltpu.make_async_copy(h_hbm.at[pl.ds(pl.multiple_of(tok_ref[base + r], nc), nc)], buf.at[pl.ds(r * nc, nc)], sem).start()

    def scatter(block, buf, sem):
        base = block * bm
        for r in range(bm):
            pltpu.make_async_copy(buf.at[pl.ds(r * nc, nc)], y_hbm.at[pl.ds(pl.multiple_of(dst_ref[base + r], nc), nc)], sem).start()

    def wait_gather(buf, sem):
        pltpu.make_async_copy(h_hbm.at[pl.ds(0, bm * nc)], buf, sem).wait()

    def wait_scatter(buf, sem):
        pltpu.make_async_copy(buf, y_hbm.at[pl.ds(0, bm * nc)], sem).wait()

    def experts(x_ref, o_ref, g_ref, wgu, bgu, wd, bd):
        x = jnp.concatenate([x_ref[pl.ds(c, bm, stride=nc), :] for c in range(nc)], axis=1)
        gu = jnp.dot(x.astype(BF16), wgu[0, 0], preferred_element_type=F32) + bgu[0]
        a = jnp.minimum(gu[:, :dff], SWIGLU_LIMIT)
        u = jnp.clip(gu[:, dff:], -SWIGLU_LIMIT, SWIGLU_LIMIT)
        y = (u + 1.0) * (a * _sigmoid(SWIGLU_ALPHA * a))
        out = (jnp.dot(y.astype(BF16), wd[0, 0], preferred_element_type=F32) + bd[0]) * g_ref[...]
        for c in range(nc):
            o_ref[pl.ds(c, bm, stride=nc), :] = out[:, c * LANES:(c + 1) * LANES]

    @pl.when(i == 0)
    def _():
        gather(0, xa, gsem.at[0])
        for buf, sem, base in ((oa, ssem.at[0], dump), (ob, ssem.at[1], dump + bm)):
            buf[...] = jnp.zeros(buf.shape, buf.dtype)
            for r in range(bm):
                pltpu.make_async_copy(buf.at[pl.ds(r * nc, nc)], y_hbm.at[pl.ds((base + r) * nc, nc)], sem).start()

    @pl.when(i < live)
    def _():
        blk_a = 2 * i
        blk_b = 2 * i + 1
        nxt_a = jnp.minimum(2 * i + 2, 2 * live - 1)

        gather(blk_b, xb, gsem.at[1])
        wait_gather(xa, gsem.at[0])
        wait_scatter(oa, ssem.at[0])
        experts(xa, oa, ga_ref, wgu_a, bgu_a, wd_a, bd_a)
        scatter(blk_a, oa, ssem.at[0])

        gather(nxt_a, xa, gsem.at[0])
        wait_gather(xb, gsem.at[1])
        wait_scatter(ob, ssem.at[1])
        experts(xb, ob, gb_ref, wgu_b, bgu_b, wd_b, bd_b)
        scatter(blk_b, ob, ssem.at[1])

    @pl.when(i == live - 1)
    def _():
        wait_gather(xa, gsem.at[0])
        wait_scatter(oa, ssem.at[0])
        wait_scatter(ob, ssem.at[1])


def _moe_experts(h, blk_exp, tok, dst, live, gates, w_gu, b_gu, w_down, b_down, *, bm, layer):
    _, e, d, f2 = w_gu.shape
    nc = d // LANES
    n = h.shape[0] // nc
    dff = f2 // 2
    cap = tok.shape[0]
    n_steps = cap // (2 * bm)
    assert cap == n_steps * 2 * bm

    def wspecs(off):
        return [
            pl.BlockSpec((1, 1, d, f2), lambda i, be, tk, ds, lv: (layer, be[2 * i + off], 0, 0)),
            pl.BlockSpec((1, 1, f2), lambda i, be, tk, ds, lv: (be[2 * i + off], 0, 0)),
            pl.BlockSpec((1, 1, dff, d), lambda i, be, tk, ds, lv: (layer, be[2 * i + off], 0, 0)),
            pl.BlockSpec((1, 1, d), lambda i, be, tk, ds, lv: (be[2 * i + off], 0, 0)),
        ]

    grid_spec = pltpu.PrefetchScalarGridSpec(
        num_scalar_prefetch=4,
        grid=(n_steps,),
        in_specs=[
            pl.BlockSpec(memory_space=pl.ANY),
            pl.BlockSpec((bm, 1), lambda i, be, tk, ds, lv: (2 * i, 0)),
            pl.BlockSpec((bm, 1), lambda i, be, tk, ds, lv: (2 * i + 1, 0)),
        ] + wspecs(0) + wspecs(1),
        out_specs=pl.BlockSpec(memory_space=pl.ANY),
        scratch_shapes=[pltpu.VMEM((bm * nc, LANES), F32)] * 4 + [pltpu.SemaphoreType.DMA((2,)),
                                                                  pltpu.SemaphoreType.DMA((2,))],
    )
    bgu3 = b_gu.reshape(e, 1, f2).astype(F32)
    bd3 = b_down.reshape(e, 1, d).astype(F32)
    g2 = gates.reshape(cap, 1)
    return pl.pallas_call(
        functools.partial(_moe_expert_kernel, bm=bm, dff=dff),
        out_shape=jax.ShapeDtypeStruct(((TOP_K * n + 2 * bm) * nc, LANES), F32),
        grid_spec=grid_spec,
        compiler_params=_cparams(("arbitrary",)),
        name="moe_experts",
    )(blk_exp, tok, dst, live, h, g2, g2, w_gu, bgu3, w_down, bd3, w_gu, bgu3, w_down, bd3)


def _moe_layer(attn_o, w_out, x, norm_g, router_w, router_b, w_gu, b_gu, w_down, b_down, layer):
    n, d = x.shape
    bm = MOE_BM
    x1, h, idx128, gate128, cnt = _router(attn_o, w_out, x, norm_g, router_w, router_b)
    n_assign = n * TOP_K
    flat_e = idx128[:, :TOP_K].reshape(-1)
    flat_g = gate128[:, :TOP_K].reshape(-1)
    order = jnp.argsort(flat_e).astype(jnp.int32)
    counts = cnt[0, :N_EXPERTS].astype(jnp.int32)
    padded = (counts + bm - 1) // bm * bm
    start = jnp.cumsum(counts) - counts
    pend = jnp.cumsum(padded)
    pstart = pend - padded
    n_blocks = -(-n_assign // bm) + N_EXPERTS
    n_blocks += n_blocks % 2
    blk_start = jnp.arange(n_blocks, dtype=jnp.int32) * bm
    blk_exp = jnp.minimum(jnp.sum((pend[None, :] <= blk_start[:, None]).astype(jnp.int32), axis=1),
                          N_EXPERTS - 1).astype(jnp.int32)
    row = jnp.arange(bm, dtype=jnp.int32)[None, :]
    off = (blk_start - pstart[blk_exp])[:, None] + row
    valid = off < counts[blk_exp][:, None]
    a = order[jnp.clip(start[blk_exp][:, None] + off, 0, n_assign - 1).reshape(-1)].reshape(n_blocks, bm)
    nc = d // LANES
    tok = (jnp.where(valid, a // TOP_K, 0) * nc).astype(jnp.int32).reshape(-1)
    dump_row = TOP_K * n + (jnp.arange(n_blocks, dtype=jnp.int32) % 2)[:, None] * bm + row
    dst = (jnp.where(valid, (a % TOP_K) * n + a // TOP_K, dump_row) * nc).astype(jnp.int32).reshape(-1)
    live = jnp.maximum((pend[-1] // bm + 1) // 2, 1).astype(jnp.int32).reshape(1)
    gates = jnp.where(valid, flat_g[a.reshape(-1)].reshape(n_blocks, bm), 0.0).reshape(-1)
    return _moe_experts(h, blk_exp, tok, dst, live, gates, w_gu, b_gu, w_down, b_down, bm=bm, layer=layer), x1


def _ple_kernel(x_ref, y0_ref, y1_ref, y2_ref, y3_ref, g_ref, wg_ref, p_ref, wp_ref, o_ref):
    tm, d = x_ref.shape
    nc = d // LANES

    def plane(y_ref):
        return jnp.concatenate([y_ref[pl.ds(c, tm, stride=nc), :] for c in range(nc)], axis=1)

    xf = x_ref[...] + (((plane(y0_ref) + plane(y1_ref)) + plane(y2_ref)) + plane(y3_ref))
    ms = jnp.mean(xf * xf, axis=-1, keepdims=True)
    h = (xf * lax.rsqrt(ms + EPS) * g_ref[...]).astype(BF16)
    gate = _sigmoid(jnp.dot(h, wg_ref[...], preferred_element_type=F32))
    pp = jnp.dot(p_ref[...].astype(BF16), wp_ref[...], preferred_element_type=F32)
    o_ref[...] = xf + gate * pp


def _ple(x, y, g, wg, p, wp, *, tm=512):
    n, d = x.shape
    pd = p.shape[1]
    tm = min(tm, n)
    nb = n // tm
    nc = d // LANES
    yspecs = [pl.BlockSpec((tm * nc, LANES), functools.partial(lambda i, k: (k * nb + i, 0), k=k)) for k in range(TOP_K)]
    return pl.pallas_call(
        _ple_kernel,
        out_shape=jax.ShapeDtypeStruct((n, d), F32),
        grid=(nb,),
        in_specs=[pl.BlockSpec((tm, d), lambda i: (i, 0))] + yspecs + [
            pl.BlockSpec((1, d), lambda i: (0, 0)),
            pl.BlockSpec((d, d), lambda i: (0, 0)),
            pl.BlockSpec((tm, pd), lambda i: (i, 0)),
            pl.BlockSpec((pd, d), lambda i: (0, 0)),
        ],
        out_specs=pl.BlockSpec((tm, d), lambda i: (i, 0)),
        compiler_params=_cparams(("parallel",)),
        name="ple",
    )(x, y, y, y, y, g.reshape(1, d).astype(F32), wg.astype(BF16), p, wp.astype(BF16))


def _cmp_kernel(z_ref, w1_ref, pos_ref, w2_ref, gain_ref, c_ref, s1_ref, s2_ref, o_ref, *, rows, half):
    j = pl.program_id(1)
    z = z_ref[0, 0, 0]
    pos = pos_ref[0]
    a1 = (z + pos[:, :half]).astype(BF16)
    a2 = (z + pos[:, half:]).astype(BF16)
    u1 = jnp.dot(a1, w1_ref[0, :half, :], preferred_element_type=F32)
    u2 = jnp.dot(a2, w1_ref[0, half:, :], preferred_element_type=F32)
    h = u1 + pltpu.roll(u2, rows - 1, 0)
    hs = h * _sigmoid(h)
    y = jnp.dot(hs.astype(BF16), w2_ref[0], preferred_element_type=F32)

    @pl.when(j == 0)
    def _():
        ss = jnp.sum(y * y, axis=-1, keepdims=True) * (1.0 / LANES)
        yn = y * lax.rsqrt(ss + EPS) * gain_ref[...]
        yn = (yn * c_ref[...] + pltpu.roll(yn, LANES - ROPE_DIM // 2, 1) * s1_ref[...]
              + pltpu.roll(yn, ROPE_DIM // 2, 1) * s2_ref[...])
        o_ref[0, 0, 0] = yn.astype(o_ref.dtype)

    @pl.when(j != 0)
    def _():
        o_ref[0, 0, 0] = y.astype(o_ref.dtype)


def _compress(zr, w1, pos, w2dup, gain_dup, tabs, *, batch):
    rows, width = zr.shape[3], zr.shape[4]
    hid = w1.shape[2]
    g = NSA_GROUPS
    return pl.pallas_call(
        functools.partial(_cmp_kernel, rows=rows, half=width),
        out_shape=jax.ShapeDtypeStruct((batch, 2, g, rows, LANES), BF16),
        grid=(batch, 2, g),
        in_specs=[
            pl.BlockSpec((1, 1, 1, rows, width), lambda b, j, gg: (b, j, gg, 0, 0)),
            pl.BlockSpec((1, 2 * width, hid), lambda b, j, gg: (j, 0, 0)),
            pl.BlockSpec((1, 1, 2 * width), lambda b, j, gg: (j, 0, 0)),
            pl.BlockSpec((1, hid, LANES), lambda b, j, gg: (j, 0, 0)),
            pl.BlockSpec((1, LANES), lambda b, j, gg: (0, 0)),
            pl.BlockSpec((rows, LANES), lambda b, j, gg: (0, 0)),
            pl.BlockSpec((rows, LANES), lambda b, j, gg: (0, 0)),
            pl.BlockSpec((rows, LANES), lambda b, j, gg: (0, 0)),
        ],
        out_specs=pl.BlockSpec((1, 1, 1, rows, LANES), lambda b, j, gg: (b, j, gg, 0, 0)),
        compiler_params=_cparams(("parallel", "parallel", "parallel")),
        name="nsa_compress",
    )(zr, w1, pos, w2dup, gain_dup, *tabs)


def _pipelined(n_items, first, second, skew):
    pending, out = {}, [None] * n_items
    for idx in range(n_items + skew):
        if idx < n_items:
            pending[idx] = first(idx)
        if idx >= skew:
            out[idx - skew] = second(idx - skew, pending.pop(idx - skew))
    return out


def _nsa_attn_kernel(q_ref, gz_ref, kc_ref, vct_ref, ovt_ref, ks_ref, vs_ref, kw_ref, vw_ref, o_ref,
                     e_ref, oc_ref, imp_ref, notsel_ref, *, n_cmp_rows, tk_sel, skew, grp, sub):
    g = pl.program_id(1)
    i = pl.program_id(2)
    qb = NSA_QB
    mh = HEADS_PER_GROUP
    rows = mh * qb
    q0 = i * qb
    seq = ks_ref.shape[0]

    @pl.when(i == 0)
    def _():
        r = lax.broadcasted_iota(jnp.int32, (seq, LANES), 0)
        lb = lax.broadcasted_iota(jnp.int32, (seq, LANES), 1)
        e_ref[...] = jnp.where((r >> 6) == lb, NEG, 0.0).astype(BF16)

    qh = [q_ref[:, m * LANES:(m + 1) * LANES] for m in range(mh)]
    q4 = jnp.concatenate(qh, axis=0)
    jl = lax.broadcasted_iota(jnp.int32, (qb, LANES), 1)

    tlane = q0 + (lax.broadcasted_iota(jnp.int32, (1, rows), 1) & (qb - 1))

    def cmp_branch(nrow):
        st = _dot_nt(kc_ref[0, 0, 0, :nrow, :], q4)
        cend = lax.broadcasted_iota(jnp.int32, (nrow, rows), 0) * CMP_STRIDE + (CMP_BLOCK - 1)
        valid = cend <= tlane
        st = jnp.where(valid, st, NEG)
        e = jnp.where(valid, jnp.exp2(st - jnp.max(st, axis=0, keepdims=True)), 0.0)
        pt = e / jnp.maximum(jnp.sum(e, axis=0, keepdims=True), 1e-30)
        oct = jnp.dot(vct_ref[0, 0, :, :nrow], pt.astype(BF16), preferred_element_type=F32)
        for m in range(mh):
            oc_ref[m * qb:(m + 1) * qb, :] = oct[:, m * qb:(m + 1) * qb].T
        psum = (pt[:, 0:qb] + pt[:, qb:2 * qb]) + (pt[:, 2 * qb:3 * qb] + pt[:, 3 * qb:4 * qb])
        phi, plo = _split2(psum)
        ovt = ovt_ref[:, :nrow]
        imp_ref[...] = (jnp.dot(ovt, phi, preferred_element_type=F32)
                        + jnp.dot(ovt, plo, preferred_element_type=F32))

    n_chunks = n_cmp_rows // LANES
    last_visible = (q0 + qb - CMP_BLOCK) // CMP_STRIDE
    need = jnp.clip(last_visible // LANES + 1, 1, n_chunks)
    for nch in range(1, n_chunks + 1):
        pl.when(need == nch)(functools.partial(cmp_branch, nch * LANES))

    def select_blocks(nrow):
        jb = lax.broadcasted_iota(jnp.int32, (nrow, qb), 0)
        tq = q0 + lax.broadcasted_iota(jnp.int32, (nrow, qb), 1)
        cur = tq >> 6
        forced = (jb == 0) | (jb == cur) | (jb == cur - 1)
        causal = (jb << 6) <= tq
        score = jnp.where(causal, jnp.where(forced, FORCE, imp_ref[:nrow, :]), -jnp.inf)
        notsel_t = jnp.ones((nrow, qb), F32)
        jbf = jb.astype(F32)
        for _ in range(SEL_TOPK):
            smx = jnp.max(score, axis=0, keepdims=True)
            ix = jnp.min(jnp.where(score == smx, jbf, float(LANES)), axis=0, keepdims=True)
            hit = jbf == ix
            notsel_t = jnp.where(hit, 0.0, notsel_t)
            score = jnp.where(hit, -jnp.inf, score)
        notsel_ref[:nrow, :] = notsel_t
        if nrow < LANES:
            notsel_ref[nrow:, :] = jnp.ones((LANES - nrow, qb), F32)

    sel_chunk = 32
    n_sel_chunks = LANES // sel_chunk
    sel_need = jnp.clip(((q0 + qb - 1) // SEL_BLOCK) // sel_chunk + 1, 1, n_sel_chunks)
    for nch in range(1, n_sel_chunks + 1):
        pl.when(sel_need == nch)(functools.partial(select_blocks, nch * sel_chunk))
    notsel_b = notsel_ref[...].T.astype(BF16)
    n_grp = mh // grp
    q_aug = [jnp.concatenate([jnp.concatenate([qh[m], notsel_b], axis=1) for m in range(c * grp, (c + 1) * grp)], axis=0)
             for c in range(n_grp)]
    tpos = q0 + lax.broadcasted_iota(jnp.int32, (qb, 1), 0)
    tpos_g = jnp.concatenate([tpos] * grp, axis=0)

    def sel_tiles(first, n_sub):
        ks, vs = [], []
        for t in range(n_sub):
            k0 = pl.multiple_of((first + t) * tk_sel, tk_sel)
            k_aug = jnp.concatenate([ks_ref[pl.ds(k0, tk_sel), :], e_ref[pl.ds(k0, tk_sel), :]], axis=1)
            ks.append([k_aug] * n_grp)
            vs.append([vs_ref[pl.ds(k0, tk_sel), :]] * n_grp)
        return ks, vs

    def sel_step(first, n_sub, state):
        ks, vs = sel_tiles(first, n_sub)
        return _attend(q_aug, ks, vs, state, skew)

    def sel_tail(first, n_rem, state):
        def mask(t, c, sc):
            if t < n_rem:
                return sc
            kpos = (first + n_rem) * tk_sel + lax.broadcasted_iota(jnp.int32, (grp * qb, tk_sel), 1)
            return jnp.where(kpos <= tpos_g, sc, NEG)

        ks, vs = sel_tiles(first, n_rem + 1)
        return _attend(q_aug, ks, vs, state, skew, mask=mask)

    state = tuple((jnp.full((grp * qb, 1), NEG, F32), jnp.zeros((grp * qb, LANES), F32)) for _ in range(n_grp))
    n_below = q0 // tk_sel
    state = _tile_loops(n_below, sub, sel_step, sel_tail, state)
    sel_acc = [state[m // grp][1][(m % grp) * qb:(m % grp + 1) * qb] for m in range(mh)]

    wk = WINDOW + qb
    k0w = pl.multiple_of(jnp.maximum(i - WINDOW // qb, 0) * qb, qb)
    kwt = kw_ref[pl.ds(k0w, wk), :]
    vwt = vw_ref[pl.ds(k0w, wk), :]
    kposw = k0w + lax.broadcasted_iota(jnp.int32, (grp * qb, wk), 1)
    wmask = (kposw <= tpos_g) & (kposw > tpos_g - WINDOW)
    q_grp = [jnp.concatenate(qh[c * grp:(c + 1) * grp], axis=0) for c in range(n_grp)]

    def win_scores(c):
        return jnp.where(wmask, _dot_nt(q_grp[c], kwt), NEG)

    def win_out(c, sc):
        pe = jnp.exp2(sc - jnp.max(sc, axis=-1, keepdims=True))
        return jnp.dot(pe.astype(BF16), vwt, preferred_element_type=F32)

    win_grp = _pipelined(n_grp, win_scores, win_out, min(skew, n_grp - 1))
    win_acc = [win_grp[m // grp][(m % grp) * qb:(m % grp + 1) * qb] for m in range(mh)]

    sig = _sigmoid(gz_ref[...])
    den_lane = jl == HEAD_DIM
    heads = []
    for m in range(mh):
        col = g * mh + m
        gates = [jnp.sum(jnp.where(jl == br * N_HEADS + col, sig, 0.0), axis=-1, keepdims=True)
                 for br in range(3)]
        acc_s = sel_acc[m]
        acc_w = win_acc[m]
        den_s = jnp.sum(jnp.where(den_lane, acc_s, 0.0), axis=-1, keepdims=True)
        den_w = jnp.sum(jnp.where(den_lane, acc_w, 0.0), axis=-1, keepdims=True)
        heads.append(gates[0] * oc_ref[m * qb:(m + 1) * qb, :] + (gates[1] / den_s) * acc_s
                     + (gates[2] / den_w) * acc_w)
    lane_lo = jl < HEAD_DIM
    o_ref[:, 0:LANES] = jnp.where(lane_lo, heads[0], pltpu.roll(heads[1], HEAD_DIM, 1)).astype(o_ref.dtype)
    o_ref[:, LANES:2 * LANES] = jnp.where(lane_lo, heads[2], pltpu.roll(heads[3], HEAD_DIM, 1)).astype(o_ref.dtype)


def _nsa_attn(qn, gz, kvc, overlap, kvd, *, batch, seq, skew=2, grp=2, sub=4):
    n = qn.shape[0]
    qb = NSA_QB
    nq = seq // qb
    g = NSA_GROUPS
    rows = HEADS_PER_GROUP * qb
    n_cmp_rows = kvc.shape[3]
    tk_sel = min(512, seq)
    assert seq >= WINDOW + qb and seq % tk_sel == 0 and tk_sel % qb == 0
    in_specs = [
        pl.BlockSpec((qb, HEADS_PER_GROUP * LANES), lambda b, gg, i: (b * nq + i, gg)),
        pl.BlockSpec((qb, LANES), lambda b, gg, i: (b * nq + i, 0)),
        pl.BlockSpec((1, 1, 1, n_cmp_rows, LANES), lambda b, gg, i: (b, 0, gg, 0, 0)),
        pl.BlockSpec((1, 1, LANES, n_cmp_rows), lambda b, gg, i: (b, gg, 0, 0)),
        pl.BlockSpec((LANES, n_cmp_rows), lambda b, gg, i: (0, 0)),
        pl.BlockSpec((seq, LANES), lambda b, gg, i: (b, 0 * g + gg)),
        pl.BlockSpec((seq, LANES), lambda b, gg, i: (b, 1 * g + gg)),
        pl.BlockSpec((seq, LANES), lambda b, gg, i: (b, 2 * g + gg)),
        pl.BlockSpec((seq, LANES), lambda b, gg, i: (b, 3 * g + gg)),
    ]
    return pl.pallas_call(
        functools.partial(_nsa_attn_kernel, n_cmp_rows=n_cmp_rows, tk_sel=tk_sel, skew=skew, grp=grp, sub=sub),
        out_shape=jax.ShapeDtypeStruct((n, N_HEADS * HEAD_DIM), BF16),
        grid=(batch, g, nq),
        in_specs=in_specs,
        out_specs=pl.BlockSpec((qb, HEADS_PER_GROUP * HEAD_DIM), lambda b, gg, i: (b * nq + i, gg)),
        scratch_shapes=[pltpu.VMEM((seq, LANES), BF16), pltpu.VMEM((rows, LANES), F32), pltpu.VMEM((LANES, qb), F32),
                        pltpu.VMEM((LANES, qb), F32)],
        compiler_params=_cparams(("parallel", "parallel", "arbitrary")),
        name="nsa_attn",
    )(qn, gz, kvc, jnp.swapaxes(kvc[:, 1], -1, -2), overlap.T, kvd, kvd, kvd, kvd)


def _pad_cols(w, width=LANES):
    d, f = w.shape
    return jnp.zeros((d, width), w.dtype).at[:, :f].set(w)


def _tile_gain(gain, n_heads, scale=1.0, dup=False):
    g = gain.astype(F32) * scale
    second = g if dup else jnp.zeros_like(g)
    return jnp.tile(jnp.concatenate([g, second]), n_heads)


def _fox_aug_tables():
    hw = N_HEADS * LANES
    piece = jnp.arange(3)[:, None]
    head = jnp.arange(N_HEADS)[None, :]
    src = (piece * N_HEADS + head).reshape(-1)
    q_dst = (head * LANES + HEAD_DIM + piece).reshape(-1)
    k_dst = (hw + head * LANES + HEAD_DIM + 3 + piece).reshape(-1)
    place = jnp.zeros((LANES, 3 * hw), F32).at[src, q_dst].set(1.0).at[src, k_dst].set(-1.0)
    lane = jnp.arange(LANES)
    q_bias = ((lane >= HEAD_DIM + 3) & (lane < HEAD_DIM + 6)).astype(F32)
    k_bias = ((lane >= HEAD_DIM) & (lane < HEAD_DIM + 3)).astype(F32)
    v_bias = (lane == HEAD_DIM).astype(F32)
    bias = jnp.concatenate([jnp.tile(q_bias, N_HEADS), jnp.tile(k_bias, N_HEADS), jnp.tile(v_bias, N_HEADS)])
    return place.astype(BF16), bias


def _fox_layer(x, batch, seq, attn_g, w_in, b_f, qk_gain, w_out):
    hd = N_HEADS * HEAD_DIM
    hw = N_HEADS * LANES
    scale = HEAD_DIM ** -0.5
    z = _rms_proj(x, attn_g, _pad_cols(w_in[:, 3 * hd:]).astype(BF16), out_dtype=F32, seq=seq)
    b_pad = jnp.zeros((1, LANES), F32).at[0, :N_HEADS].set(b_f.astype(F32))
    c3 = _fox_c(z, b_pad, batch=batch, seq=seq)
    w_bf = w_in.astype(BF16)
    gain = jnp.concatenate([_tile_gain(qk_gain[0], N_HEADS, scale * LOG2E), _tile_gain(qk_gain[1], N_HEADS),
                            jnp.zeros((hw,), F32)])
    tn = 1024
    place, bias = _fox_aug_tables()
    qkv = _rms_proj(x, attn_g, w_bf, out_dtype=BF16, seq=seq, tn=tn, gain=gain, w_cols=3 * hd,
                    norm_blocks=tuple(range(2 * hw // tn)), placed=(c3, place), col_bias=bias, expand=True)
    o = _fox_attn(qkv, batch=batch, seq=seq)
    return o, w_out.astype(BF16)


def _nsa_shared_kv(x, batch, seq, kv_norm, kv_w, kv_k_gain, cmp_pos, cmp_w1, cmp_w2):
    g = NSA_GROUPS
    gw = g * HEAD_DIM
    tabs = _rope_tables(jnp.arange(seq))
    kv_bf = kv_w.astype(BF16)
    zero = jnp.zeros((g * LANES,), F32)
    one_lane = jnp.tile((jnp.arange(LANES) == HEAD_DIM).astype(F32), g)
    gain_kvd = jnp.concatenate([_tile_gain(kv_k_gain[1], g), zero, _tile_gain(kv_k_gain[2], g), zero])
    bias_kvd = jnp.concatenate([zero, one_lane, zero, one_lane])
    kvd = _rms_proj(x, kv_norm, kv_bf, out_dtype=BF16, seq=seq, tn=g * LANES, gain=gain_kvd, w_col0=2 * gw,
                    norm_blocks=(0, 2), rope_tabs=tabs, col_bias=bias_kvd, expand=True)
    zc = _rms_proj(x, kv_norm, kv_bf, out_dtype=F32, seq=seq, w_cols=2 * gw)
    rows = seq // CMP_STRIDE
    zr = zc.reshape(batch, rows, CMP_STRIDE, 2, g, HEAD_DIM).transpose(0, 3, 4, 1, 2, 5)
    zr = zr.reshape(batch, 2, g, rows, CMP_STRIDE * HEAD_DIM)
    cmp_end = jnp.arange(rows) * CMP_STRIDE + CMP_BLOCK - 1
    ctabs = _rope_tables(cmp_end)
    w2dup = jnp.concatenate([cmp_w2, cmp_w2], axis=-1).astype(BF16)
    kvc = _compress(zr, cmp_w1.astype(BF16), cmp_pos.reshape(2, 1, CMP_BLOCK * HEAD_DIM).astype(F32), w2dup,
                    _tile_gain(kv_k_gain[0], 1, dup=True).reshape(1, LANES), ctabs, batch=batch)
    return kvc, kvd


def _nsa_layer(x, batch, seq, attn_g, w_in, q_gain, w_out, kvc, kvd):
    hd = N_HEADS * HEAD_DIM
    scale = HEAD_DIM ** -0.5
    rows = seq // CMP_STRIDE
    tabs = _rope_tables(jnp.arange(seq))
    qn = _rms_proj(x, attn_g, w_in.astype(BF16), out_dtype=BF16, seq=seq, expand=True, w_cols=hd,
                   gain=_tile_gain(q_gain, N_HEADS, scale * LOG2E), norm_blocks=tuple(range(N_HEADS * LANES // 512)),
                   rope_tabs=tabs)
    gz = _rms_proj(x, attn_g, _pad_cols(w_in[:, hd:]).astype(BF16), out_dtype=F32, seq=seq)

    n_sel = seq // SEL_BLOCK
    assert n_sel <= LANES
    cmp_start = jnp.arange(rows) * CMP_STRIDE
    sel_start = jnp.arange(LANES) * SEL_BLOCK
    overlap = jnp.clip(jnp.minimum(cmp_start[:, None] + CMP_BLOCK, sel_start[None, :] + SEL_BLOCK)
                       - jnp.maximum(cmp_start[:, None], sel_start[None, :]), 0)
    overlap = jnp.where((jnp.arange(rows) < rows - 1)[:, None] & (jnp.arange(LANES) < n_sel)[None, :], overlap, 0)
    o = _nsa_attn(qn, gz, kvc, overlap.astype(BF16), kvd, batch=batch, seq=seq)
    return o, w_out.astype(BF16)


def kernel(x, p, attn_norm, ffn_norm, ple_norm, ple_gate_w, ple_proj_w, router_w, router_b, w_gu, b_gu, w_down, b_down, fox_w_in, fox_b_f, fox_qk_gain, fox_w_out, kv_norm, kv_w, kv_k_gain, cmp_pos, cmp_w1, cmp_w2, nsa_w_in, nsa_q_gain, nsa_w_out):
    batch, seq, d = x.shape
    depth = p.shape[0]
    n_a = fox_w_in.shape[0]
    xt = x.reshape(batch * seq, d)
    w_gu_bf = w_gu.astype(BF16)
    w_down_bf = w_down.astype(BF16)
    shared = None
    for layer in range(depth):
        if layer == n_a:
            shared = _nsa_shared_kv(xt, batch, seq, kv_norm, kv_w, kv_k_gain, cmp_pos, cmp_w1, cmp_w2)
        if layer < n_a:
            o, wo = _fox_layer(xt, batch, seq, attn_norm[layer], fox_w_in[layer], fox_b_f[layer],
                               fox_qk_gain[layer], fox_w_out[layer])
        else:
            i = layer - n_a
            o, wo = _nsa_layer(xt, batch, seq, attn_norm[layer], nsa_w_in[i], nsa_q_gain[i], nsa_w_out[i], *shared)
        y, xt = _moe_layer(o, wo, xt, ffn_norm[layer], router_w[layer], router_b[layer], w_gu_bf, b_gu[layer],
                           w_down_bf, b_down[layer], layer)
        xt = _ple(xt, y, ple_norm[layer], ple_gate_w[layer], p[layer].reshape(batch * seq, -1), ple_proj_w[layer])
    return xt.reshape(batch, seq, d)
```
